```python
import jax, jax.numpy as jnp
from jax import lax
import numpy as np

D_MODEL = 1024
BATCH = 16
SEQ = 2048
DEPTH = 4

CHUNK = 64
N_MIXERS = 3
EPS = 1e-6

GMLP_BLOCK = 128
A_WIDTH = D_MODEL
A_GROUPS = 8
A_GROUP_DIM = A_WIDTH // A_GROUPS

B_WIDTH = D_MODEL
CONV_WIDTH = 3

C_WIDTH = D_MODEL
POOL_WINDOWS = (2, 4, 8, 16)
C_GROUPS = len(POOL_WINDOWS)
C_GROUP_DIM = C_WIDTH // C_GROUPS

FFN_HIDDEN = ((8 * D_MODEL + 3 * 256 - 1) // (3 * 256)) * 256

N_A = (DEPTH + 2) // 3
N_B = (DEPTH + 1) // 3
N_C = DEPTH // 3

kernel_name = "interleaved_gmlp_shortconv_pool_trunk"


def _rms_norm(x, g):
    x32 = x.astype(jnp.float32)
    y = x32 * lax.rsqrt(jnp.mean(x32 * x32, axis=-1, keepdims=True) + EPS)
    return (y * g.astype(jnp.float32)).astype(x.dtype)


def _chunk_causal_mask(n):
    pos = jnp.arange(n)
    return (pos[None, :] // CHUNK) <= (pos[:, None] // CHUNK)


def _spatial_gating_mixer(h, w_in, v_norm_g, w_s, b_s, w_out):
    b, s, _ = h.shape
    z = jax.nn.gelu(h @ w_in, approximate=False)
    u, v = jnp.split(z, 2, axis=-1)
    v32 = v.astype(jnp.float32)
    mu = jnp.mean(v32, axis=-1, keepdims=True)
    var = jnp.mean(jnp.square(v32 - mu), axis=-1, keepdims=True)
    v = ((v32 - mu) * lax.rsqrt(var + EPS) * v_norm_g.astype(jnp.float32)).astype(h.dtype)
    v = v.reshape(b, s // GMLP_BLOCK, GMLP_BLOCK, A_GROUPS, A_GROUP_DIM)
    w_masked = jnp.where(_chunk_causal_mask(GMLP_BLOCK)[None], w_s, 0)
    sv = jnp.einsum('gij,bnjgc->bnigc', w_masked, v) + b_s.T[None, None, :, :, None]
    y = u * sv.reshape(b, s, A_WIDTH)
    return y @ w_out


def _causal_depthwise_conv(x, conv_w):
    return lax.conv_general_dilated(
        x, conv_w[:, None, :],
        window_strides=(1,),
        padding=[(CONV_WIDTH - 1, 0)],
        dimension_numbers=('NWC', 'WIO', 'NWC'),
        feature_group_count=x.shape[-1])


def _short_conv_mixer(h, w_in, conv_w, w_out):
    gate_b, gate_c, xt = jnp.split(h @ w_in, 3, axis=-1)
    y = _causal_depthwise_conv(gate_c * xt, conv_w)
    return (gate_b * y) @ w_out


def _multiscale_pool_mixer(h, w_in, w_grp, scale, w_out):
    b, s, _ = h.shape
    p = (h @ w_in).reshape(b, s, C_GROUPS, C_GROUP_DIM)
    p32 = p.astype(jnp.float32)
    cs = jnp.cumsum(p32, axis=1)
    t = jnp.arange(1, s + 1, dtype=jnp.float32)
    outs = []
    for g, w in enumerate(POOL_WINDOWS):
        csg = cs[:, :, g]
        shifted = jnp.pad(csg[:, :s - w], ((0, 0), (w, 0), (0, 0)))
        mean = (csg - shifted) / jnp.minimum(t, w)[None, :, None]
        outs.append(mean - p32[:, :, g])
    d = jnp.stack(outs, axis=2).astype(h.dtype)
    y = jnp.einsum('bsgc,gcd->bsgd', d, w_grp).reshape(b, s, C_WIDTH) * scale
    return y @ w_out


def _swiglu(h, w_gate, w_up, w_down):
    return (jax.nn.silu(h @ w_gate) * (h @ w_up)) @ w_down


def _fwd_setup_inputs(seed: int = 0) -> dict:
    key = jax.random.key(seed)
    ks = jax.random.split(key, 24)
    f32 = jnp.float32

    def nrm(k, shape, scale):
        return jax.random.normal(k, shape, f32) * scale

    x = jax.random.normal(ks[0], (BATCH, SEQ, D_MODEL), f32)
    norm_mix_g = 1.0 + nrm(ks[1], (DEPTH, D_MODEL), 0.05)
    norm_ffn_g = 1.0 + nrm(ks[2], (DEPTH, D_MODEL), 0.05)
    final_norm_g = 1.0 + nrm(ks[3], (D_MODEL,), 0.05)

    a_w_in = nrm(ks[4], (N_A, D_MODEL, 2 * A_WIDTH), D_MODEL ** -0.5)
    a_v_norm_g = 1.0 + nrm(ks[5], (N_A, A_WIDTH), 0.05)
    a_w_s = nrm(ks[6], (N_A, A_GROUPS, GMLP_BLOCK, GMLP_BLOCK), GMLP_BLOCK ** -0.5)
    a_b_s = 1.0 + nrm(ks[7], (N_A, A_GROUPS, GMLP_BLOCK), 0.05)
    a_w_out = nrm(ks[8], (N_A, A_WIDTH, D_MODEL), A_WIDTH ** -0.5)

    b_w_in = nrm(ks[9], (N_B, D_MODEL, 3 * B_WIDTH), D_MODEL ** -0.5)
    b_conv_w = nrm(ks[10], (N_B, CONV_WIDTH, B_WIDTH), CONV_WIDTH ** -0.5)
    b_w_out = nrm(ks[11], (N_B, B_WIDTH, D_MODEL), B_WIDTH ** -0.5)

    c_w_in = nrm(ks[12], (N_C, D_MODEL, C_WIDTH), D_MODEL ** -0.5)
    c_w_grp = nrm(ks[13], (N_C, C_GROUPS, C_GROUP_DIM, C_GROUP_DIM), C_GROUP_DIM ** -0.5)
    c_scale = 1.0 + nrm(ks[14], (N_C, C_WIDTH), 0.1)
    c_w_out = nrm(ks[15], (N_C, C_WIDTH, D_MODEL), C_WIDTH ** -0.5)

    f_w_gate = nrm(ks[16], (DEPTH, D_MODEL, FFN_HIDDEN), D_MODEL ** -0.5)
    f_w_up = nrm(ks[17], (DEPTH, D_MODEL, FFN_HIDDEN), D_MODEL ** -0.5)
    f_w_down = nrm(ks[18], (DEPTH, FFN_HIDDEN, D_MODEL), FFN_HIDDEN ** -0.5)

    return {
        "x": x,
        "norm_mix_g": norm_mix_g, "norm_ffn_g": norm_ffn_g, "final_norm_g": final_norm_g,
        "a_w_in": a_w_in, "a_v_norm_g": a_v_norm_g, "a_w_s": a_w_s, "a_b_s": a_b_s,
        "a_w_out": a_w_out,
        "b_w_in": b_w_in, "b_conv_w": b_conv_w, "b_w_out": b_w_out,
        "c_w_in": c_w_in, "c_w_grp": c_w_grp, "c_scale": c_scale, "c_w_out": c_w_out,
        "f_w_gate": f_w_gate, "f_w_up": f_w_up, "f_w_down": f_w_down,
    }


def _fwd_reference(x, norm_mix_g, norm_ffn_g, final_norm_g,
              a_w_in, a_v_norm_g, a_w_s, a_b_s, a_w_out,
              b_w_in, b_conv_w, b_w_out,
              c_w_in, c_w_grp, c_scale, c_w_out,
              f_w_gate, f_w_up, f_w_down):
    for i in range(DEPTH):
        kind = i % N_MIXERS
        j = i // N_MIXERS
        h = _rms_norm(x, norm_mix_g[i])
        if kind == 0:
            m = _spatial_gating_mixer(h, a_w_in[j], a_v_norm_g[j], a_w_s[j], a_b_s[j], a_w_out[j])
        elif kind == 1:
            m = _short_conv_mixer(h, b_w_in[j], b_conv_w[j], b_w_out[j])
        else:
            m = _multiscale_pool_mixer(h, c_w_in[j], c_w_grp[j], c_scale[j], c_w_out[j])
        x = x + m
        h = _rms_norm(x, norm_ffn_g[i])
        x = x + _swiglu(h, f_w_gate[i], f_w_up[i], f_w_down[i])
    return _rms_norm(x, final_norm_g)


import jax as _jax
import jax.numpy as _jnp

TWIN_FORMAT = 'train_step'
FWD_PARAMS = ['x', 'norm_mix_g', 'norm_ffn_g', 'final_norm_g', 'a_w_in', 'a_v_norm_g', 'a_w_s', 'a_b_s', 'a_w_out', 'b_w_in', 'b_conv_w', 'b_w_out', 'c_w_in', 'c_w_grp', 'c_scale', 'c_w_out', 'f_w_gate', 'f_w_up', 'f_w_down']
TWIN_WEIGHTS = ['norm_mix_g', 'norm_ffn_g', 'final_norm_g', 'a_w_in', 'a_v_norm_g', 'a_w_s', 'a_b_s', 'a_w_out', 'b_w_in', 'b_conv_w', 'b_w_out', 'c_w_in', 'c_w_grp', 'c_scale', 'c_w_out', 'f_w_gate', 'f_w_up', 'f_w_down']
TWIN_DIFF_INPUT = 'x'
TWIN_INPUTS = ['x', 'norm_mix_g', 'norm_ffn_g', 'final_norm_g', 'a_w_in', 'a_v_norm_g', 'a_w_s', 'a_b_s', 'a_w_out', 'b_w_in', 'b_conv_w', 'b_w_out', 'c_w_in', 'c_w_grp', 'c_scale', 'c_w_out', 'f_w_gate', 'f_w_up', 'f_w_down', 'loss_target', 'm_norm_mix_g', 'm_norm_ffn_g', 'm_final_norm_g', 'm_a_w_in', 'm_a_v_norm_g', 'm_a_w_s', 'm_a_b_s', 'm_a_w_out', 'm_b_w_in', 'm_b_conv_w', 'm_b_w_out', 'm_c_w_in', 'm_c_w_grp', 'm_c_scale', 'm_c_w_out', 'm_f_w_gate', 'm_f_w_up', 'm_f_w_down', 'v_norm_mix_g', 'v_norm_ffn_g', 'v_final_norm_g', 'v_a_w_in', 'v_a_v_norm_g', 'v_a_w_s', 'v_a_b_s', 'v_a_w_out', 'v_b_w_in', 'v_b_conv_w', 'v_b_w_out', 'v_c_w_in', 'v_c_w_grp', 'v_c_scale', 'v_c_w_out', 'v_f_w_gate', 'v_f_w_up', 'v_f_w_down']
TWIN_OUTPUTS = ['loss', 'grad_x', 'grad_norm_mix_g', 'grad_norm_ffn_g', 'grad_final_norm_g', 'grad_a_w_in', 'grad_a_v_norm_g', 'grad_a_w_s', 'grad_a_b_s', 'grad_a_w_out', 'grad_b_w_in', 'grad_b_conv_w', 'grad_b_w_out', 'grad_c_w_in', 'grad_c_w_grp', 'grad_c_scale', 'grad_c_w_out', 'grad_f_w_gate', 'grad_f_w_up', 'grad_f_w_down', 'delta_norm_mix_g', 'delta_norm_ffn_g', 'delta_final_norm_g', 'delta_a_w_in', 'delta_a_v_norm_g', 'delta_a_w_s', 'delta_a_b_s', 'delta_a_w_out', 'delta_b_w_in', 'delta_b_conv_w', 'delta_b_w_out', 'delta_c_w_in', 'delta_c_w_grp', 'delta_c_scale', 'delta_c_w_out', 'delta_f_w_gate', 'delta_f_w_up', 'delta_f_w_down', 'new_m_norm_mix_g', 'new_m_norm_ffn_g', 'new_m_final_norm_g', 'new_m_a_w_in', 'new_m_a_v_norm_g', 'new_m_a_w_s', 'new_m_a_b_s', 'new_m_a_w_out', 'new_m_b_w_in', 'new_m_b_conv_w', 'new_m_b_w_out', 'new_m_c_w_in', 'new_m_c_w_grp', 'new_m_c_scale', 'new_m_c_w_out', 'new_m_f_w_gate', 'new_m_f_w_up', 'new_m_f_w_down', 'new_v_norm_mix_g', 'new_v_norm_ffn_g', 'new_v_final_norm_g', 'new_v_a_w_in', 'new_v_a_v_norm_g', 'new_v_a_w_s', 'new_v_a_b_s', 'new_v_a_w_out', 'new_v_b_w_in', 'new_v_b_conv_w', 'new_v_b_w_out', 'new_v_c_w_in', 'new_v_c_w_grp', 'new_v_c_scale', 'new_v_c_w_out', 'new_v_f_w_gate', 'new_v_f_w_up', 'new_v_f_w_down']
TWIN_LEAF_KINDS = {'loss': 'loss', 'grad_x': 'grad_x', 'grad_norm_mix_g': 'grad_w', 'grad_norm_ffn_g': 'grad_w', 'grad_final_norm_g': 'grad_w', 'grad_a_w_in': 'grad_w', 'grad_a_v_norm_g': 'grad_w', 'grad_a_w_s': 'grad_w', 'grad_a_b_s': 'grad_w', 'grad_a_w_out': 'grad_w', 'grad_b_w_in': 'grad_w', 'grad_b_conv_w': 'grad_w', 'grad_b_w_out': 'grad_w', 'grad_c_w_in': 'grad_w', 'grad_c_w_grp': 'grad_w', 'grad_c_scale': 'grad_w', 'grad_c_w_out': 'grad_w', 'grad_f_w_gate': 'grad_w', 'grad_f_w_up': 'grad_w', 'grad_f_w_down': 'grad_w', 'delta_norm_mix_g': 'delta_w', 'delta_norm_ffn_g': 'delta_w', 'delta_final_norm_g': 'delta_w', 'delta_a_w_in': 'delta_w', 'delta_a_v_norm_g': 'delta_w', 'delta_a_w_s': 'delta_w', 'delta_a_b_s': 'delta_w', 'delta_a_w_out': 'delta_w', 'delta_b_w_in': 'delta_w', 'delta_b_conv_w': 'delta_w', 'delta_b_w_out': 'delta_w', 'delta_c_w_in': 'delta_w', 'delta_c_w_grp': 'delta_w', 'delta_c_scale': 'delta_w', 'delta_c_w_out': 'delta_w', 'delta_f_w_gate': 'delta_w', 'delta_f_w_up': 'delta_w', 'delta_f_w_down': 'delta_w', 'new_m_norm_mix_g': 'new_m', 'new_m_norm_ffn_g': 'new_m', 'new_m_final_norm_g': 'new_m', 'new_m_a_w_in': 'new_m', 'new_m_a_v_norm_g': 'new_m', 'new_m_a_w_s': 'new_m', 'new_m_a_b_s': 'new_m', 'new_m_a_w_out': 'new_m', 'new_m_b_w_in': 'new_m', 'new_m_b_conv_w': 'new_m', 'new_m_b_w_out': 'new_m', 'new_m_c_w_in': 'new_m', 'new_m_c_w_grp': 'new_m', 'new_m_c_scale': 'new_m', 'new_m_c_w_out': 'new_m', 'new_m_f_w_gate': 'new_m', 'new_m_f_w_up': 'new_m', 'new_m_f_w_down': 'new_m', 'new_v_norm_mix_g': 'new_v', 'new_v_norm_ffn_g': 'new_v', 'new_v_final_norm_g': 'new_v', 'new_v_a_w_in': 'new_v', 'new_v_a_v_norm_g': 'new_v', 'new_v_a_w_s': 'new_v', 'new_v_a_b_s': 'new_v', 'new_v_a_w_out': 'new_v', 'new_v_b_w_in': 'new_v', 'new_v_b_conv_w': 'new_v', 'new_v_b_w_out': 'new_v', 'new_v_c_w_in': 'new_v', 'new_v_c_w_grp': 'new_v', 'new_v_c_scale': 'new_v', 'new_v_c_w_out': 'new_v', 'new_v_f_w_gate': 'new_v', 'new_v_f_w_up': 'new_v', 'new_v_f_w_down': 'new_v'}


def _forward(args):
    return _fwd_reference(*[args[k] for k in FWD_PARAMS])


def _output_shape():
    out = _jax.eval_shape(lambda: _forward(_fwd_setup_inputs(0)))
    return out.shape, out.dtype

N_MICROBATCH = 1
ADAM_LR = 0.001
ADAM_B1 = 0.9
ADAM_B2 = 0.999
ADAM_EPS = 1e-08
ADAM_WD = 0.01
ADAM_STEP = 10
PER_EXAMPLE_BATCH_AXIS = {'x': 0, 'loss_target': 0}
SHARED_INPUTS = []
_WEIGHT_DTYPES = {'norm_mix_g': _jnp.float32, 'norm_ffn_g': _jnp.float32, 'final_norm_g': _jnp.float32, 'a_w_in': _jnp.float32, 'a_v_norm_g': _jnp.float32, 'a_w_s': _jnp.float32, 'a_b_s': _jnp.float32, 'a_w_out': _jnp.float32, 'b_w_in': _jnp.float32, 'b_conv_w': _jnp.float32, 'b_w_out': _jnp.float32, 'c_w_in': _jnp.float32, 'c_w_grp': _jnp.float32, 'c_scale': _jnp.float32, 'c_w_out': _jnp.float32, 'f_w_gate': _jnp.float32, 'f_w_up': _jnp.float32, 'f_w_down': _jnp.float32}
MOMENT_SCALE = {'norm_mix_g': 1.703456e-01, 'norm_ffn_g': 1.071345e-01, 'final_norm_g': 3.207169e+01, 'a_w_in': 1.295819e-01, 'a_v_norm_g': 1.004196e-01, 'a_w_s': 9.361586e-02, 'a_b_s': 1.119388e-01, 'a_w_out': 1.471364e-01, 'b_w_in': 1.195484e-01, 'b_conv_w': 1.223218e-01, 'b_w_out': 1.212977e-01, 'c_w_in': 8.706406e-02, 'c_w_grp': 8.683842e-02, 'c_scale': 9.369078e-02, 'c_w_out': 8.741504e-02, 'f_w_gate': 4.580447e-02, 'f_w_up': 4.451665e-02, 'f_w_down': 7.366888e-02}


def _to_microbatches(a, axis):
    t = _jnp.moveaxis(a, axis, 0)
    t = t.reshape((N_MICROBATCH, t.shape[0] // N_MICROBATCH) + t.shape[1:])
    return _jnp.moveaxis(t, 1, axis + 1)


def setup_inputs(seed: int = 0) -> dict:
    inp = _fwd_setup_inputs(seed)
    key = _jax.random.fold_in(_jax.random.key(seed), 7919)
    shape, _ = _output_shape()
    out = dict(inp)
    out["loss_target"] = _jax.random.normal(_jax.random.fold_in(key, 0), shape, _jnp.float32)
    for i, name in enumerate(TWIN_WEIGHTS):
        w = inp[name].astype(_jnp.float32)
        if MOMENT_SCALE is None:
            s = _jnp.sqrt(_jnp.mean(_jnp.square(w)) + 1e-30)
        else:
            s = MOMENT_SCALE[name]
        km, kv = _jax.random.split(_jax.random.fold_in(key, i + 1))
        out[name] = w
        out["m_" + name] = s * _jax.random.normal(km, w.shape, _jnp.float32)
        out["v_" + name] = (s * s) * _jax.random.uniform(kv, w.shape, _jnp.float32, 0.5, 1.5)
    if N_MICROBATCH > 1:
        for name, axis in PER_EXAMPLE_BATCH_AXIS.items():
            out[name] = _to_microbatches(out[name], axis)
    return {'x': out['x'], 'norm_mix_g': out['norm_mix_g'], 'norm_ffn_g': out['norm_ffn_g'], 'final_norm_g': out['final_norm_g'], 'a_w_in': out['a_w_in'], 'a_v_norm_g': out['a_v_norm_g'], 'a_w_s': out['a_w_s'], 'a_b_s': out['a_b_s'], 'a_w_out': out['a_w_out'], 'b_w_in': out['b_w_in'], 'b_conv_w': out['b_conv_w'], 'b_w_out': out['b_w_out'], 'c_w_in': out['c_w_in'], 'c_w_grp': out['c_w_grp'], 'c_scale': out['c_scale'], 'c_w_out': out['c_w_out'], 'f_w_gate': out['f_w_gate'], 'f_w_up': out['f_w_up'], 'f_w_down': out['f_w_down'], 'loss_target': out['loss_target'], 'm_norm_mix_g': out['m_norm_mix_g'], 'm_norm_ffn_g': out['m_norm_ffn_g'], 'm_final_norm_g': out['m_final_norm_g'], 'm_a_w_in': out['m_a_w_in'], 'm_a_v_norm_g': out['m_a_v_norm_g'], 'm_a_w_s': out['m_a_w_s'], 'm_a_b_s': out['m_a_b_s'], 'm_a_w_out': out['m_a_w_out'], 'm_b_w_in': out['m_b_w_in'], 'm_b_conv_w': out['m_b_conv_w'], 'm_b_w_out': out['m_b_w_out'], 'm_c_w_in': out['m_c_w_in'], 'm_c_w_grp': out['m_c_w_grp'], 'm_c_scale': out['m_c_scale'], 'm_c_w_out': out['m_c_w_out'], 'm_f_w_gate': out['m_f_w_gate'], 'm_f_w_up': out['m_f_w_up'], 'm_f_w_down': out['m_f_w_down'], 'v_norm_mix_g': out['v_norm_mix_g'], 'v_norm_ffn_g': out['v_norm_ffn_g'], 'v_final_norm_g': out['v_final_norm_g'], 'v_a_w_in': out['v_a_w_in'], 'v_a_v_norm_g': out['v_a_v_norm_g'], 'v_a_w_s': out['v_a_w_s'], 'v_a_b_s': out['v_a_b_s'], 'v_a_w_out': out['v_a_w_out'], 'v_b_w_in': out['v_b_w_in'], 'v_b_conv_w': out['v_b_conv_w'], 'v_b_w_out': out['v_b_w_out'], 'v_c_w_in': out['v_c_w_in'], 'v_c_w_grp': out['v_c_w_grp'], 'v_c_scale': out['v_c_scale'], 'v_c_w_out': out['v_c_w_out'], 'v_f_w_gate': out['v_f_w_gate'], 'v_f_w_up': out['v_f_w_up'], 'v_f_w_down': out['v_f_w_down']}


def _loss(weights, diff, rest, loss_target):
    with _jax.named_scope("forward"):
        args = {**rest, TWIN_DIFF_INPUT: diff, **{k: w.astype(_WEIGHT_DTYPES[k]) for k, w in weights.items()}}
        y = _forward(args)
    with _jax.named_scope("loss_head"):
        err = _jnp.square(y.astype(_jnp.float32) - loss_target)
        return 0.5 * _jnp.sum(_jnp.mean(err, axis=-1)) if err.ndim else 0.5 * err


def _adamw(w, g, m, v):
    m = ADAM_B1 * m + (1.0 - ADAM_B1) * g
    v = ADAM_B2 * v + (1.0 - ADAM_B2) * _jnp.square(g)
    m_hat = m / (1.0 - ADAM_B1 ** ADAM_STEP)
    v_hat = v / (1.0 - ADAM_B2 ** ADAM_STEP)
    delta = -ADAM_LR * (m_hat / (_jnp.sqrt(v_hat) + ADAM_EPS) + ADAM_WD * w)
    return delta, m, v


def reference(x, norm_mix_g, norm_ffn_g, final_norm_g, a_w_in, a_v_norm_g, a_w_s, a_b_s, a_w_out, b_w_in, b_conv_w, b_w_out, c_w_in, c_w_grp, c_scale, c_w_out, f_w_gate, f_w_up, f_w_down, loss_target, m_norm_mix_g, m_norm_ffn_g, m_final_norm_g, m_a_w_in, m_a_v_norm_g, m_a_w_s, m_a_b_s, m_a_w_out, m_b_w_in, m_b_conv_w, m_b_w_out, m_c_w_in, m_c_w_grp, m_c_scale, m_c_w_out, m_f_w_gate, m_f_w_up, m_f_w_down, v_norm_mix_g, v_norm_ffn_g, v_final_norm_g, v_a_w_in, v_a_v_norm_g, v_a_w_s, v_a_b_s, v_a_w_out, v_b_w_in, v_b_conv_w, v_b_w_out, v_c_w_in, v_c_w_grp, v_c_scale, v_c_w_out, v_f_w_gate, v_f_w_up, v_f_w_down):
    given = dict(x=x, norm_mix_g=norm_mix_g, norm_ffn_g=norm_ffn_g, final_norm_g=final_norm_g, a_w_in=a_w_in, a_v_norm_g=a_v_norm_g, a_w_s=a_w_s, a_b_s=a_b_s, a_w_out=a_w_out, b_w_in=b_w_in, b_conv_w=b_conv_w, b_w_out=b_w_out, c_w_in=c_w_in, c_w_grp=c_w_grp, c_scale=c_scale, c_w_out=c_w_out, f_w_gate=f_w_gate, f_w_up=f_w_up, f_w_down=f_w_down, loss_target=loss_target, m_norm_mix_g=m_norm_mix_g, m_norm_ffn_g=m_norm_ffn_g, m_final_norm_g=m_final_norm_g, m_a_w_in=m_a_w_in, m_a_v_norm_g=m_a_v_norm_g, m_a_w_s=m_a_w_s, m_a_b_s=m_a_b_s, m_a_w_out=m_a_w_out, m_b_w_in=m_b_w_in, m_b_conv_w=m_b_conv_w, m_b_w_out=m_b_w_out, m_c_w_in=m_c_w_in, m_c_w_grp=m_c_w_grp, m_c_scale=m_c_scale, m_c_w_out=m_c_w_out, m_f_w_gate=m_f_w_gate, m_f_w_up=m_f_w_up, m_f_w_down=m_f_w_down, v_norm_mix_g=v_norm_mix_g, v_norm_ffn_g=v_norm_ffn_g, v_final_norm_g=v_final_norm_g, v_a_w_in=v_a_w_in, v_a_v_norm_g=v_a_v_norm_g, v_a_w_s=v_a_w_s, v_a_b_s=v_a_b_s, v_a_w_out=v_a_w_out, v_b_w_in=v_b_w_in, v_b_conv_w=v_b_conv_w, v_b_w_out=v_b_w_out, v_c_w_in=v_c_w_in, v_c_w_grp=v_c_w_grp, v_c_scale=v_c_scale, v_c_w_out=v_c_w_out, v_f_w_gate=v_f_w_gate, v_f_w_up=v_f_w_up, v_f_w_down=v_f_w_down)
    weights = {n: given[n] for n in TWIN_WEIGHTS}
    shared = {n: given[n] for n in SHARED_INPUTS}
    per_example = {n: given[n] for n in ['x']}
    grad_fn = _jax.value_and_grad(_loss, argnums=(0, 1))

    def one_microbatch(ex, loss_target):
        ex = dict(ex)
        diff = ex.pop(TWIN_DIFF_INPUT)
        return grad_fn(weights, diff, {**shared, **ex}, loss_target)

    if N_MICROBATCH == 1:
        loss, (grad_w, grad_x) = one_microbatch(per_example, given["loss_target"])
    else:
        def body(carry, xs):
            loss_sum, grad_sum = carry
            l_k, (gw_k, gx_k) = one_microbatch(xs[0], xs[1])
            with _jax.named_scope("update"):
                return (loss_sum + l_k, _jax.tree.map(_jnp.add, grad_sum, gw_k)), gx_k

        init = (_jnp.zeros((), _jnp.float32), _jax.tree.map(_jnp.zeros_like, weights))
        (loss, grad_w), grad_x = _jax.lax.scan(body, init, (per_example, given["loss_target"]))
    with _jax.named_scope("update"):
        delta_w, new_m, new_v = {}, {}, {}
        for n in TWIN_WEIGHTS:
            delta_w[n], new_m[n], new_v[n] = _adamw(weights[n], grad_w[n], given["m_" + n], given["v_" + n])
    return (loss, grad_x, *[grad_w[n] for n in TWIN_WEIGHTS], *[delta_w[n] for n in TWIN_WEIGHTS],
            *[new_m[n] for n in TWIN_WEIGHTS], *[new_v[n] for n in TWIN_WEIGHTS])
```

```python
import jax
import jax.numpy as jnp
from jax import lax
from jax.experimental import pallas as pl
from jax.experimental.pallas import tpu as pltpu

F32 = jnp.float32
BF16 = jnp.bfloat16
D = 1024
FFN_SHARD = 704
GMLP_BLOCK = 128
A_GROUPS = 8
POOL_WINDOWS = (2, 4, 8, 16)
C_GROUP_DIM = 256
N_CHIPS = 4
EPS = 1e-6
ADAM_LR, ADAM_B1, ADAM_B2, ADAM_EPS, ADAM_WD, ADAM_STEP = 0.001, 0.9, 0.999, 1e-08, 0.01, 10
VMEM_LIMIT_BYTES = 56 * 1024 * 1024
MESH = pl.DeviceIdType.MESH
ANY = pl.BlockSpec(memory_space=pl.ANY)
NT_DIMS = (((1,), (1,)), ((), ()))
TN_DIMS = (((0,), (0,)), ((), ()))
INV_SQRT2 = 0.7071067811865476
INV_SQRT_2PI = 0.3989422804014327


def _params(*semantics):
    return pltpu.CompilerParams(dimension_semantics=semantics, vmem_limit_bytes=VMEM_LIMIT_BYTES)


def _dot(a, b):
    return jnp.dot(a, b, preferred_element_type=F32)


def _dot_nt(a, b):
    return lax.dot_general(a, b, NT_DIMS, preferred_element_type=F32)


def _rms(x):
    r = lax.rsqrt(jnp.mean(x * x, axis=-1, keepdims=True) + EPS)
    return x * r, r


def _rms_bwd(x, g, dh):
    xh, r = _rms(x)
    dxh = dh * g
    dx = r * (dxh - xh * jnp.mean(dxh * xh, axis=-1, keepdims=True))
    return dx, jnp.sum(dh * xh, axis=0, keepdims=True)


def _gelu(x):
    return 0.5 * x * (1.0 + lax.erf(x * INV_SQRT2))


def _gelu_grad(x):
    return 0.5 * (1.0 + lax.erf(x * INV_SQRT2)) + x * jnp.exp(-0.5 * x * x) * INV_SQRT_2PI


def _shift_down(v, s, row):
    return jnp.where(row >= s, pltpu.roll(v, s, 0), 0.0)


def _shift_up(v, s, row):
    n = v.shape[0]
    return jnp.where(row < n - s, pltpu.roll(v, n - s, 0), 0.0)


def _row_tile(t, want):
    return want if t % want == 0 else t


def norm_mm(x, g, w, chunks, n_parts, part_width, name):
    t = x.shape[0]
    tm = _row_tile(t, 512)
    n_shards, _, hs = w.shape

    def body(x_ref, g_ref, w_ref, h_ref, p_ref):
        xh, _ = _rms(x_ref[...])
        h = (xh * g_ref[...]).astype(BF16)
        h_ref[...] = h
        for s in range(n_shards):
            res = _dot(h, w_ref[s]).astype(BF16)
            for (cs, wc, width, part, pc) in chunks:
                if cs == s:
                    p_ref[part, :, pc:pc + width] = res[:, wc:wc + width]

    return pl.pallas_call(
        body, name=name, grid=(t // tm,),
        in_specs=[pl.BlockSpec((tm, D), lambda i: (i, 0)), pl.BlockSpec((1, D), lambda i: (0, 0)),
                  pl.BlockSpec((n_shards, D, hs), lambda i: (0, 0, 0))],
        out_specs=[pl.BlockSpec((tm, D), lambda i: (i, 0)), pl.BlockSpec((n_parts, tm, part_width), lambda i: (0, i, 0))],
        out_shape=[jax.ShapeDtypeStruct((t, D), BF16), jax.ShapeDtypeStruct((n_parts, t, part_width), BF16)],
        compiler_params=_params("arbitrary"),
    )(x, g, w)


def mm_res(a, w, res, name):
    t, k = a.shape
    n = w.shape[1]
    tm = _row_tile(t, 512)

    def body(a_ref, w_ref, r_ref, o_ref):
        o_ref[...] = r_ref[...] + _dot(a_ref[...], w_ref[...])

    return pl.pallas_call(
        body, name=name, grid=(t // tm,),
        in_specs=[pl.BlockSpec((tm, k), lambda i: (i, 0)), pl.BlockSpec((k, n), lambda i: (0, 0)),
                  pl.BlockSpec((tm, n), lambda i: (i, 0))],
        out_specs=pl.BlockSpec((tm, n), lambda i: (i, 0)),
        out_shape=jax.ShapeDtypeStruct((t, n), F32),
        compiler_params=_params("arbitrary"),
    )(a, w, res)


def mm_nt(a, w, name):
    t, n = a.shape
    k = w.shape[0]
    tm = _row_tile(t, 512)

    def body(a_ref, w_ref, o_ref):
        o_ref[...] = _dot_nt(a_ref[...].astype(BF16), w_ref[...]).astype(BF16)

    return pl.pallas_call(
        body, name=name, grid=(t // tm,),
        in_specs=[pl.BlockSpec((tm, n), lambda i: (i, 0)), pl.BlockSpec((k, n), lambda i: (0, 0))],
        out_specs=pl.BlockSpec((tm, k), lambda i: (i, 0)),
        out_shape=jax.ShapeDtypeStruct((t, k), BF16),
        compiler_params=_params("arbitrary"),
    )(a, w)


def bwd_in(dp, w, chunks, x, g, dres, name):
    n_parts, t, part_width = dp.shape
    n_shards, _, hs = w.shape
    tm = _row_tile(t, 512)

    def body(dp_ref, w_ref, x_ref, g_ref, dres_ref, dx_ref, dg_ref):
        acc = jnp.zeros((tm, D), F32)
        for (cs, wc, width, part, pc) in chunks:
            acc = acc + _dot_nt(dp_ref[part, :, pc:pc + width], w_ref[cs, :, wc:wc + width])
        dx, dg = _rms_bwd(x_ref[...], g_ref[...], acc)
        dx_ref[...] = dres_ref[...] + dx

        @pl.when(pl.program_id(0) == 0)
        def _():
            dg_ref[...] = jnp.zeros_like(dg_ref)

        dg_ref[0:1, :] += dg

    return pl.pallas_call(
        body, name=name, grid=(t // tm,),
        in_specs=[pl.BlockSpec((n_parts, tm, part_width), lambda i: (0, i, 0)),
                  pl.BlockSpec((n_shards, D, hs), lambda i: (0, 0, 0)),
                  pl.BlockSpec((tm, D), lambda i: (i, 0)), pl.BlockSpec((1, D), lambda i: (0, 0)),
                  pl.BlockSpec((tm, D), lambda i: (i, 0))],
        out_specs=[pl.BlockSpec((tm, D), lambda i: (i, 0)), pl.BlockSpec((8, D), lambda i: (0, 0))],
        out_shape=[jax.ShapeDtypeStruct((t, D), F32), jax.ShapeDtypeStruct((8, D), F32)],
        compiler_params=_params("arbitrary"),
    )(dp, w, x, g, dres)


def mm_tn(a, b, out_shape, tm, tn, n_tiles, a_idx, b_idx, o_idx, name):
    t = a.shape[1]
    tk = _row_tile(t, 512)
    n_k = t // tk

    def body(a_ref, b_ref, o_ref, acc_ref):
        kk = pl.program_id(1)

        @pl.when(kk == 0)
        def _():
            acc_ref[...] = jnp.zeros_like(acc_ref)

        acc_ref[...] += lax.dot_general(a_ref[0].astype(BF16), b_ref[0].astype(BF16), TN_DIMS, preferred_element_type=F32)

        @pl.when(kk == n_k - 1)
        def _():
            o_ref[0] = acc_ref[...].astype(BF16)

    return pl.pallas_call(
        body, name=name, grid=(n_tiles, n_k),
        in_specs=[pl.BlockSpec((1, tk, tm), lambda j, kk: (a_idx(j)[0], kk, a_idx(j)[1])),
                  pl.BlockSpec((1, tk, tn), lambda j, kk: (b_idx(j)[0], kk, b_idx(j)[1]))],
        out_specs=pl.BlockSpec((1, tm, tn), lambda j, kk: o_idx(j)),
        out_shape=jax.ShapeDtypeStruct(out_shape, BF16),
        scratch_shapes=[pltpu.VMEM((tm, tn), F32)],
        compiler_params=_params("arbitrary", "arbitrary"),
    )(a, b)


def ffn_fwd(x, g, wg, wu, wd, name):
    t = x.shape[0]
    tm = _row_tile(t, 512)
    hs = wg.shape[2]

    def body(x_ref, g_ref, wg_ref, wu_ref, wd_ref, h_ref, a_ref, b_ref, s_ref, o_ref, acc_ref):
        kk = pl.program_id(1)

        @pl.when(kk == 0)
        def _():
            xh, _ = _rms(x_ref[...])
            h_ref[...] = (xh * g_ref[...]).astype(BF16)
            acc_ref[...] = jnp.zeros_like(acc_ref)

        h = h_ref[...]
        a = _dot(h, wg_ref[0])
        b = _dot(h, wu_ref[0])
        s = (a * jax.nn.sigmoid(a) * b).astype(BF16)
        a_ref[0] = a.astype(BF16)
        b_ref[0] = b.astype(BF16)
        s_ref[0] = s
        acc_ref[...] += _dot(s, wd_ref[0])

        @pl.when(kk == N_CHIPS - 1)
        def _():
            o_ref[...] = x_ref[...] + acc_ref[...]

    act = pl.BlockSpec((1, tm, hs), lambda i, kk: (kk, i, 0))
    act_shape = jax.ShapeDtypeStruct((N_CHIPS, t, hs), BF16)
    return pl.pallas_call(
        body, name=name, grid=(t // tm, N_CHIPS),
        in_specs=[pl.BlockSpec((tm, D), lambda i, kk: (i, 0)), pl.BlockSpec((1, D), lambda i, kk: (0, 0)),
                  pl.BlockSpec((1, D, hs), lambda i, kk: (kk, 0, 0)), pl.BlockSpec((1, D, hs), lambda i, kk: (kk, 0, 0)),
                  pl.BlockSpec((1, hs, D), lambda i, kk: (kk, 0, 0))],
        out_specs=[pl.BlockSpec((tm, D), lambda i, kk: (i, 0)), act, act, act, pl.BlockSpec((tm, D), lambda i, kk: (i, 0))],
        out_shape=[jax.ShapeDtypeStruct((t, D), BF16), act_shape, act_shape, act_shape, jax.ShapeDtypeStruct((t, D), F32)],
        scratch_shapes=[pltpu.VMEM((tm, D), F32)],
        compiler_params=_params("arbitrary", "arbitrary"),
    )(x, g, wg, wu, wd)


def ffn_bwd(dxo, a, b, x, g, wg, wu, wd, name):
    t = x.shape[0]
    tm = _row_tile(t, 512)
    hs = wg.shape[2]

    def body(dxo_ref, a_ref, b_ref, x_ref, g_ref, wg_ref, wu_ref, wd_ref, dx_ref, da_ref, db_ref, dg_ref, acc_ref):
        i, kk = pl.program_id(0), pl.program_id(1)

        @pl.when(kk == 0)
        def _():
            acc_ref[...] = jnp.zeros_like(acc_ref)

        @pl.when((kk == 0) & (i == 0))
        def _():
            dg_ref[...] = jnp.zeros_like(dg_ref)

        ds = _dot_nt(dxo_ref[...].astype(BF16), wd_ref[0])
        av = a_ref[0].astype(F32)
        bv = b_ref[0].astype(F32)
        sig = jax.nn.sigmoid(av)
        da = (ds * bv * (sig * (1.0 + av * (1.0 - sig)))).astype(BF16)
        db = (ds * (av * sig)).astype(BF16)
        da_ref[0] = da
        db_ref[0] = db
        acc_ref[...] += _dot_nt(da, wg_ref[0]) + _dot_nt(db, wu_ref[0])

        @pl.when(kk == N_CHIPS - 1)
        def _():
            dx, dg = _rms_bwd(x_ref[...], g_ref[...], acc_ref[...])
            dx_ref[...] = dxo_ref[...] + dx
            dg_ref[0:1, :] += dg

    act = pl.BlockSpec((1, tm, hs), lambda i, kk: (kk, i, 0))
    act_shape = jax.ShapeDtypeStruct((N_CHIPS, t, hs), BF16)
    row = pl.BlockSpec((tm, D), lambda i, kk: (i, 0))
    return pl.pallas_call(
        body, name=name, grid=(t // tm, N_CHIPS),
        in_specs=[row, act, act, row, pl.BlockSpec((1, D), lambda i, kk: (0, 0)),
                  pl.BlockSpec((1, D, hs), lambda i, kk: (kk, 0, 0)), pl.BlockSpec((1, D, hs), lambda i, kk: (kk, 0, 0)),
                  pl.BlockSpec((1, hs, D), lambda i, kk: (kk, 0, 0))],
        out_specs=[row, act, act, pl.BlockSpec((8, D), lambda i, kk: (0, 0))],
        out_shape=[jax.ShapeDtypeStruct((t, D), F32), act_shape, act_shape, jax.ShapeDtypeStruct((8, D), F32)],
        scratch_shapes=[pltpu.VMEM((tm, D), F32)],
        compiler_params=_params("arbitrary", "arbitrary"),
    )(dxo, a, b, x, g, wg, wu, wd)


def _layer_norm_stats(v):
    mu = jnp.mean(v, axis=-1, keepdims=True)
    vc = v - mu
    rstd = lax.rsqrt(jnp.mean(vc * vc, axis=-1, keepdims=True) + EPS)
    return vc * rstd, rstd


def a_mid_fwd(z, gv, wm, bs, name):
    t = z.shape[0]
    tm = _row_tile(t, 256)

    def body(z_ref, gv_ref, wm_ref, bs_ref, y_ref, vn_ref):
        zz = z_ref[...].astype(F32)
        u = _gelu(zz[:, :D])
        vhat, _ = _layer_norm_stats(_gelu(zz[:, D:]))
        vnb = (vhat * gv_ref[...]).astype(BF16)
        vn_ref[...] = vnb
        for n in range(tm // GMLP_BLOCK):
            rows = slice(n * GMLP_BLOCK, (n + 1) * GMLP_BLOCK)
            for grp in range(A_GROUPS):
                cols = slice(grp * 128, (grp + 1) * 128)
                sv = _dot(wm_ref[grp], vnb[rows, cols]) + bs_ref[grp]
                y_ref[rows, cols] = (u[rows, cols] * sv).astype(BF16)

    small = pl.BlockSpec((A_GROUPS, 128, 128), lambda i: (0, 0, 0))
    return pl.pallas_call(
        body, name=name, grid=(t // tm,),
        in_specs=[pl.BlockSpec((tm, 2 * D), lambda i: (i, 0)), pl.BlockSpec((1, D), lambda i: (0, 0)), small, small],
        out_specs=[pl.BlockSpec((tm, D), lambda i: (i, 0)), pl.BlockSpec((tm, D), lambda i: (i, 0))],
        out_shape=[jax.ShapeDtypeStruct((t, D), BF16), jax.ShapeDtypeStruct((t, D), BF16)],
        compiler_params=_params("arbitrary"),
    )(z, gv, wm, bs)


def a_mid_bwd(dy, z, vn, gv, wm, wmt, bs, name):
    t = z.shape[0]
    tm = _row_tile(t, 256)

    def body(dy_ref, z_ref, vn_ref, gv_ref, wm_ref, wmt_ref, bs_ref, dz_ref, dwm_ref, dbs_ref, dgv_ref, du_ref, dvn_ref):
        @pl.when(pl.program_id(0) == 0)
        def _():
            dwm_ref[...] = jnp.zeros_like(dwm_ref)
            dbs_ref[...] = jnp.zeros_like(dbs_ref)
            dgv_ref[...] = jnp.zeros_like(dgv_ref)

        zz = z_ref[...].astype(F32)
        zu, zv = zz[:, :D], zz[:, D:]
        u = _gelu(zu)
        vhat, rstd = _layer_norm_stats(_gelu(zv))
        dyv = dy_ref[...].astype(F32)
        vnb = vn_ref[...]
        ones = jnp.ones((128, 128), BF16)
        for n in range(tm // GMLP_BLOCK):
            rows = slice(n * GMLP_BLOCK, (n + 1) * GMLP_BLOCK)
            for grp in range(A_GROUPS):
                cols = slice(grp * 128, (grp + 1) * 128)
                blk = vnb[rows, cols]
                sv = _dot(wm_ref[grp], blk) + bs_ref[grp]
                dyb = dyv[rows, cols]
                du_ref[rows, cols] = dyb * sv
                dsv = (dyb * u[rows, cols]).astype(BF16)
                dvn_ref[rows, cols] = _dot(wmt_ref[grp], dsv)
                dwm_ref[grp] += _dot_nt(dsv, blk)
                dbs_ref[grp] += _dot(dsv, ones)
        dvn = dvn_ref[...]
        dgv_ref[0:1, :] += jnp.sum(dvn * vhat, axis=0, keepdims=True)
        dvh = dvn * gv_ref[...]
        dv = rstd * (dvh - jnp.mean(dvh, axis=-1, keepdims=True) - vhat * jnp.mean(dvh * vhat, axis=-1, keepdims=True))
        dz_ref[:, :D] = (du_ref[...] * _gelu_grad(zu)).astype(BF16)
        dz_ref[:, D:] = (dv * _gelu_grad(zv)).astype(BF16)

    small = pl.BlockSpec((A_GROUPS, 128, 128), lambda i: (0, 0, 0))
    row = pl.BlockSpec((tm, D), lambda i: (i, 0))
    small_shape = jax.ShapeDtypeStruct((A_GROUPS, 128, 128), F32)
    return pl.pallas_call(
        body, name=name, grid=(t // tm,),
        in_specs=[row, pl.BlockSpec((tm, 2 * D), lambda i: (i, 0)), row, pl.BlockSpec((1, D), lambda i: (0, 0)), small, small, small],
        out_specs=[pl.BlockSpec((tm, 2 * D), lambda i: (i, 0)), small, small, pl.BlockSpec((8, D), lambda i: (0, 0))],
        out_shape=[jax.ShapeDtypeStruct((t, 2 * D), BF16), small_shape, small_shape, jax.ShapeDtypeStruct((8, D), F32)],
        scratch_shapes=[pltpu.VMEM((tm, D), F32), pltpu.VMEM((tm, D), F32)],
        compiler_params=_params("arbitrary"),
    )(dy, z, vn, gv, wm, wmt, bs)


def _conv_terms(p_ref, row):
    gb = p_ref[0].astype(F32)
    gc = p_ref[1].astype(F32)
    xt = p_ref[2].astype(F32)
    q = gc * xt
    return gb, gc, xt, q, _shift_down(q, 1, row), _shift_down(q, 2, row)


def b_conv_fwd(p3, cw, seq, name):
    t = p3.shape[1]
    cb = 256

    def body(p_ref, cw_ref, y_ref):
        row = lax.broadcasted_iota(jnp.int32, (seq, cb), 0)
        gb, _, _, q, q1, q2 = _conv_terms(p_ref, row)
        y_ref[...] = (gb * (cw_ref[2:3, :] * q + cw_ref[1:2, :] * q1 + cw_ref[0:1, :] * q2)).astype(BF16)

    return pl.pallas_call(
        body, name=name, grid=(t // seq, D // cb),
        in_specs=[pl.BlockSpec((3, seq, cb), lambda e, c: (0, e, c)), pl.BlockSpec((3, cb), lambda e, c: (0, c))],
        out_specs=pl.BlockSpec((seq, cb), lambda e, c: (e, c)),
        out_shape=jax.ShapeDtypeStruct((t, D), BF16),
        compiler_params=_params("arbitrary", "arbitrary"),
    )(p3, cw)


def b_conv_bwd(dy, p3, cw, seq, name):
    t = p3.shape[1]
    cb = 256

    def body(dy_ref, p_ref, cw_ref, dp_ref, dcw_ref):
        @pl.when(pl.program_id(1) == 0)
        def _():
            dcw_ref[...] = jnp.zeros_like(dcw_ref)

        row = lax.broadcasted_iota(jnp.int32, (seq, cb), 0)
        gb, gc, xt, q, q1, q2 = _conv_terms(p_ref, row)
        dyv = dy_ref[...].astype(F32)
        conv = cw_ref[2:3, :] * q + cw_ref[1:2, :] * q1 + cw_ref[0:1, :] * q2
        dyc = dyv * gb
        dq = cw_ref[2:3, :] * dyc + cw_ref[1:2, :] * _shift_up(dyc, 1, row) + cw_ref[0:1, :] * _shift_up(dyc, 2, row)
        dp_ref[0] = (dyv * conv).astype(BF16)
        dp_ref[1] = (dq * xt).astype(BF16)
        dp_ref[2] = (dq * gc).astype(BF16)
        dcw_ref[0:1, :] += jnp.sum(dyc * q2, axis=0, keepdims=True)
        dcw_ref[1:2, :] += jnp.sum(dyc * q1, axis=0, keepdims=True)
        dcw_ref[2:3, :] += jnp.sum(dyc * q, axis=0, keepdims=True)

    return pl.pallas_call(
        body, name=name, grid=(D // cb, t // seq),
        in_specs=[pl.BlockSpec((seq, cb), lambda c, e: (e, c)), pl.BlockSpec((3, seq, cb), lambda c, e: (0, e, c)),
                  pl.BlockSpec((3, cb), lambda c, e: (0, c))],
        out_specs=[pl.BlockSpec((3, seq, cb), lambda c, e: (0, e, c)), pl.BlockSpec((8, cb), lambda c, e: (0, c))],
        out_shape=[jax.ShapeDtypeStruct((3, t, D), BF16), jax.ShapeDtypeStruct((8, D), F32)],
        compiler_params=_params("arbitrary", "arbitrary"),
    )(dy, p3, cw)


def c_pool_fwd(p, seq, name):
    t = p.shape[0]

    def make(grp):
        w = POOL_WINDOWS[grp]

        def body_g(p_ref, d_ref):
            row = lax.broadcasted_iota(jnp.int32, (seq, C_GROUP_DIM), 0)
            pv = p_ref[...].astype(F32)
            acc = pv
            sh = 1
            while sh < w:
                acc = acc + _shift_down(acc, sh, row)
                sh *= 2
            d_ref[...] = (acc / jnp.minimum(row + 1, w).astype(F32) - pv).astype(BF16)

        return body_g

    outs = []
    for grp in range(len(POOL_WINDOWS)):
        outs.append(pl.pallas_call(
            make(grp), name=f"{name}_g{grp}", grid=(t // seq,),
            in_specs=[pl.BlockSpec((seq, C_GROUP_DIM), lambda e, grp=grp: (e, grp))],
            out_specs=pl.BlockSpec((seq, C_GROUP_DIM), lambda e: (e, 0)),
            out_shape=jax.ShapeDtypeStruct((t, C_GROUP_DIM), BF16),
            compiler_params=_params("arbitrary"),
        )(p))
    return outs


def c_pool_bwd(dd, seq, name):
    t = dd[0].shape[0]

    def make(w):
        def body_g(dd_ref, dp_ref):
            row = lax.broadcasted_iota(jnp.int32, (seq, C_GROUP_DIM), 0)
            ddv = dd_ref[...]
            acc = ddv / jnp.minimum(row + 1, w).astype(F32)
            sh = 1
            while sh < w:
                acc = acc + _shift_up(acc, sh, row)
                sh *= 2
            dp_ref[...] = (acc - ddv).astype(BF16)

        return body_g

    outs = []
    for grp, w in enumerate(POOL_WINDOWS):
        outs.append(pl.pallas_call(
            make(w), name=f"{name}_g{grp}", grid=(t // seq,),
            in_specs=[pl.BlockSpec((seq, C_GROUP_DIM), lambda e: (e, 0))],
            out_specs=pl.BlockSpec((seq, C_GROUP_DIM), lambda e: (e, 0)),
            out_shape=jax.ShapeDtypeStruct((t, C_GROUP_DIM), BF16),
            compiler_params=_params("arbitrary"),
        )(dd[grp]))
    return outs


def c_out_fwd(d, wgrp, scale, wo, x, name):
    t = x.shape[0]
    tm = _row_tile(t, 512)
    n_g = len(POOL_WINDOWS)

    def body(d0, d1, d2, d3, wg_ref, sc_ref, wo_ref, x_ref, y_ref, o_ref):
        parts = [_dot(dr[...], wg_ref[grp]) for grp, dr in enumerate((d0, d1, d2, d3))]
        y = (jnp.concatenate(parts, axis=1) * sc_ref[...]).astype(BF16)
        y_ref[...] = y
        o_ref[...] = x_ref[...] + _dot(y, wo_ref[...])

    dspec = pl.BlockSpec((tm, C_GROUP_DIM), lambda i: (i, 0))
    row = pl.BlockSpec((tm, D), lambda i: (i, 0))
    return pl.pallas_call(
        body, name=name, grid=(t // tm,),
        in_specs=[dspec] * n_g + [pl.BlockSpec((n_g, C_GROUP_DIM, C_GROUP_DIM), lambda i: (0, 0, 0)),
                                  pl.BlockSpec((1, D), lambda i: (0, 0)), pl.BlockSpec((D, D), lambda i: (0, 0)), row],
        out_specs=[row, row],
        out_shape=[jax.ShapeDtypeStruct((t, D), BF16), jax.ShapeDtypeStruct((t, D), F32)],
        compiler_params=_params("arbitrary"),
    )(*d, wgrp, scale, wo, x)


def c_out_bwd(dxm, d, wgrp, scale, wo, name):
    t = dxm.shape[0]
    tm = _row_tile(t, 512)
    n_g = len(POOL_WINDOWS)

    def body(dxm_ref, d0, d1, d2, d3, wg_ref, sc_ref, wo_ref, dyp_ref, dd0, dd1, dd2, dd3, dsc_ref):
        @pl.when(pl.program_id(0) == 0)
        def _():
            dsc_ref[...] = jnp.zeros_like(dsc_ref)

        dyo = _dot_nt(dxm_ref[...].astype(BF16), wo_ref[...])
        ypre = jnp.concatenate([_dot(dr[...], wg_ref[grp]) for grp, dr in enumerate((d0, d1, d2, d3))], axis=1)
        dsc_ref[0:1, :] += jnp.sum(dyo * ypre, axis=0, keepdims=True)
        dyp = (dyo * sc_ref[...]).astype(BF16)
        dyp_ref[...] = dyp
        for grp, ddr in enumerate((dd0, dd1, dd2, dd3)):
            ddr[...] = _dot_nt(dyp[:, grp * C_GROUP_DIM:(grp + 1) * C_GROUP_DIM], wg_ref[grp])

    dspec = pl.BlockSpec((tm, C_GROUP_DIM), lambda i: (i, 0))
    row = pl.BlockSpec((tm, D), lambda i: (i, 0))
    dshape = jax.ShapeDtypeStruct((t, C_GROUP_DIM), F32)
    return pl.pallas_call(
        body, name=name, grid=(t // tm,),
        in_specs=[row] + [dspec] * n_g + [pl.BlockSpec((n_g, C_GROUP_DIM, C_GROUP_DIM), lambda i: (0, 0, 0)),
                                          pl.BlockSpec((1, D), lambda i: (0, 0)), pl.BlockSpec((D, D), lambda i: (0, 0))],
        out_specs=[row] + [dspec] * n_g + [pl.BlockSpec((8, D), lambda i: (0, 0))],
        out_shape=[jax.ShapeDtypeStruct((t, D), BF16)] + [dshape] * n_g + [jax.ShapeDtypeStruct((8, D), F32)],
        compiler_params=_params("arbitrary"),
    )(dxm, *d, wgrp, scale, wo)


def loss_head(x, tgt, g, name):
    t = x.shape[0]
    tm = _row_tile(t, 512)

    def body(x_ref, t_ref, g_ref, dx_ref, dg_ref, loss_ref):
        @pl.when(pl.program_id(0) == 0)
        def _():
            dg_ref[...] = jnp.zeros_like(dg_ref)
            loss_ref[...] = jnp.zeros_like(loss_ref)

        xv, gvv = x_ref[...], g_ref[...]
        xh, _ = _rms(xv)
        diff = xh * gvv - t_ref[...]
        loss_ref[...] += 0.5 * jnp.sum(jnp.mean(diff * diff, axis=-1, keepdims=True))
        dx, dg = _rms_bwd(xv, gvv, diff * (1.0 / D))
        dx_ref[...] = dx
        dg_ref[0:1, :] += dg

    row = pl.BlockSpec((tm, D), lambda i: (i, 0))
    return pl.pallas_call(
        body, name=name, grid=(t // tm,),
        in_specs=[row, row, pl.BlockSpec((1, D), lambda i: (0, 0))],
        out_specs=[row, pl.BlockSpec((8, D), lambda i: (0, 0)), pl.BlockSpec((8, 128), lambda i: (0, 0))],
        out_shape=[jax.ShapeDtypeStruct((t, D), F32), jax.ShapeDtypeStruct((8, D), F32), jax.ShapeDtypeStruct((8, 128), F32)],
        compiler_params=_params("arbitrary"),
    )(x, tgt, g)


def adamw(w, g, m, v, name):
    rows, cols = w.shape
    tr = rows
    for cand in (512, 256, 128, 64, 32, 16, 8):
        if rows % cand == 0 and rows > cand:
            tr = cand
            break

    def body(w_ref, g_ref, m_ref, v_ref, d_ref, mo_ref, vo_ref):
        gv = g_ref[...]
        mn = ADAM_B1 * m_ref[...] + (1.0 - ADAM_B1) * gv
        vn = ADAM_B2 * v_ref[...] + (1.0 - ADAM_B2) * (gv * gv)
        m_hat = mn / (1.0 - ADAM_B1 ** ADAM_STEP)
        v_hat = vn / (1.0 - ADAM_B2 ** ADAM_STEP)
        d_ref[...] = -ADAM_LR * (m_hat / (jnp.sqrt(v_hat) + ADAM_EPS) + ADAM_WD * w_ref[...])
        mo_ref[...] = mn
        vo_ref[...] = vn

    spec = pl.BlockSpec((tr, cols), lambda i: (i, 0))
    shape = jax.ShapeDtypeStruct((rows, cols), F32)
    return pl.pallas_call(
        body, name=name, grid=(rows // tr,),
        in_specs=[spec] * 4, out_specs=[spec] * 3, out_shape=[shape] * 3,
        compiler_params=_params("arbitrary"),
    )(w, g, m, v)


def add_halves(gs, ps, core, name):
    n = len(gs)

    def body(core_ref, *refs):
        for i in range(n):
            refs[2 * n + i][...] = (refs[i][...].astype(F32) + refs[n + i][...].astype(F32)).astype(BF16)

    in_specs, out_specs, out_shape = [], [], []
    for gt in gs:
        h, c = gt.shape[1] // 2, gt.shape[2]
        in_specs.append(pl.BlockSpec((1, h, c), lambda b, core_ref: (b, core_ref[0], 0)))
    for gt in gs:
        h, c = gt.shape[1] // 2, gt.shape[2]
        in_specs.append(pl.BlockSpec((1, h, c), lambda b, core_ref: (b, 0, 0)))
        out_specs.append(pl.BlockSpec((1, h, c), lambda b, core_ref: (b, 0, 0)))
        out_shape.append(jax.ShapeDtypeStruct((N_CHIPS, h, c), BF16))
    return pl.pallas_call(
        body, name=name,
        grid_spec=pltpu.PrefetchScalarGridSpec(num_scalar_prefetch=1, grid=(N_CHIPS,), in_specs=in_specs, out_specs=out_specs),
        out_shape=out_shape, compiler_params=_params("arbitrary"),
    )(core, *gs, *ps)


def add_final(hs, qs, chip, name):
    n = len(hs)

    def body(chip_ref, *refs):
        for i in range(n):
            q = refs[n + i]
            refs[2 * n + i][...] = ((refs[i][0].astype(F32) + q[0].astype(F32)) + q[1].astype(F32)) + q[2].astype(F32)

    in_specs, out_specs, out_shape = [], [], []
    for ht in hs:
        h, c = ht.shape[1], ht.shape[2]
        in_specs.append(pl.BlockSpec((1, h, c), lambda i, chip_ref: (chip_ref[0], 0, 0)))
    for ht in hs:
        h, c = ht.shape[1], ht.shape[2]
        in_specs.append(pl.BlockSpec((N_CHIPS - 1, h, c), lambda i, chip_ref: (0, 0, 0)))
        out_specs.append(pl.BlockSpec((h, c), lambda i, chip_ref: (0, 0)))
        out_shape.append(jax.ShapeDtypeStruct((h, c), F32))
    return pl.pallas_call(
        body, name=name,
        grid_spec=pltpu.PrefetchScalarGridSpec(num_scalar_prefetch=1, grid=(1,), in_specs=in_specs, out_specs=out_specs),
        out_shape=out_shape, compiler_params=_params("arbitrary"),
    )(chip, *hs, *qs)


def sum_devices(gathered, name):
    rows = gathered.shape[1]

    def body(g_ref, o_ref):
        acc = g_ref[0]
        for dev in range(1, 8):
            acc = acc + g_ref[dev]
        o_ref[...] = acc

    return pl.pallas_call(
        body, name=name, grid=(rows // 8,),
        in_specs=[pl.BlockSpec((8, 8, D), lambda i: (0, i, 0))],
        out_specs=pl.BlockSpec((8, D), lambda i: (i, 0)),
        out_shape=jax.ShapeDtypeStruct((rows, D), F32),
        compiler_params=_params("arbitrary"),
    )(gathered)


def _mesh_pos():
    return lax.axis_index("x"), lax.axis_index("y"), lax.axis_index("c")


def _other_chips(x, y):
    return [(1 - x, y), (x, 1 - y), (1 - x, 1 - y)]


def all_gather_weights(shards, name):
    n = len(shards)

    def body(*refs):
        ins, outs = refs[:n], refs[n:2 * n]
        send, recv, fsend, frecv, lsem = refs[2 * n:]
        x, y, c = _mesh_pos()
        k = 2 * x + y
        chips = _other_chips(x, y)
        local = [pltpu.make_async_copy(ins[i], outs[i].at[k], lsem.at[i]) for i in range(n)]
        for cp in local:
            cp.start()

        def half(ref, i, rows_half):
            h = shards[i].shape[0] // 2
            return ref.at[pl.ds(pl.multiple_of(rows_half * h, 8), h), :]

        first = []
        for i in range(n):
            for j, (cx, cy) in enumerate(chips):
                first.append(pltpu.make_async_remote_copy(
                    src_ref=half(ins[i], i, c), dst_ref=half(outs[i].at[k], i, c),
                    send_sem=send.at[i, j], recv_sem=recv.at[i, j], device_id=(cx, cy, c), device_id_type=MESH))
        for cp in first:
            cp.start()
        passed = []
        for i in range(n):
            for j, (cx, cy) in enumerate(chips):
                blk = half(outs[i].at[2 * cx + cy], i, c)
                pltpu.make_async_remote_copy(src_ref=blk, dst_ref=blk, send_sem=send.at[i, j], recv_sem=recv.at[i, j],
                                             device_id=(cx, cy, c), device_id_type=MESH).wait_recv()
                fw = pltpu.make_async_remote_copy(src_ref=blk, dst_ref=blk, send_sem=fsend.at[i, j], recv_sem=frecv.at[i, j],
                                                  device_id=(x, y, 1 - c), device_id_type=MESH)
                fw.start()
                passed.append(fw)
        for i in range(n):
            for j, (cx, cy) in enumerate(chips):
                blk = half(outs[i].at[2 * cx + cy], i, 1 - c)
                pltpu.make_async_remote_copy(src_ref=blk, dst_ref=blk, send_sem=fsend.at[i, j], recv_sem=frecv.at[i, j],
                                             device_id=(x, y, 1 - c), device_id_type=MESH).wait_recv()
        for cp in first + passed:
            cp.wait_send()
        for cp in local:
            cp.wait()

    return pl.pallas_call(
        body, name=name,
        in_specs=[ANY] * n, out_specs=[ANY] * n,
        out_shape=[jax.ShapeDtypeStruct((N_CHIPS,) + s.shape, s.dtype) for s in shards],
        scratch_shapes=[pltpu.SemaphoreType.DMA((n, 3))] * 4 + [pltpu.SemaphoreType.DMA((n,))],
    )(*shards)


def all_gather_rows(shard, name):
    def body(in_ref, out_ref, send, recv, lsem):
        x, y, c = _mesh_pos()
        k = 2 * x + y
        chips = _other_chips(x, y)
        local = pltpu.make_async_copy(in_ref, out_ref.at[k], lsem)
        local.start()
        sent = [pltpu.make_async_remote_copy(src_ref=in_ref, dst_ref=out_ref.at[k], send_sem=send.at[j], recv_sem=recv.at[j],
                                             device_id=(cx, cy, c), device_id_type=MESH) for j, (cx, cy) in enumerate(chips)]
        for cp in sent:
            cp.start()
        for j, (cx, cy) in enumerate(chips):
            blk = out_ref.at[2 * cx + cy]
            pltpu.make_async_remote_copy(src_ref=blk, dst_ref=blk, send_sem=send.at[j], recv_sem=recv.at[j],
                                         device_id=(cx, cy, c), device_id_type=MESH).wait_recv()
        for cp in sent:
            cp.wait_send()
        local.wait()

    return pl.pallas_call(
        body, name=name, in_specs=[ANY], out_specs=ANY,
        out_shape=jax.ShapeDtypeStruct((N_CHIPS,) + shard.shape, shard.dtype),
        scratch_shapes=[pltpu.SemaphoreType.DMA((3,)), pltpu.SemaphoreType.DMA((3,)), pltpu.SemaphoreType.DMA],
    )(shard)


def swap_halves(gs, name):
    n = len(gs)

    def body(*refs):
        ins, outs = refs[:n], refs[n:2 * n]
        send, recv = refs[2 * n:]
        x, y, c = _mesh_pos()
        sent = []
        for i in range(n):
            h = gs[i].shape[1] // 2
            src = ins[i].at[:, pl.ds(pl.multiple_of((1 - c) * h, 8), h), :]
            cp = pltpu.make_async_remote_copy(src_ref=src, dst_ref=outs[i], send_sem=send.at[i], recv_sem=recv.at[i],
                                              device_id=(x, y, 1 - c), device_id_type=MESH)
            cp.start()
            sent.append(cp)
        for cp in sent:
            cp.wait()

    return pl.pallas_call(
        body, name=name, in_specs=[ANY] * n, out_specs=[ANY] * n,
        out_shape=[jax.ShapeDtypeStruct((N_CHIPS, g.shape[1] // 2, g.shape[2]), g.dtype) for g in gs],
        scratch_shapes=[pltpu.SemaphoreType.DMA((n,)), pltpu.SemaphoreType.DMA((n,))],
    )(*gs)


def scatter_chips(hs, name):
    n = len(hs)

    def body(*refs):
        ins, outs = refs[:n], refs[n:2 * n]
        send, recv = refs[2 * n:]
        x, y, c = _mesh_pos()
        chips = _other_chips(x, y)
        sent = []
        for i in range(n):
            for j, (cx, cy) in enumerate(chips):
                cp = pltpu.make_async_remote_copy(src_ref=ins[i].at[2 * cx + cy], dst_ref=outs[i].at[j],
                                                  send_sem=send.at[i, j], recv_sem=recv.at[i, j],
                                                  device_id=(cx, cy, c), device_id_type=MESH)
                cp.start()
                sent.append(cp)
        for cp in sent:
            cp.wait()

    return pl.pallas_call(
        body, name=name, in_specs=[ANY] * n, out_specs=[ANY] * n,
        out_shape=[jax.ShapeDtypeStruct((N_CHIPS - 1,) + h.shape[1:], h.dtype) for h in hs],
        scratch_shapes=[pltpu.SemaphoreType.DMA((n, 3)), pltpu.SemaphoreType.DMA((n, 3))],
    )(*hs)


def join_halves(rs, name):
    n = len(rs)

    def body(*refs):
        ins, outs = refs[:n], refs[n:2 * n]
        send, recv, lsem = refs[2 * n:]
        x, y, c = _mesh_pos()
        work = []
        for i in range(n):
            h = rs[i].shape[0]
            mine = outs[i].at[pl.ds(pl.multiple_of(c * h, 8), h), :]
            loc = pltpu.make_async_copy(ins[i], mine, lsem.at[i])
            loc.start()
            cp = pltpu.make_async_remote_copy(src_ref=ins[i], dst_ref=mine, send_sem=send.at[i], recv_sem=recv.at[i],
                                              device_id=(x, y, 1 - c), device_id_type=MESH)
            cp.start()
            work.append((loc, cp))
        for i, (loc, cp) in enumerate(work):
            h = rs[i].shape[0]
            theirs = outs[i].at[pl.ds(pl.multiple_of((1 - c) * h, 8), h), :]
            cp.wait_send()
            pltpu.make_async_remote_copy(src_ref=theirs, dst_ref=theirs, send_sem=send.at[i], recv_sem=recv.at[i],
                                         device_id=(x, y, 1 - c), device_id_type=MESH).wait_recv()
            loc.wait()

    return pl.pallas_call(
        body, name=name, in_specs=[ANY] * n, out_specs=[ANY] * n,
        out_shape=[jax.ShapeDtypeStruct((2 * r.shape[0], r.shape[1]), r.dtype) for r in rs],
        scratch_shapes=[pltpu.SemaphoreType.DMA((n,)), pltpu.SemaphoreType.DMA((n,)), pltpu.SemaphoreType.DMA((n,))],
    )(*rs)


def all_gather_devices(part, name):
    def body(in_ref, out_ref, send, recv, lsem):
        x, y, c = _mesh_pos()
        me = 4 * x + 2 * y + c
        local = pltpu.make_async_copy(in_ref, out_ref.at[me], lsem)
        local.start()
        sent = []
        for rel in range(1, 8):
            fx, fy, fc = (rel >> 2) & 1, (rel >> 1) & 1, rel & 1
            peer = (1 - x if fx else x, 1 - y if fy else y, 1 - c if fc else c)
            cp = pltpu.make_async_remote_copy(src_ref=in_ref, dst_ref=out_ref.at[me], send_sem=send.at[rel - 1],
                                              recv_sem=recv.at[rel - 1], device_id=peer, device_id_type=MESH)
            cp.start()
            sent.append(cp)
        for rel in range(1, 8):
            fx, fy, fc = (rel >> 2) & 1, (rel >> 1) & 1, rel & 1
            peer = (1 - x if fx else x, 1 - y if fy else y, 1 - c if fc else c)
            blk = out_ref.at[4 * peer[0] + 2 * peer[1] + peer[2]]
            pltpu.make_async_remote_copy(src_ref=blk, dst_ref=blk, send_sem=send.at[rel - 1], recv_sem=recv.at[rel - 1],
                                         device_id=peer, device_id_type=MESH).wait_recv()
        for cp in sent:
            cp.wait_send()
        local.wait()

    return pl.pallas_call(
        body, name=name, in_specs=[ANY], out_specs=ANY,
        out_shape=jax.ShapeDtypeStruct((8,) + part.shape, part.dtype),
        scratch_shapes=[pltpu.SemaphoreType.DMA((7,)), pltpu.SemaphoreType.DMA((7,)), pltpu.SemaphoreType.DMA],
    )(part)


def reduce_scatter(grads, core, chip, name):
    from_sibling = swap_halves(grads, name + "_swap")
    core_sums = add_halves(grads, from_sibling, core, name + "_add2")
    from_chips = scatter_chips(core_sums, name + "_scatter")
    half_sums = add_final(core_sums, from_chips, chip, name + "_add4")
    return join_halves(half_sums, name + "_join")


def _blocked(w):
    return w.reshape(w.shape[0] * w.shape[1], w.shape[2])


def _grp_from_blocks(w):
    return w.reshape(N_CHIPS, 4, 64, C_GROUP_DIM).transpose(1, 0, 2, 3).reshape(4, C_GROUP_DIM, C_GROUP_DIM)


def _grp_to_blocks(w):
    return w.reshape(4, N_CHIPS, 64, C_GROUP_DIM).transpose(1, 0, 2, 3).reshape(N_CHIPS, C_GROUP_DIM, C_GROUP_DIM)


def _dw_cols(h, dact, hs, name):
    tm = 512
    return mm_tn(h[None], dact, (N_CHIPS, D, hs), tm, hs, N_CHIPS * (D // tm),
                 lambda j: (0, j % 2), lambda j: (0, j // 2), lambda j: (j // 2, j % 2, 0), name)


def _dw_rows(y, dxm, name):
    tm = 512
    out = mm_tn(y[None], dxm[None], (1, D, D), tm, D, D // tm, lambda j: (0, j), lambda j: (0, 0), lambda j: (0, j, 0), name)
    return out.reshape(N_CHIPS, D // N_CHIPS, D)


def kernel(x, norm_mix_g, norm_ffn_g, final_norm_g, a_w_in, a_v_norm_g, a_w_s, a_b_s, a_w_out, b_w_in, b_conv_w, b_w_out, c_w_in, c_w_grp, c_scale, c_w_out, f_w_gate, f_w_up, f_w_down, loss_target, m_norm_mix_g, m_norm_ffn_g, m_final_norm_g, m_a_w_in, m_a_v_norm_g, m_a_w_s, m_a_b_s, m_a_w_out, m_b_w_in, m_b_conv_w, m_b_w_out, m_c_w_in, m_c_w_grp, m_c_scale, m_c_w_out, m_f_w_gate, m_f_w_up, m_f_w_down, v_norm_mix_g, v_norm_ffn_g, v_final_norm_g, v_a_w_in, v_a_v_norm_g, v_a_w_s, v_a_b_s, v_a_w_out, v_b_w_in, v_b_conv_w, v_b_w_out, v_c_w_in, v_c_w_grp, v_c_scale, v_c_w_out, v_f_w_gate, v_f_w_up, v_f_w_down):
    n_ex, seq, _ = x.shape
    t = n_ex * seq
    xi, yi, ci = lax.axis_index("x"), lax.axis_index("y"), lax.axis_index("c")
    chip = (2 * xi + yi).astype(jnp.int32)
    core_arr = ci.astype(jnp.int32).reshape(1)
    chip_arr = chip.reshape(1)
    bf = lambda w: w.astype(BF16)

    pad8 = lambda v: jnp.pad(v, ((0, 8 - v.shape[0]), (0, 0)))
    small_rows = jnp.concatenate([pad8(a_v_norm_g), pad8(b_conv_w[0]), pad8(c_scale)], axis=0)
    small_full = all_gather_rows(small_rows, "ag_small").transpose(1, 0, 2).reshape(24, D)
    gv_full = [small_full[0:1], small_full[1:2]]
    cw_full = small_full[8:11]
    scale_full = small_full[16:17]

    mixer_shards = [
        [bf(a_w_in[0]), bf(a_w_out[0])],
        [bf(b_w_in[0]), bf(b_w_out[0])],
        [bf(c_w_in[0]), bf(c_w_grp[0]).reshape(C_GROUP_DIM, C_GROUP_DIM), bf(c_w_out[0])],
        [bf(a_w_in[1]), bf(a_w_out[1])],
    ]
    gathered = []
    for i in range(4):
        gathered.append(all_gather_weights(mixer_shards[i] + [bf(f_w_gate[i]), bf(f_w_up[i]), bf(f_w_down[i])], f"ag_l{i}"))

    mask = (jnp.arange(GMLP_BLOCK)[None, :] // 64) <= (jnp.arange(GMLP_BLOCK)[:, None] // 64)
    gmix = [norm_mix_g[i:i + 1] for i in range(4)]
    gffn = [norm_ffn_g[i:i + 1] for i in range(4)]
    a_chunks = [(s, 0, 512, 0, s * 512) for s in range(N_CHIPS)]
    b_chunks = [(j // 3, (j % 3) * 256, 256, j // 4, (j % 4) * 256) for j in range(12)]
    c_chunks = [(0, 0, D, 0, 0)]

    xs = [x.reshape(t, D)]
    saved = []
    for i in range(4):
        ws = gathered[i]
        wg, wu, wd = ws[-3], ws[-2], ws[-1]
        xin = xs[-1]
        if i in (0, 3):
            j = 0 if i == 0 else 1
            win, wout = ws[0], _blocked(ws[1])
            wm32 = jnp.where(mask[None], a_w_s[j], 0.0)
            wm, wmt = bf(wm32), bf(wm32.transpose(0, 2, 1))
            bs = jnp.broadcast_to(a_b_s[j][:, :, None], (A_GROUPS, GMLP_BLOCK, 128))
            h, z = norm_mm(xin, gmix[i], win, a_chunks, 1, 2 * D, f"a_in_l{i}")
            y, vn = a_mid_fwd(z[0], gv_full[j], wm, bs, f"a_mid_l{i}")
            xmid = mm_res(y, wout, xin, f"a_out_l{i}")
            saved.append(dict(h=h, z=z, y=y, vn=vn, win=win, wout=wout, wm=wm, wmt=wmt, bs=bs, gv=gv_full[j]))
        elif i == 1:
            win, wout = ws[0], _blocked(ws[1])
            h, p3 = norm_mm(xin, gmix[i], win, b_chunks, 3, D, "b_in")
            y = b_conv_fwd(p3, cw_full, seq, "b_conv")
            xmid = mm_res(y, wout, xin, "b_out")
            saved.append(dict(h=h, p3=p3, y=y, win=win, wout=wout))
        else:
            win, wgrp, wout = _blocked(ws[0])[None], _grp_from_blocks(ws[1]), _blocked(ws[2])
            h, p = norm_mm(xin, gmix[i], win, c_chunks, 1, D, "c_in")
            dpool = c_pool_fwd(p[0], seq, "c_pool")
            y, xmid = c_out_fwd(dpool, wgrp, scale_full, wout, xin, "c_out")
            saved.append(dict(h=h, d=dpool, y=y, win=win, wgrp=wgrp, wout=wout))
        h2, fa, fb, fs, xout = ffn_fwd(xmid, gffn[i], wg, wu, wd, f"ffn_l{i}")
        saved[-1].update(h2=h2, fa=fa, fb=fb, fs=fs, xmid=xmid, wg=wg, wu=wu, wd=wd)
        xs.append(xout)

    dx, dg_final, loss_part = loss_head(xs[4], loss_target.reshape(t, D), final_norm_g[None], "loss_head")
    loss = lax.psum(loss_part[0, 0], ("x", "y", "c"))

    dg_mix, dg_ffn = [None] * 4, [None] * 4
    reduced = [None] * 4
    small = {}
    for i in (3, 2, 1, 0):
        sv = saved[i]
        xin = xs[i]
        dxm, da, db, dg_ffn[i] = ffn_bwd(dx, sv["fa"], sv["fb"], sv["xmid"], gffn[i], sv["wg"], sv["wu"], sv["wd"], f"ffn_bwd_l{i}")
        g_gate = mm_tn(sv["h2"][None], da, (N_CHIPS, D, FFN_SHARD), 512, FFN_SHARD, 8,
                       lambda j: (0, j % 2), lambda j: (j // 2, 0), lambda j: (j // 2, j % 2, 0), f"dw_gate_l{i}")
        g_up = mm_tn(sv["h2"][None], db, (N_CHIPS, D, FFN_SHARD), 512, FFN_SHARD, 8,
                     lambda j: (0, j % 2), lambda j: (j // 2, 0), lambda j: (j // 2, j % 2, 0), f"dw_up_l{i}")
        g_down = mm_tn(sv["fs"], dx[None], (N_CHIPS, FFN_SHARD, D), FFN_SHARD, 512, 8,
                       lambda j: (j // 2, 0), lambda j: (0, j % 2), lambda j: (j // 2, 0, j % 2), f"dw_down_l{i}")
        if i in (0, 3):
            j = 0 if i == 0 else 1
            dy = mm_nt(dxm, sv["wout"], f"a_dy_l{i}")
            dz, dwm, dbs, dgv = a_mid_bwd(dy, sv["z"][0], sv["vn"], sv["gv"], sv["wm"], sv["wmt"], sv["bs"], f"a_mid_bwd_l{i}")
            dz = dz[None]
            dx, dg_mix[i] = bwd_in(dz, sv["win"], a_chunks, xin, gmix[i], dxm, f"a_bwd_in_l{i}")
            g_in = _dw_cols(sv["h"], dz, 512, f"dw_a_in_l{i}")
            g_out = _dw_rows(sv["y"], dxm, f"dw_a_out_l{i}")
            small[f"wm{j}"], small[f"bs{j}"], small[f"gv{j}"] = dwm, dbs, dgv
            mixer_grads = [g_in, g_out]
        elif i == 1:
            dy = mm_nt(dxm, sv["wout"], "b_dy")
            dp3, small["cw"] = b_conv_bwd(dy, sv["p3"], cw_full, seq, "b_conv_bwd")
            dx, dg_mix[i] = bwd_in(dp3, sv["win"], b_chunks, xin, gmix[i], dxm, "b_bwd_in")
            g_in = mm_tn(sv["h"][None], dp3, (N_CHIPS, D, 768), 512, 256, 24,
                         lambda j: (0, j % 2), lambda j: ((j // 2) // 4, (j // 2) % 4),
                         lambda j: ((j // 2) // 3, j % 2, (j // 2) % 3), "dw_b_in")
            g_out = _dw_rows(sv["y"], dxm, "dw_b_out")
            mixer_grads = [g_in, g_out]
        else:
            outs = c_out_bwd(dxm, sv["d"], sv["wgrp"], scale_full, sv["wout"], "c_out_bwd")
            dyp, dd, small["scale"] = outs[0], list(outs[1:5]), outs[5]
            dpool = c_pool_bwd(dd, seq, "c_pool_bwd")
            dp = jnp.concatenate(dpool, axis=1)[None]
            dx, dg_mix[i] = bwd_in(dp, sv["win"], c_chunks, xin, gmix[i], dxm, "c_bwd_in")
            g_in = _dw_rows(sv["h"], dp[0], "dw_c_in")
            dcat = jnp.concatenate(sv["d"], axis=1)
            g_grp = mm_tn(dcat[None], dyp[None], (4, C_GROUP_DIM, C_GROUP_DIM), C_GROUP_DIM, C_GROUP_DIM, 4,
                          lambda j: (0, j), lambda j: (0, j), lambda j: (j, 0, 0), "dw_c_grp")
            g_out = _dw_rows(sv["y"], dxm, "dw_c_out")
            mixer_grads = [g_in, _grp_to_blocks(g_grp), g_out]
        reduced[i] = reduce_scatter(mixer_grads + [g_gate, g_up, g_down], core_arr, chip_arr, f"rs_l{i}")
    grad_x = dx.reshape(n_ex, seq, D)

    def bs_rows(v):
        return jnp.pad(v[:, :, 0].reshape(1, D), ((0, 7), (0, 0)))

    parts = dg_mix + dg_ffn + [dg_final, small["gv0"], small["gv1"], small["cw"], small["scale"],
                               small["wm0"].reshape(128, D), small["wm1"].reshape(128, D), bs_rows(small["bs0"]), bs_rows(small["bs1"])]
    packed = jnp.concatenate(parts, axis=0)
    total = sum_devices(all_gather_devices(packed, "ag_small_grads"), "sum_small_grads")
    first_row = lambda lo, n: total[lo:lo + 8 * n].reshape(n, 8, D)[:, 0]
    g_norm_mix = first_row(0, 4)
    g_norm_ffn = first_row(32, 4)
    g_final = total[64]
    g_gv = first_row(72, 2)
    g_cw = total[88:91]
    g_scale = total[96:97]
    g_ws = jnp.where(mask[None, None], total[104:360].reshape(2, A_GROUPS, 128, 128), 0.0)
    g_bs = first_row(360, 2).reshape(2, A_GROUPS, 128)
    col0 = chip * (D // N_CHIPS)
    cols = lambda v: lax.dynamic_slice_in_dim(v, col0, D // N_CHIPS, axis=1)

    grads = {
        "norm_mix_g": g_norm_mix, "norm_ffn_g": g_norm_ffn, "final_norm_g": g_final,
        "a_w_in": jnp.stack([reduced[0][0], reduced[3][0]]), "a_v_norm_g": cols(g_gv), "a_w_s": g_ws, "a_b_s": g_bs,
        "a_w_out": jnp.stack([reduced[0][1], reduced[3][1]]),
        "b_w_in": reduced[1][0][None], "b_conv_w": cols(g_cw)[None], "b_w_out": reduced[1][1][None],
        "c_w_in": reduced[2][0][None], "c_w_grp": reduced[2][1].reshape(1, 4, 64, C_GROUP_DIM), "c_scale": cols(g_scale),
        "c_w_out": reduced[2][2][None],
        "f_w_gate": jnp.stack([reduced[i][-3] for i in range(4)]), "f_w_up": jnp.stack([reduced[i][-2] for i in range(4)]),
        "f_w_down": jnp.stack([reduced[i][-1] for i in range(4)]),
    }
    weights = dict(norm_mix_g=norm_mix_g, norm_ffn_g=norm_ffn_g, final_norm_g=final_norm_g, a_w_in=a_w_in, a_v_norm_g=a_v_norm_g,
                   a_w_s=a_w_s, a_b_s=a_b_s, a_w_out=a_w_out, b_w_in=b_w_in, b_conv_w=b_conv_w, b_w_out=b_w_out, c_w_in=c_w_in,
                   c_w_grp=c_w_grp, c_scale=c_scale, c_w_out=c_w_out, f_w_gate=f_w_gate, f_w_up=f_w_up, f_w_down=f_w_down)
    m_in = dict(norm_mix_g=m_norm_mix_g, norm_ffn_g=m_norm_ffn_g, final_norm_g=m_final_norm_g, a_w_in=m_a_w_in, a_v_norm_g=m_a_v_norm_g,
                a_w_s=m_a_w_s, a_b_s=m_a_b_s, a_w_out=m_a_w_out, b_w_in=m_b_w_in, b_conv_w=m_b_conv_w, b_w_out=m_b_w_out, c_w_in=m_c_w_in,
                c_w_grp=m_c_w_grp, c_scale=m_c_scale, c_w_out=m_c_w_out, f_w_gate=m_f_w_gate, f_w_up=m_f_w_up, f_w_down=m_f_w_down)
    v_in = dict(norm_mix_g=v_norm_mix_g, norm_ffn_g=v_norm_ffn_g, final_norm_g=v_final_norm_g, a_w_in=v_a_w_in, a_v_norm_g=v_a_v_norm_g,
                a_w_s=v_a_w_s, a_b_s=v_a_b_s, a_w_out=v_a_w_out, b_w_in=v_b_w_in, b_conv_w=v_b_conv_w, b_w_out=v_b_w_out, c_w_in=v_c_w_in,
                c_w_grp=v_c_w_grp, c_scale=v_c_scale, c_w_out=v_c_w_out, f_w_gate=v_f_w_gate, f_w_up=v_f_w_up, f_w_down=v_f_w_down)

    names = list(weights)
    deltas, new_m, new_v = {}, {}, {}
    for nme in names:
        w = weights[nme]
        cols_n = w.shape[-1]
        flat = lambda a: a.reshape(-1, cols_n)
        dl, mn, vn = adamw(flat(w), flat(grads[nme]), flat(m_in[nme]), flat(v_in[nme]), f"adamw_{nme}")
        deltas[nme], new_m[nme], new_v[nme] = dl.reshape(w.shape), mn.reshape(w.shape), vn.reshape(w.shape)
        grads[nme] = grads[nme].reshape(w.shape)

    return (loss, grad_x, *[grads[n] for n in names], *[deltas[n] for n in names],
            *[new_m[n] for n in names], *[new_v[n] for n in names])
```

```python
import jax
import jax.numpy as jnp
from jax import lax
from jax.experimental import pallas as pl
from jax.experimental.pallas import tpu as pltpu
from jax.experimental.pallas import tpu_sc as plsc

F32 = jnp.float32
BF16 = jnp.bfloat16
D = 1024
FFN_SHARD = 704
GMLP_BLOCK = 128
A_GROUPS = 8
POOL_WINDOWS = (2, 4, 8, 16)
C_GROUP_DIM = 256
N_CHIPS = 4
EPS = 1e-6
ADAM_LR, ADAM_B1, ADAM_B2, ADAM_EPS, ADAM_WD, ADAM_STEP = 0.001, 0.9, 0.999, 1e-08, 0.01, 10
VMEM_LIMIT_BYTES = 56 * 1024 * 1024
MESH = pl.DeviceIdType.MESH
GATHER_COLLECTIVE_ID = 1
SIBLING_COLLECTIVE_ID = 2
CHIPS_COLLECTIVE_ID = 3
ALL_COLLECTIVE_ID = 4
ANY = pl.BlockSpec(memory_space=pl.ANY)
NT_DIMS = (((1,), (1,)), ((), ()))
TN_DIMS = (((0,), (0,)), ((), ()))
INV_SQRT2 = 0.7071067811865476
INV_SQRT_2PI = 0.3989422804014327


def _params(*semantics):
    return pltpu.CompilerParams(dimension_semantics=semantics, vmem_limit_bytes=VMEM_LIMIT_BYTES)


def _dot(a, b):
    return jnp.dot(a, b, preferred_element_type=F32)


def _dot_nt(a, b):
    return lax.dot_general(a, b, NT_DIMS, preferred_element_type=F32)


def _rms(x):
    r = lax.rsqrt(jnp.mean(x * x, axis=-1, keepdims=True) + EPS)
    return x * r, r


def _rms_bwd(x, g, dh):
    xh, r = _rms(x)
    dxh = dh * g
    dx = r * (dxh - xh * jnp.mean(dxh * xh, axis=-1, keepdims=True))
    return dx, jnp.sum(dh * xh, axis=0, keepdims=True)


def _gelu(x):
    return 0.5 * x * (1.0 + lax.erf(x * INV_SQRT2))


def _gelu_grad(x):
    return 0.5 * (1.0 + lax.erf(x * INV_SQRT2)) + x * jnp.exp(-0.5 * x * x) * INV_SQRT_2PI


def _shift_down(v, s, row):
    return jnp.where(row >= s, pltpu.roll(v, s, 0), 0.0)


def _shift_up(v, s, row):
    n = v.shape[0]
    return jnp.where(row < n - s, pltpu.roll(v, n - s, 0), 0.0)


def _row_tile(t, want):
    return want if t % want == 0 else t


def _after(body, first, deps):
    if not deps:
        return body

    def ordered(*refs):
        return body(*refs[:first], *refs[first + len(deps):])

    return ordered


def norm_mm(x, g, w, chunks, n_parts, part_width, name):
    t = x.shape[0]
    tm = _row_tile(t, 512)
    n_shards, _, hs = w.shape

    def body(x_ref, g_ref, w_ref, h_ref, p_ref):
        xh, _ = _rms(x_ref[...])
        h = (xh * g_ref[...]).astype(BF16)
        h_ref[...] = h
        for s in range(n_shards):
            res = _dot(h, w_ref[s]).astype(BF16)
            for (cs, wc, width, part, pc) in chunks:
                if cs == s:
                    p_ref[part, :, pc:pc + width] = res[:, wc:wc + width]

    return pl.pallas_call(
        body, name=name, grid=(t // tm,),
        in_specs=[pl.BlockSpec((tm, D), lambda i: (i, 0)), pl.BlockSpec((1, D), lambda i: (0, 0)),
                  pl.BlockSpec((n_shards, D, hs), lambda i: (0, 0, 0))],
        out_specs=[pl.BlockSpec((tm, D), lambda i: (i, 0)), pl.BlockSpec((n_parts, tm, part_width), lambda i: (0, i, 0))],
        out_shape=[jax.ShapeDtypeStruct((t, D), BF16), jax.ShapeDtypeStruct((n_parts, t, part_width), BF16)],
        compiler_params=_params("arbitrary"),
    )(x, g, w)


def mm_res(a, w, res, name):
    t, k = a.shape
    n = w.shape[1]
    tm = _row_tile(t, 512)

    def body(a_ref, w_ref, r_ref, o_ref):
        o_ref[...] = r_ref[...] + _dot(a_ref[...], w_ref[...])

    return pl.pallas_call(
        body, name=name, grid=(t // tm,),
        in_specs=[pl.BlockSpec((tm, k), lambda i: (i, 0)), pl.BlockSpec((k, n), lambda i: (0, 0)),
                  pl.BlockSpec((tm, n), lambda i: (i, 0))],
        out_specs=pl.BlockSpec((tm, n), lambda i: (i, 0)),
        out_shape=jax.ShapeDtypeStruct((t, n), F32),
        compiler_params=_params("arbitrary"),
    )(a, w, res)


def mm_nt(a, w, name):
    t, n = a.shape
    k = w.shape[0]
    tm = _row_tile(t, 512)

    def body(a_ref, w_ref, o_ref):
        o_ref[...] = _dot_nt(a_ref[...].astype(BF16), w_ref[...]).astype(BF16)

    return pl.pallas_call(
        body, name=name, grid=(t // tm,),
        in_specs=[pl.BlockSpec((tm, n), lambda i: (i, 0)), pl.BlockSpec((k, n), lambda i: (0, 0))],
        out_specs=pl.BlockSpec((tm, k), lambda i: (i, 0)),
        out_shape=jax.ShapeDtypeStruct((t, k), BF16),
        compiler_params=_params("arbitrary"),
    )(a, w)


def bwd_in(dp, w, chunks, x, g, dres, name):
    n_parts, t, part_width = dp.shape
    n_shards, _, hs = w.shape
    tm = _row_tile(t, 512)

    def body(dp_ref, w_ref, x_ref, g_ref, dres_ref, dx_ref, dg_ref):
        acc = jnp.zeros((tm, D), F32)
        for (cs, wc, width, part, pc) in chunks:
            acc = acc + _dot_nt(dp_ref[part, :, pc:pc + width], w_ref[cs, :, wc:wc + width])
        dx, dg = _rms_bwd(x_ref[...], g_ref[...], acc)
        dx_ref[...] = dres_ref[...] + dx

        @pl.when(pl.program_id(0) == 0)
        def _():
            dg_ref[...] = jnp.zeros_like(dg_ref)

        dg_ref[0:1, :] += dg

    return pl.pallas_call(
        body, name=name, grid=(t // tm,),
        in_specs=[pl.BlockSpec((n_parts, tm, part_width), lambda i: (0, i, 0)),
                  pl.BlockSpec((n_shards, D, hs), lambda i: (0, 0, 0)),
                  pl.BlockSpec((tm, D), lambda i: (i, 0)), pl.BlockSpec((1, D), lambda i: (0, 0)),
                  pl.BlockSpec((tm, D), lambda i: (i, 0))],
        out_specs=[pl.BlockSpec((tm, D), lambda i: (i, 0)), pl.BlockSpec((8, D), lambda i: (0, 0))],
        out_shape=[jax.ShapeDtypeStruct((t, D), F32), jax.ShapeDtypeStruct((8, D), F32)],
        compiler_params=_params("arbitrary"),
    )(dp, w, x, g, dres)


def mm_tn(a, b, out_shape, tm, tn, n_tiles, a_idx, b_idx, o_idx, name, deps=()):
    t = a.shape[1]

    def body(a_ref, b_ref, o_ref):
        o_ref[0] = lax.dot_general(a_ref[0].astype(BF16), b_ref[0].astype(BF16), TN_DIMS, preferred_element_type=F32).astype(BF16)

    return pl.pallas_call(
        _after(body, 2, deps), name=name, grid=(n_tiles,),
        in_specs=[pl.BlockSpec((1, t, tm), lambda j: (a_idx(j)[0], 0, a_idx(j)[1])),
                  pl.BlockSpec((1, t, tn), lambda j: (b_idx(j)[0], 0, b_idx(j)[1]))] + [ANY] * len(deps),
        out_specs=pl.BlockSpec((1, tm, tn), lambda j: o_idx(j)),
        out_shape=jax.ShapeDtypeStruct(out_shape, BF16),
        compiler_params=_params("arbitrary"),
    )(a, b, *deps)


def ffn_fwd(x, g, wg, wu, wd, name):
    t = x.shape[0]
    tm = _row_tile(t, 512)
    hs = wg.shape[1]

    def body(x_ref, g_ref, wg_ref, wu_ref, wd_ref, h_ref, a_ref, b_ref, s_ref, o_ref, acc_ref):
        kk = pl.program_id(1)

        @pl.when(kk == 0)
        def _():
            xh, _ = _rms(x_ref[...])
            h_ref[...] = (xh * g_ref[...]).astype(BF16)
            acc_ref[...] = jnp.zeros_like(acc_ref)

        h = h_ref[...]
        a = _dot_nt(h, wg_ref[0])
        b = _dot_nt(h, wu_ref[0])
        sig = jax.nn.sigmoid(a)
        silu = a * sig
        s = (silu * b).astype(BF16)
        a_ref[0] = (b * (sig * (1.0 + a * (1.0 - sig)))).astype(BF16)
        b_ref[0] = silu.astype(BF16)
        s_ref[0] = s
        acc_ref[...] += _dot(s, wd_ref[0])

        @pl.when(kk == N_CHIPS - 1)
        def _():
            o_ref[...] = x_ref[...] + acc_ref[...]

    act = pl.BlockSpec((1, tm, hs), lambda i, kk: (kk, i, 0))
    act_shape = jax.ShapeDtypeStruct((N_CHIPS, t, hs), BF16)
    wspec = pl.BlockSpec((1, hs, D), lambda i, kk: (kk, 0, 0))
    return pl.pallas_call(
        body, name=name, grid=(t // tm, N_CHIPS),
        in_specs=[pl.BlockSpec((tm, D), lambda i, kk: (i, 0)), pl.BlockSpec((1, D), lambda i, kk: (0, 0)), wspec, wspec, wspec],
        out_specs=[pl.BlockSpec((tm, D), lambda i, kk: (i, 0)), act, act, act, pl.BlockSpec((tm, D), lambda i, kk: (i, 0))],
        out_shape=[jax.ShapeDtypeStruct((t, D), BF16), act_shape, act_shape, act_shape, jax.ShapeDtypeStruct((t, D), F32)],
        scratch_shapes=[pltpu.VMEM((tm, D), F32)],
        compiler_params=_params("arbitrary", "arbitrary"),
    )(x, g, wg, wu, wd)


def ffn_bwd(dxo, a, b, x, g, wg, wu, wd, name, deps=()):
    t = x.shape[0]
    tm = _row_tile(t, 512)
    hs = wg.shape[1]

    def body(dxo_ref, a_ref, b_ref, x_ref, g_ref, wg_ref, wu_ref, wd_ref, dx_ref, da_ref, db_ref, dg_ref, acc_ref):
        i, kk = pl.program_id(0), pl.program_id(1)

        @pl.when(kk == 0)
        def _():
            acc_ref[...] = jnp.zeros_like(acc_ref)

        @pl.when((kk == 0) & (i == 0))
        def _():
            dg_ref[...] = jnp.zeros_like(dg_ref)

        ds = _dot_nt(dxo_ref[...].astype(BF16), wd_ref[0])
        da = (ds * a_ref[0].astype(F32)).astype(BF16)
        db = (ds * b_ref[0].astype(F32)).astype(BF16)
        da_ref[0] = da
        db_ref[0] = db
        acc_ref[...] += _dot(da, wg_ref[0]) + _dot(db, wu_ref[0])

        @pl.when(kk == N_CHIPS - 1)
        def _():
            dx, dg = _rms_bwd(x_ref[...], g_ref[...], acc_ref[...])
            dx_ref[...] = dxo_ref[...] + dx
            dg_ref[0:1, :] += dg

    act = pl.BlockSpec((1, tm, hs), lambda i, kk: (kk, i, 0))
    act_shape = jax.ShapeDtypeStruct((N_CHIPS, t, hs), BF16)
    row = pl.BlockSpec((tm, D), lambda i, kk: (i, 0))
    wspec = pl.BlockSpec((1, hs, D), lambda i, kk: (kk, 0, 0))
    return pl.pallas_call(
        _after(body, 8, deps), name=name, grid=(t // tm, N_CHIPS),
        in_specs=[row, act, act, row, pl.BlockSpec((1, D), lambda i, kk: (0, 0)), wspec, wspec, wspec] + [ANY] * len(deps),
        out_specs=[row, act, act, pl.BlockSpec((8, D), lambda i, kk: (0, 0))],
        out_shape=[jax.ShapeDtypeStruct((t, D), F32), act_shape, act_shape, jax.ShapeDtypeStruct((8, D), F32)],
        scratch_shapes=[pltpu.VMEM((tm, D), F32)],
        compiler_params=_params("arbitrary", "arbitrary"),
    )(dxo, a, b, x, g, wg, wu, wd, *deps)


def _layer_norm_stats(v):
    mu = jnp.mean(v, axis=-1, keepdims=True)
    vc = v - mu
    rstd = lax.rsqrt(jnp.mean(vc * vc, axis=-1, keepdims=True) + EPS)
    return vc * rstd, rstd


def a_mid_fwd(z, gv, wm, bs, name):
    t = z.shape[0]
    tm = _row_tile(t, 256)

    def body(z_ref, gv_ref, wm_ref, bs_ref, y_ref, vn_ref):
        zz = z_ref[...].astype(F32)
        u = _gelu(zz[:, :D])
        vhat, _ = _layer_norm_stats(_gelu(zz[:, D:]))
        vnb = (vhat * gv_ref[...]).astype(BF16)
        vn_ref[...] = vnb
        for n in range(tm // GMLP_BLOCK):
            rows = slice(n * GMLP_BLOCK, (n + 1) * GMLP_BLOCK)
            for grp in range(A_GROUPS):
                cols = slice(grp * 128, (grp + 1) * 128)
                sv = _dot(wm_ref[grp], vnb[rows, cols]) + bs_ref[grp]
                y_ref[rows, cols] = (u[rows, cols] * sv).astype(BF16)

    small = pl.BlockSpec((A_GROUPS, 128, 128), lambda i: (0, 0, 0))
    return pl.pallas_call(
        body, name=name, grid=(t // tm,),
        in_specs=[pl.BlockSpec((tm, 2 * D), lambda i: (i, 0)), pl.BlockSpec((1, D), lambda i: (0, 0)), small, small],
        out_specs=[pl.BlockSpec((tm, D), lambda i: (i, 0)), pl.BlockSpec((tm, D), lambda i: (i, 0))],
        out_shape=[jax.ShapeDtypeStruct((t, D), BF16), jax.ShapeDtypeStruct((t, D), BF16)],
        compiler_params=_params("arbitrary"),
    )(z, gv, wm, bs)


def a_mid_bwd(dy, z, vn, gv, wm, wmt, bs, name):
    t = z.shape[0]
    tm = _row_tile(t, 256)

    def body(dy_ref, z_ref, vn_ref, gv_ref, wm_ref, wmt_ref, bs_ref, dz_ref, dwm_ref, dbs_ref, dgv_ref, du_ref, dvn_ref):
        @pl.when(pl.program_id(0) == 0)
        def _():
            dwm_ref[...] = jnp.zeros_like(dwm_ref)
            dbs_ref[...] = jnp.zeros_like(dbs_ref)
            dgv_ref[...] = jnp.zeros_like(dgv_ref)

        zz = z_ref[...].astype(F32)
        zu, zv = zz[:, :D], zz[:, D:]
        u = _gelu(zu)
        vhat, rstd = _layer_norm_stats(_gelu(zv))
        dyv = dy_ref[...].astype(F32)
        vnb = vn_ref[...]
        ones = jnp.ones((128, 128), BF16)
        for n in range(tm // GMLP_BLOCK):
            rows = slice(n * GMLP_BLOCK, (n + 1) * GMLP_BLOCK)
            for grp in range(A_GROUPS):
                cols = slice(grp * 128, (grp + 1) * 128)
                blk = vnb[rows, cols]
                sv = _dot(wm_ref[grp], blk) + bs_ref[grp]
                dyb = dyv[rows, cols]
                du_ref[rows, cols] = dyb * sv
                dsv = (dyb * u[rows, cols]).astype(BF16)
                dvn_ref[rows, cols] = _dot(wmt_ref[grp], dsv)
                dwm_ref[grp] += _dot_nt(dsv, blk)
                dbs_ref[grp] += _dot(dsv, ones)
        dvn = dvn_ref[...]
        dgv_ref[0:1, :] += jnp.sum(dvn * vhat, axis=0, keepdims=True)
        dvh = dvn * gv_ref[...]
        dv = rstd * (dvh - jnp.mean(dvh, axis=-1, keepdims=True) - vhat * jnp.mean(dvh * vhat, axis=-1, keepdims=True))
        dz_ref[:, :D] = (du_ref[...] * _gelu_grad(zu)).astype(BF16)
        dz_ref[:, D:] = (dv * _gelu_grad(zv)).astype(BF16)

    small = pl.BlockSpec((A_GROUPS, 128, 128), lambda i: (0, 0, 0))
    row = pl.BlockSpec((tm, D), lambda i: (i, 0))
    small_shape = jax.ShapeDtypeStruct((A_GROUPS, 128, 128), F32)
    return pl.pallas_call(
        body, name=name, grid=(t // tm,),
        in_specs=[row, pl.BlockSpec((tm, 2 * D), lambda i: (i, 0)), row, pl.BlockSpec((1, D), lambda i: (0, 0)), small, small, small],
        out_specs=[pl.BlockSpec((tm, 2 * D), lambda i: (i, 0)), small, small, pl.BlockSpec((8, D), lambda i: (0, 0))],
        out_shape=[jax.ShapeDtypeStruct((t, 2 * D), BF16), small_shape, small_shape, jax.ShapeDtypeStruct((8, D), F32)],
        scratch_shapes=[pltpu.VMEM((tm, D), F32), pltpu.VMEM((tm, D), F32)],
        compiler_params=_params("arbitrary"),
    )(dy, z, vn, gv, wm, wmt, bs)


def _conv_terms(p_ref, row):
    gb = p_ref[0].astype(F32)
    gc = p_ref[1].astype(F32)
    xt = p_ref[2].astype(F32)
    q = gc * xt
    return gb, gc, xt, q, _shift_down(q, 1, row), _shift_down(q, 2, row)


def b_conv_fwd(p3, cw, seq, name):
    t = p3.shape[1]
    cb = 256

    def body(p_ref, cw_ref, y_ref):
        row = lax.broadcasted_iota(jnp.int32, (seq, cb), 0)
        gb, _, _, q, q1, q2 = _conv_terms(p_ref, row)
        y_ref[...] = (gb * (cw_ref[2:3, :] * q + cw_ref[1:2, :] * q1 + cw_ref[0:1, :] * q2)).astype(BF16)

    return pl.pallas_call(
        body, name=name, grid=(t // seq, D // cb),
        in_specs=[pl.BlockSpec((3, seq, cb), lambda e, c: (0, e, c)), pl.BlockSpec((3, cb), lambda e, c: (0, c))],
        out_specs=pl.BlockSpec((seq, cb), lambda e, c: (e, c)),
        out_shape=jax.ShapeDtypeStruct((t, D), BF16),
        compiler_params=_params("arbitrary", "arbitrary"),
    )(p3, cw)


def b_conv_bwd(dy, p3, cw, seq, name):
    t = p3.shape[1]
    cb = 256

    def body(dy_ref, p_ref, cw_ref, dp_ref, dcw_ref):
        @pl.when(pl.program_id(1) == 0)
        def _():
            dcw_ref[...] = jnp.zeros_like(dcw_ref)

        row = lax.broadcasted_iota(jnp.int32, (seq, cb), 0)
        gb, gc, xt, q, q1, q2 = _conv_terms(p_ref, row)
        dyv = dy_ref[...].astype(F32)
        conv = cw_ref[2:3, :] * q + cw_ref[1:2, :] * q1 + cw_ref[0:1, :] * q2
        dyc = dyv * gb
        dq = cw_ref[2:3, :] * dyc + cw_ref[1:2, :] * _shift_up(dyc, 1, row) + cw_ref[0:1, :] * _shift_up(dyc, 2, row)
        dp_ref[0] = (dyv * conv).astype(BF16)
        dp_ref[1] = (dq * xt).astype(BF16)
        dp_ref[2] = (dq * gc).astype(BF16)
        dcw_ref[0:1, :] += jnp.sum(dyc * q2, axis=0, keepdims=True)
        dcw_ref[1:2, :] += jnp.sum(dyc * q1, axis=0, keepdims=True)
        dcw_ref[2:3, :] += jnp.sum(dyc * q, axis=0, keepdims=True)

    return pl.pallas_call(
        body, name=name, grid=(D // cb, t // seq),
        in_specs=[pl.BlockSpec((seq, cb), lambda c, e: (e, c)), pl.BlockSpec((3, seq, cb), lambda c, e: (0, e, c)),
                  pl.BlockSpec((3, cb), lambda c, e: (0, c))],
        out_specs=[pl.BlockSpec((3, seq, cb), lambda c, e: (0, e, c)), pl.BlockSpec((8, cb), lambda c, e: (0, c))],
        out_shape=[jax.ShapeDtypeStruct((3, t, D), BF16), jax.ShapeDtypeStruct((8, D), F32)],
        compiler_params=_params("arbitrary", "arbitrary"),
    )(dy, p3, cw)


def c_pool_fwd(p, seq, name):
    t = p.shape[0]

    def make(grp):
        w = POOL_WINDOWS[grp]

        def body_g(p_ref, d_ref):
            row = lax.broadcasted_iota(jnp.int32, (seq, C_GROUP_DIM), 0)
            pv = p_ref[...].astype(F32)
            acc = pv
            sh = 1
            while sh < w:
                acc = acc + _shift_down(acc, sh, row)
                sh *= 2
            d_ref[...] = (acc / jnp.minimum(row + 1, w).astype(F32) - pv).astype(BF16)

        return body_g

    outs = []
    for grp in range(len(POOL_WINDOWS)):
        outs.append(pl.pallas_call(
            make(grp), name=f"{name}_g{grp}", grid=(t // seq,),
            in_specs=[pl.BlockSpec((seq, C_GROUP_DIM), lambda e, grp=grp: (e, grp))],
            out_specs=pl.BlockSpec((seq, C_GROUP_DIM), lambda e: (e, 0)),
            out_shape=jax.ShapeDtypeStruct((t, C_GROUP_DIM), BF16),
            compiler_params=_params("arbitrary"),
        )(p))
    return outs


def c_pool_bwd(dd, seq, name):
    t = dd[0].shape[0]

    def make(w):
        def body_g(dd_ref, dp_ref):
            row = lax.broadcasted_iota(jnp.int32, (seq, C_GROUP_DIM), 0)
            ddv = dd_ref[...]
            acc = ddv / jnp.minimum(row + 1, w).astype(F32)
            sh = 1
            while sh < w:
                acc = acc + _shift_up(acc, sh, row)
                sh *= 2
            dp_ref[...] = (acc - ddv).astype(BF16)

        return body_g

    outs = []
    for grp, w in enumerate(POOL_WINDOWS):
        outs.append(pl.pallas_call(
            make(w), name=f"{name}_g{grp}", grid=(t // seq,),
            in_specs=[pl.BlockSpec((seq, C_GROUP_DIM), lambda e: (e, 0))],
            out_specs=pl.BlockSpec((seq, C_GROUP_DIM), lambda e: (e, 0)),
            out_shape=jax.ShapeDtypeStruct((t, C_GROUP_DIM), BF16),
            compiler_params=_params("arbitrary"),
        )(dd[grp]))
    return outs


def c_out_fwd(d, wgrp, scale, wo, x, name):
    t = x.shape[0]
    tm = _row_tile(t, 512)
    n_g = len(POOL_WINDOWS)

    def body(d0, d1, d2, d3, wg_ref, sc_ref, wo_ref, x_ref, y_ref, o_ref):
        parts = [_dot(dr[...], wg_ref[grp]) for grp, dr in enumerate((d0, d1, d2, d3))]
        y = (jnp.concatenate(parts, axis=1) * sc_ref[...]).astype(BF16)
        y_ref[...] = y
        o_ref[...] = x_ref[...] + _dot(y, wo_ref[...])

    dspec = pl.BlockSpec((tm, C_GROUP_DIM), lambda i: (i, 0))
    row = pl.BlockSpec((tm, D), lambda i: (i, 0))
    return pl.pallas_call(
        body, name=name, grid=(t // tm,),
        in_specs=[dspec] * n_g + [pl.BlockSpec((n_g, C_GROUP_DIM, C_GROUP_DIM), lambda i: (0, 0, 0)),
                                  pl.BlockSpec((1, D), lambda i: (0, 0)), pl.BlockSpec((D, D), lambda i: (0, 0)), row],
        out_specs=[row, row],
        out_shape=[jax.ShapeDtypeStruct((t, D), BF16), jax.ShapeDtypeStruct((t, D), F32)],
        compiler_params=_params("arbitrary"),
    )(*d, wgrp, scale, wo, x)


def c_out_bwd(dxm, d, wgrp, scale, wo, name):
    t = dxm.shape[0]
    tm = _row_tile(t, 512)
    n_g = len(POOL_WINDOWS)

    def body(dxm_ref, d0, d1, d2, d3, wg_ref, sc_ref, wo_ref, dyp_ref, dd0, dd1, dd2, dd3, dsc_ref):
        @pl.when(pl.program_id(0) == 0)
        def _():
            dsc_ref[...] = jnp.zeros_like(dsc_ref)

        dyo = _dot_nt(dxm_ref[...].astype(BF16), wo_ref[...])
        ypre = jnp.concatenate([_dot(dr[...], wg_ref[grp]) for grp, dr in enumerate((d0, d1, d2, d3))], axis=1)
        dsc_ref[0:1, :] += jnp.sum(dyo * ypre, axis=0, keepdims=True)
        dyp = (dyo * sc_ref[...]).astype(BF16)
        dyp_ref[...] = dyp
        for grp, ddr in enumerate((dd0, dd1, dd2, dd3)):
            ddr[...] = _dot_nt(dyp[:, grp * C_GROUP_DIM:(grp + 1) * C_GROUP_DIM], wg_ref[grp])

    dspec = pl.BlockSpec((tm, C_GROUP_DIM), lambda i: (i, 0))
    row = pl.BlockSpec((tm, D), lambda i: (i, 0))
    dshape = jax.ShapeDtypeStruct((t, C_GROUP_DIM), F32)
    return pl.pallas_call(
        body, name=name, grid=(t // tm,),
        in_specs=[row] + [dspec] * n_g + [pl.BlockSpec((n_g, C_GROUP_DIM, C_GROUP_DIM), lambda i: (0, 0, 0)),
                                          pl.BlockSpec((1, D), lambda i: (0, 0)), pl.BlockSpec((D, D), lambda i: (0, 0))],
        out_specs=[row] + [dspec] * n_g + [pl.BlockSpec((8, D), lambda i: (0, 0))],
        out_shape=[jax.ShapeDtypeStruct((t, D), BF16)] + [dshape] * n_g + [jax.ShapeDtypeStruct((8, D), F32)],
        compiler_params=_params("arbitrary"),
    )(dxm, *d, wgrp, scale, wo)


def loss_head(x, tgt, g, name):
    t = x.shape[0]
    tm = _row_tile(t, 512)

    def body(x_ref, t_ref, g_ref, dx_ref, dg_ref, loss_ref):
        @pl.when(pl.program_id(0) == 0)
        def _():
            dg_ref[...] = jnp.zeros_like(dg_ref)
            loss_ref[...] = jnp.zeros_like(loss_ref)

        xv, gvv = x_ref[...], g_ref[...]
        xh, _ = _rms(xv)
        diff = xh * gvv - t_ref[...]
        loss_ref[...] += 0.5 * jnp.sum(jnp.mean(diff * diff, axis=-1, keepdims=True))
        dx, dg = _rms_bwd(xv, gvv, diff * (1.0 / D))
        dx_ref[...] = dx
        dg_ref[0:1, :] += dg

    row = pl.BlockSpec((tm, D), lambda i: (i, 0))
    return pl.pallas_call(
        body, name=name, grid=(t // tm,),
        in_specs=[row, row, pl.BlockSpec((1, D), lambda i: (0, 0))],
        out_specs=[row, pl.BlockSpec((8, D), lambda i: (0, 0)), pl.BlockSpec((8, 128), lambda i: (0, 0))],
        out_shape=[jax.ShapeDtypeStruct((t, D), F32), jax.ShapeDtypeStruct((8, D), F32), jax.ShapeDtypeStruct((8, 128), F32)],
        compiler_params=_params("arbitrary"),
    )(x, tgt, g)


def adamw(w, g, m, v, name):
    rows, cols = w.shape
    tr = rows
    for cand in (512, 256, 128, 64, 32, 16, 8):
        if rows % cand == 0 and rows > cand:
            tr = cand
            break

    def body(w_ref, g_ref, m_ref, v_ref, d_ref, mo_ref, vo_ref):
        gv = g_ref[...]
        mn = ADAM_B1 * m_ref[...] + (1.0 - ADAM_B1) * gv
        vn = ADAM_B2 * v_ref[...] + (1.0 - ADAM_B2) * (gv * gv)
        m_hat = mn / (1.0 - ADAM_B1 ** ADAM_STEP)
        v_hat = vn / (1.0 - ADAM_B2 ** ADAM_STEP)
        d_ref[...] = -ADAM_LR * (m_hat / (jnp.sqrt(v_hat) + ADAM_EPS) + ADAM_WD * w_ref[...])
        mo_ref[...] = mn
        vo_ref[...] = vn

    spec = pl.BlockSpec((tr, cols), lambda i: (i, 0))
    shape = jax.ShapeDtypeStruct((rows, cols), F32)
    return pl.pallas_call(
        body, name=name, grid=(rows // tr,),
        in_specs=[spec] * 4, out_specs=[spec] * 3, out_shape=[shape] * 3,
        compiler_params=_params("arbitrary"),
    )(w, g, m, v)


def add_halves(gs, ps, core, name, deps=()):
    n = len(gs)

    def body(core_ref, *refs):
        for i in range(n):
            refs[2 * n + i][...] = (refs[i][...].astype(F32) + refs[n + i][...].astype(F32)).astype(BF16)

    in_specs, out_specs, out_shape = [], [], []
    for gt in gs:
        h, c = gt.shape[1] // 2, gt.shape[2]
        in_specs.append(pl.BlockSpec((1, h, c), lambda b, core_ref: (b, core_ref[0], 0)))
    for gt in gs:
        h, c = gt.shape[1] // 2, gt.shape[2]
        in_specs.append(pl.BlockSpec((1, h, c), lambda b, core_ref: (b, 0, 0)))
        out_specs.append(pl.BlockSpec((1, h, c), lambda b, core_ref: (b, 0, 0)))
        out_shape.append(jax.ShapeDtypeStruct((N_CHIPS, h, c), BF16))
    in_specs += [ANY] * len(deps)
    return pl.pallas_call(
        _after(body, 1 + 2 * n, deps), name=name,
        grid_spec=pltpu.PrefetchScalarGridSpec(num_scalar_prefetch=1, grid=(N_CHIPS,), in_specs=in_specs, out_specs=out_specs),
        out_shape=out_shape, compiler_params=_params("arbitrary"),
    )(core, *gs, *ps, *deps)


def add_final(hs, qs, chip, name, deps=()):
    n = len(hs)

    def body(chip_ref, *refs):
        for i in range(n):
            q = refs[n + i]
            refs[2 * n + i][...] = ((refs[i][0].astype(F32) + q[0].astype(F32)) + q[1].astype(F32)) + q[2].astype(F32)

    in_specs, out_specs, out_shape = [], [], []
    for ht in hs:
        h, c = ht.shape[1], ht.shape[2]
        in_specs.append(pl.BlockSpec((1, h, c), lambda i, chip_ref: (chip_ref[0], 0, 0)))
    for ht in hs:
        h, c = ht.shape[1], ht.shape[2]
        in_specs.append(pl.BlockSpec((N_CHIPS - 1, h, c), lambda i, chip_ref: (0, 0, 0)))
        out_specs.append(pl.BlockSpec((h, c), lambda i, chip_ref: (0, 0)))
        out_shape.append(jax.ShapeDtypeStruct((h, c), F32))
    in_specs += [ANY] * len(deps)
    return pl.pallas_call(
        _after(body, 1 + 2 * n, deps), name=name,
        grid_spec=pltpu.PrefetchScalarGridSpec(num_scalar_prefetch=1, grid=(1,), in_specs=in_specs, out_specs=out_specs),
        out_shape=out_shape, compiler_params=_params("arbitrary"),
    )(chip, *hs, *qs, *deps)


def sum_devices(gathered, name):
    rows = gathered.shape[1]

    def body(g_ref, o_ref):
        acc = g_ref[0]
        for dev in range(1, 8):
            acc = acc + g_ref[dev]
        o_ref[...] = acc

    return pl.pallas_call(
        body, name=name, grid=(rows // 8,),
        in_specs=[pl.BlockSpec((8, 8, D), lambda i: (0, i, 0))],
        out_specs=pl.BlockSpec((8, D), lambda i: (i, 0)),
        out_shape=jax.ShapeDtypeStruct((rows, D), F32),
        compiler_params=_params("arbitrary"),
    )(gathered)


def _mesh_pos():
    return lax.axis_index("x"), lax.axis_index("y"), lax.axis_index("c")


def _other_chips(x, y):
    return [(1 - x, y), (x, 1 - y), (1 - x, 1 - y)]


def _sibling():
    x, y, c = _mesh_pos()
    return [(x, y, 1 - c)]


def _same_core_of_other_chips():
    x, y, c = _mesh_pos()
    return [(cx, cy, c) for (cx, cy) in _other_chips(x, y)]


def _on_sequencer(body, name, operands, out_shapes, sems, peers, collective_id):
    def seq_body(*refs):
        barrier = pltpu.get_barrier_semaphore()
        with_whom = peers()
        for peer in with_whom:
            pl.semaphore_signal(barrier, inc=1, device_id=peer, device_id_type=MESH)
        pl.semaphore_wait(barrier, len(with_whom))
        body(*refs)

    return pl.kernel(
        seq_body, name=name, out_type=out_shapes,
        mesh=plsc.ScalarSubcoreMesh(axis_name="seq", num_cores=1),
        scratch_types=sems, compiler_params=pltpu.CompilerParams(collective_id=collective_id),
    )(*operands)


def all_gather_weights(shards, name):
    n = len(shards)

    def body(*refs):
        ins, outs = refs[:n], refs[n:2 * n]
        send, recv, fsend, frecv, lsem = refs[2 * n:]
        x, y, c = _mesh_pos()
        k = 2 * x + y
        chips = _other_chips(x, y)
        local = [pltpu.make_async_copy(ins[i], outs[i].at[k], lsem.at[i]) for i in range(n)]
        for cp in local:
            cp.start()

        def half(ref, i, rows_half):
            h = shards[i].shape[0] // 2
            return ref.at[pl.ds(pl.multiple_of(rows_half * h, 8), h), :]

        first = []
        for i in range(n):
            for j, (cx, cy) in enumerate(chips):
                first.append(pltpu.make_async_remote_copy(
                    src_ref=half(ins[i], i, c), dst_ref=half(outs[i].at[k], i, c),
                    send_sem=send.at[i, j], recv_sem=recv.at[i, j], device_id=(cx, cy, c), device_id_type=MESH))
        for cp in first:
            cp.start()
        passed = []
        for i in range(n):
            for j, (cx, cy) in enumerate(chips):
                blk = half(outs[i].at[2 * cx + cy], i, c)
                pltpu.make_async_remote_copy(src_ref=blk, dst_ref=blk, send_sem=send.at[i, j], recv_sem=recv.at[i, j],
                                             device_id=(cx, cy, c), device_id_type=MESH).wait_recv()
                fw = pltpu.make_async_remote_copy(src_ref=blk, dst_ref=blk, send_sem=fsend.at[i, j], recv_sem=frecv.at[i, j],
                                                  device_id=(x, y, 1 - c), device_id_type=MESH)
                fw.start()
                passed.append(fw)
        for i in range(n):
            for j, (cx, cy) in enumerate(chips):
                blk = half(outs[i].at[2 * cx + cy], i, 1 - c)
                pltpu.make_async_remote_copy(src_ref=blk, dst_ref=blk, send_sem=fsend.at[i, j], recv_sem=frecv.at[i, j],
                                             device_id=(x, y, 1 - c), device_id_type=MESH).wait_recv()
        for cp in first + passed:
            cp.wait_send()
        for cp in local:
            cp.wait()

    def peers():
        x, y, c = _mesh_pos()
        return [(cx, cy, c) for (cx, cy) in _other_chips(x, y)] + [(x, y, 1 - c)]

    return _on_sequencer(
        body, name, shards, [jax.ShapeDtypeStruct((N_CHIPS,) + s.shape, s.dtype) for s in shards],
        [pltpu.SemaphoreType.DMA((n, 3))] * 4 + [pltpu.SemaphoreType.DMA((n,))], peers, GATHER_COLLECTIVE_ID)


def all_gather_rows(shard, name):
    def body(in_ref, out_ref, send, recv, lsem):
        x, y, c = _mesh_pos()
        k = 2 * x + y
        chips = _other_chips(x, y)
        local = pltpu.make_async_copy(in_ref, out_ref.at[k], lsem)
        local.start()
        sent = [pltpu.make_async_remote_copy(src_ref=in_ref, dst_ref=out_ref.at[k], send_sem=send.at[j], recv_sem=recv.at[j],
                                             device_id=(cx, cy, c), device_id_type=MESH) for j, (cx, cy) in enumerate(chips)]
        for cp in sent:
            cp.start()
        for j, (cx, cy) in enumerate(chips):
            blk = out_ref.at[2 * cx + cy]
            pltpu.make_async_remote_copy(src_ref=blk, dst_ref=blk, send_sem=send.at[j], recv_sem=recv.at[j],
                                         device_id=(cx, cy, c), device_id_type=MESH).wait_recv()
        for cp in sent:
            cp.wait_send()
        local.wait()

    return pl.pallas_call(
        body, name=name, in_specs=[ANY], out_specs=ANY,
        out_shape=jax.ShapeDtypeStruct((N_CHIPS,) + shard.shape, shard.dtype),
        scratch_shapes=[pltpu.SemaphoreType.DMA((3,)), pltpu.SemaphoreType.DMA((3,)), pltpu.SemaphoreType.DMA],
    )(shard)


def swap_halves(gs, name):
    n = len(gs)

    def body(*refs):
        ins, outs = refs[:n], refs[n:2 * n]
        send, recv = refs[2 * n:]
        x, y, c = _mesh_pos()
        sent = []
        for i in range(n):
            h = gs[i].shape[1] // 2
            src = ins[i].at[:, pl.ds(pl.multiple_of((1 - c) * h, 8), h), :]
            cp = pltpu.make_async_remote_copy(src_ref=src, dst_ref=outs[i], send_sem=send.at[i], recv_sem=recv.at[i],
                                              device_id=(x, y, 1 - c), device_id_type=MESH)
            cp.start()
            sent.append(cp)
        for cp in sent:
            cp.wait()

    return _on_sequencer(
        body, name, gs, [jax.ShapeDtypeStruct((N_CHIPS, g.shape[1] // 2, g.shape[2]), g.dtype) for g in gs],
        [pltpu.SemaphoreType.DMA((n,)), pltpu.SemaphoreType.DMA((n,))], _sibling, SIBLING_COLLECTIVE_ID)


def scatter_chips(hs, name):
    n = len(hs)

    def body(*refs):
        ins, outs = refs[:n], refs[n:2 * n]
        send, recv = refs[2 * n:]
        x, y, c = _mesh_pos()
        chips = _other_chips(x, y)
        sent = []
        for i in range(n):
            for j, (cx, cy) in enumerate(chips):
                cp = pltpu.make_async_remote_copy(src_ref=ins[i].at[2 * cx + cy], dst_ref=outs[i].at[j],
                                                  send_sem=send.at[i, j], recv_sem=recv.at[i, j],
                                                  device_id=(cx, cy, c), device_id_type=MESH)
                cp.start()
                sent.append(cp)
        for cp in sent:
            cp.wait()

    return _on_sequencer(
        body, name, hs, [jax.ShapeDtypeStruct((N_CHIPS - 1,) + h.shape[1:], h.dtype) for h in hs],
        [pltpu.SemaphoreType.DMA((n, 3)), pltpu.SemaphoreType.DMA((n, 3))], _same_core_of_other_chips, CHIPS_COLLECTIVE_ID)


def join_halves(rs, name):
    n = len(rs)

    def body(*refs):
        ins, outs = refs[:n], refs[n:2 * n]
        send, recv, lsem = refs[2 * n:]
        x, y, c = _mesh_pos()
        work = []
        for i in range(n):
            h = rs[i].shape[0]
            mine = outs[i].at[pl.ds(pl.multiple_of(c * h, 8), h), :]
            loc = pltpu.make_async_copy(ins[i], mine, lsem.at[i])
            loc.start()
            cp = pltpu.make_async_remote_copy(src_ref=ins[i], dst_ref=mine, send_sem=send.at[i], recv_sem=recv.at[i],
                                              device_id=(x, y, 1 - c), device_id_type=MESH)
            cp.start()
            work.append((loc, cp))
        for i, (loc, cp) in enumerate(work):
            h = rs[i].shape[0]
            theirs = outs[i].at[pl.ds(pl.multiple_of((1 - c) * h, 8), h), :]
            cp.wait_send()
            pltpu.make_async_remote_copy(src_ref=theirs, dst_ref=theirs, send_sem=send.at[i], recv_sem=recv.at[i],
                                         device_id=(x, y, 1 - c), device_id_type=MESH).wait_recv()
            loc.wait()

    return _on_sequencer(
        body, name, rs, [jax.ShapeDtypeStruct((2 * r.shape[0], r.shape[1]), r.dtype) for r in rs],
        [pltpu.SemaphoreType.DMA((n,)), pltpu.SemaphoreType.DMA((n,)), pltpu.SemaphoreType.DMA((n,))], _sibling, SIBLING_COLLECTIVE_ID)


def all_gather_devices(part, name):
    def body(in_ref, out_ref, send, recv, lsem):
        x, y, c = _mesh_pos()
        me = 4 * x + 2 * y + c
        local = pltpu.make_async_copy(in_ref, out_ref.at[me], lsem)
        local.start()
        sent = []
        for rel in range(1, 8):
            fx, fy, fc = (rel >> 2) & 1, (rel >> 1) & 1, rel & 1
            peer = (1 - x if fx else x, 1 - y if fy else y, 1 - c if fc else c)
            cp = pltpu.make_async_remote_copy(src_ref=in_ref, dst_ref=out_ref.at[me], send_sem=send.at[rel - 1],
                                              recv_sem=recv.at[rel - 1], device_id=peer, device_id_type=MESH)
            cp.start()
            sent.append(cp)
        for rel in range(1, 8):
            fx, fy, fc = (rel >> 2) & 1, (rel >> 1) & 1, rel & 1
            peer = (1 - x if fx else x, 1 - y if fy else y, 1 - c if fc else c)
            blk = out_ref.at[4 * peer[0] + 2 * peer[1] + peer[2]]
            pltpu.make_async_remote_copy(src_ref=blk, dst_ref=blk, send_sem=send.at[rel - 1], recv_sem=recv.at[rel - 1],
                                         device_id=peer, device_id_type=MESH).wait_recv()
        for cp in sent:
            cp.wait_send()
        local.wait()

    def everyone_else():
        x, y, c = _mesh_pos()
        return [(1 - x if (rel >> 2) & 1 else x, 1 - y if (rel >> 1) & 1 else y, 1 - c if rel & 1 else c) for rel in range(1, 8)]

    return _on_sequencer(
        body, name, [part], jax.ShapeDtypeStruct((8,) + part.shape, part.dtype),
        [pltpu.SemaphoreType.DMA((7,)), pltpu.SemaphoreType.DMA((7,)), pltpu.SemaphoreType.DMA], everyone_else, ALL_COLLECTIVE_ID)


class ReduceScatter:
    def __init__(self, grads, core, chip, name):
        self.grads, self.core, self.chip, self.name = grads, core, chip, name
        self.from_sibling = swap_halves(grads, name + "_swap")

    def sum_cores(self, deps=()):
        self.core_sums = add_halves(self.grads, self.from_sibling, self.core, self.name + "_add2", deps)
        self.from_chips = scatter_chips(self.core_sums, self.name + "_scatter")
        return self.core_sums[0]

    def sum_chips(self, deps=()):
        self.half_sums = add_final(self.core_sums, self.from_chips, self.chip, self.name + "_add4", deps)
        self.reduced = join_halves(self.half_sums, self.name + "_join")
        return self.half_sums[0]


def _blocked(w):
    return w.reshape(w.shape[0] * w.shape[1], w.shape[2])


def _grp_from_blocks(w):
    return w.reshape(N_CHIPS, 4, 64, C_GROUP_DIM).transpose(1, 0, 2, 3).reshape(4, C_GROUP_DIM, C_GROUP_DIM)


def _grp_to_blocks(w):
    return w.reshape(4, N_CHIPS, 64, C_GROUP_DIM).transpose(1, 0, 2, 3).reshape(N_CHIPS, C_GROUP_DIM, C_GROUP_DIM)


def _dw_cols(h, dact, hs, name, deps):
    tm = 512
    return mm_tn(h[None], dact, (N_CHIPS, D, hs), tm, hs, N_CHIPS * (D // tm),
                 lambda j: (0, j % 2), lambda j: (0, j // 2), lambda j: (j // 2, j % 2, 0), name, deps)


def _dw_rows(y, dxm, name, deps):
    tm = 512
    out = mm_tn(y[None], dxm[None], (1, D, D), tm, D, D // tm, lambda j: (0, j), lambda j: (0, 0), lambda j: (0, j, 0), name, deps)
    return out.reshape(N_CHIPS, D // N_CHIPS, D)


def _dw_hidden(act, other, name, deps):
    tn = 512
    return mm_tn(act, other[None], (N_CHIPS, FFN_SHARD, D), FFN_SHARD, tn, N_CHIPS * (D // tn),
                 lambda j: (j // 2, 0), lambda j: (0, j % 2), lambda j: (j // 2, 0, j % 2), name, deps)


def kernel(x, norm_mix_g, norm_ffn_g, final_norm_g, a_w_in, a_v_norm_g, a_w_s, a_b_s, a_w_out, b_w_in, b_conv_w, b_w_out, c_w_in, c_w_grp, c_scale, c_w_out, f_w_gate, f_w_up, f_w_down, loss_target, m_norm_mix_g, m_norm_ffn_g, m_final_norm_g, m_a_w_in, m_a_v_norm_g, m_a_w_s, m_a_b_s, m_a_w_out, m_b_w_in, m_b_conv_w, m_b_w_out, m_c_w_in, m_c_w_grp, m_c_scale, m_c_w_out, m_f_w_gate, m_f_w_up, m_f_w_down, v_norm_mix_g, v_norm_ffn_g, v_final_norm_g, v_a_w_in, v_a_v_norm_g, v_a_w_s, v_a_b_s, v_a_w_out, v_b_w_in, v_b_conv_w, v_b_w_out, v_c_w_in, v_c_w_grp, v_c_scale, v_c_w_out, v_f_w_gate, v_f_w_up, v_f_w_down):
    n_ex, seq, _ = x.shape
    t = n_ex * seq
    xi, yi, ci = lax.axis_index("x"), lax.axis_index("y"), lax.axis_index("c")
    chip = (2 * xi + yi).astype(jnp.int32)
    core_arr = ci.astype(jnp.int32).reshape(1)
    chip_arr = chip.reshape(1)
    bf = lambda w: w.astype(BF16)

    pad8 = lambda v: jnp.pad(v, ((0, 8 - v.shape[0]), (0, 0)))
    small_rows = jnp.concatenate([pad8(a_v_norm_g), pad8(b_conv_w[0]), pad8(c_scale)], axis=0)
    small_full = all_gather_rows(small_rows, "ag_small").transpose(1, 0, 2).reshape(24, D)
    gv_full = [small_full[0:1], small_full[1:2]]
    cw_full = small_full[8:11]
    scale_full = small_full[16:17]

    mixer_shards = [
        [bf(a_w_in[0]), bf(a_w_out[0])],
        [bf(b_w_in[0]), bf(b_w_out[0])],
        [bf(c_w_in[0]), bf(c_w_grp[0]).reshape(C_GROUP_DIM, C_GROUP_DIM), bf(c_w_out[0])],
        [bf(a_w_in[1]), bf(a_w_out[1])],
    ]
    hidden_major = lambda w: jnp.swapaxes(w, 1, 2)
    gate_t, up_t = hidden_major(f_w_gate), hidden_major(f_w_up)
    gathered = []
    for i in range(4):
        gathered.append(all_gather_weights(mixer_shards[i] + [bf(gate_t[i]), bf(up_t[i]), bf(f_w_down[i])], f"ag_l{i}"))

    mask = (jnp.arange(GMLP_BLOCK)[None, :] // 64) <= (jnp.arange(GMLP_BLOCK)[:, None] // 64)
    gmix = [norm_mix_g[i:i + 1] for i in range(4)]
    gffn = [norm_ffn_g[i:i + 1] for i in range(4)]
    a_chunks = [(s, 0, 512, 0, s * 512) for s in range(N_CHIPS)]
    b_chunks = [(j // 3, (j % 3) * 256, 256, j // 4, (j % 4) * 256) for j in range(12)]
    c_chunks = [(0, 0, D, 0, 0)]

    xs = [x.reshape(t, D)]
    saved = []
    for i in range(4):
        ws = gathered[i]
        wg, wu, wd = ws[-3], ws[-2], ws[-1]
        xin = xs[-1]
        if i in (0, 3):
            j = 0 if i == 0 else 1
            win, wout = ws[0], _blocked(ws[1])
            wm32 = jnp.where(mask[None], a_w_s[j], 0.0)
            wm, wmt = bf(wm32), bf(wm32.transpose(0, 2, 1))
            bs = jnp.broadcast_to(a_b_s[j][:, :, None], (A_GROUPS, GMLP_BLOCK, 128))
            h, z = norm_mm(xin, gmix[i], win, a_chunks, 1, 2 * D, f"a_in_l{i}")
            y, vn = a_mid_fwd(z[0], gv_full[j], wm, bs, f"a_mid_l{i}")
            xmid = mm_res(y, wout, xin, f"a_out_l{i}")
            saved.append(dict(h=h, z=z, y=y, vn=vn, win=win, wout=wout, wm=wm, wmt=wmt, bs=bs, gv=gv_full[j]))
        elif i == 1:
            win, wout = ws[0], _blocked(ws[1])
            h, p3 = norm_mm(xin, gmix[i], win, b_chunks, 3, D, "b_in")
            y = b_conv_fwd(p3, cw_full, seq, "b_conv")
            xmid = mm_res(y, wout, xin, "b_out")
            saved.append(dict(h=h, p3=p3, y=y, win=win, wout=wout))
        else:
            win, wgrp, wout = _blocked(ws[0])[None], _grp_from_blocks(ws[1]), _blocked(ws[2])
            h, p = norm_mm(xin, gmix[i], win, c_chunks, 1, D, "c_in")
            dpool = c_pool_fwd(p[0], seq, "c_pool")
            y, xmid = c_out_fwd(dpool, wgrp, scale_full, wout, xin, "c_out")
            saved.append(dict(h=h, d=dpool, y=y, win=win, wgrp=wgrp, wout=wout))
        h2, fa, fb, fs, xout = ffn_fwd(xmid, gffn[i], wg, wu, wd, f"ffn_l{i}")
        saved[-1].update(h2=h2, fa=fa, fb=fb, fs=fs, xmid=xmid, wg=wg, wu=wu, wd=wd)
        xs.append(xout)

    dx, dg_final, loss_part = loss_head(xs[4], loss_target.reshape(t, D), final_norm_g[None], "loss_head")
    loss = lax.psum(loss_part[0, 0], ("x", "y", "c"))

    dg_mix, dg_ffn = [None] * 4, [None] * 4
    rs = [None] * 4
    small = {}
    for i in (3, 2, 1, 0):
        sv = saved[i]
        xin = xs[i]
        newer = rs[i + 1] if i < 3 else None
        older = rs[i + 2] if i < 2 else None
        deps = ([newer.grads[0]] if newer else []) + ([older.half_sums[0]] if older else [])
        dxm, da, db, dg_ffn[i] = ffn_bwd(dx, sv["fa"], sv["fb"], sv["xmid"], gffn[i], sv["wg"], sv["wu"], sv["wd"], f"ffn_bwd_l{i}", deps)
        last = [newer.sum_cores([dxm])] if newer else []
        g_gate = _dw_hidden(da, sv["h2"], f"dw_gate_l{i}", last)
        g_up = _dw_hidden(db, sv["h2"], f"dw_up_l{i}", [g_gate])
        g_down = _dw_hidden(sv["fs"], dx, f"dw_down_l{i}", [g_up])
        if i in (0, 3):
            j = 0 if i == 0 else 1
            dy = mm_nt(dxm, sv["wout"], f"a_dy_l{i}")
            dz, dwm, dbs, dgv = a_mid_bwd(dy, sv["z"][0], sv["vn"], sv["gv"], sv["wm"], sv["wmt"], sv["bs"], f"a_mid_bwd_l{i}")
            dz = dz[None]
            dx, dg_mix[i] = bwd_in(dz, sv["win"], a_chunks, xin, gmix[i], dxm, f"a_bwd_in_l{i}")
            g_in = _dw_cols(sv["h"], dz, 512, f"dw_a_in_l{i}", [g_down])
            g_out = _dw_rows(sv["y"], dxm, f"dw_a_out_l{i}", [g_in])
            small[f"wm{j}"], small[f"bs{j}"], small[f"gv{j}"] = dwm, dbs, dgv
            mixer_grads = [g_out, g_in]
        elif i == 1:
            dy = mm_nt(dxm, sv["wout"], "b_dy")
            dp3, small["cw"] = b_conv_bwd(dy, sv["p3"], cw_full, seq, "b_conv_bwd")
            dx, dg_mix[i] = bwd_in(dp3, sv["win"], b_chunks, xin, gmix[i], dxm, "b_bwd_in")
            g_in = mm_tn(sv["h"][None], dp3, (N_CHIPS, D, 768), 512, 256, 24,
                         lambda j: (0, j % 2), lambda j: ((j // 2) // 4, (j // 2) % 4),
                         lambda j: ((j // 2) // 3, j % 2, (j // 2) % 3), "dw_b_in", [g_down])
            g_out = _dw_rows(sv["y"], dxm, "dw_b_out", [g_in])
            mixer_grads = [g_out, g_in]
        else:
            outs = c_out_bwd(dxm, sv["d"], sv["wgrp"], scale_full, sv["wout"], "c_out_bwd")
            dyp, dd, small["scale"] = outs[0], list(outs[1:5]), outs[5]
            dpool = c_pool_bwd(dd, seq, "c_pool_bwd")
            dp = jnp.concatenate(dpool, axis=1)[None]
            dx, dg_mix[i] = bwd_in(dp, sv["win"], c_chunks, xin, gmix[i], dxm, "c_bwd_in")
            g_in = _dw_rows(sv["h"], dp[0], "dw_c_in", [g_down])
            dcat = jnp.concatenate(sv["d"], axis=1)
            g_grp = mm_tn(dcat[None], dyp[None], (4, C_GROUP_DIM, C_GROUP_DIM), C_GROUP_DIM, C_GROUP_DIM, 4,
                          lambda j: (0, j), lambda j: (0, j), lambda j: (j, 0, 0), "dw_c_grp", [g_in])
            g_out = _dw_rows(sv["y"], dxm, "dw_c_out", [g_grp])
            mixer_grads = [g_out, g_in, _grp_to_blocks(g_grp)]
        rs[i] = ReduceScatter(mixer_grads + [g_gate, g_up, g_down], core_arr, chip_arr, f"rs_l{i}")
        if newer:
            newer.sum_chips([mixer_grads[0]])
    rs[0].sum_cores()
    rs[0].sum_chips()
    grad_x = dx.reshape(n_ex, seq, D)
    by_name = []
    for i in range(4):
        red = rs[i].reduced
        names_i = {0: ["a_w_out", "a_w_in"], 1: ["b_w_out", "b_w_in"], 2: ["c_w_out", "c_w_in", "c_w_grp"], 3: ["a_w_out", "a_w_in"]}[i]
        by_name.append(dict(zip(names_i + ["f_w_gate", "f_w_up", "f_w_down"], red)))

    def bs_rows(v):
        return jnp.pad(v[:, :, 0].reshape(1, D), ((0, 7), (0, 0)))

    parts = dg_mix + dg_ffn + [dg_final, small["gv0"], small["gv1"], small["cw"], small["scale"],
                               small["wm0"].reshape(128, D), small["wm1"].reshape(128, D), bs_rows(small["bs0"]), bs_rows(small["bs1"])]
    packed = jnp.concatenate(parts, axis=0)
    total = sum_devices(all_gather_devices(packed, "ag_small_grads"), "sum_small_grads")
    first_row = lambda lo, n: total[lo:lo + 8 * n].reshape(n, 8, D)[:, 0]
    g_norm_mix = first_row(0, 4)
    g_norm_ffn = first_row(32, 4)
    g_final = total[64]
    g_gv = first_row(72, 2)
    g_cw = total[88:91]
    g_scale = total[96:97]
    g_ws = jnp.where(mask[None, None], total[104:360].reshape(2, A_GROUPS, 128, 128), 0.0)
    g_bs = first_row(360, 2).reshape(2, A_GROUPS, 128)
    col0 = chip * (D // N_CHIPS)
    cols = lambda v: lax.dynamic_slice_in_dim(v, col0, D // N_CHIPS, axis=1)

    grads = {
        "norm_mix_g": g_norm_mix, "norm_ffn_g": g_norm_ffn, "final_norm_g": g_final,
        "a_w_in": jnp.stack([by_name[0]["a_w_in"], by_name[3]["a_w_in"]]), "a_v_norm_g": cols(g_gv), "a_w_s": g_ws, "a_b_s": g_bs,
        "a_w_out": jnp.stack([by_name[0]["a_w_out"], by_name[3]["a_w_out"]]),
        "b_w_in": by_name[1]["b_w_in"][None], "b_conv_w": cols(g_cw)[None], "b_w_out": by_name[1]["b_w_out"][None],
        "c_w_in": by_name[2]["c_w_in"][None], "c_w_grp": by_name[2]["c_w_grp"].reshape(1, 4, 64, C_GROUP_DIM), "c_scale": cols(g_scale),
        "c_w_out": by_name[2]["c_w_out"][None],
        "f_w_gate": jnp.stack([by_name[i]["f_w_gate"] for i in range(4)]), "f_w_up": jnp.stack([by_name[i]["f_w_up"] for i in range(4)]),
        "f_w_down": jnp.stack([by_name[i]["f_w_down"] for i in range(4)]),
    }
    weights = dict(norm_mix_g=norm_mix_g, norm_ffn_g=norm_ffn_g, final_norm_g=final_norm_g, a_w_in=a_w_in, a_v_norm_g=a_v_norm_g,
                   a_w_s=a_w_s, a_b_s=a_b_s, a_w_out=a_w_out, b_w_in=b_w_in, b_conv_w=b_conv_w, b_w_out=b_w_out, c_w_in=c_w_in,
                   c_w_grp=c_w_grp, c_scale=c_scale, c_w_out=c_w_out, f_w_gate=f_w_gate, f_w_up=f_w_up, f_w_down=f_w_down)
    m_in = dict(norm_mix_g=m_norm_mix_g, norm_ffn_g=m_norm_ffn_g, final_norm_g=m_final_norm_g, a_w_in=m_a_w_in, a_v_norm_g=m_a_v_norm_g,
                a_w_s=m_a_w_s, a_b_s=m_a_b_s, a_w_out=m_a_w_out, b_w_in=m_b_w_in, b_conv_w=m_b_conv_w, b_w_out=m_b_w_out, c_w_in=m_c_w_in,
                c_w_grp=m_c_w_grp, c_scale=m_c_scale, c_w_out=m_c_w_out, f_w_gate=m_f_w_gate, f_w_up=m_f_w_up, f_w_down=m_f_w_down)
    v_in = dict(norm_mix_g=v_norm_mix_g, norm_ffn_g=v_norm_ffn_g, final_norm_g=v_final_norm_g, a_w_in=v_a_w_in, a_v_norm_g=v_a_v_norm_g,
                a_w_s=v_a_w_s, a_b_s=v_a_b_s, a_w_out=v_a_w_out, b_w_in=v_b_w_in, b_conv_w=v_b_conv_w, b_w_out=v_b_w_out, c_w_in=v_c_w_in,
                c_w_grp=v_c_w_grp, c_scale=v_c_scale, c_w_out=v_c_w_out, f_w_gate=v_f_w_gate, f_w_up=v_f_w_up, f_w_down=v_f_w_down)

    names = list(weights)
    deltas, new_m, new_v = {}, {}, {}
    for nme in names:
        w = weights[nme]
        if nme in ("f_w_gate", "f_w_up"):
            view, back = hidden_major, hidden_major
        else:
            view, back = (lambda a: a), (lambda a: a.reshape(w.shape))
        wv = view(w)
        flat = lambda a: a.reshape(-1, wv.shape[-1])
        dl, mn, vn = adamw(flat(wv), flat(grads[nme]), flat(view(m_in[nme])), flat(view(v_in[nme])), f"adamw_{nme}")
        deltas[nme], new_m[nme], new_v[nme] = (back(o.reshape(wv.shape)) for o in (dl, mn, vn))
        grads[nme] = back(grads[nme].reshape(wv.shape))

    return (loss, grad_x, *[grads[n] for n in names], *[deltas[n] for n in names],
            *[new_m[n] for n in names], *[new_v[n] for n in names])
```

```python
import jax
import jax.numpy as jnp
from jax import lax
from jax.experimental import pallas as pl
from jax.experimental.pallas import tpu as pltpu
from jax.experimental.pallas import tpu_sc as plsc

F32 = jnp.float32
BF16 = jnp.bfloat16
D = 1024
FFN_SHARD = 704
GMLP_BLOCK = 128
A_GROUPS = 8
POOL_WINDOWS = (2, 4, 8, 16)
C_GROUP_DIM = 256
N_CHIPS = 4
EPS = 1e-6
ADAM_LR, ADAM_B1, ADAM_B2, ADAM_EPS, ADAM_WD, ADAM_STEP = 0.001, 0.9, 0.999, 1e-08, 0.01, 10
VMEM_LIMIT_BYTES = 56 * 1024 * 1024
FFN_FWD_ROW_GROUPS = 1
FFN_BWD_ROW_GROUPS = 2
MESH = pl.DeviceIdType.MESH
GATHER_COLLECTIVE_ID = 1
SIBLING_COLLECTIVE_ID = 2
CHIPS_COLLECTIVE_ID = 3
ALL_COLLECTIVE_ID = 4
ANY = pl.BlockSpec(memory_space=pl.ANY)
NT_DIMS = (((1,), (1,)), ((), ()))
TN_DIMS = (((0,), (0,)), ((), ()))
INV_SQRT2 = 0.7071067811865476
INV_SQRT_2PI = 0.3989422804014327


def _params(*semantics):
    return pltpu.CompilerParams(dimension_semantics=semantics, vmem_limit_bytes=VMEM_LIMIT_BYTES)


def _dot(a, b):
    return jnp.dot(a, b, preferred_element_type=F32)


def _dot_nt(a, b):
    return lax.dot_general(a, b, NT_DIMS, preferred_element_type=F32)


def _rms(x):
    r = lax.rsqrt(jnp.mean(x * x, axis=-1, keepdims=True) + EPS)
    return x * r, r


def _rms_bwd(x, g, dh):
    xh, r = _rms(x)
    dxh = dh * g
    dx = r * (dxh - xh * jnp.mean(dxh * xh, axis=-1, keepdims=True))
    return dx, jnp.sum(dh * xh, axis=0, keepdims=True)


def _gelu(x):
    return 0.5 * x * (1.0 + lax.erf(x * INV_SQRT2))


def _gelu_grad(x):
    return 0.5 * (1.0 + lax.erf(x * INV_SQRT2)) + x * jnp.exp(-0.5 * x * x) * INV_SQRT_2PI


def _shift_down(v, s, row):
    return jnp.where(row >= s, pltpu.roll(v, s, 0), 0.0)


def _shift_up(v, s, row):
    n = v.shape[0]
    return jnp.where(row < n - s, pltpu.roll(v, n - s, 0), 0.0)


def _row_tile(t, want):
    return want if t % want == 0 else t


def _after(body, first, deps):
    if not deps:
        return body

    def ordered(*refs):
        return body(*refs[:first], *refs[first + len(deps):])

    return ordered


def norm_mm(x, g, w, chunks, n_parts, part_width, name):
    t = x.shape[0]
    tm = _row_tile(t, 512)
    n_shards, _, hs = w.shape

    def body(x_ref, g_ref, w_ref, h_ref, p_ref):
        xh, _ = _rms(x_ref[...])
        h = (xh * g_ref[...]).astype(BF16)
        h_ref[...] = h
        for s in range(n_shards):
            res = _dot(h, w_ref[s]).astype(BF16)
            for (cs, wc, width, part, pc) in chunks:
                if cs == s:
                    p_ref[part, :, pc:pc + width] = res[:, wc:wc + width]

    return pl.pallas_call(
        body, name=name, grid=(t // tm,),
        in_specs=[pl.BlockSpec((tm, D), lambda i: (i, 0)), pl.BlockSpec((1, D), lambda i: (0, 0)),
                  pl.BlockSpec((n_shards, D, hs), lambda i: (0, 0, 0))],
        out_specs=[pl.BlockSpec((tm, D), lambda i: (i, 0)), pl.BlockSpec((n_parts, tm, part_width), lambda i: (0, i, 0))],
        out_shape=[jax.ShapeDtypeStruct((t, D), BF16), jax.ShapeDtypeStruct((n_parts, t, part_width), BF16)],
        compiler_params=_params("arbitrary"),
    )(x, g, w)


def mm_res(a, w, res, name):
    t, k = a.shape
    n = w.shape[1]
    tm = _row_tile(t, 512)

    def body(a_ref, w_ref, r_ref, o_ref):
        o_ref[...] = r_ref[...] + _dot(a_ref[...], w_ref[...])

    return pl.pallas_call(
        body, name=name, grid=(t // tm,),
        in_specs=[pl.BlockSpec((tm, k), lambda i: (i, 0)), pl.BlockSpec((k, n), lambda i: (0, 0)),
                  pl.BlockSpec((tm, n), lambda i: (i, 0))],
        out_specs=pl.BlockSpec((tm, n), lambda i: (i, 0)),
        out_shape=jax.ShapeDtypeStruct((t, n), F32),
        compiler_params=_params("arbitrary"),
    )(a, w, res)


def mm_nt(a, w, name):
    t, n = a.shape
    k = w.shape[0]
    tm = _row_tile(t, 512)

    def body(a_ref, w_ref, o_ref):
        o_ref[...] = _dot_nt(a_ref[...].astype(BF16), w_ref[...]).astype(BF16)

    return pl.pallas_call(
        body, name=name, grid=(t // tm,),
        in_specs=[pl.BlockSpec((tm, n), lambda i: (i, 0)), pl.BlockSpec((k, n), lambda i: (0, 0))],
        out_specs=pl.BlockSpec((tm, k), lambda i: (i, 0)),
        out_shape=jax.ShapeDtypeStruct((t, k), BF16),
        compiler_params=_params("arbitrary"),
    )(a, w)


def bwd_in(dp, w, chunks, x, g, dres, name):
    n_parts, t, part_width = dp.shape
    n_shards, _, hs = w.shape
    tm = _row_tile(t, 512)

    def body(dp_ref, w_ref, x_ref, g_ref, dres_ref, dx_ref, dg_ref):
        acc = jnp.zeros((tm, D), F32)
        for (cs, wc, width, part, pc) in chunks:
            acc = acc + _dot_nt(dp_ref[part, :, pc:pc + width], w_ref[cs, :, wc:wc + width])
        dx, dg = _rms_bwd(x_ref[...], g_ref[...], acc)
        dx_ref[...] = dres_ref[...] + dx

        @pl.when(pl.program_id(0) == 0)
        def _():
            dg_ref[...] = jnp.zeros_like(dg_ref)

        dg_ref[0:1, :] += dg

    return pl.pallas_call(
        body, name=name, grid=(t // tm,),
        in_specs=[pl.BlockSpec((n_parts, tm, part_width), lambda i: (0, i, 0)),
                  pl.BlockSpec((n_shards, D, hs), lambda i: (0, 0, 0)),
                  pl.BlockSpec((tm, D), lambda i: (i, 0)), pl.BlockSpec((1, D), lambda i: (0, 0)),
                  pl.BlockSpec((tm, D), lambda i: (i, 0))],
        out_specs=[pl.BlockSpec((tm, D), lambda i: (i, 0)), pl.BlockSpec((8, D), lambda i: (0, 0))],
        out_shape=[jax.ShapeDtypeStruct((t, D), F32), jax.ShapeDtypeStruct((8, D), F32)],
        compiler_params=_params("arbitrary"),
    )(dp, w, x, g, dres)


def mm_tn(a, b, out_shape, tm, tn, n_tiles, a_idx, b_idx, o_idx, name, deps=()):
    t = a.shape[1]

    def body(a_ref, b_ref, o_ref):
        o_ref[0] = lax.dot_general(a_ref[0].astype(BF16), b_ref[0].astype(BF16), TN_DIMS, preferred_element_type=F32).astype(BF16)

    return pl.pallas_call(
        _after(body, 2, deps), name=name, grid=(n_tiles,),
        in_specs=[pl.BlockSpec((1, t, tm), lambda j: (a_idx(j)[0], 0, a_idx(j)[1])),
                  pl.BlockSpec((1, t, tn), lambda j: (b_idx(j)[0], 0, b_idx(j)[1]))] + [ANY] * len(deps),
        out_specs=pl.BlockSpec((1, tm, tn), lambda j: o_idx(j)),
        out_shape=jax.ShapeDtypeStruct(out_shape, BF16),
        compiler_params=_params("arbitrary"),
    )(a, b, *deps)


def ffn_fwd(x, g, wg, wu, wd, name):
    t = x.shape[0]
    tm = _row_tile(t, 512)
    hs = wg.shape[1]

    def body(x_ref, g_ref, wg_ref, wu_ref, wd_ref, h_ref, a_ref, b_ref, s_ref, o_ref, acc_ref):
        kk = pl.program_id(1)

        @pl.when(kk == 0)
        def _():
            xh, _ = _rms(x_ref[...])
            h_ref[...] = (xh * g_ref[...]).astype(BF16)
            acc_ref[...] = jnp.zeros_like(acc_ref)

        for r in range(FFN_FWD_ROW_GROUPS):
            rows = pl.ds(r * (tm // FFN_FWD_ROW_GROUPS), tm // FFN_FWD_ROW_GROUPS)
            h = h_ref[rows, :]
            a = _dot_nt(h, wg_ref[0])
            b = _dot_nt(h, wu_ref[0])
            sig = jax.nn.sigmoid(a)
            silu = a * sig
            s = (silu * b).astype(BF16)
            a_ref[0, rows, :] = (b * (sig * (1.0 + a * (1.0 - sig)))).astype(BF16)
            b_ref[0, rows, :] = silu.astype(BF16)
            s_ref[0, rows, :] = s
            acc_ref[rows, :] += _dot(s, wd_ref[0])

        @pl.when(kk == N_CHIPS - 1)
        def _():
            o_ref[...] = x_ref[...] + acc_ref[...]

    act = pl.BlockSpec((1, tm, hs), lambda i, kk: (kk, i, 0))
    act_shape = jax.ShapeDtypeStruct((N_CHIPS, t, hs), BF16)
    wspec = pl.BlockSpec((1, hs, D), lambda i, kk: (kk, 0, 0))
    return pl.pallas_call(
        body, name=name, grid=(t // tm, N_CHIPS),
        in_specs=[pl.BlockSpec((tm, D), lambda i, kk: (i, 0)), pl.BlockSpec((1, D), lambda i, kk: (0, 0)), wspec, wspec, wspec],
        out_specs=[pl.BlockSpec((tm, D), lambda i, kk: (i, 0)), act, act, act, pl.BlockSpec((tm, D), lambda i, kk: (i, 0))],
        out_shape=[jax.ShapeDtypeStruct((t, D), BF16), act_shape, act_shape, act_shape, jax.ShapeDtypeStruct((t, D), F32)],
        scratch_shapes=[pltpu.VMEM((tm, D), F32)],
        compiler_params=_params("arbitrary", "arbitrary"),
    )(x, g, wg, wu, wd)


def ffn_bwd(dxo, a, b, x, g, wg, wu, wd, name, deps=()):
    t = x.shape[0]
    tm = _row_tile(t, 512)
    hs = wg.shape[1]

    def body(dxo_ref, a_ref, b_ref, x_ref, g_ref, wg_ref, wu_ref, wd_ref, dx_ref, da_ref, db_ref, dg_ref, acc_ref):
        i, kk = pl.program_id(0), pl.program_id(1)

        @pl.when(kk == 0)
        def _():
            acc_ref[...] = jnp.zeros_like(acc_ref)

        @pl.when((kk == 0) & (i == 0))
        def _():
            dg_ref[...] = jnp.zeros_like(dg_ref)

        for r in range(FFN_BWD_ROW_GROUPS):
            rows = pl.ds(r * (tm // FFN_BWD_ROW_GROUPS), tm // FFN_BWD_ROW_GROUPS)
            ds = _dot_nt(dxo_ref[rows, :].astype(BF16), wd_ref[0])
            da = (ds * a_ref[0, rows, :].astype(F32)).astype(BF16)
            db = (ds * b_ref[0, rows, :].astype(F32)).astype(BF16)
            da_ref[0, rows, :] = da
            db_ref[0, rows, :] = db
            acc_ref[rows, :] += _dot(da, wg_ref[0]) + _dot(db, wu_ref[0])

        @pl.when(kk == N_CHIPS - 1)
        def _():
            dx, dg = _rms_bwd(x_ref[...], g_ref[...], acc_ref[...])
            dx_ref[...] = dxo_ref[...] + dx
            dg_ref[0:1, :] += dg

    act = pl.BlockSpec((1, tm, hs), lambda i, kk: (kk, i, 0))
    act_shape = jax.ShapeDtypeStruct((N_CHIPS, t, hs), BF16)
    row = pl.BlockSpec((tm, D), lambda i, kk: (i, 0))
    wspec = pl.BlockSpec((1, hs, D), lambda i, kk: (kk, 0, 0))
    return pl.pallas_call(
        _after(body, 8, deps), name=name, grid=(t // tm, N_CHIPS),
        in_specs=[row, act, act, row, pl.BlockSpec((1, D), lambda i, kk: (0, 0)), wspec, wspec, wspec] + [ANY] * len(deps),
        out_specs=[row, act, act, pl.BlockSpec((8, D), lambda i, kk: (0, 0))],
        out_shape=[jax.ShapeDtypeStruct((t, D), F32), act_shape, act_shape, jax.ShapeDtypeStruct((8, D), F32)],
        scratch_shapes=[pltpu.VMEM((tm, D), F32)],
        compiler_params=_params("arbitrary", "arbitrary"),
    )(dxo, a, b, x, g, wg, wu, wd, *deps)


def _layer_norm_stats(v):
    mu = jnp.mean(v, axis=-1, keepdims=True)
    vc = v - mu
    rstd = lax.rsqrt(jnp.mean(vc * vc, axis=-1, keepdims=True) + EPS)
    return vc * rstd, rstd


def a_mid_fwd(z, gv, wm, bs, name):
    t = z.shape[0]
    tm = _row_tile(t, 256)

    def body(z_ref, gv_ref, wm_ref, bs_ref, y_ref, vn_ref):
        zz = z_ref[...].astype(F32)
        u = _gelu(zz[:, :D])
        vhat, _ = _layer_norm_stats(_gelu(zz[:, D:]))
        vnb = (vhat * gv_ref[...]).astype(BF16)
        vn_ref[...] = vnb
        for n in range(tm // GMLP_BLOCK):
            rows = slice(n * GMLP_BLOCK, (n + 1) * GMLP_BLOCK)
            for grp in range(A_GROUPS):
                cols = slice(grp * 128, (grp + 1) * 128)
                sv = _dot(wm_ref[grp], vnb[rows, cols]) + bs_ref[grp]
                y_ref[rows, cols] = (u[rows, cols] * sv).astype(BF16)

    small = pl.BlockSpec((A_GROUPS, 128, 128), lambda i: (0, 0, 0))
    return pl.pallas_call(
        body, name=name, grid=(t // tm,),
        in_specs=[pl.BlockSpec((tm, 2 * D), lambda i: (i, 0)), pl.BlockSpec((1, D), lambda i: (0, 0)), small, small],
        out_specs=[pl.BlockSpec((tm, D), lambda i: (i, 0)), pl.BlockSpec((tm, D), lambda i: (i, 0))],
        out_shape=[jax.ShapeDtypeStruct((t, D), BF16), jax.ShapeDtypeStruct((t, D), BF16)],
        compiler_params=_params("arbitrary"),
    )(z, gv, wm, bs)


def a_mid_bwd(dy, z, vn, gv, wm, wmt, bs, name):
    t = z.shape[0]
    tm = _row_tile(t, 256)

    def body(dy_ref, z_ref, vn_ref, gv_ref, wm_ref, wmt_ref, bs_ref, dz_ref, dwm_ref, dbs_ref, dgv_ref, du_ref, dvn_ref):
        @pl.when(pl.program_id(0) == 0)
        def _():
            dwm_ref[...] = jnp.zeros_like(dwm_ref)
            dbs_ref[...] = jnp.zeros_like(dbs_ref)
            dgv_ref[...] = jnp.zeros_like(dgv_ref)

        zz = z_ref[...].astype(F32)
        zu, zv = zz[:, :D], zz[:, D:]
        u = _gelu(zu)
        vhat, rstd = _layer_norm_stats(_gelu(zv))
        dyv = dy_ref[...].astype(F32)
        vnb = vn_ref[...]
        ones = jnp.ones((128, 128), BF16)
        for n in range(tm // GMLP_BLOCK):
            rows = slice(n * GMLP_BLOCK, (n + 1) * GMLP_BLOCK)
            for grp in range(A_GROUPS):
                cols = slice(grp * 128, (grp + 1) * 128)
                blk = vnb[rows, cols]
                sv = _dot(wm_ref[grp], blk) + bs_ref[grp]
                dyb = dyv[rows, cols]
                du_ref[rows, cols] = dyb * sv
                dsv = (dyb * u[rows, cols]).astype(BF16)
                dvn_ref[rows, cols] = _dot(wmt_ref[grp], dsv)
                dwm_ref[grp] += _dot_nt(dsv, blk)
                dbs_ref[grp] += _dot(dsv, ones)
        dvn = dvn_ref[...]
        dgv_ref[0:1, :] += jnp.sum(dvn * vhat, axis=0, keepdims=True)
        dvh = dvn * gv_ref[...]
        dv = rstd * (dvh - jnp.mean(dvh, axis=-1, keepdims=True) - vhat * jnp.mean(dvh * vhat, axis=-1, keepdims=True))
        dz_ref[:, :D] = (du_ref[...] * _gelu_grad(zu)).astype(BF16)
        dz_ref[:, D:] = (dv * _gelu_grad(zv)).astype(BF16)

    small = pl.BlockSpec((A_GROUPS, 128, 128), lambda i: (0, 0, 0))
    row = pl.BlockSpec((tm, D), lambda i: (i, 0))
    small_shape = jax.ShapeDtypeStruct((A_GROUPS, 128, 128), F32)
    return pl.pallas_call(
        body, name=name, grid=(t // tm,),
        in_specs=[row, pl.BlockSpec((tm, 2 * D), lambda i: (i, 0)), row, pl.BlockSpec((1, D), lambda i: (0, 0)), small, small, small],
        out_specs=[pl.BlockSpec((tm, 2 * D), lambda i: (i, 0)), small, small, pl.BlockSpec((8, D), lambda i: (0, 0))],
        out_shape=[jax.ShapeDtypeStruct((t, 2 * D), BF16), small_shape, small_shape, jax.ShapeDtypeStruct((8, D), F32)],
        scratch_shapes=[pltpu.VMEM((tm, D), F32), pltpu.VMEM((tm, D), F32)],
        compiler_params=_params("arbitrary"),
    )(dy, z, vn, gv, wm, wmt, bs)


def _conv_terms(p_ref, row):
    gb = p_ref[0].astype(F32)
    gc = p_ref[1].astype(F32)
    xt = p_ref[2].astype(F32)
    q = gc * xt
    return gb, gc, xt, q, _shift_down(q, 1, row), _shift_down(q, 2, row)


def b_conv_fwd(p3, cw, seq, name):
    t = p3.shape[1]
    cb = 256

    def body(p_ref, cw_ref, y_ref):
        row = lax.broadcasted_iota(jnp.int32, (seq, cb), 0)
        gb, _, _, q, q1, q2 = _conv_terms(p_ref, row)
        y_ref[...] = (gb * (cw_ref[2:3, :] * q + cw_ref[1:2, :] * q1 + cw_ref[0:1, :] * q2)).astype(BF16)

    return pl.pallas_call(
        body, name=name, grid=(t // seq, D // cb),
        in_specs=[pl.BlockSpec((3, seq, cb), lambda e, c: (0, e, c)), pl.BlockSpec((3, cb), lambda e, c: (0, c))],
        out_specs=pl.BlockSpec((seq, cb), lambda e, c: (e, c)),
        out_shape=jax.ShapeDtypeStruct((t, D), BF16),
        compiler_params=_params("arbitrary", "arbitrary"),
    )(p3, cw)


def b_conv_bwd(dy, p3, cw, seq, name):
    t = p3.shape[1]
    cb = 256

    def body(dy_ref, p_ref, cw_ref, dp_ref, dcw_ref):
        @pl.when(pl.program_id(1) == 0)
        def _():
            dcw_ref[...] = jnp.zeros_like(dcw_ref)

        row = lax.broadcasted_iota(jnp.int32, (seq, cb), 0)
        gb, gc, xt, q, q1, q2 = _conv_terms(p_ref, row)
        dyv = dy_ref[...].astype(F32)
        conv = cw_ref[2:3, :] * q + cw_ref[1:2, :] * q1 + cw_ref[0:1, :] * q2
        dyc = dyv * gb
        dq = cw_ref[2:3, :] * dyc + cw_ref[1:2, :] * _shift_up(dyc, 1, row) + cw_ref[0:1, :] * _shift_up(dyc, 2, row)
        dp_ref[0] = (dyv * conv).astype(BF16)
        dp_ref[1] = (dq * xt).astype(BF16)
        dp_ref[2] = (dq * gc).astype(BF16)
        dcw_ref[0:1, :] += jnp.sum(dyc * q2, axis=0, keepdims=True)
        dcw_ref[1:2, :] += jnp.sum(dyc * q1, axis=0, keepdims=True)
        dcw_ref[2:3, :] += jnp.sum(dyc * q, axis=0, keepdims=True)

    return pl.pallas_call(
        body, name=name, grid=(D // cb, t // seq),
        in_specs=[pl.BlockSpec((seq, cb), lambda c, e: (e, c)), pl.BlockSpec((3, seq, cb), lambda c, e: (0, e, c)),
                  pl.BlockSpec((3, cb), lambda c, e: (0, c))],
        out_specs=[pl.BlockSpec((3, seq, cb), lambda c, e: (0, e, c)), pl.BlockSpec((8, cb), lambda c, e: (0, c))],
        out_shape=[jax.ShapeDtypeStruct((3, t, D), BF16), jax.ShapeDtypeStruct((8, D), F32)],
        compiler_params=_params("arbitrary", "arbitrary"),
    )(dy, p3, cw)


def c_pool_fwd(p, seq, name):
    t = p.shape[0]

    def make(grp):
        w = POOL_WINDOWS[grp]

        def body_g(p_ref, d_ref):
            row = lax.broadcasted_iota(jnp.int32, (seq, C_GROUP_DIM), 0)
            pv = p_ref[...].astype(F32)
            acc = pv
            sh = 1
            while sh < w:
                acc = acc + _shift_down(acc, sh, row)
                sh *= 2
            d_ref[...] = (acc / jnp.minimum(row + 1, w).astype(F32) - pv).astype(BF16)

        return body_g

    outs = []
    for grp in range(len(POOL_WINDOWS)):
        outs.append(pl.pallas_call(
            make(grp), name=f"{name}_g{grp}", grid=(t // seq,),
            in_specs=[pl.BlockSpec((seq, C_GROUP_DIM), lambda e, grp=grp: (e, grp))],
            out_specs=pl.BlockSpec((seq, C_GROUP_DIM), lambda e: (e, 0)),
            out_shape=jax.ShapeDtypeStruct((t, C_GROUP_DIM), BF16),
            compiler_params=_params("arbitrary"),
        )(p))
    return outs


def c_pool_bwd(dd, seq, name):
    t = dd[0].shape[0]

    def make(w):
        def body_g(dd_ref, dp_ref):
            row = lax.broadcasted_iota(jnp.int32, (seq, C_GROUP_DIM), 0)
            ddv = dd_ref[...]
            acc = ddv / jnp.minimum(row + 1, w).astype(F32)
            sh = 1
            while sh < w:
                acc = acc + _shift_up(acc, sh, row)
                sh *= 2
            dp_ref[...] = (acc - ddv).astype(BF16)

        return body_g

    outs = []
    for grp, w in enumerate(POOL_WINDOWS):
        outs.append(pl.pallas_call(
            make(w), name=f"{name}_g{grp}", grid=(t // seq,),
            in_specs=[pl.BlockSpec((seq, C_GROUP_DIM), lambda e: (e, 0))],
            out_specs=pl.BlockSpec((seq, C_GROUP_DIM), lambda e: (e, 0)),
            out_shape=jax.ShapeDtypeStruct((t, C_GROUP_DIM), BF16),
            compiler_params=_params("arbitrary"),
        )(dd[grp]))
    return outs


def c_out_fwd(d, wgrp, scale, wo, x, name):
    t = x.shape[0]
    tm = _row_tile(t, 512)
    n_g = len(POOL_WINDOWS)

    def body(d0, d1, d2, d3, wg_ref, sc_ref, wo_ref, x_ref, y_ref, o_ref):
        parts = [_dot(dr[...], wg_ref[grp]) for grp, dr in enumerate((d0, d1, d2, d3))]
        y = (jnp.concatenate(parts, axis=1) * sc_ref[...]).astype(BF16)
        y_ref[...] = y
        o_ref[...] = x_ref[...] + _dot(y, wo_ref[...])

    dspec = pl.BlockSpec((tm, C_GROUP_DIM), lambda i: (i, 0))
    row = pl.BlockSpec((tm, D), lambda i: (i, 0))
    return pl.pallas_call(
        body, name=name, grid=(t // tm,),
        in_specs=[dspec] * n_g + [pl.BlockSpec((n_g, C_GROUP_DIM, C_GROUP_DIM), lambda i: (0, 0, 0)),
                                  pl.BlockSpec((1, D), lambda i: (0, 0)), pl.BlockSpec((D, D), lambda i: (0, 0)), row],
        out_specs=[row, row],
        out_shape=[jax.ShapeDtypeStruct((t, D), BF16), jax.ShapeDtypeStruct((t, D), F32)],
        compiler_params=_params("arbitrary"),
    )(*d, wgrp, scale, wo, x)


def c_out_bwd(dxm, d, wgrp, scale, wo, name):
    t = dxm.shape[0]
    tm = _row_tile(t, 512)
    n_g = len(POOL_WINDOWS)

    def body(dxm_ref, d0, d1, d2, d3, wg_ref, sc_ref, wo_ref, dyp_ref, dd0, dd1, dd2, dd3, dsc_ref):
        @pl.when(pl.program_id(0) == 0)
        def _():
            dsc_ref[...] = jnp.zeros_like(dsc_ref)

        dyo = _dot_nt(dxm_ref[...].astype(BF16), wo_ref[...])
        ypre = jnp.concatenate([_dot(dr[...], wg_ref[grp]) for grp, dr in enumerate((d0, d1, d2, d3))], axis=1)
        dsc_ref[0:1, :] += jnp.sum(dyo * ypre, axis=0, keepdims=True)
        dyp = (dyo * sc_ref[...]).astype(BF16)
        dyp_ref[...] = dyp
        for grp, ddr in enumerate((dd0, dd1, dd2, dd3)):
            ddr[...] = _dot_nt(dyp[:, grp * C_GROUP_DIM:(grp + 1) * C_GROUP_DIM], wg_ref[grp])

    dspec = pl.BlockSpec((tm, C_GROUP_DIM), lambda i: (i, 0))
    row = pl.BlockSpec((tm, D), lambda i: (i, 0))
    dshape = jax.ShapeDtypeStruct((t, C_GROUP_DIM), F32)
    return pl.pallas_call(
        body, name=name, grid=(t // tm,),
        in_specs=[row] + [dspec] * n_g + [pl.BlockSpec((n_g, C_GROUP_DIM, C_GROUP_DIM), lambda i: (0, 0, 0)),
                                          pl.BlockSpec((1, D), lambda i: (0, 0)), pl.BlockSpec((D, D), lambda i: (0, 0))],
        out_specs=[row] + [dspec] * n_g + [pl.BlockSpec((8, D), lambda i: (0, 0))],
        out_shape=[jax.ShapeDtypeStruct((t, D), BF16)] + [dshape] * n_g + [jax.ShapeDtypeStruct((8, D), F32)],
        compiler_params=_params("arbitrary"),
    )(dxm, *d, wgrp, scale, wo)


def loss_head(x, tgt, g, name):
    t = x.shape[0]
    tm = _row_tile(t, 512)

    def body(x_ref, t_ref, g_ref, dx_ref, dg_ref, loss_ref):
        @pl.when(pl.program_id(0) == 0)
        def _():
            dg_ref[...] = jnp.zeros_like(dg_ref)
            loss_ref[...] = jnp.zeros_like(loss_ref)

        xv, gvv = x_ref[...], g_ref[...]
        xh, _ = _rms(xv)
        diff = xh * gvv - t_ref[...]
        loss_ref[...] += 0.5 * jnp.sum(jnp.mean(diff * diff, axis=-1, keepdims=True))
        dx, dg = _rms_bwd(xv, gvv, diff * (1.0 / D))
        dx_ref[...] = dx
        dg_ref[0:1, :] += dg

    row = pl.BlockSpec((tm, D), lambda i: (i, 0))
    return pl.pallas_call(
        body, name=name, grid=(t // tm,),
        in_specs=[row, row, pl.BlockSpec((1, D), lambda i: (0, 0))],
        out_specs=[row, pl.BlockSpec((8, D), lambda i: (0, 0)), pl.BlockSpec((8, 128), lambda i: (0, 0))],
        out_shape=[jax.ShapeDtypeStruct((t, D), F32), jax.ShapeDtypeStruct((8, D), F32), jax.ShapeDtypeStruct((8, 128), F32)],
        compiler_params=_params("arbitrary"),
    )(x, tgt, g)


def adamw(w, g, m, v, name):
    rows, cols = w.shape
    tr = rows
    for cand in (512, 256, 128, 64, 32, 16, 8):
        if rows % cand == 0 and rows > cand:
            tr = cand
            break

    def body(w_ref, g_ref, m_ref, v_ref, d_ref, mo_ref, vo_ref):
        gv = g_ref[...]
        mn = ADAM_B1 * m_ref[...] + (1.0 - ADAM_B1) * gv
        vn = ADAM_B2 * v_ref[...] + (1.0 - ADAM_B2) * (gv * gv)
        m_hat = mn / (1.0 - ADAM_B1 ** ADAM_STEP)
        v_hat = vn / (1.0 - ADAM_B2 ** ADAM_STEP)
        d_ref[...] = -ADAM_LR * (m_hat / (jnp.sqrt(v_hat) + ADAM_EPS) + ADAM_WD * w_ref[...])
        mo_ref[...] = mn
        vo_ref[...] = vn

    spec = pl.BlockSpec((tr, cols), lambda i: (i, 0))
    shape = jax.ShapeDtypeStruct((rows, cols), F32)
    return pl.pallas_call(
        body, name=name, grid=(rows // tr,),
        in_specs=[spec] * 4, out_specs=[spec] * 3, out_shape=[shape] * 3,
        compiler_params=_params("arbitrary"),
    )(w, g, m, v)


def adamw_layer(w, m, v, own, recv, core, layer, carried, name):
    n_layers, rows, cols = w.shape
    h = rows // 2

    def body(core_ref, w_ref, m_ref, v_ref, own_ref, recv_ref, *rest):
        g_ref, d_ref, mo_ref, vo_ref = rest[-4:]
        gv = jnp.where(pl.program_id(0) == core_ref[0], own_ref[...], recv_ref[...])
        mn = ADAM_B1 * m_ref[0] + (1.0 - ADAM_B1) * gv
        vn = ADAM_B2 * v_ref[0] + (1.0 - ADAM_B2) * (gv * gv)
        m_hat = mn / (1.0 - ADAM_B1 ** ADAM_STEP)
        v_hat = vn / (1.0 - ADAM_B2 ** ADAM_STEP)
        g_ref[0] = gv
        d_ref[0] = -ADAM_LR * (m_hat / (jnp.sqrt(v_hat) + ADAM_EPS) + ADAM_WD * w_ref[0])
        mo_ref[0] = mn
        vo_ref[0] = vn

    stacked = pl.BlockSpec((1, h, cols), lambda half, core_ref: (layer, half, 0))
    halfspec = pl.BlockSpec((h, cols), lambda half, core_ref: (0, 0))
    n_carried = 0 if carried is None else 4
    shape = jax.ShapeDtypeStruct(w.shape, F32)
    return pl.pallas_call(
        body, name=name,
        grid_spec=pltpu.PrefetchScalarGridSpec(
            num_scalar_prefetch=1, grid=(2,),
            in_specs=[stacked] * 3 + [halfspec] * 2 + [ANY] * n_carried, out_specs=[stacked] * 4),
        out_shape=[shape] * 4,
        input_output_aliases={6 + i: i for i in range(n_carried)},
        compiler_params=_params("arbitrary"),
    )(core, w, m, v, own, recv, *(carried or ()))


def add_halves(gs, ps, core, name, deps=()):
    n = len(gs)

    def body(core_ref, *refs):
        for i in range(n):
            refs[2 * n + i][...] = (refs[i][...].astype(F32) + refs[n + i][...].astype(F32)).astype(BF16)

    in_specs, out_specs, out_shape = [], [], []
    for gt in gs:
        h, c = gt.shape[1] // 2, gt.shape[2]
        in_specs.append(pl.BlockSpec((1, h, c), lambda b, core_ref: (b, core_ref[0], 0)))
    for gt in gs:
        h, c = gt.shape[1] // 2, gt.shape[2]
        in_specs.append(pl.BlockSpec((1, h, c), lambda b, core_ref: (b, 0, 0)))
        out_specs.append(pl.BlockSpec((1, h, c), lambda b, core_ref: (b, 0, 0)))
        out_shape.append(jax.ShapeDtypeStruct((N_CHIPS, h, c), BF16))
    in_specs += [ANY] * len(deps)
    return pl.pallas_call(
        _after(body, 1 + 2 * n, deps), name=name,
        grid_spec=pltpu.PrefetchScalarGridSpec(num_scalar_prefetch=1, grid=(N_CHIPS,), in_specs=in_specs, out_specs=out_specs),
        out_shape=out_shape, compiler_params=_params("arbitrary"),
    )(core, *gs, *ps, *deps)


def add_final(hs, qs, chip, name, deps=()):
    n = len(hs)

    def body(chip_ref, *refs):
        for i in range(n):
            q = refs[n + i]
            refs[2 * n + i][...] = ((refs[i][0].astype(F32) + q[0].astype(F32)) + q[1].astype(F32)) + q[2].astype(F32)

    in_specs, out_specs, out_shape = [], [], []
    for ht in hs:
        h, c = ht.shape[1], ht.shape[2]
        in_specs.append(pl.BlockSpec((1, h, c), lambda i, chip_ref: (chip_ref[0], 0, 0)))
    for ht in hs:
        h, c = ht.shape[1], ht.shape[2]
        in_specs.append(pl.BlockSpec((N_CHIPS - 1, h, c), lambda i, chip_ref: (0, 0, 0)))
        out_specs.append(pl.BlockSpec((h, c), lambda i, chip_ref: (0, 0)))
        out_shape.append(jax.ShapeDtypeStruct((h, c), F32))
    in_specs += [ANY] * len(deps)
    return pl.pallas_call(
        _after(body, 1 + 2 * n, deps), name=name,
        grid_spec=pltpu.PrefetchScalarGridSpec(num_scalar_prefetch=1, grid=(1,), in_specs=in_specs, out_specs=out_specs),
        out_shape=out_shape, compiler_params=_params("arbitrary"),
    )(chip, *hs, *qs, *deps)


def sum_devices(own, gathered, me, name):
    rows = own.shape[0]

    def body(me_ref, own_ref, g_ref, o_ref):
        me_dev = me_ref[0]
        acc = None
        for dev in range(8):
            slot = jnp.maximum((me_dev ^ dev) - 1, 0)
            term = jnp.where(me_dev == dev, own_ref[...], g_ref[slot])
            acc = term if acc is None else acc + term
        o_ref[...] = acc

    return pl.pallas_call(
        body, name=name,
        grid_spec=pltpu.PrefetchScalarGridSpec(
            num_scalar_prefetch=1, grid=(rows // 8,),
            in_specs=[pl.BlockSpec((8, D), lambda i, me_ref: (i, 0)), pl.BlockSpec((7, 8, D), lambda i, me_ref: (0, i, 0))],
            out_specs=pl.BlockSpec((8, D), lambda i, me_ref: (i, 0))),
        out_shape=jax.ShapeDtypeStruct((rows, D), F32),
        compiler_params=_params("arbitrary"),
    )(me, own, gathered)


def _mesh_pos():
    return lax.axis_index("x"), lax.axis_index("y"), lax.axis_index("c")


def _other_chips(x, y):
    return [(1 - x, y), (x, 1 - y), (1 - x, 1 - y)]


def _sibling():
    x, y, c = _mesh_pos()
    return [(x, y, 1 - c)]


def _same_core_of_other_chips():
    x, y, c = _mesh_pos()
    return [(cx, cy, c) for (cx, cy) in _other_chips(x, y)]


def _on_sequencer(body, name, operands, out_shapes, sems, peers, collective_id):
    def seq_body(*refs):
        barrier = pltpu.get_barrier_semaphore()
        with_whom = peers()
        for peer in with_whom:
            pl.semaphore_signal(barrier, inc=1, device_id=peer, device_id_type=MESH)
        pl.semaphore_wait(barrier, len(with_whom))
        body(*refs)

    return pl.kernel(
        seq_body, name=name, out_type=out_shapes,
        mesh=plsc.ScalarSubcoreMesh(axis_name="seq", num_cores=1),
        scratch_types=sems, compiler_params=pltpu.CompilerParams(collective_id=collective_id),
    )(*operands)


def all_gather_weights(shards, name):
    n = len(shards)

    def body(*refs):
        ins, outs = refs[:n], refs[n:2 * n]
        send, recv, fsend, frecv = refs[2 * n:]
        x, y, c = _mesh_pos()
        k = 2 * x + y
        chips = _other_chips(x, y)

        def half(ref, i, rows_half):
            h = shards[i].shape[0] // 2
            return ref.at[pl.ds(pl.multiple_of(rows_half * h, 8), h), :]

        first = []
        for i in range(n):
            for j, (cx, cy) in enumerate(chips):
                first.append(pltpu.make_async_remote_copy(
                    src_ref=half(ins[i], i, c), dst_ref=half(outs[i].at[k], i, c),
                    send_sem=send.at[i, j], recv_sem=recv.at[i, j], device_id=(cx, cy, c), device_id_type=MESH))
        for cp in first:
            cp.start()
        passed = []
        for i in range(n):
            for j, (cx, cy) in enumerate(chips):
                blk = half(outs[i].at[2 * cx + cy], i, c)
                pltpu.make_async_remote_copy(src_ref=blk, dst_ref=blk, send_sem=send.at[i, j], recv_sem=recv.at[i, j],
                                             device_id=(cx, cy, c), device_id_type=MESH).wait_recv()
                fw = pltpu.make_async_remote_copy(src_ref=blk, dst_ref=blk, send_sem=fsend.at[i, j], recv_sem=frecv.at[i, j],
                                                  device_id=(x, y, 1 - c), device_id_type=MESH)
                fw.start()
                passed.append(fw)
        for i in range(n):
            for j, (cx, cy) in enumerate(chips):
                blk = half(outs[i].at[2 * cx + cy], i, 1 - c)
                pltpu.make_async_remote_copy(src_ref=blk, dst_ref=blk, send_sem=fsend.at[i, j], recv_sem=frecv.at[i, j],
                                             device_id=(x, y, 1 - c), device_id_type=MESH).wait_recv()
        for cp in first + passed:
            cp.wait_send()

    def peers():
        x, y, c = _mesh_pos()
        return [(cx, cy, c) for (cx, cy) in _other_chips(x, y)] + [(x, y, 1 - c)]

    return _on_sequencer(
        body, name, shards, [jax.ShapeDtypeStruct((N_CHIPS,) + s.shape, s.dtype) for s in shards],
        [pltpu.SemaphoreType.DMA((n, 3))] * 4, peers, GATHER_COLLECTIVE_ID)


def place_own(gathered, shards, chip, name):
    n = len(shards)

    def body(chip_ref, *refs):
        for i in range(n):
            refs[2 * n + i][0] = refs[i][...]

    in_specs = [pl.BlockSpec(s.shape, lambda i, chip_ref: (0, 0)) for s in shards] + [ANY] * n
    out_specs = [pl.BlockSpec((1,) + s.shape, lambda i, chip_ref: (chip_ref[0], 0, 0)) for s in shards]
    return pl.pallas_call(
        body, name=name,
        grid_spec=pltpu.PrefetchScalarGridSpec(num_scalar_prefetch=1, grid=(1,), in_specs=in_specs, out_specs=out_specs),
        out_shape=[jax.ShapeDtypeStruct(g.shape, g.dtype) for g in gathered],
        input_output_aliases={1 + n + i: i for i in range(n)},
        compiler_params=_params("arbitrary"),
    )(chip, *shards, *gathered)


def all_gather_rows(shard, name):
    def body(in_ref, out_ref, send, recv, lsem):
        x, y, c = _mesh_pos()
        k = 2 * x + y
        chips = _other_chips(x, y)
        local = pltpu.make_async_copy(in_ref, out_ref.at[k], lsem)
        local.start()
        sent = [pltpu.make_async_remote_copy(src_ref=in_ref, dst_ref=out_ref.at[k], send_sem=send.at[j], recv_sem=recv.at[j],
                                             device_id=(cx, cy, c), device_id_type=MESH) for j, (cx, cy) in enumerate(chips)]
        for cp in sent:
            cp.start()
        for j, (cx, cy) in enumerate(chips):
            blk = out_ref.at[2 * cx + cy]
            pltpu.make_async_remote_copy(src_ref=blk, dst_ref=blk, send_sem=send.at[j], recv_sem=recv.at[j],
                                         device_id=(cx, cy, c), device_id_type=MESH).wait_recv()
        for cp in sent:
            cp.wait_send()
        local.wait()

    return pl.pallas_call(
        body, name=name, in_specs=[ANY], out_specs=ANY,
        out_shape=jax.ShapeDtypeStruct((N_CHIPS,) + shard.shape, shard.dtype),
        scratch_shapes=[pltpu.SemaphoreType.DMA((3,)), pltpu.SemaphoreType.DMA((3,)), pltpu.SemaphoreType.DMA],
    )(shard)


def swap_halves(gs, name):
    n = len(gs)

    def body(*refs):
        ins, outs = refs[:n], refs[n:2 * n]
        send, recv = refs[2 * n:]
        x, y, c = _mesh_pos()
        sent = []
        for i in range(n):
            h = gs[i].shape[1] // 2
            src = ins[i].at[:, pl.ds(pl.multiple_of((1 - c) * h, 8), h), :]
            cp = pltpu.make_async_remote_copy(src_ref=src, dst_ref=outs[i], send_sem=send.at[i], recv_sem=recv.at[i],
                                              device_id=(x, y, 1 - c), device_id_type=MESH)
            cp.start()
            sent.append(cp)
        for cp in sent:
            cp.wait()

    return _on_sequencer(
        body, name, gs, [jax.ShapeDtypeStruct((N_CHIPS, g.shape[1] // 2, g.shape[2]), g.dtype) for g in gs],
        [pltpu.SemaphoreType.DMA((n,)), pltpu.SemaphoreType.DMA((n,))], _sibling, SIBLING_COLLECTIVE_ID)


def scatter_chips(hs, name):
    n = len(hs)

    def body(*refs):
        ins, outs = refs[:n], refs[n:2 * n]
        send, recv = refs[2 * n:]
        x, y, c = _mesh_pos()
        chips = _other_chips(x, y)
        sent = []
        for i in range(n):
            for j, (cx, cy) in enumerate(chips):
                cp = pltpu.make_async_remote_copy(src_ref=ins[i].at[2 * cx + cy], dst_ref=outs[i].at[j],
                                                  send_sem=send.at[i, j], recv_sem=recv.at[i, j],
                                                  device_id=(cx, cy, c), device_id_type=MESH)
                cp.start()
                sent.append(cp)
        for cp in sent:
            cp.wait()

    return _on_sequencer(
        body, name, hs, [jax.ShapeDtypeStruct((N_CHIPS - 1,) + h.shape[1:], h.dtype) for h in hs],
        [pltpu.SemaphoreType.DMA((n, 3)), pltpu.SemaphoreType.DMA((n, 3))], _same_core_of_other_chips, CHIPS_COLLECTIVE_ID)


def swap_reduced(rs, name):
    n = len(rs)

    def body(*refs):
        ins, outs = refs[:n], refs[n:2 * n]
        send, recv = refs[2 * n:]
        x, y, c = _mesh_pos()
        sent = []
        for i in range(n):
            cp = pltpu.make_async_remote_copy(src_ref=ins[i], dst_ref=outs[i], send_sem=send.at[i], recv_sem=recv.at[i],
                                              device_id=(x, y, 1 - c), device_id_type=MESH)
            cp.start()
            sent.append(cp)
        for cp in sent:
            cp.wait()

    return _on_sequencer(
        body, name, rs, [jax.ShapeDtypeStruct(r.shape, r.dtype) for r in rs],
        [pltpu.SemaphoreType.DMA((n,)), pltpu.SemaphoreType.DMA((n,))], _sibling, SIBLING_COLLECTIVE_ID)


def all_gather_devices(part, name):
    def everyone_else():
        x, y, c = _mesh_pos()
        return [(1 - x if (rel >> 2) & 1 else x, 1 - y if (rel >> 1) & 1 else y, 1 - c if rel & 1 else c) for rel in range(1, 8)]

    def body(in_ref, out_ref, send, recv):
        sent = []
        for slot, peer in enumerate(everyone_else()):
            cp = pltpu.make_async_remote_copy(src_ref=in_ref, dst_ref=out_ref.at[slot], send_sem=send.at[slot],
                                              recv_sem=recv.at[slot], device_id=peer, device_id_type=MESH)
            cp.start()
            sent.append(cp)
        for cp in sent:
            cp.wait()

    return _on_sequencer(
        body, name, [part], jax.ShapeDtypeStruct((7,) + part.shape, part.dtype),
        [pltpu.SemaphoreType.DMA((7,)), pltpu.SemaphoreType.DMA((7,))], everyone_else, ALL_COLLECTIVE_ID)


class ReduceScatter:
    def __init__(self, grads, core, chip, name):
        self.grads, self.core, self.chip, self.name = grads, core, chip, name
        self.from_sibling = swap_halves(grads, name + "_swap")

    def sum_cores(self, deps=()):
        self.core_sums = add_halves(self.grads, self.from_sibling, self.core, self.name + "_add2", deps)
        self.from_chips = scatter_chips(self.core_sums, self.name + "_scatter")
        return self.core_sums[0]

    def sum_chips(self, deps=()):
        self.half_sums = add_final(self.core_sums, self.from_chips, self.chip, self.name + "_add4", deps)
        self.other_half = swap_reduced(self.half_sums, self.name + "_join")
        return self.half_sums[0]


def _blocked(w):
    return w.reshape(w.shape[0] * w.shape[1], w.shape[2])


def _grp_from_blocks(w):
    return w.reshape(N_CHIPS, 4, 64, C_GROUP_DIM).transpose(1, 0, 2, 3).reshape(4, C_GROUP_DIM, C_GROUP_DIM)


def _grp_to_blocks(w):
    return w.reshape(4, N_CHIPS, 64, C_GROUP_DIM).transpose(1, 0, 2, 3).reshape(N_CHIPS, C_GROUP_DIM, C_GROUP_DIM)


def _dw_cols(h, dact, hs, name, deps):
    tm = 512
    return mm_tn(h[None], dact, (N_CHIPS, D, hs), tm, hs, N_CHIPS * (D // tm),
                 lambda j: (0, j % 2), lambda j: (0, j // 2), lambda j: (j // 2, j % 2, 0), name, deps)


def _dw_rows(y, dxm, name, deps):
    tm = 512
    out = mm_tn(y[None], dxm[None], (1, D, D), tm, D, D // tm, lambda j: (0, j), lambda j: (0, 0), lambda j: (0, j, 0), name, deps)
    return out.reshape(N_CHIPS, D // N_CHIPS, D)


def _dw_hidden(act, other, name, deps):
    tn = 512
    return mm_tn(act, other[None], (N_CHIPS, FFN_SHARD, D), FFN_SHARD, tn, N_CHIPS * (D // tn),
                 lambda j: (j // 2, 0), lambda j: (0, j % 2), lambda j: (j // 2, 0, j % 2), name, deps)


def kernel(x, norm_mix_g, norm_ffn_g, final_norm_g, a_w_in, a_v_norm_g, a_w_s, a_b_s, a_w_out, b_w_in, b_conv_w, b_w_out, c_w_in, c_w_grp, c_scale, c_w_out, f_w_gate, f_w_up, f_w_down, loss_target, m_norm_mix_g, m_norm_ffn_g, m_final_norm_g, m_a_w_in, m_a_v_norm_g, m_a_w_s, m_a_b_s, m_a_w_out, m_b_w_in, m_b_conv_w, m_b_w_out, m_c_w_in, m_c_w_grp, m_c_scale, m_c_w_out, m_f_w_gate, m_f_w_up, m_f_w_down, v_norm_mix_g, v_norm_ffn_g, v_final_norm_g, v_a_w_in, v_a_v_norm_g, v_a_w_s, v_a_b_s, v_a_w_out, v_b_w_in, v_b_conv_w, v_b_w_out, v_c_w_in, v_c_w_grp, v_c_scale, v_c_w_out, v_f_w_gate, v_f_w_up, v_f_w_down):
    n_ex, seq, _ = x.shape
    t = n_ex * seq
    xi, yi, ci = lax.axis_index("x"), lax.axis_index("y"), lax.axis_index("c")
    chip = (2 * xi + yi).astype(jnp.int32)
    core_arr = ci.astype(jnp.int32).reshape(1)
    chip_arr = chip.reshape(1)
    bf = lambda w: w.astype(BF16)

    pad8 = lambda v: jnp.pad(v, ((0, 8 - v.shape[0]), (0, 0)))
    small_rows = jnp.concatenate([pad8(a_v_norm_g), pad8(b_conv_w[0]), pad8(c_scale)], axis=0)
    small_full = all_gather_rows(small_rows, "ag_small").transpose(1, 0, 2).reshape(24, D)
    gv_full = [small_full[0:1], small_full[1:2]]
    cw_full = small_full[8:11]
    scale_full = small_full[16:17]

    mixer_shards = [
        [bf(a_w_in[0]), bf(a_w_out[0])],
        [bf(b_w_in[0]), bf(b_w_out[0])],
        [bf(c_w_in[0]), bf(c_w_grp[0]).reshape(C_GROUP_DIM, C_GROUP_DIM), bf(c_w_out[0])],
        [bf(a_w_in[1]), bf(a_w_out[1])],
    ]
    hidden_major = lambda w: jnp.swapaxes(w, 1, 2)
    gate_t, up_t = hidden_major(f_w_gate), hidden_major(f_w_up)
    gathered = []
    for i in range(4):
        shards = mixer_shards[i] + [bf(gate_t[i]), bf(up_t[i]), bf(f_w_down[i])]
        gathered.append(place_own(all_gather_weights(shards, f"ag_l{i}"), shards, chip_arr, f"own_l{i}"))

    mask = (jnp.arange(GMLP_BLOCK)[None, :] // 64) <= (jnp.arange(GMLP_BLOCK)[:, None] // 64)
    gmix = [norm_mix_g[i:i + 1] for i in range(4)]
    gffn = [norm_ffn_g[i:i + 1] for i in range(4)]
    a_chunks = [(s, 0, 512, 0, s * 512) for s in range(N_CHIPS)]
    b_chunks = [(j // 3, (j % 3) * 256, 256, j // 4, (j % 4) * 256) for j in range(12)]
    c_chunks = [(0, 0, D, 0, 0)]

    xs = [x.reshape(t, D)]
    saved = []
    for i in range(4):
        ws = gathered[i]
        wg, wu, wd = ws[-3], ws[-2], ws[-1]
        xin = xs[-1]
        if i in (0, 3):
            j = 0 if i == 0 else 1
            win, wout = ws[0], _blocked(ws[1])
            wm32 = jnp.where(mask[None], a_w_s[j], 0.0)
            wm, wmt = bf(wm32), bf(wm32.transpose(0, 2, 1))
            bs = jnp.broadcast_to(a_b_s[j][:, :, None], (A_GROUPS, GMLP_BLOCK, 128))
            h, z = norm_mm(xin, gmix[i], win, a_chunks, 1, 2 * D, f"a_in_l{i}")
            y, vn = a_mid_fwd(z[0], gv_full[j], wm, bs, f"a_mid_l{i}")
            xmid = mm_res(y, wout, xin, f"a_out_l{i}")
            saved.append(dict(h=h, z=z, y=y, vn=vn, win=win, wout=wout, wm=wm, wmt=wmt, bs=bs, gv=gv_full[j]))
        elif i == 1:
            win, wout = ws[0], _blocked(ws[1])
            h, p3 = norm_mm(xin, gmix[i], win, b_chunks, 3, D, "b_in")
            y = b_conv_fwd(p3, cw_full, seq, "b_conv")
            xmid = mm_res(y, wout, xin, "b_out")
            saved.append(dict(h=h, p3=p3, y=y, win=win, wout=wout))
        else:
            win, wgrp, wout = _blocked(ws[0])[None], _grp_from_blocks(ws[1]), _blocked(ws[2])
            h, p = norm_mm(xin, gmix[i], win, c_chunks, 1, D, "c_in")
            dpool = c_pool_fwd(p[0], seq, "c_pool")
            y, xmid = c_out_fwd(dpool, wgrp, scale_full, wout, xin, "c_out")
            saved.append(dict(h=h, d=dpool, y=y, win=win, wgrp=wgrp, wout=wout))
        h2, fa, fb, fs, xout = ffn_fwd(xmid, gffn[i], wg, wu, wd, f"ffn_l{i}")
        saved[-1].update(h2=h2, fa=fa, fb=fb, fs=fs, xmid=xmid, wg=wg, wu=wu, wd=wd)
        xs.append(xout)

    dx, dg_final, loss_part = loss_head(xs[4], loss_target.reshape(t, D), final_norm_g[None], "loss_head")
    loss = lax.psum(loss_part[0, 0], ("x", "y", "c"))

    weights = dict(norm_mix_g=norm_mix_g, norm_ffn_g=norm_ffn_g, final_norm_g=final_norm_g, a_w_in=a_w_in, a_v_norm_g=a_v_norm_g,
                   a_w_s=a_w_s, a_b_s=a_b_s, a_w_out=a_w_out, b_w_in=b_w_in, b_conv_w=b_conv_w, b_w_out=b_w_out, c_w_in=c_w_in,
                   c_w_grp=c_w_grp, c_scale=c_scale, c_w_out=c_w_out, f_w_gate=f_w_gate, f_w_up=f_w_up, f_w_down=f_w_down)
    m_in = dict(norm_mix_g=m_norm_mix_g, norm_ffn_g=m_norm_ffn_g, final_norm_g=m_final_norm_g, a_w_in=m_a_w_in, a_v_norm_g=m_a_v_norm_g,
                a_w_s=m_a_w_s, a_b_s=m_a_b_s, a_w_out=m_a_w_out, b_w_in=m_b_w_in, b_conv_w=m_b_conv_w, b_w_out=m_b_w_out, c_w_in=m_c_w_in,
                c_w_grp=m_c_w_grp, c_scale=m_c_scale, c_w_out=m_c_w_out, f_w_gate=m_f_w_gate, f_w_up=m_f_w_up, f_w_down=m_f_w_down)
    v_in = dict(norm_mix_g=v_norm_mix_g, norm_ffn_g=v_norm_ffn_g, final_norm_g=v_final_norm_g, a_w_in=v_a_w_in, a_v_norm_g=v_a_v_norm_g,
                a_w_s=v_a_w_s, a_b_s=v_a_b_s, a_w_out=v_a_w_out, b_w_in=v_b_w_in, b_conv_w=v_b_conv_w, b_w_out=v_b_w_out, c_w_in=v_c_w_in,
                c_w_grp=v_c_w_grp, c_scale=v_c_scale, c_w_out=v_c_w_out, f_w_gate=v_f_w_gate, f_w_up=v_f_w_up, f_w_down=v_f_w_down)
    grp_rows = lambda a: a.reshape(1, C_GROUP_DIM, C_GROUP_DIM)
    same = lambda a: a
    to_stacked = {nme: same for nme in ("a_w_in", "a_w_out", "b_w_in", "b_w_out", "c_w_in", "c_w_out", "f_w_down")}
    to_stacked.update(f_w_gate=hidden_major, f_w_up=hidden_major, c_w_grp=grp_rows)
    from_stacked = dict(to_stacked, c_w_grp=lambda a: a.reshape(c_w_grp.shape))
    layer_tensors = {0: ["a_w_out", "a_w_in"], 1: ["b_w_out", "b_w_in"], 2: ["c_w_out", "c_w_in", "c_w_grp"], 3: ["a_w_out", "a_w_in"]}
    carried = {}

    def update_layer(i):
        for pos, nme in enumerate(layer_tensors[i] + ["f_w_gate", "f_w_up", "f_w_down"]):
            stacked_layer = i if nme.startswith("f_") else (i // 3 if nme.startswith("a_") else 0)
            view = to_stacked[nme]
            carried[nme] = adamw_layer(view(weights[nme]), view(m_in[nme]), view(v_in[nme]), rs[i].half_sums[pos], rs[i].other_half[pos],
                                       core_arr, stacked_layer, carried.get(nme), f"adamw_{nme}_l{i}")

    dg_mix, dg_ffn = [None] * 4, [None] * 4
    rs = [None] * 4
    small = {}
    for i in (3, 2, 1, 0):
        sv = saved[i]
        xin = xs[i]
        newer = rs[i + 1] if i < 3 else None
        older = rs[i + 2] if i < 2 else None
        deps = ([newer.grads[0]] if newer else []) + ([older.half_sums[0]] if older else [])
        dxm, da, db, dg_ffn[i] = ffn_bwd(dx, sv["fa"], sv["fb"], sv["xmid"], gffn[i], sv["wg"], sv["wu"], sv["wd"], f"ffn_bwd_l{i}", deps)
        last = [newer.sum_cores([dxm])] if newer else []
        g_gate = _dw_hidden(da, sv["h2"], f"dw_gate_l{i}", last)
        g_up = _dw_hidden(db, sv["h2"], f"dw_up_l{i}", [g_gate])
        g_down = _dw_hidden(sv["fs"], dx, f"dw_down_l{i}", [g_up])
        if i in (0, 3):
            j = 0 if i == 0 else 1
            dy = mm_nt(dxm, sv["wout"], f"a_dy_l{i}")
            dz, dwm, dbs, dgv = a_mid_bwd(dy, sv["z"][0], sv["vn"], sv["gv"], sv["wm"], sv["wmt"], sv["bs"], f"a_mid_bwd_l{i}")
            dz = dz[None]
            dx, dg_mix[i] = bwd_in(dz, sv["win"], a_chunks, xin, gmix[i], dxm, f"a_bwd_in_l{i}")
            g_in = _dw_cols(sv["h"], dz, 512, f"dw_a_in_l{i}", [g_down])
            g_out = _dw_rows(sv["y"], dxm, f"dw_a_out_l{i}", [g_in])
            small[f"wm{j}"], small[f"bs{j}"], small[f"gv{j}"] = dwm, dbs, dgv
            mixer_grads = [g_out, g_in]
        elif i == 1:
            dy = mm_nt(dxm, sv["wout"], "b_dy")
            dp3, small["cw"] = b_conv_bwd(dy, sv["p3"], cw_full, seq, "b_conv_bwd")
            dx, dg_mix[i] = bwd_in(dp3, sv["win"], b_chunks, xin, gmix[i], dxm, "b_bwd_in")
            g_in = mm_tn(sv["h"][None], dp3, (N_CHIPS, D, 768), 512, 256, 24,
                         lambda j: (0, j % 2), lambda j: ((j // 2) // 4, (j // 2) % 4),
                         lambda j: ((j // 2) // 3, j % 2, (j // 2) % 3), "dw_b_in", [g_down])
            g_out = _dw_rows(sv["y"], dxm, "dw_b_out", [g_in])
            mixer_grads = [g_out, g_in]
        else:
            outs = c_out_bwd(dxm, sv["d"], sv["wgrp"], scale_full, sv["wout"], "c_out_bwd")
            dyp, dd, small["scale"] = outs[0], list(outs[1:5]), outs[5]
            dpool = c_pool_bwd(dd, seq, "c_pool_bwd")
            dp = jnp.concatenate(dpool, axis=1)[None]
            dx, dg_mix[i] = bwd_in(dp, sv["win"], c_chunks, xin, gmix[i], dxm, "c_bwd_in")
            g_in = _dw_rows(sv["h"], dp[0], "dw_c_in", [g_down])
            dcat = jnp.concatenate(sv["d"], axis=1)
            g_grp = mm_tn(dcat[None], dyp[None], (4, C_GROUP_DIM, C_GROUP_DIM), C_GROUP_DIM, C_GROUP_DIM, 4,
                          lambda j: (0, j), lambda j: (0, j), lambda j: (j, 0, 0), "dw_c_grp", [g_in])
            g_out = _dw_rows(sv["y"], dxm, "dw_c_out", [g_grp])
            mixer_grads = [g_out, g_in, _grp_to_blocks(g_grp)]
        rs[i] = ReduceScatter(mixer_grads + [g_gate, g_up, g_down], core_arr, chip_arr, f"rs_l{i}")
        if newer:
            newer.sum_chips([mixer_grads[0]])
        if older:
            update_layer(i + 2)
    rs[0].sum_cores()
    update_layer(1)
    rs[0].sum_chips()
    grad_x = dx.reshape(n_ex, seq, D)

    def bs_rows(v):
        return jnp.pad(v[:, :, 0].reshape(1, D), ((0, 7), (0, 0)))

    parts = dg_mix + dg_ffn + [dg_final, small["gv0"], small["gv1"], small["cw"], small["scale"],
                               small["wm0"].reshape(128, D), small["wm1"].reshape(128, D), bs_rows(small["bs0"]), bs_rows(small["bs1"])]
    packed = jnp.concatenate(parts, axis=0)
    me_arr = (4 * xi + 2 * yi + ci).astype(jnp.int32).reshape(1)
    total = sum_devices(packed, all_gather_devices(packed, "ag_small_grads"), me_arr, "sum_small_grads")
    update_layer(0)
    first_row = lambda lo, n: total[lo:lo + 8 * n].reshape(n, 8, D)[:, 0]
    g_norm_mix = first_row(0, 4)
    g_norm_ffn = first_row(32, 4)
    g_final = total[64]
    g_gv = first_row(72, 2)
    g_cw = total[88:91]
    g_scale = total[96:97]
    g_ws = jnp.where(mask[None, None], total[104:360].reshape(2, A_GROUPS, 128, 128), 0.0)
    g_bs = first_row(360, 2).reshape(2, A_GROUPS, 128)
    col0 = chip * (D // N_CHIPS)
    cols = lambda v: lax.dynamic_slice_in_dim(v, col0, D // N_CHIPS, axis=1)

    small_grads = {
        "norm_mix_g": g_norm_mix, "norm_ffn_g": g_norm_ffn, "final_norm_g": g_final, "a_v_norm_g": cols(g_gv), "a_w_s": g_ws,
        "a_b_s": g_bs, "b_conv_w": cols(g_cw)[None], "c_scale": cols(g_scale),
    }
    results = {}
    for nme, g in small_grads.items():
        w = weights[nme]
        flat = lambda a: a.reshape(-1, w.shape[-1])
        dl, mn, vn = adamw(flat(w), flat(g), flat(m_in[nme]), flat(v_in[nme]), f"adamw_{nme}")
        results[nme] = tuple(o.reshape(w.shape) for o in (g, dl, mn, vn))
    for nme, outs in carried.items():
        results[nme] = tuple(from_stacked[nme](o) for o in outs)

    names = list(weights)
    return (loss, grad_x, *[results[n][0] for n in names], *[results[n][1] for n in names],
            *[results[n][2] for n in names], *[results[n][3] for n in names])
```

```python
import jax
import jax.numpy as jnp
from jax import lax
from jax.experimental import pallas as pl
from jax.experimental.pallas import tpu as pltpu
from jax.experimental.pallas import tpu_sc as plsc

F32 = jnp.float32
BF16 = jnp.bfloat16
D = 1024
FFN_SHARD = 704
GMLP_BLOCK = 128
A_GROUPS = 8
POOL_WINDOWS = (2, 4, 8, 16)
C_GROUP_DIM = 256
N_CHIPS = 4
EPS = 1e-6
ADAM_LR, ADAM_B1, ADAM_B2, ADAM_EPS, ADAM_WD, ADAM_STEP = 0.001, 0.9, 0.999, 1e-08, 0.01, 10
VMEM_LIMIT_BYTES = 56 * 1024 * 1024
FFN_FWD_ROW_GROUPS = 1
FFN_BWD_ROW_GROUPS = 2
MESH = pl.DeviceIdType.MESH
GATHER_COLLECTIVE_ID = 1
SIBLING_COLLECTIVE_ID = 2
CHIPS_COLLECTIVE_ID = 3
ALL_COLLECTIVE_ID = 4
ANY = pl.BlockSpec(memory_space=pl.ANY)
NT_DIMS = (((1,), (1,)), ((), ()))
TN_DIMS = (((0,), (0,)), ((), ()))
INV_SQRT2 = 0.7071067811865476
INV_SQRT_2PI = 0.3989422804014327


def _params(*semantics):
    return pltpu.CompilerParams(dimension_semantics=semantics, vmem_limit_bytes=VMEM_LIMIT_BYTES)


def _dot(a, b):
    return jnp.dot(a, b, preferred_element_type=F32)


def _dot_nt(a, b):
    return lax.dot_general(a, b, NT_DIMS, preferred_element_type=F32)


def _rms(x):
    r = lax.rsqrt(jnp.mean(x * x, axis=-1, keepdims=True) + EPS)
    return x * r, r


def _rms_bwd(x, g, dh):
    xh, r = _rms(x)
    dxh = dh * g
    dx = r * (dxh - xh * jnp.mean(dxh * xh, axis=-1, keepdims=True))
    return dx, jnp.sum(dh * xh, axis=0, keepdims=True)


def _gelu(x):
    return 0.5 * x * (1.0 + lax.erf(x * INV_SQRT2))


def _gelu_grad(x):
    return 0.5 * (1.0 + lax.erf(x * INV_SQRT2)) + x * jnp.exp(-0.5 * x * x) * INV_SQRT_2PI


def _shift_down(v, s, row):
    return jnp.where(row >= s, pltpu.roll(v, s, 0), 0.0)


def _shift_up(v, s, row):
    n = v.shape[0]
    return jnp.where(row < n - s, pltpu.roll(v, n - s, 0), 0.0)


def _row_tile(t, want):
    return want if t % want == 0 else t


def _after(body, first, deps):
    if not deps:
        return body

    def ordered(*refs):
        return body(*refs[:first], *refs[first + len(deps):])

    return ordered


def norm_mm(x, g, w, chunks, n_parts, part_width, name):
    t = x.shape[0]
    tm = _row_tile(t, 512)
    n_shards, _, hs = w.shape

    def body(x_ref, g_ref, w_ref, h_ref, p_ref):
        xh, _ = _rms(x_ref[...])
        h = (xh * g_ref[...]).astype(BF16)
        h_ref[...] = h
        for s in range(n_shards):
            res = _dot(h, w_ref[s]).astype(BF16)
            for (cs, wc, width, part, pc) in chunks:
                if cs == s:
                    p_ref[part, :, pc:pc + width] = res[:, wc:wc + width]

    return pl.pallas_call(
        body, name=name, grid=(t // tm,),
        in_specs=[pl.BlockSpec((tm, D), lambda i: (i, 0)), pl.BlockSpec((1, D), lambda i: (0, 0)),
                  pl.BlockSpec((n_shards, D, hs), lambda i: (0, 0, 0))],
        out_specs=[pl.BlockSpec((tm, D), lambda i: (i, 0)), pl.BlockSpec((n_parts, tm, part_width), lambda i: (0, i, 0))],
        out_shape=[jax.ShapeDtypeStruct((t, D), BF16), jax.ShapeDtypeStruct((n_parts, t, part_width), BF16)],
        compiler_params=_params("arbitrary"),
    )(x, g, w)


def mm_res(a, w, res, name):
    t, k = a.shape
    n = w.shape[1]
    tm = _row_tile(t, 512)

    def body(a_ref, w_ref, r_ref, o_ref):
        o_ref[...] = r_ref[...] + _dot(a_ref[...], w_ref[...])

    return pl.pallas_call(
        body, name=name, grid=(t // tm,),
        in_specs=[pl.BlockSpec((tm, k), lambda i: (i, 0)), pl.BlockSpec((k, n), lambda i: (0, 0)),
                  pl.BlockSpec((tm, n), lambda i: (i, 0))],
        out_specs=pl.BlockSpec((tm, n), lambda i: (i, 0)),
        out_shape=jax.ShapeDtypeStruct((t, n), F32),
        compiler_params=_params("arbitrary"),
    )(a, w, res)


def mm_nt(a, w, name):
    t, n = a.shape
    k = w.shape[0]
    tm = _row_tile(t, 512)

    def body(a_ref, w_ref, o_ref):
        o_ref[...] = _dot_nt(a_ref[...].astype(BF16), w_ref[...]).astype(BF16)

    return pl.pallas_call(
        body, name=name, grid=(t // tm,),
        in_specs=[pl.BlockSpec((tm, n), lambda i: (i, 0)), pl.BlockSpec((k, n), lambda i: (0, 0))],
        out_specs=pl.BlockSpec((tm, k), lambda i: (i, 0)),
        out_shape=jax.ShapeDtypeStruct((t, k), BF16),
        compiler_params=_params("arbitrary"),
    )(a, w)


def bwd_in(dp, w, chunks, x, g, dres, name):
    n_parts, t, part_width = dp.shape
    n_shards, _, hs = w.shape
    tm = _row_tile(t, 512)

    def body(dp_ref, w_ref, x_ref, g_ref, dres_ref, dx_ref, dg_ref):
        acc = jnp.zeros((tm, D), F32)
        for (cs, wc, width, part, pc) in chunks:
            acc = acc + _dot_nt(dp_ref[part, :, pc:pc + width], w_ref[cs, :, wc:wc + width])
        dx, dg = _rms_bwd(x_ref[...], g_ref[...], acc)
        dx_ref[...] = dres_ref[...] + dx

        @pl.when(pl.program_id(0) == 0)
        def _():
            dg_ref[...] = jnp.zeros_like(dg_ref)

        dg_ref[0:1, :] += dg

    return pl.pallas_call(
        body, name=name, grid=(t // tm,),
        in_specs=[pl.BlockSpec((n_parts, tm, part_width), lambda i: (0, i, 0)),
                  pl.BlockSpec((n_shards, D, hs), lambda i: (0, 0, 0)),
                  pl.BlockSpec((tm, D), lambda i: (i, 0)), pl.BlockSpec((1, D), lambda i: (0, 0)),
                  pl.BlockSpec((tm, D), lambda i: (i, 0))],
        out_specs=[pl.BlockSpec((tm, D), lambda i: (i, 0)), pl.BlockSpec((8, D), lambda i: (0, 0))],
        out_shape=[jax.ShapeDtypeStruct((t, D), F32), jax.ShapeDtypeStruct((8, D), F32)],
        compiler_params=_params("arbitrary"),
    )(dp, w, x, g, dres)


def mm_tn(a, b, out_shape, tm, tn, n_tiles, a_idx, b_idx, o_idx, name, deps=()):
    t = a.shape[1]

    def body(a_ref, b_ref, o_ref):
        o_ref[0] = lax.dot_general(a_ref[0].astype(BF16), b_ref[0].astype(BF16), TN_DIMS, preferred_element_type=F32).astype(BF16)

    return pl.pallas_call(
        _after(body, 2, deps), name=name, grid=(n_tiles,),
        in_specs=[pl.BlockSpec((1, t, tm), lambda j: (a_idx(j)[0], 0, a_idx(j)[1])),
                  pl.BlockSpec((1, t, tn), lambda j: (b_idx(j)[0], 0, b_idx(j)[1]))] + [ANY] * len(deps),
        out_specs=pl.BlockSpec((1, tm, tn), lambda j: o_idx(j)),
        out_shape=jax.ShapeDtypeStruct(out_shape, BF16),
        compiler_params=_params("arbitrary"),
    )(a, b, *deps)


def ffn_fwd(x, g, wg, wu, wd, name):
    t = x.shape[0]
    tm = _row_tile(t, 512)
    hs = wg.shape[1]

    def body(x_ref, g_ref, wg_ref, wu_ref, wd_ref, h_ref, a_ref, b_ref, s_ref, o_ref, acc_ref):
        kk = pl.program_id(1)

        @pl.when(kk == 0)
        def _():
            xh, _ = _rms(x_ref[...])
            h_ref[...] = (xh * g_ref[...]).astype(BF16)
            acc_ref[...] = jnp.zeros_like(acc_ref)

        for r in range(FFN_FWD_ROW_GROUPS):
            rows = pl.ds(r * (tm // FFN_FWD_ROW_GROUPS), tm // FFN_FWD_ROW_GROUPS)
            h = h_ref[rows, :]
            a = _dot_nt(h, wg_ref[0])
            b = _dot_nt(h, wu_ref[0])
            sig = jax.nn.sigmoid(a)
            silu = a * sig
            s = (silu * b).astype(BF16)
            a_ref[0, rows, :] = (b * (sig * (1.0 + a * (1.0 - sig)))).astype(BF16)
            b_ref[0, rows, :] = silu.astype(BF16)
            s_ref[0, rows, :] = s
            acc_ref[rows, :] += _dot(s, wd_ref[0])

        @pl.when(kk == N_CHIPS - 1)
        def _():
            o_ref[...] = x_ref[...] + acc_ref[...]

    act = pl.BlockSpec((1, tm, hs), lambda i, kk: (kk, i, 0))
    act_shape = jax.ShapeDtypeStruct((N_CHIPS, t, hs), BF16)
    wspec = pl.BlockSpec((1, hs, D), lambda i, kk: (kk, 0, 0))
    return pl.pallas_call(
        body, name=name, grid=(t // tm, N_CHIPS),
        in_specs=[pl.BlockSpec((tm, D), lambda i, kk: (i, 0)), pl.BlockSpec((1, D), lambda i, kk: (0, 0)), wspec, wspec, wspec],
        out_specs=[pl.BlockSpec((tm, D), lambda i, kk: (i, 0)), act, act, act, pl.BlockSpec((tm, D), lambda i, kk: (i, 0))],
        out_shape=[jax.ShapeDtypeStruct((t, D), BF16), act_shape, act_shape, act_shape, jax.ShapeDtypeStruct((t, D), F32)],
        scratch_shapes=[pltpu.VMEM((tm, D), F32)],
        compiler_params=_params("arbitrary", "arbitrary"),
    )(x, g, wg, wu, wd)


def ffn_bwd(dxo, a, b, x, g, wg, wu, wd, name, deps=()):
    t = x.shape[0]
    tm = _row_tile(t, 512)
    hs = wg.shape[1]

    def body(dxo_ref, a_ref, b_ref, x_ref, g_ref, wg_ref, wu_ref, wd_ref, dx_ref, da_ref, db_ref, dg_ref, acc_ref):
        i, kk = pl.program_id(0), pl.program_id(1)

        @pl.when(kk == 0)
        def _():
            acc_ref[...] = jnp.zeros_like(acc_ref)

        @pl.when((kk == 0) & (i == 0))
        def _():
            dg_ref[...] = jnp.zeros_like(dg_ref)

        for r in range(FFN_BWD_ROW_GROUPS):
            rows = pl.ds(r * (tm // FFN_BWD_ROW_GROUPS), tm // FFN_BWD_ROW_GROUPS)
            ds = _dot_nt(dxo_ref[rows, :].astype(BF16), wd_ref[0])
            da = (ds * a_ref[0, rows, :].astype(F32)).astype(BF16)
            db = (ds * b_ref[0, rows, :].astype(F32)).astype(BF16)
            da_ref[0, rows, :] = da
            db_ref[0, rows, :] = db
            acc_ref[rows, :] += _dot(da, wg_ref[0]) + _dot(db, wu_ref[0])

        @pl.when(kk == N_CHIPS - 1)
        def _():
            dx, dg = _rms_bwd(x_ref[...], g_ref[...], acc_ref[...])
            dx_ref[...] = dxo_ref[...] + dx
            dg_ref[0:1, :] += dg

    act = pl.BlockSpec((1, tm, hs), lambda i, kk: (kk, i, 0))
    act_shape = jax.ShapeDtypeStruct((N_CHIPS, t, hs), BF16)
    row = pl.BlockSpec((tm, D), lambda i, kk: (i, 0))
    wspec = pl.BlockSpec((1, hs, D), lambda i, kk: (kk, 0, 0))
    return pl.pallas_call(
        _after(body, 8, deps), name=name, grid=(t // tm, N_CHIPS),
        in_specs=[row, act, act, row, pl.BlockSpec((1, D), lambda i, kk: (0, 0)), wspec, wspec, wspec] + [ANY] * len(deps),
        out_specs=[row, act, act, pl.BlockSpec((8, D), lambda i, kk: (0, 0))],
        out_shape=[jax.ShapeDtypeStruct((t, D), F32), act_shape, act_shape, jax.ShapeDtypeStruct((8, D), F32)],
        scratch_shapes=[pltpu.VMEM((tm, D), F32)],
        compiler_params=_params("arbitrary", "arbitrary"),
    )(dxo, a, b, x, g, wg, wu, wd, *deps)


def _layer_norm_stats(v):
    mu = jnp.mean(v, axis=-1, keepdims=True)
    vc = v - mu
    rstd = lax.rsqrt(jnp.mean(vc * vc, axis=-1, keepdims=True) + EPS)
    return vc * rstd, rstd


def a_mid_fwd(z, gv, wm, bs, name):
    t = z.shape[0]
    tm = _row_tile(t, 256)

    def body(z_ref, gv_ref, wm_ref, bs_ref, y_ref, vn_ref):
        zz = z_ref[...].astype(F32)
        u = _gelu(zz[:, :D])
        vhat, _ = _layer_norm_stats(_gelu(zz[:, D:]))
        vnb = (vhat * gv_ref[...]).astype(BF16)
        vn_ref[...] = vnb
        for n in range(tm // GMLP_BLOCK):
            rows = slice(n * GMLP_BLOCK, (n + 1) * GMLP_BLOCK)
            for grp in range(A_GROUPS):
                cols = slice(grp * 128, (grp + 1) * 128)
                sv = _dot(wm_ref[grp], vnb[rows, cols]) + bs_ref[grp]
                y_ref[rows, cols] = (u[rows, cols] * sv).astype(BF16)

    small = pl.BlockSpec((A_GROUPS, 128, 128), lambda i: (0, 0, 0))
    return pl.pallas_call(
        body, name=name, grid=(t // tm,),
        in_specs=[pl.BlockSpec((tm, 2 * D), lambda i: (i, 0)), pl.BlockSpec((1, D), lambda i: (0, 0)), small, small],
        out_specs=[pl.BlockSpec((tm, D), lambda i: (i, 0)), pl.BlockSpec((tm, D), lambda i: (i, 0))],
        out_shape=[jax.ShapeDtypeStruct((t, D), BF16), jax.ShapeDtypeStruct((t, D), BF16)],
        compiler_params=_params("arbitrary"),
    )(z, gv, wm, bs)


def a_mid_bwd(dy, z, vn, gv, wm, wmt, bs, name):
    t = z.shape[0]
    tm = _row_tile(t, 256)

    def body(dy_ref, z_ref, vn_ref, gv_ref, wm_ref, wmt_ref, bs_ref, dz_ref, dwm_ref, dbs_ref, dgv_ref, du_ref, dvn_ref):
        @pl.when(pl.program_id(0) == 0)
        def _():
            dwm_ref[...] = jnp.zeros_like(dwm_ref)
            dbs_ref[...] = jnp.zeros_like(dbs_ref)
            dgv_ref[...] = jnp.zeros_like(dgv_ref)

        zz = z_ref[...].astype(F32)
        zu, zv = zz[:, :D], zz[:, D:]
        u = _gelu(zu)
        vhat, rstd = _layer_norm_stats(_gelu(zv))
        dyv = dy_ref[...].astype(F32)
        vnb = vn_ref[...]
        ones = jnp.ones((128, 128), BF16)
        for n in range(tm // GMLP_BLOCK):
            rows = slice(n * GMLP_BLOCK, (n + 1) * GMLP_BLOCK)
            for grp in range(A_GROUPS):
                cols = slice(grp * 128, (grp + 1) * 128)
                blk = vnb[rows, cols]
                sv = _dot(wm_ref[grp], blk) + bs_ref[grp]
                dyb = dyv[rows, cols]
                du_ref[rows, cols] = dyb * sv
                dsv = (dyb * u[rows, cols]).astype(BF16)
                dvn_ref[rows, cols] = _dot(wmt_ref[grp], dsv)
                dwm_ref[grp] += _dot_nt(dsv, blk)
                dbs_ref[grp] += _dot(dsv, ones)
        dvn = dvn_ref[...]
        dgv_ref[0:1, :] += jnp.sum(dvn * vhat, axis=0, keepdims=True)
        dvh = dvn * gv_ref[...]
        dv = rstd * (dvh - jnp.mean(dvh, axis=-1, keepdims=True) - vhat * jnp.mean(dvh * vhat, axis=-1, keepdims=True))
        dz_ref[:, :D] = (du_ref[...] * _gelu_grad(zu)).astype(BF16)
        dz_ref[:, D:] = (dv * _gelu_grad(zv)).astype(BF16)

    small = pl.BlockSpec((A_GROUPS, 128, 128), lambda i: (0, 0, 0))
    row = pl.BlockSpec((tm, D), lambda i: (i, 0))
    small_shape = jax.ShapeDtypeStruct((A_GROUPS, 128, 128), F32)
    return pl.pallas_call(
        body, name=name, grid=(t // tm,),
        in_specs=[row, pl.BlockSpec((tm, 2 * D), lambda i: (i, 0)), row, pl.BlockSpec((1, D), lambda i: (0, 0)), small, small, small],
        out_specs=[pl.BlockSpec((tm, 2 * D), lambda i: (i, 0)), small, small, pl.BlockSpec((8, D), lambda i: (0, 0))],
        out_shape=[jax.ShapeDtypeStruct((t, 2 * D), BF16), small_shape, small_shape, jax.ShapeDtypeStruct((8, D), F32)],
        scratch_shapes=[pltpu.VMEM((tm, D), F32), pltpu.VMEM((tm, D), F32)],
        compiler_params=_params("arbitrary"),
    )(dy, z, vn, gv, wm, wmt, bs)


def _conv_terms(p_ref, row):
    gb = p_ref[0].astype(F32)
    gc = p_ref[1].astype(F32)
    xt = p_ref[2].astype(F32)
    q = gc * xt
    return gb, gc, xt, q, _shift_down(q, 1, row), _shift_down(q, 2, row)


def b_conv_fwd(p3, cw, seq, name):
    t = p3.shape[1]
    cb = 256

    def body(p_ref, cw_ref, y_ref):
        row = lax.broadcasted_iota(jnp.int32, (seq, cb), 0)
        gb, _, _, q, q1, q2 = _conv_terms(p_ref, row)
        y_ref[...] = (gb * (cw_ref[2:3, :] * q + cw_ref[1:2, :] * q1 + cw_ref[0:1, :] * q2)).astype(BF16)

    return pl.pallas_call(
        body, name=name, grid=(t // seq, D // cb),
        in_specs=[pl.BlockSpec((3, seq, cb), lambda e, c: (0, e, c)), pl.BlockSpec((3, cb), lambda e, c: (0, c))],
        out_specs=pl.BlockSpec((seq, cb), lambda e, c: (e, c)),
        out_shape=jax.ShapeDtypeStruct((t, D), BF16),
        compiler_params=_params("arbitrary", "arbitrary"),
    )(p3, cw)


def b_conv_bwd(dy, p3, cw, seq, name):
    t = p3.shape[1]
    cb = 256

    def body(dy_ref, p_ref, cw_ref, dp_ref, dcw_ref):
        @pl.when(pl.program_id(1) == 0)
        def _():
            dcw_ref[...] = jnp.zeros_like(dcw_ref)

        row = lax.broadcasted_iota(jnp.int32, (seq, cb), 0)
        gb, gc, xt, q, q1, q2 = _conv_terms(p_ref, row)
        dyv = dy_ref[...].astype(F32)
        conv = cw_ref[2:3, :] * q + cw_ref[1:2, :] * q1 + cw_ref[0:1, :] * q2
        dyc = dyv * gb
        dq = cw_ref[2:3, :] * dyc + cw_ref[1:2, :] * _shift_up(dyc, 1, row) + cw_ref[0:1, :] * _shift_up(dyc, 2, row)
        dp_ref[0] = (dyv * conv).astype(BF16)
        dp_ref[1] = (dq * xt).astype(BF16)
        dp_ref[2] = (dq * gc).astype(BF16)
        dcw_ref[0:1, :] += jnp.sum(dyc * q2, axis=0, keepdims=True)
        dcw_ref[1:2, :] += jnp.sum(dyc * q1, axis=0, keepdims=True)
        dcw_ref[2:3, :] += jnp.sum(dyc * q, axis=0, keepdims=True)

    return pl.pallas_call(
        body, name=name, grid=(D // cb, t // seq),
        in_specs=[pl.BlockSpec((seq, cb), lambda c, e: (e, c)), pl.BlockSpec((3, seq, cb), lambda c, e: (0, e, c)),
                  pl.BlockSpec((3, cb), lambda c, e: (0, c))],
        out_specs=[pl.BlockSpec((3, seq, cb), lambda c, e: (0, e, c)), pl.BlockSpec((8, cb), lambda c, e: (0, c))],
        out_shape=[jax.ShapeDtypeStruct((3, t, D), BF16), jax.ShapeDtypeStruct((8, D), F32)],
        compiler_params=_params("arbitrary", "arbitrary"),
    )(dy, p3, cw)


def c_pool_fwd(p, seq, name):
    t = p.shape[0]

    def make(grp):
        w = POOL_WINDOWS[grp]

        def body_g(p_ref, d_ref):
            row = lax.broadcasted_iota(jnp.int32, (seq, C_GROUP_DIM), 0)
            pv = p_ref[...].astype(F32)
            acc = pv
            sh = 1
            while sh < w:
                acc = acc + _shift_down(acc, sh, row)
                sh *= 2
            d_ref[...] = (acc / jnp.minimum(row + 1, w).astype(F32) - pv).astype(BF16)

        return body_g

    outs = []
    for grp in range(len(POOL_WINDOWS)):
        outs.append(pl.pallas_call(
            make(grp), name=f"{name}_g{grp}", grid=(t // seq,),
            in_specs=[pl.BlockSpec((seq, C_GROUP_DIM), lambda e, grp=grp: (e, grp))],
            out_specs=pl.BlockSpec((seq, C_GROUP_DIM), lambda e: (e, 0)),
            out_shape=jax.ShapeDtypeStruct((t, C_GROUP_DIM), BF16),
            compiler_params=_params("arbitrary"),
        )(p))
    return outs


def c_pool_bwd(dd, seq, name):
    t = dd[0].shape[0]

    def make(w):
        def body_g(dd_ref, dp_ref):
            row = lax.broadcasted_iota(jnp.int32, (seq, C_GROUP_DIM), 0)
            ddv = dd_ref[...]
            acc = ddv / jnp.minimum(row + 1, w).astype(F32)
            sh = 1
            while sh < w:
                acc = acc + _shift_up(acc, sh, row)
                sh *= 2
            dp_ref[...] = (acc - ddv).astype(BF16)

        return body_g

    outs = []
    for grp, w in enumerate(POOL_WINDOWS):
        outs.append(pl.pallas_call(
            make(w), name=f"{name}_g{grp}", grid=(t // seq,),
            in_specs=[pl.BlockSpec((seq, C_GROUP_DIM), lambda e: (e, 0))],
            out_specs=pl.BlockSpec((seq, C_GROUP_DIM), lambda e: (e, 0)),
            out_shape=jax.ShapeDtypeStruct((t, C_GROUP_DIM), BF16),
            compiler_params=_params("arbitrary"),
        )(dd[grp]))
    return outs


def c_out_fwd(d, wgrp, scale, wo, x, name):
    t = x.shape[0]
    tm = _row_tile(t, 512)
    n_g = len(POOL_WINDOWS)

    def body(d0, d1, d2, d3, wg_ref, sc_ref, wo_ref, x_ref, y_ref, o_ref):
        parts = [_dot(dr[...], wg_ref[grp]) for grp, dr in enumerate((d0, d1, d2, d3))]
        y = (jnp.concatenate(parts, axis=1) * sc_ref[...]).astype(BF16)
        y_ref[...] = y
        o_ref[...] = x_ref[...] + _dot(y, wo_ref[...])

    dspec = pl.BlockSpec((tm, C_GROUP_DIM), lambda i: (i, 0))
    row = pl.BlockSpec((tm, D), lambda i: (i, 0))
    return pl.pallas_call(
        body, name=name, grid=(t // tm,),
        in_specs=[dspec] * n_g + [pl.BlockSpec((n_g, C_GROUP_DIM, C_GROUP_DIM), lambda i: (0, 0, 0)),
                                  pl.BlockSpec((1, D), lambda i: (0, 0)), pl.BlockSpec((D, D), lambda i: (0, 0)), row],
        out_specs=[row, row],
        out_shape=[jax.ShapeDtypeStruct((t, D), BF16), jax.ShapeDtypeStruct((t, D), F32)],
        compiler_params=_params("arbitrary"),
    )(*d, wgrp, scale, wo, x)


def c_out_bwd(dxm, d, wgrp, scale, wo, name):
    t = dxm.shape[0]
    tm = _row_tile(t, 512)
    n_g = len(POOL_WINDOWS)

    def body(dxm_ref, d0, d1, d2, d3, wg_ref, sc_ref, wo_ref, dyp_ref, dd0, dd1, dd2, dd3, dsc_ref):
        @pl.when(pl.program_id(0) == 0)
        def _():
            dsc_ref[...] = jnp.zeros_like(dsc_ref)

        dyo = _dot_nt(dxm_ref[...].astype(BF16), wo_ref[...])
        ypre = jnp.concatenate([_dot(dr[...], wg_ref[grp]) for grp, dr in enumerate((d0, d1, d2, d3))], axis=1)
        dsc_ref[0:1, :] += jnp.sum(dyo * ypre, axis=0, keepdims=True)
        dyp = (dyo * sc_ref[...]).astype(BF16)
        dyp_ref[...] = dyp
        for grp, ddr in enumerate((dd0, dd1, dd2, dd3)):
            ddr[...] = _dot_nt(dyp[:, grp * C_GROUP_DIM:(grp + 1) * C_GROUP_DIM], wg_ref[grp])

    dspec = pl.BlockSpec((tm, C_GROUP_DIM), lambda i: (i, 0))
    row = pl.BlockSpec((tm, D), lambda i: (i, 0))
    dshape = jax.ShapeDtypeStruct((t, C_GROUP_DIM), F32)
    return pl.pallas_call(
        body, name=name, grid=(t // tm,),
        in_specs=[row] + [dspec] * n_g + [pl.BlockSpec((n_g, C_GROUP_DIM, C_GROUP_DIM), lambda i: (0, 0, 0)),
                                          pl.BlockSpec((1, D), lambda i: (0, 0)), pl.BlockSpec((D, D), lambda i: (0, 0))],
        out_specs=[row] + [dspec] * n_g + [pl.BlockSpec((8, D), lambda i: (0, 0))],
        out_shape=[jax.ShapeDtypeStruct((t, D), BF16)] + [dshape] * n_g + [jax.ShapeDtypeStruct((8, D), F32)],
        compiler_params=_params("arbitrary"),
    )(dxm, *d, wgrp, scale, wo)


def loss_head(x, tgt, g, name):
    t = x.shape[0]
    tm = _row_tile(t, 512)

    def body(x_ref, t_ref, g_ref, dx_ref, dg_ref, loss_ref):
        @pl.when(pl.program_id(0) == 0)
        def _():
            dg_ref[...] = jnp.zeros_like(dg_ref)
            loss_ref[...] = jnp.zeros_like(loss_ref)

        xv, gvv = x_ref[...], g_ref[...]
        xh, _ = _rms(xv)
        diff = xh * gvv - t_ref[...]
        loss_ref[...] += 0.5 * jnp.sum(jnp.mean(diff * diff, axis=-1, keepdims=True))
        dx, dg = _rms_bwd(xv, gvv, diff * (1.0 / D))
        dx_ref[...] = dx
        dg_ref[0:1, :] += dg

    row = pl.BlockSpec((tm, D), lambda i: (i, 0))
    return pl.pallas_call(
        body, name=name, grid=(t // tm,),
        in_specs=[row, row, pl.BlockSpec((1, D), lambda i: (0, 0))],
        out_specs=[row, pl.BlockSpec((8, D), lambda i: (0, 0)), pl.BlockSpec((8, 128), lambda i: (0, 0))],
        out_shape=[jax.ShapeDtypeStruct((t, D), F32), jax.ShapeDtypeStruct((8, D), F32), jax.ShapeDtypeStruct((8, 128), F32)],
        compiler_params=_params("arbitrary"),
    )(x, tgt, g)


def adamw(w, g, m, v, name):
    rows, cols = w.shape
    tr = rows
    for cand in (512, 256, 128, 64, 32, 16, 8):
        if rows % cand == 0 and rows > cand:
            tr = cand
            break

    def body(w_ref, g_ref, m_ref, v_ref, d_ref, mo_ref, vo_ref):
        gv = g_ref[...]
        mn = ADAM_B1 * m_ref[...] + (1.0 - ADAM_B1) * gv
        vn = ADAM_B2 * v_ref[...] + (1.0 - ADAM_B2) * (gv * gv)
        m_hat = mn / (1.0 - ADAM_B1 ** ADAM_STEP)
        v_hat = vn / (1.0 - ADAM_B2 ** ADAM_STEP)
        d_ref[...] = -ADAM_LR * (m_hat / (jnp.sqrt(v_hat) + ADAM_EPS) + ADAM_WD * w_ref[...])
        mo_ref[...] = mn
        vo_ref[...] = vn

    spec = pl.BlockSpec((tr, cols), lambda i: (i, 0))
    shape = jax.ShapeDtypeStruct((rows, cols), F32)
    return pl.pallas_call(
        body, name=name, grid=(rows // tr,),
        in_specs=[spec] * 4, out_specs=[spec] * 3, out_shape=[shape] * 3,
        compiler_params=_params("arbitrary"),
    )(w, g, m, v)


def adamw_layer(w, m, v, own, recv, core, layer, carried, name):
    n_layers, rows, cols = w.shape
    h = rows // 2

    def body(core_ref, w_ref, m_ref, v_ref, own_ref, recv_ref, *rest):
        g_ref, d_ref, mo_ref, vo_ref = rest[-4:]
        gv = jnp.where(pl.program_id(0) == core_ref[0], own_ref[...], recv_ref[...])
        mn = ADAM_B1 * m_ref[0] + (1.0 - ADAM_B1) * gv
        vn = ADAM_B2 * v_ref[0] + (1.0 - ADAM_B2) * (gv * gv)
        m_hat = mn / (1.0 - ADAM_B1 ** ADAM_STEP)
        v_hat = vn / (1.0 - ADAM_B2 ** ADAM_STEP)
        g_ref[0] = gv
        d_ref[0] = -ADAM_LR * (m_hat / (jnp.sqrt(v_hat) + ADAM_EPS) + ADAM_WD * w_ref[0])
        mo_ref[0] = mn
        vo_ref[0] = vn

    stacked = pl.BlockSpec((1, h, cols), lambda half, core_ref: (layer, half, 0))
    halfspec = pl.BlockSpec((h, cols), lambda half, core_ref: (0, 0))
    n_carried = 0 if carried is None else 4
    shape = jax.ShapeDtypeStruct(w.shape, F32)
    return pl.pallas_call(
        body, name=name,
        grid_spec=pltpu.PrefetchScalarGridSpec(
            num_scalar_prefetch=1, grid=(2,),
            in_specs=[stacked] * 3 + [halfspec] * 2 + [ANY] * n_carried, out_specs=[stacked] * 4),
        out_shape=[shape] * 4,
        input_output_aliases={6 + i: i for i in range(n_carried)},
        compiler_params=_params("arbitrary"),
    )(core, w, m, v, own, recv, *(carried or ()))


def add_halves(gs, ps, core, name, deps=()):
    n = len(gs)

    def body(core_ref, *refs):
        for i in range(n):
            refs[2 * n + i][...] = (refs[i][...].astype(F32) + refs[n + i][...].astype(F32)).astype(BF16)

    in_specs, out_specs, out_shape = [], [], []
    for gt in gs:
        h, c = gt.shape[1] // 2, gt.shape[2]
        in_specs.append(pl.BlockSpec((1, h, c), lambda b, core_ref: (b, core_ref[0], 0)))
    for gt in gs:
        h, c = gt.shape[1] // 2, gt.shape[2]
        in_specs.append(pl.BlockSpec((1, h, c), lambda b, core_ref: (b, 0, 0)))
        out_specs.append(pl.BlockSpec((1, h, c), lambda b, core_ref: (b, 0, 0)))
        out_shape.append(jax.ShapeDtypeStruct((N_CHIPS, h, c), BF16))
    in_specs += [ANY] * len(deps)
    return pl.pallas_call(
        _after(body, 1 + 2 * n, deps), name=name,
        grid_spec=pltpu.PrefetchScalarGridSpec(num_scalar_prefetch=1, grid=(N_CHIPS,), in_specs=in_specs, out_specs=out_specs),
        out_shape=out_shape, compiler_params=_params("arbitrary"),
    )(core, *gs, *ps, *deps)


def add_final(hs, qs, chip, name, deps=()):
    n = len(hs)

    def body(chip_ref, *refs):
        for i in range(n):
            q = refs[n + i]
            refs[2 * n + i][...] = ((refs[i][0].astype(F32) + q[0].astype(F32)) + q[1].astype(F32)) + q[2].astype(F32)

    in_specs, out_specs, out_shape = [], [], []
    for ht in hs:
        h, c = ht.shape[1], ht.shape[2]
        in_specs.append(pl.BlockSpec((1, h, c), lambda i, chip_ref: (chip_ref[0], 0, 0)))
    for ht in hs:
        h, c = ht.shape[1], ht.shape[2]
        in_specs.append(pl.BlockSpec((N_CHIPS - 1, h, c), lambda i, chip_ref: (0, 0, 0)))
        out_specs.append(pl.BlockSpec((h, c), lambda i, chip_ref: (0, 0)))
        out_shape.append(jax.ShapeDtypeStruct((h, c), F32))
    in_specs += [ANY] * len(deps)
    return pl.pallas_call(
        _after(body, 1 + 2 * n, deps), name=name,
        grid_spec=pltpu.PrefetchScalarGridSpec(num_scalar_prefetch=1, grid=(1,), in_specs=in_specs, out_specs=out_specs),
        out_shape=out_shape, compiler_params=_params("arbitrary"),
    )(chip, *hs, *qs, *deps)


def sum_devices(own, gathered, me, name, deps=()):
    rows = own.shape[0]

    def body(me_ref, own_ref, g_ref, o_ref):
        me_dev = me_ref[0]
        acc = None
        for dev in range(8):
            slot = jnp.maximum((me_dev ^ dev) - 1, 0)
            term = jnp.where(me_dev == dev, own_ref[...], g_ref[slot])
            acc = term if acc is None else acc + term
        o_ref[...] = acc

    return pl.pallas_call(
        _after(body, 3, deps), name=name,
        grid_spec=pltpu.PrefetchScalarGridSpec(
            num_scalar_prefetch=1, grid=(rows // 8,),
            in_specs=[pl.BlockSpec((8, D), lambda i, me_ref: (i, 0)), pl.BlockSpec((7, 8, D), lambda i, me_ref: (0, i, 0))]
            + [ANY] * len(deps),
            out_specs=pl.BlockSpec((8, D), lambda i, me_ref: (i, 0))),
        out_shape=jax.ShapeDtypeStruct((rows, D), F32),
        compiler_params=_params("arbitrary"),
    )(me, own, gathered, *deps)


def _mesh_pos():
    return lax.axis_index("x"), lax.axis_index("y"), lax.axis_index("c")


def _other_chips(x, y):
    return [(1 - x, y), (x, 1 - y), (1 - x, 1 - y)]


def _sibling():
    x, y, c = _mesh_pos()
    return [(x, y, 1 - c)]


def _same_core_of_other_chips():
    x, y, c = _mesh_pos()
    return [(cx, cy, c) for (cx, cy) in _other_chips(x, y)]


def _on_sequencer(body, name, operands, out_shapes, sems, peers, collective_id, deps=()):
    ordered = _after(body, len(operands), deps)

    def seq_body(*refs):
        barrier = pltpu.get_barrier_semaphore()
        with_whom = peers()
        for peer in with_whom:
            pl.semaphore_signal(barrier, inc=1, device_id=peer, device_id_type=MESH)
        pl.semaphore_wait(barrier, len(with_whom))
        ordered(*refs)

    return pl.kernel(
        seq_body, name=name, out_type=out_shapes,
        mesh=plsc.ScalarSubcoreMesh(axis_name="seq", num_cores=1),
        scratch_types=sems, compiler_params=pltpu.CompilerParams(collective_id=collective_id, has_side_effects=True),
    )(*operands, *deps)


def all_gather_weights(shards, name, deps=()):
    n = len(shards)

    def body(*refs):
        ins, outs = refs[:n], refs[n:2 * n]
        send, recv, fsend, frecv = refs[2 * n:]
        x, y, c = _mesh_pos()
        k = 2 * x + y
        chips = _other_chips(x, y)

        def half(ref, i, rows_half):
            h = shards[i].shape[0] // 2
            return ref.at[pl.ds(pl.multiple_of(rows_half * h, 8), h), :]

        first = []
        for i in range(n):
            for j, (cx, cy) in enumerate(chips):
                first.append(pltpu.make_async_remote_copy(
                    src_ref=half(ins[i], i, c), dst_ref=half(outs[i].at[k], i, c),
                    send_sem=send.at[i, j], recv_sem=recv.at[i, j], device_id=(cx, cy, c), device_id_type=MESH))
        for cp in first:
            cp.start()
        passed = []
        for i in range(n):
            for j, (cx, cy) in enumerate(chips):
                blk = half(outs[i].at[2 * cx + cy], i, c)
                pltpu.make_async_remote_copy(src_ref=blk, dst_ref=blk, send_sem=send.at[i, j], recv_sem=recv.at[i, j],
                                             device_id=(cx, cy, c), device_id_type=MESH).wait_recv()
                fw = pltpu.make_async_remote_copy(src_ref=blk, dst_ref=blk, send_sem=fsend.at[i, j], recv_sem=frecv.at[i, j],
                                                  device_id=(x, y, 1 - c), device_id_type=MESH)
                fw.start()
                passed.append(fw)
        for i in range(n):
            for j, (cx, cy) in enumerate(chips):
                blk = half(outs[i].at[2 * cx + cy], i, 1 - c)
                pltpu.make_async_remote_copy(src_ref=blk, dst_ref=blk, send_sem=fsend.at[i, j], recv_sem=frecv.at[i, j],
                                             device_id=(x, y, 1 - c), device_id_type=MESH).wait_recv()
        for cp in first + passed:
            cp.wait_send()

    def peers():
        x, y, c = _mesh_pos()
        return [(cx, cy, c) for (cx, cy) in _other_chips(x, y)] + [(x, y, 1 - c)]

    return _on_sequencer(
        body, name, shards, [jax.ShapeDtypeStruct((N_CHIPS,) + s.shape, s.dtype) for s in shards],
        [pltpu.SemaphoreType.DMA((n, 3))] * 4, peers, GATHER_COLLECTIVE_ID, deps)


def place_own(gathered, shards, chip, name):
    n = len(shards)

    def body(chip_ref, *refs):
        for i in range(n):
            refs[2 * n + i][0] = refs[i][...]

    in_specs = [pl.BlockSpec(s.shape, lambda i, chip_ref: (0, 0)) for s in shards] + [ANY] * n
    out_specs = [pl.BlockSpec((1,) + s.shape, lambda i, chip_ref: (chip_ref[0], 0, 0)) for s in shards]
    return pl.pallas_call(
        body, name=name,
        grid_spec=pltpu.PrefetchScalarGridSpec(num_scalar_prefetch=1, grid=(1,), in_specs=in_specs, out_specs=out_specs),
        out_shape=[jax.ShapeDtypeStruct(g.shape, g.dtype) for g in gathered],
        input_output_aliases={1 + n + i: i for i in range(n)},
        compiler_params=_params("arbitrary"),
    )(chip, *shards, *gathered)


def all_gather_rows(shard, name):
    def body(in_ref, out_ref, send, recv, lsem):
        x, y, c = _mesh_pos()
        k = 2 * x + y
        chips = _other_chips(x, y)
        local = pltpu.make_async_copy(in_ref, out_ref.at[k], lsem)
        local.start()
        sent = [pltpu.make_async_remote_copy(src_ref=in_ref, dst_ref=out_ref.at[k], send_sem=send.at[j], recv_sem=recv.at[j],
                                             device_id=(cx, cy, c), device_id_type=MESH) for j, (cx, cy) in enumerate(chips)]
        for cp in sent:
            cp.start()
        for j, (cx, cy) in enumerate(chips):
            blk = out_ref.at[2 * cx + cy]
            pltpu.make_async_remote_copy(src_ref=blk, dst_ref=blk, send_sem=send.at[j], recv_sem=recv.at[j],
                                         device_id=(cx, cy, c), device_id_type=MESH).wait_recv()
        for cp in sent:
            cp.wait_send()
        local.wait()

    return pl.pallas_call(
        body, name=name, in_specs=[ANY], out_specs=ANY,
        out_shape=jax.ShapeDtypeStruct((N_CHIPS,) + shard.shape, shard.dtype),
        scratch_shapes=[pltpu.SemaphoreType.DMA((3,)), pltpu.SemaphoreType.DMA((3,)), pltpu.SemaphoreType.DMA],
    )(shard)


def swap_halves(gs, name):
    n = len(gs)

    def body(*refs):
        ins, outs = refs[:n], refs[n:2 * n]
        send, recv = refs[2 * n:]
        x, y, c = _mesh_pos()
        sent = []
        for i in range(n):
            h = gs[i].shape[1] // 2
            src = ins[i].at[:, pl.ds(pl.multiple_of((1 - c) * h, 8), h), :]
            cp = pltpu.make_async_remote_copy(src_ref=src, dst_ref=outs[i], send_sem=send.at[i], recv_sem=recv.at[i],
                                              device_id=(x, y, 1 - c), device_id_type=MESH)
            cp.start()
            sent.append(cp)
        for cp in sent:
            cp.wait()

    return _on_sequencer(
        body, name, gs, [jax.ShapeDtypeStruct((N_CHIPS, g.shape[1] // 2, g.shape[2]), g.dtype) for g in gs],
        [pltpu.SemaphoreType.DMA((n,)), pltpu.SemaphoreType.DMA((n,))], _sibling, SIBLING_COLLECTIVE_ID)


def scatter_chips(hs, name):
    n = len(hs)

    def body(*refs):
        ins, outs = refs[:n], refs[n:2 * n]
        send, recv = refs[2 * n:]
        x, y, c = _mesh_pos()
        chips = _other_chips(x, y)
        sent = []
        for i in range(n):
            for j, (cx, cy) in enumerate(chips):
                cp = pltpu.make_async_remote_copy(src_ref=ins[i].at[2 * cx + cy], dst_ref=outs[i].at[j],
                                                  send_sem=send.at[i, j], recv_sem=recv.at[i, j],
                                                  device_id=(cx, cy, c), device_id_type=MESH)
                cp.start()
                sent.append(cp)
        for cp in sent:
            cp.wait()

    return _on_sequencer(
        body, name, hs, [jax.ShapeDtypeStruct((N_CHIPS - 1,) + h.shape[1:], h.dtype) for h in hs],
        [pltpu.SemaphoreType.DMA((n, 3)), pltpu.SemaphoreType.DMA((n, 3))], _same_core_of_other_chips, CHIPS_COLLECTIVE_ID)


def swap_reduced(rs, name):
    n = len(rs)

    def body(*refs):
        ins, outs = refs[:n], refs[n:2 * n]
        send, recv = refs[2 * n:]
        x, y, c = _mesh_pos()
        sent = []
        for i in range(n):
            cp = pltpu.make_async_remote_copy(src_ref=ins[i], dst_ref=outs[i], send_sem=send.at[i], recv_sem=recv.at[i],
                                              device_id=(x, y, 1 - c), device_id_type=MESH)
            cp.start()
            sent.append(cp)
        for cp in sent:
            cp.wait()

    return _on_sequencer(
        body, name, rs, [jax.ShapeDtypeStruct(r.shape, r.dtype) for r in rs],
        [pltpu.SemaphoreType.DMA((n,)), pltpu.SemaphoreType.DMA((n,))], _sibling, SIBLING_COLLECTIVE_ID)


def all_gather_devices(part, name):
    def everyone_else():
        x, y, c = _mesh_pos()
        return [(1 - x if (rel >> 2) & 1 else x, 1 - y if (rel >> 1) & 1 else y, 1 - c if rel & 1 else c) for rel in range(1, 8)]

    def body(in_ref, out_ref, send, recv):
        sent = []
        for slot, peer in enumerate(everyone_else()):
            cp = pltpu.make_async_remote_copy(src_ref=in_ref, dst_ref=out_ref.at[slot], send_sem=send.at[slot],
                                              recv_sem=recv.at[slot], device_id=peer, device_id_type=MESH)
            cp.start()
            sent.append(cp)
        for cp in sent:
            cp.wait()

    return _on_sequencer(
        body, name, [part], jax.ShapeDtypeStruct((7,) + part.shape, part.dtype),
        [pltpu.SemaphoreType.DMA((7,)), pltpu.SemaphoreType.DMA((7,))], everyone_else, ALL_COLLECTIVE_ID)


class ReduceScatter:
    def __init__(self, grads, names, layer, core, chip, name):
        self.grads, self.names, self.layer, self.core, self.chip, self.name = grads, names, layer, core, chip, name
        self.from_sibling = swap_halves(grads, name + "_swap")

    def sum_cores(self, deps=()):
        self.core_sums = add_halves(self.grads, self.from_sibling, self.core, self.name + "_add2", deps)
        self.from_chips = scatter_chips(self.core_sums, self.name + "_scatter")
        return self.core_sums[0]

    def sum_chips(self, deps=()):
        self.half_sums = add_final(self.core_sums, self.from_chips, self.chip, self.name + "_add4", deps)
        self.other_half = swap_reduced(self.half_sums, self.name + "_join")
        return self.half_sums[0]


def _blocked(w):
    return w.reshape(w.shape[0] * w.shape[1], w.shape[2])


def _grp_from_blocks(w):
    return w.reshape(N_CHIPS, 4, 64, C_GROUP_DIM).transpose(1, 0, 2, 3).reshape(4, C_GROUP_DIM, C_GROUP_DIM)


def _grp_to_blocks(w):
    return w.reshape(4, N_CHIPS, 64, C_GROUP_DIM).transpose(1, 0, 2, 3).reshape(N_CHIPS, C_GROUP_DIM, C_GROUP_DIM)


def _dw_cols(h, dact, hs, name, deps):
    tm = 512
    return mm_tn(h[None], dact, (N_CHIPS, D, hs), tm, hs, N_CHIPS * (D // tm),
                 lambda j: (0, j % 2), lambda j: (0, j // 2), lambda j: (j // 2, j % 2, 0), name, deps)


def _dw_rows(y, dxm, name, deps):
    tm = 512
    out = mm_tn(y[None], dxm[None], (1, D, D), tm, D, D // tm, lambda j: (0, j), lambda j: (0, 0), lambda j: (0, j, 0), name, deps)
    return out.reshape(N_CHIPS, D // N_CHIPS, D)


def _dw_hidden(act, other, name, deps):
    tn = 512
    return mm_tn(act, other[None], (N_CHIPS, FFN_SHARD, D), FFN_SHARD, tn, N_CHIPS * (D // tn),
                 lambda j: (j // 2, 0), lambda j: (0, j % 2), lambda j: (j // 2, 0, j % 2), name, deps)


def kernel(x, norm_mix_g, norm_ffn_g, final_norm_g, a_w_in, a_v_norm_g, a_w_s, a_b_s, a_w_out, b_w_in, b_conv_w, b_w_out, c_w_in, c_w_grp, c_scale, c_w_out, f_w_gate, f_w_up, f_w_down, loss_target, m_norm_mix_g, m_norm_ffn_g, m_final_norm_g, m_a_w_in, m_a_v_norm_g, m_a_w_s, m_a_b_s, m_a_w_out, m_b_w_in, m_b_conv_w, m_b_w_out, m_c_w_in, m_c_w_grp, m_c_scale, m_c_w_out, m_f_w_gate, m_f_w_up, m_f_w_down, v_norm_mix_g, v_norm_ffn_g, v_final_norm_g, v_a_w_in, v_a_v_norm_g, v_a_w_s, v_a_b_s, v_a_w_out, v_b_w_in, v_b_conv_w, v_b_w_out, v_c_w_in, v_c_w_grp, v_c_scale, v_c_w_out, v_f_w_gate, v_f_w_up, v_f_w_down):
    n_ex, seq, _ = x.shape
    t = n_ex * seq
    xi, yi, ci = lax.axis_index("x"), lax.axis_index("y"), lax.axis_index("c")
    chip = (2 * xi + yi).astype(jnp.int32)
    core_arr = ci.astype(jnp.int32).reshape(1)
    chip_arr = chip.reshape(1)
    me_arr = (4 * xi + 2 * yi + ci).astype(jnp.int32).reshape(1)
    bf = lambda w: w.astype(BF16)

    pad8 = lambda v: jnp.pad(v, ((0, 8 - v.shape[0]), (0, 0)))
    small_rows = jnp.concatenate([pad8(a_v_norm_g), pad8(b_conv_w[0]), pad8(c_scale)], axis=0)
    small_gathered = all_gather_rows(small_rows, "ag_small")
    small_full = small_gathered.transpose(1, 0, 2).reshape(24, D)
    gv_full = [small_full[0:1], small_full[1:2]]
    cw_full = small_full[8:11]
    scale_full = small_full[16:17]

    mixer_shards = [
        [bf(a_w_in[0]), bf(a_w_out[0])],
        [bf(b_w_in[0]), bf(b_w_out[0])],
        [bf(c_w_in[0]), bf(c_w_grp[0]).reshape(C_GROUP_DIM, C_GROUP_DIM), bf(c_w_out[0])],
        [bf(a_w_in[1]), bf(a_w_out[1])],
    ]
    hidden_major = lambda w: jnp.swapaxes(w, 1, 2)
    gate_t, up_t = hidden_major(f_w_gate), hidden_major(f_w_up)
    gathered = []
    for i in range(4):
        ffn_shards = [bf(gate_t[i]), bf(up_t[i]), bf(f_w_down[i])]
        if i == 0:
            parts = [(mixer_shards[0], "ag_l0_mixer", [small_gathered]), (ffn_shards, "ag_l0_ffn", [])]
        else:
            parts = [(mixer_shards[i] + ffn_shards, f"ag_l{i}", [])]
        layer = []
        for shards, name, deps in parts:
            layer += place_own(all_gather_weights(shards, name, deps), shards, chip_arr, name.replace("ag", "own"))
        gathered.append(layer)

    mask = (jnp.arange(GMLP_BLOCK)[None, :] // 64) <= (jnp.arange(GMLP_BLOCK)[:, None] // 64)
    gmix = [norm_mix_g[i:i + 1] for i in range(4)]
    gffn = [norm_ffn_g[i:i + 1] for i in range(4)]
    a_chunks = [(s, 0, 512, 0, s * 512) for s in range(N_CHIPS)]
    b_chunks = [(j // 3, (j % 3) * 256, 256, j // 4, (j % 4) * 256) for j in range(12)]
    c_chunks = [(0, 0, D, 0, 0)]

    xs = [x.reshape(t, D)]
    saved = []
    for i in range(4):
        ws = gathered[i]
        wg, wu, wd = ws[-3], ws[-2], ws[-1]
        xin = xs[-1]
        if i in (0, 3):
            j = 0 if i == 0 else 1
            win, wout = ws[0], _blocked(ws[1])
            wm32 = jnp.where(mask[None], a_w_s[j], 0.0)
            wm, wmt = bf(wm32), bf(wm32.transpose(0, 2, 1))
            bs = jnp.broadcast_to(a_b_s[j][:, :, None], (A_GROUPS, GMLP_BLOCK, 128))
            h, z = norm_mm(xin, gmix[i], win, a_chunks, 1, 2 * D, f"a_in_l{i}")
            y, vn = a_mid_fwd(z[0], gv_full[j], wm, bs, f"a_mid_l{i}")
            xmid = mm_res(y, wout, xin, f"a_out_l{i}")
            saved.append(dict(h=h, z=z, y=y, vn=vn, win=win, wout=wout, wm=wm, wmt=wmt, bs=bs, gv=gv_full[j]))
        elif i == 1:
            win, wout = ws[0], _blocked(ws[1])
            h, p3 = norm_mm(xin, gmix[i], win, b_chunks, 3, D, "b_in")
            y = b_conv_fwd(p3, cw_full, seq, "b_conv")
            xmid = mm_res(y, wout, xin, "b_out")
            saved.append(dict(h=h, p3=p3, y=y, win=win, wout=wout))
        else:
            win, wgrp, wout = _blocked(ws[0])[None], _grp_from_blocks(ws[1]), _blocked(ws[2])
            h, p = norm_mm(xin, gmix[i], win, c_chunks, 1, D, "c_in")
            dpool = c_pool_fwd(p[0], seq, "c_pool")
            y, xmid = c_out_fwd(dpool, wgrp, scale_full, wout, xin, "c_out")
            saved.append(dict(h=h, d=dpool, y=y, win=win, wgrp=wgrp, wout=wout))
        h2, fa, fb, fs, xout = ffn_fwd(xmid, gffn[i], wg, wu, wd, f"ffn_l{i}")
        saved[-1].update(h2=h2, fa=fa, fb=fb, fs=fs, xmid=xmid, wg=wg, wu=wu, wd=wd)
        xs.append(xout)

    dx, dg_final, loss_part = loss_head(xs[4], loss_target.reshape(t, D), final_norm_g[None], "loss_head")
    loss = lax.psum(loss_part[0, 0], ("x", "y", "c"))

    weights = dict(norm_mix_g=norm_mix_g, norm_ffn_g=norm_ffn_g, final_norm_g=final_norm_g, a_w_in=a_w_in, a_v_norm_g=a_v_norm_g,
                   a_w_s=a_w_s, a_b_s=a_b_s, a_w_out=a_w_out, b_w_in=b_w_in, b_conv_w=b_conv_w, b_w_out=b_w_out, c_w_in=c_w_in,
                   c_w_grp=c_w_grp, c_scale=c_scale, c_w_out=c_w_out, f_w_gate=f_w_gate, f_w_up=f_w_up, f_w_down=f_w_down)
    m_in = dict(norm_mix_g=m_norm_mix_g, norm_ffn_g=m_norm_ffn_g, final_norm_g=m_final_norm_g, a_w_in=m_a_w_in, a_v_norm_g=m_a_v_norm_g,
                a_w_s=m_a_w_s, a_b_s=m_a_b_s, a_w_out=m_a_w_out, b_w_in=m_b_w_in, b_conv_w=m_b_conv_w, b_w_out=m_b_w_out, c_w_in=m_c_w_in,
                c_w_grp=m_c_w_grp, c_scale=m_c_scale, c_w_out=m_c_w_out, f_w_gate=m_f_w_gate, f_w_up=m_f_w_up, f_w_down=m_f_w_down)
    v_in = dict(norm_mix_g=v_norm_mix_g, norm_ffn_g=v_norm_ffn_g, final_norm_g=v_final_norm_g, a_w_in=v_a_w_in, a_v_norm_g=v_a_v_norm_g,
                a_w_s=v_a_w_s, a_b_s=v_a_b_s, a_w_out=v_a_w_out, b_w_in=v_b_w_in, b_conv_w=v_b_conv_w, b_w_out=v_b_w_out, c_w_in=v_c_w_in,
                c_w_grp=v_c_w_grp, c_scale=v_c_scale, c_w_out=v_c_w_out, f_w_gate=v_f_w_gate, f_w_up=v_f_w_up, f_w_down=v_f_w_down)
    grp_rows = lambda a: a.reshape(1, C_GROUP_DIM, C_GROUP_DIM)
    same = lambda a: a
    to_stacked = {nme: same for nme in ("a_w_in", "a_w_out", "b_w_in", "b_w_out", "c_w_in", "c_w_out", "f_w_down")}
    to_stacked.update(f_w_gate=hidden_major, f_w_up=hidden_major, c_w_grp=grp_rows)
    from_stacked = dict(to_stacked, c_w_grp=lambda a: a.reshape(c_w_grp.shape))
    layer_tensors = {0: ["a_w_out", "a_w_in"], 1: ["b_w_out", "b_w_in"], 2: ["c_w_out", "c_w_in", "c_w_grp"], 3: ["a_w_out", "a_w_in"]}
    carried = {}

    def bs_rows(v):
        return jnp.pad(v[:, :, 0].reshape(1, D), ((0, 7), (0, 0)))

    def update(unit):
        done = []
        for pos, nme in enumerate(unit.names):
            stacked_layer = unit.layer if nme.startswith("f_") else (unit.layer // 3 if nme.startswith("a_") else 0)
            view = to_stacked[nme]
            carried[nme] = adamw_layer(view(weights[nme]), view(m_in[nme]), view(v_in[nme]), unit.half_sums[pos], unit.other_half[pos],
                                       core_arr, stacked_layer, carried.get(nme), f"adamw_{nme}_l{unit.layer}")
            done.append(carried[nme][0])
        return done

    ffn_names = ["f_w_gate", "f_w_up", "f_w_down"]
    dg_mix, dg_ffn = [None] * 4, [None] * 4
    small = {}
    newer = older = None
    for i in (3, 2, 1, 0):
        sv = saved[i]
        xin = xs[i]
        deps = ([newer.grads[0]] if newer else []) + ([older.half_sums[0]] if older else [])
        dxm, da, db, dg_ffn[i] = ffn_bwd(dx, sv["fa"], sv["fb"], sv["xmid"], gffn[i], sv["wg"], sv["wu"], sv["wd"], f"ffn_bwd_l{i}", deps)
        last = [newer.sum_cores([dxm])] if newer else []
        g_gate = _dw_hidden(da, sv["h2"], f"dw_gate_l{i}", last)
        g_up = _dw_hidden(db, sv["h2"], f"dw_up_l{i}", [g_gate])
        g_down = _dw_hidden(sv["fs"], dx, f"dw_down_l{i}", [g_up])
        last_ffn = [g_down]
        if i == 0:
            ffn_unit = ReduceScatter([g_down, g_gate, g_up], ["f_w_down", "f_w_gate", "f_w_up"], 0, core_arr, chip_arr, "rs_l0_ffn")
        if i in (0, 3):
            j = 0 if i == 0 else 1
            dy = mm_nt(dxm, sv["wout"], f"a_dy_l{i}")
            dz, dwm, dbs, dgv = a_mid_bwd(dy, sv["z"][0], sv["vn"], sv["gv"], sv["wm"], sv["wmt"], sv["bs"], f"a_mid_bwd_l{i}")
            if i == 0:
                early = sum_devices(early_rows, early_gathered, me_arr, "sum_small_grads_l123")
                newer.sum_chips([dz] + update(older) + [early])
                last_ffn.append(ffn_unit.sum_cores([newer.half_sums[0]]))
            dz = dz[None]
            dx, dg_mix[i] = bwd_in(dz, sv["win"], a_chunks, xin, gmix[i], dxm, f"a_bwd_in_l{i}")
            g_in = _dw_cols(sv["h"], dz, 512, f"dw_a_in_l{i}", last_ffn)
            g_out = _dw_rows(sv["y"], dxm, f"dw_a_out_l{i}", [g_in])
            small[f"wm{j}"], small[f"bs{j}"], small[f"gv{j}"] = dwm, dbs, dgv
            mixer_grads = [g_out, g_in]
        elif i == 1:
            dy = mm_nt(dxm, sv["wout"], "b_dy")
            dp3, small["cw"] = b_conv_bwd(dy, sv["p3"], cw_full, seq, "b_conv_bwd")
            dx, dg_mix[i] = bwd_in(dp3, sv["win"], b_chunks, xin, gmix[i], dxm, "b_bwd_in")
            g_in = mm_tn(sv["h"][None], dp3, (N_CHIPS, D, 768), 512, 256, 24,
                         lambda j: (0, j % 2), lambda j: ((j // 2) // 4, (j // 2) % 4),
                         lambda j: ((j // 2) // 3, j % 2, (j // 2) % 3), "dw_b_in", [g_down])
            g_out = _dw_rows(sv["y"], dxm, "dw_b_out", [g_in])
            mixer_grads = [g_out, g_in]
        else:
            outs = c_out_bwd(dxm, sv["d"], sv["wgrp"], scale_full, sv["wout"], "c_out_bwd")
            dyp, dd, small["scale"] = outs[0], list(outs[1:5]), outs[5]
            dpool = c_pool_bwd(dd, seq, "c_pool_bwd")
            dp = jnp.concatenate(dpool, axis=1)[None]
            dx, dg_mix[i] = bwd_in(dp, sv["win"], c_chunks, xin, gmix[i], dxm, "c_bwd_in")
            g_in = _dw_rows(sv["h"], dp[0], "dw_c_in", [g_down])
            dcat = jnp.concatenate(sv["d"], axis=1)
            g_grp = mm_tn(dcat[None], dyp[None], (4, C_GROUP_DIM, C_GROUP_DIM), C_GROUP_DIM, C_GROUP_DIM, 4,
                          lambda j: (0, j), lambda j: (0, j), lambda j: (j, 0, 0), "dw_c_grp", [g_in])
            g_out = _dw_rows(sv["y"], dxm, "dw_c_out", [g_grp])
            mixer_grads = [g_out, g_in, _grp_to_blocks(g_grp)]
        if i > 0:
            unit = ReduceScatter(mixer_grads + [g_gate, g_up, g_down], layer_tensors[i] + ffn_names, i, core_arr, chip_arr, f"rs_l{i}")
        else:
            unit = ReduceScatter(mixer_grads, layer_tensors[0], 0, core_arr, chip_arr, "rs_l0_mixer")
        if newer and i > 0:
            newer.sum_chips([mixer_grads[0]] + (update(older) if older else []))
        if i == 1:
            early_rows = jnp.concatenate(dg_mix[1:] + dg_ffn[1:] + [dg_final, small["gv1"], small["cw"], small["scale"],
                                                                   small["wm1"].reshape(128, D), bs_rows(small["bs1"])], axis=0)
            early_gathered = all_gather_devices(early_rows, "ag_small_grads_l123")
        older, newer = newer, unit
    grad_x = dx.reshape(n_ex, seq, D)
    mixer_unit = newer
    ffn_unit.sum_chips(update(older) + [mixer_unit.grads[0]])
    mixer_unit.sum_cores([ffn_unit.half_sums[0]])
    mixer_unit.sum_chips(update(ffn_unit))
    updated = update(mixer_unit)

    late_rows = jnp.concatenate([dg_mix[0], dg_ffn[0], small["gv0"], small["wm0"].reshape(128, D), bs_rows(small["bs0"])], axis=0)
    late = sum_devices(late_rows, all_gather_devices(late_rows, "ag_small_grads_l0"), me_arr, "sum_small_grads_l0", updated)
    first_rows = lambda a, b, n: jnp.concatenate([a, b], axis=0).reshape(n, 8, D)[:, 0]
    g_norm_mix = first_rows(late[0:8], early[0:24], 4)
    g_norm_ffn = first_rows(late[8:16], early[24:48], 4)
    g_final = early[48]
    g_gv = first_rows(late[16:24], early[56:64], 2)
    g_cw = early[64:67]
    g_scale = early[72:73]
    g_ws = jnp.where(mask[None, None], jnp.concatenate([late[24:152], early[80:208]], axis=0).reshape(2, A_GROUPS, 128, 128), 0.0)
    g_bs = first_rows(late[152:160], early[208:216], 2).reshape(2, A_GROUPS, 128)
    col0 = chip * (D // N_CHIPS)
    cols = lambda v: lax.dynamic_slice_in_dim(v, col0, D // N_CHIPS, axis=1)

    small_grads = {
        "norm_mix_g": g_norm_mix, "norm_ffn_g": g_norm_ffn, "final_norm_g": g_final, "a_v_norm_g": cols(g_gv), "a_w_s": g_ws,
        "a_b_s": g_bs, "b_conv_w": cols(g_cw)[None], "c_scale": cols(g_scale),
    }
    results = {}
    for nme, g in small_grads.items():
        w = weights[nme]
        flat = lambda a: a.reshape(-1, w.shape[-1])
        dl, mn, vn = adamw(flat(w), flat(g), flat(m_in[nme]), flat(v_in[nme]), f"adamw_{nme}")
        results[nme] = tuple(o.reshape(w.shape) for o in (g, dl, mn, vn))
    for nme, outs in carried.items():
        results[nme] = tuple(from_stacked[nme](o) for o in outs)

    names = list(weights)
    return (loss, grad_x, *[results[n][0] for n in names], *[results[n][1] for n in names],
            *[results[n][2] for n in names], *[results[n][3] for n in names])
```

```python
import jax
import jax.numpy as jnp
from jax import lax
from jax.experimental import pallas as pl
from jax.experimental.pallas import tpu as pltpu
from jax.experimental.pallas import tpu_sc as plsc

F32 = jnp.float32
BF16 = jnp.bfloat16
D = 1024
FFN_SHARD = 704
GMLP_BLOCK = 128
A_GROUPS = 8
POOL_WINDOWS = (2, 4, 8, 16)
C_GROUP_DIM = 256
N_CHIPS = 4
EPS = 1e-6
ADAM_LR, ADAM_B1, ADAM_B2, ADAM_EPS, ADAM_WD, ADAM_STEP = 0.001, 0.9, 0.999, 1e-08, 0.01, 10
VMEM_LIMIT_BYTES = 56 * 1024 * 1024
FFN_HIDDEN = N_CHIPS * FFN_SHARD
FFN_CHUNKS = ((0, 768), (768, 768), (1536, 768), (2304, 512))
FFN_FWD_ROWS = 512
FFN_BWD_ROWS = 256
MESH = pl.DeviceIdType.MESH
GATHER_COLLECTIVE_ID = 1
SIBLING_COLLECTIVE_ID = 2
CHIPS_COLLECTIVE_ID = 3
ALL_COLLECTIVE_ID = 4
ANY = pl.BlockSpec(memory_space=pl.ANY)
NT_DIMS = (((1,), (1,)), ((), ()))
TN_DIMS = (((0,), (0,)), ((), ()))
INV_SQRT2 = 0.7071067811865476
INV_SQRT_2PI = 0.3989422804014327


def _params(*semantics):
    return pltpu.CompilerParams(dimension_semantics=semantics, vmem_limit_bytes=VMEM_LIMIT_BYTES)


def _dot(a, b):
    return jnp.dot(a, b, preferred_element_type=F32)


def _dot_nt(a, b):
    return lax.dot_general(a, b, NT_DIMS, preferred_element_type=F32)


def _rms(x):
    r = lax.rsqrt(jnp.mean(x * x, axis=-1, keepdims=True) + EPS)
    return x * r, r


def _rms_bwd(x, g, dh):
    xh, r = _rms(x)
    dxh = dh * g
    dx = r * (dxh - xh * jnp.mean(dxh * xh, axis=-1, keepdims=True))
    return dx, jnp.sum(dh * xh, axis=0, keepdims=True)


def _gelu(x):
    return 0.5 * x * (1.0 + lax.erf(x * INV_SQRT2))


def _gelu_grad(x):
    return 0.5 * (1.0 + lax.erf(x * INV_SQRT2)) + x * jnp.exp(-0.5 * x * x) * INV_SQRT_2PI


def _shift_down(v, s, row):
    return jnp.where(row >= s, pltpu.roll(v, s, 0), 0.0)


def _shift_up(v, s, row):
    n = v.shape[0]
    return jnp.where(row < n - s, pltpu.roll(v, n - s, 0), 0.0)


def _row_tile(t, want):
    return want if t % want == 0 else t


def _after(body, first, deps):
    if not deps:
        return body

    def ordered(*refs):
        return body(*refs[:first], *refs[first + len(deps):])

    return ordered


def norm_mm(x, g, w, chunks, n_parts, part_width, name):
    t = x.shape[0]
    tm = _row_tile(t, 512)
    n_shards, _, hs = w.shape

    def body(x_ref, g_ref, w_ref, h_ref, p_ref):
        xh, _ = _rms(x_ref[...])
        h = (xh * g_ref[...]).astype(BF16)
        h_ref[...] = h
        for s in range(n_shards):
            res = _dot(h, w_ref[s]).astype(BF16)
            for (cs, wc, width, part, pc) in chunks:
                if cs == s:
                    p_ref[part, :, pc:pc + width] = res[:, wc:wc + width]

    return pl.pallas_call(
        body, name=name, grid=(t // tm,),
        in_specs=[pl.BlockSpec((tm, D), lambda i: (i, 0)), pl.BlockSpec((1, D), lambda i: (0, 0)),
                  pl.BlockSpec((n_shards, D, hs), lambda i: (0, 0, 0))],
        out_specs=[pl.BlockSpec((tm, D), lambda i: (i, 0)), pl.BlockSpec((n_parts, tm, part_width), lambda i: (0, i, 0))],
        out_shape=[jax.ShapeDtypeStruct((t, D), BF16), jax.ShapeDtypeStruct((n_parts, t, part_width), BF16)],
        compiler_params=_params("arbitrary"),
    )(x, g, w)


def mm_res(a, w, res, name):
    t, k = a.shape
    n = w.shape[1]
    tm = _row_tile(t, 512)

    def body(a_ref, w_ref, r_ref, o_ref):
        o_ref[...] = r_ref[...] + _dot(a_ref[...], w_ref[...])

    return pl.pallas_call(
        body, name=name, grid=(t // tm,),
        in_specs=[pl.BlockSpec((tm, k), lambda i: (i, 0)), pl.BlockSpec((k, n), lambda i: (0, 0)),
                  pl.BlockSpec((tm, n), lambda i: (i, 0))],
        out_specs=pl.BlockSpec((tm, n), lambda i: (i, 0)),
        out_shape=jax.ShapeDtypeStruct((t, n), F32),
        compiler_params=_params("arbitrary"),
    )(a, w, res)


def mm_nt(a, w, name):
    t, n = a.shape
    k = w.shape[0]
    tm = _row_tile(t, 512)

    def body(a_ref, w_ref, o_ref):
        o_ref[...] = _dot_nt(a_ref[...].astype(BF16), w_ref[...]).astype(BF16)

    return pl.pallas_call(
        body, name=name, grid=(t // tm,),
        in_specs=[pl.BlockSpec((tm, n), lambda i: (i, 0)), pl.BlockSpec((k, n), lambda i: (0, 0))],
        out_specs=pl.BlockSpec((tm, k), lambda i: (i, 0)),
        out_shape=jax.ShapeDtypeStruct((t, k), BF16),
        compiler_params=_params("arbitrary"),
    )(a, w)


def bwd_in(dp, w, chunks, x, g, dres, name):
    n_parts, t, part_width = dp.shape
    n_shards, _, hs = w.shape
    tm = _row_tile(t, 512)

    def body(dp_ref, w_ref, x_ref, g_ref, dres_ref, dx_ref, dg_ref):
        acc = jnp.zeros((tm, D), F32)
        for (cs, wc, width, part, pc) in chunks:
            acc = acc + _dot_nt(dp_ref[part, :, pc:pc + width], w_ref[cs, :, wc:wc + width])
        dx, dg = _rms_bwd(x_ref[...], g_ref[...], acc)
        dx_ref[...] = dres_ref[...] + dx

        @pl.when(pl.program_id(0) == 0)
        def _():
            dg_ref[...] = jnp.zeros_like(dg_ref)

        dg_ref[0:1, :] += dg

    return pl.pallas_call(
        body, name=name, grid=(t // tm,),
        in_specs=[pl.BlockSpec((n_parts, tm, part_width), lambda i: (0, i, 0)),
                  pl.BlockSpec((n_shards, D, hs), lambda i: (0, 0, 0)),
                  pl.BlockSpec((tm, D), lambda i: (i, 0)), pl.BlockSpec((1, D), lambda i: (0, 0)),
                  pl.BlockSpec((tm, D), lambda i: (i, 0))],
        out_specs=[pl.BlockSpec((tm, D), lambda i: (i, 0)), pl.BlockSpec((8, D), lambda i: (0, 0))],
        out_shape=[jax.ShapeDtypeStruct((t, D), F32), jax.ShapeDtypeStruct((8, D), F32)],
        compiler_params=_params("arbitrary"),
    )(dp, w, x, g, dres)


def mm_tn(a, b, out_shape, tm, tn, n_tiles, a_idx, b_idx, o_idx, name, deps=()):
    t = a.shape[1]

    def body(a_ref, b_ref, o_ref):
        o_ref[0] = lax.dot_general(a_ref[0].astype(BF16), b_ref[0].astype(BF16), TN_DIMS, preferred_element_type=F32).astype(BF16)

    return pl.pallas_call(
        _after(body, 2, deps), name=name, grid=(n_tiles,),
        in_specs=[pl.BlockSpec((1, t, tm), lambda j: (a_idx(j)[0], 0, a_idx(j)[1])),
                  pl.BlockSpec((1, t, tn), lambda j: (b_idx(j)[0], 0, b_idx(j)[1]))] + [ANY] * len(deps),
        out_specs=pl.BlockSpec((1, tm, tn), lambda j: o_idx(j)),
        out_shape=jax.ShapeDtypeStruct(out_shape, BF16),
        compiler_params=_params("arbitrary"),
    )(a, b, *deps)


def ffn_fwd(x, g, wg, wu, wd, name):
    t = x.shape[0]
    tm = _row_tile(t, FFN_FWD_ROWS)
    hidden = wg.shape[0]

    def body(x_ref, g_ref, wg_ref, wu_ref, wd_ref, h_ref, a_ref, b_ref, s_ref, o_ref):
        xv = x_ref[...]
        xh, _ = _rms(xv)
        h = (xh * g_ref[...]).astype(BF16)
        h_ref[...] = h
        acc = xv
        for c0, cw in FFN_CHUNKS:
            cols = slice(c0, c0 + cw)
            a = _dot_nt(h, wg_ref[cols, :])
            b = _dot_nt(h, wu_ref[cols, :])
            sig = jax.nn.sigmoid(a)
            silu = a * sig
            s = (silu * b).astype(BF16)
            a_ref[:, cols] = (b * (sig * (1.0 + a * (1.0 - sig)))).astype(BF16)
            b_ref[:, cols] = silu.astype(BF16)
            s_ref[:, cols] = s
            acc = acc + _dot(s, wd_ref[cols, :])
        o_ref[...] = acc

    act = pl.BlockSpec((tm, hidden), lambda i: (i, 0))
    act_shape = jax.ShapeDtypeStruct((t, hidden), BF16)
    wspec = pl.BlockSpec((hidden, D), lambda i: (0, 0), pipeline_mode=pl.Buffered(1))
    row = pl.BlockSpec((tm, D), lambda i: (i, 0))
    return pl.pallas_call(
        body, name=name, grid=(t // tm,),
        in_specs=[row, pl.BlockSpec((1, D), lambda i: (0, 0)), wspec, wspec, wspec],
        out_specs=[row, act, act, act, row],
        out_shape=[jax.ShapeDtypeStruct((t, D), BF16), act_shape, act_shape, act_shape, jax.ShapeDtypeStruct((t, D), F32)],
        compiler_params=_params("arbitrary"),
    )(x, g, wg, wu, wd)


def ffn_bwd(dxo, a, b, x, g, wg, wu, wd, name, deps=()):
    t = x.shape[0]
    tm = _row_tile(t, FFN_BWD_ROWS)
    hidden = wg.shape[0]

    def body(dxo_ref, a_ref, b_ref, x_ref, g_ref, wg_ref, wu_ref, wd_ref, dx_ref, da_ref, db_ref, dg_ref):
        @pl.when(pl.program_id(0) == 0)
        def _():
            dg_ref[...] = jnp.zeros_like(dg_ref)

        dxo = dxo_ref[...]
        dxb = dxo.astype(BF16)
        acc = None
        for c0, cw in FFN_CHUNKS:
            cols = slice(c0, c0 + cw)
            ds = _dot_nt(dxb, wd_ref[cols, :])
            da = (ds * a_ref[:, cols].astype(F32)).astype(BF16)
            db = (ds * b_ref[:, cols].astype(F32)).astype(BF16)
            da_ref[:, cols] = da
            db_ref[:, cols] = db
            part = _dot(da, wg_ref[cols, :]) + _dot(db, wu_ref[cols, :])
            acc = part if acc is None else acc + part
        dx, dg = _rms_bwd(x_ref[...], g_ref[...], acc)
        dx_ref[...] = dxo + dx
        dg_ref[0:1, :] += dg

    act = pl.BlockSpec((tm, hidden), lambda i: (i, 0))
    act_shape = jax.ShapeDtypeStruct((t, hidden), BF16)
    row = pl.BlockSpec((tm, D), lambda i: (i, 0))
    wspec = pl.BlockSpec((hidden, D), lambda i: (0, 0), pipeline_mode=pl.Buffered(1))
    return pl.pallas_call(
        _after(body, 8, deps), name=name, grid=(t // tm,),
        in_specs=[row, act, act, row, pl.BlockSpec((1, D), lambda i: (0, 0)), wspec, wspec, wspec] + [ANY] * len(deps),
        out_specs=[row, act, act, pl.BlockSpec((8, D), lambda i: (0, 0))],
        out_shape=[jax.ShapeDtypeStruct((t, D), F32), act_shape, act_shape, jax.ShapeDtypeStruct((8, D), F32)],
        compiler_params=_params("arbitrary"),
    )(dxo, a, b, x, g, wg, wu, wd, *deps)


def _layer_norm_stats(v):
    mu = jnp.mean(v, axis=-1, keepdims=True)
    vc = v - mu
    rstd = lax.rsqrt(jnp.mean(vc * vc, axis=-1, keepdims=True) + EPS)
    return vc * rstd, rstd


def a_mid_fwd(z, gv, wm, bs, name):
    t = z.shape[0]
    tm = _row_tile(t, 256)

    def body(z_ref, gv_ref, wm_ref, bs_ref, y_ref, vn_ref):
        zz = z_ref[...].astype(F32)
        u = _gelu(zz[:, :D])
        vhat, _ = _layer_norm_stats(_gelu(zz[:, D:]))
        vnb = (vhat * gv_ref[...]).astype(BF16)
        vn_ref[...] = vnb
        for n in range(tm // GMLP_BLOCK):
            rows = slice(n * GMLP_BLOCK, (n + 1) * GMLP_BLOCK)
            for grp in range(A_GROUPS):
                cols = slice(grp * 128, (grp + 1) * 128)
                sv = _dot(wm_ref[grp], vnb[rows, cols]) + bs_ref[grp]
                y_ref[rows, cols] = (u[rows, cols] * sv).astype(BF16)

    small = pl.BlockSpec((A_GROUPS, 128, 128), lambda i: (0, 0, 0))
    return pl.pallas_call(
        body, name=name, grid=(t // tm,),
        in_specs=[pl.BlockSpec((tm, 2 * D), lambda i: (i, 0)), pl.BlockSpec((1, D), lambda i: (0, 0)), small, small],
        out_specs=[pl.BlockSpec((tm, D), lambda i: (i, 0)), pl.BlockSpec((tm, D), lambda i: (i, 0))],
        out_shape=[jax.ShapeDtypeStruct((t, D), BF16), jax.ShapeDtypeStruct((t, D), BF16)],
        compiler_params=_params("arbitrary"),
    )(z, gv, wm, bs)


def a_mid_bwd(dy, z, vn, gv, wm, wmt, bs, name):
    t = z.shape[0]
    tm = _row_tile(t, 256)

    def body(dy_ref, z_ref, vn_ref, gv_ref, wm_ref, wmt_ref, bs_ref, dz_ref, dwm_ref, dbs_ref, dgv_ref, du_ref, dvn_ref):
        @pl.when(pl.program_id(0) == 0)
        def _():
            dwm_ref[...] = jnp.zeros_like(dwm_ref)
            dbs_ref[...] = jnp.zeros_like(dbs_ref)
            dgv_ref[...] = jnp.zeros_like(dgv_ref)

        zz = z_ref[...].astype(F32)
        zu, zv = zz[:, :D], zz[:, D:]
        u = _gelu(zu)
        vhat, rstd = _layer_norm_stats(_gelu(zv))
        dyv = dy_ref[...].astype(F32)
        vnb = vn_ref[...]
        ones = jnp.ones((128, 128), BF16)
        for n in range(tm // GMLP_BLOCK):
            rows = slice(n * GMLP_BLOCK, (n + 1) * GMLP_BLOCK)
            for grp in range(A_GROUPS):
                cols = slice(grp * 128, (grp + 1) * 128)
                blk = vnb[rows, cols]
                sv = _dot(wm_ref[grp], blk) + bs_ref[grp]
                dyb = dyv[rows, cols]
                du_ref[rows, cols] = dyb * sv
                dsv = (dyb * u[rows, cols]).astype(BF16)
                dvn_ref[rows, cols] = _dot(wmt_ref[grp], dsv)
                dwm_ref[grp] += _dot_nt(dsv, blk)
                dbs_ref[grp] += _dot(dsv, ones)
        dvn = dvn_ref[...]
        dgv_ref[0:1, :] += jnp.sum(dvn * vhat, axis=0, keepdims=True)
        dvh = dvn * gv_ref[...]
        dv = rstd * (dvh - jnp.mean(dvh, axis=-1, keepdims=True) - vhat * jnp.mean(dvh * vhat, axis=-1, keepdims=True))
        dz_ref[:, :D] = (du_ref[...] * _gelu_grad(zu)).astype(BF16)
        dz_ref[:, D:] = (dv * _gelu_grad(zv)).astype(BF16)

    small = pl.BlockSpec((A_GROUPS, 128, 128), lambda i: (0, 0, 0))
    row = pl.BlockSpec((tm, D), lambda i: (i, 0))
    small_shape = jax.ShapeDtypeStruct((A_GROUPS, 128, 128), F32)
    return pl.pallas_call(
        body, name=name, grid=(t // tm,),
        in_specs=[row, pl.BlockSpec((tm, 2 * D), lambda i: (i, 0)), row, pl.BlockSpec((1, D), lambda i: (0, 0)), small, small, small],
        out_specs=[pl.BlockSpec((tm, 2 * D), lambda i: (i, 0)), small, small, pl.BlockSpec((8, D), lambda i: (0, 0))],
        out_shape=[jax.ShapeDtypeStruct((t, 2 * D), BF16), small_shape, small_shape, jax.ShapeDtypeStruct((8, D), F32)],
        scratch_shapes=[pltpu.VMEM((tm, D), F32), pltpu.VMEM((tm, D), F32)],
        compiler_params=_params("arbitrary"),
    )(dy, z, vn, gv, wm, wmt, bs)


def _conv_terms(p_ref, row):
    gb = p_ref[0].astype(F32)
    gc = p_ref[1].astype(F32)
    xt = p_ref[2].astype(F32)
    q = gc * xt
    return gb, gc, xt, q, _shift_down(q, 1, row), _shift_down(q, 2, row)


def b_conv_fwd(p3, cw, seq, name):
    t = p3.shape[1]
    cb = 256

    def body(p_ref, cw_ref, y_ref):
        row = lax.broadcasted_iota(jnp.int32, (seq, cb), 0)
        gb, _, _, q, q1, q2 = _conv_terms(p_ref, row)
        y_ref[...] = (gb * (cw_ref[2:3, :] * q + cw_ref[1:2, :] * q1 + cw_ref[0:1, :] * q2)).astype(BF16)

    return pl.pallas_call(
        body, name=name, grid=(t // seq, D // cb),
        in_specs=[pl.BlockSpec((3, seq, cb), lambda e, c: (0, e, c)), pl.BlockSpec((3, cb), lambda e, c: (0, c))],
        out_specs=pl.BlockSpec((seq, cb), lambda e, c: (e, c)),
        out_shape=jax.ShapeDtypeStruct((t, D), BF16),
        compiler_params=_params("arbitrary", "arbitrary"),
    )(p3, cw)


def b_conv_bwd(dy, p3, cw, seq, name):
    t = p3.shape[1]
    cb = 256

    def body(dy_ref, p_ref, cw_ref, dp_ref, dcw_ref):
        @pl.when(pl.program_id(1) == 0)
        def _():
            dcw_ref[...] = jnp.zeros_like(dcw_ref)

        row = lax.broadcasted_iota(jnp.int32, (seq, cb), 0)
        gb, gc, xt, q, q1, q2 = _conv_terms(p_ref, row)
        dyv = dy_ref[...].astype(F32)
        conv = cw_ref[2:3, :] * q + cw_ref[1:2, :] * q1 + cw_ref[0:1, :] * q2
        dyc = dyv * gb
        dq = cw_ref[2:3, :] * dyc + cw_ref[1:2, :] * _shift_up(dyc, 1, row) + cw_ref[0:1, :] * _shift_up(dyc, 2, row)
        dp_ref[0] = (dyv * conv).astype(BF16)
        dp_ref[1] = (dq * xt).astype(BF16)
        dp_ref[2] = (dq * gc).astype(BF16)
        dcw_ref[0:1, :] += jnp.sum(dyc * q2, axis=0, keepdims=True)
        dcw_ref[1:2, :] += jnp.sum(dyc * q1, axis=0, keepdims=True)
        dcw_ref[2:3, :] += jnp.sum(dyc * q, axis=0, keepdims=True)

    return pl.pallas_call(
        body, name=name, grid=(D // cb, t // seq),
        in_specs=[pl.BlockSpec((seq, cb), lambda c, e: (e, c)), pl.BlockSpec((3, seq, cb), lambda c, e: (0, e, c)),
                  pl.BlockSpec((3, cb), lambda c, e: (0, c))],
        out_specs=[pl.BlockSpec((3, seq, cb), lambda c, e: (0, e, c)), pl.BlockSpec((8, cb), lambda c, e: (0, c))],
        out_shape=[jax.ShapeDtypeStruct((3, t, D), BF16), jax.ShapeDtypeStruct((8, D), F32)],
        compiler_params=_params("arbitrary", "arbitrary"),
    )(dy, p3, cw)


def c_pool_fwd(p, seq, name):
    t = p.shape[0]

    def make(grp):
        w = POOL_WINDOWS[grp]

        def body_g(p_ref, d_ref):
            row = lax.broadcasted_iota(jnp.int32, (seq, C_GROUP_DIM), 0)
            pv = p_ref[...].astype(F32)
            acc = pv
            sh = 1
            while sh < w:
                acc = acc + _shift_down(acc, sh, row)
                sh *= 2
            d_ref[...] = (acc / jnp.minimum(row + 1, w).astype(F32) - pv).astype(BF16)

        return body_g

    outs = []
    for grp in range(len(POOL_WINDOWS)):
        outs.append(pl.pallas_call(
            make(grp), name=f"{name}_g{grp}", grid=(t // seq,),
            in_specs=[pl.BlockSpec((seq, C_GROUP_DIM), lambda e, grp=grp: (e, grp))],
            out_specs=pl.BlockSpec((seq, C_GROUP_DIM), lambda e: (e, 0)),
            out_shape=jax.ShapeDtypeStruct((t, C_GROUP_DIM), BF16),
            compiler_params=_params("arbitrary"),
        )(p))
    return outs


def c_pool_bwd(dd, seq, name):
    t = dd[0].shape[0]

    def make(w):
        def body_g(dd_ref, dp_ref):
            row = lax.broadcasted_iota(jnp.int32, (seq, C_GROUP_DIM), 0)
            ddv = dd_ref[...]
            acc = ddv / jnp.minimum(row + 1, w).astype(F32)
            sh = 1
            while sh < w:
                acc = acc + _shift_up(acc, sh, row)
                sh *= 2
            dp_ref[...] = (acc - ddv).astype(BF16)

        return body_g

    outs = []
    for grp, w in enumerate(POOL_WINDOWS):
        outs.append(pl.pallas_call(
            make(w), name=f"{name}_g{grp}", grid=(t // seq,),
            in_specs=[pl.BlockSpec((seq, C_GROUP_DIM), lambda e: (e, 0))],
            out_specs=pl.BlockSpec((seq, C_GROUP_DIM), lambda e: (e, 0)),
            out_shape=jax.ShapeDtypeStruct((t, C_GROUP_DIM), BF16),
            compiler_params=_params("arbitrary"),
        )(dd[grp]))
    return outs


def c_out_fwd(d, wgrp, scale, wo, x, name):
    t = x.shape[0]
    tm = _row_tile(t, 512)
    n_g = len(POOL_WINDOWS)

    def body(d0, d1, d2, d3, wg_ref, sc_ref, wo_ref, x_ref, y_ref, o_ref):
        parts = [_dot(dr[...], wg_ref[grp]) for grp, dr in enumerate((d0, d1, d2, d3))]
        y = (jnp.concatenate(parts, axis=1) * sc_ref[...]).astype(BF16)
        y_ref[...] = y
        o_ref[...] = x_ref[...] + _dot(y, wo_ref[...])

    dspec = pl.BlockSpec((tm, C_GROUP_DIM), lambda i: (i, 0))
    row = pl.BlockSpec((tm, D), lambda i: (i, 0))
    return pl.pallas_call(
        body, name=name, grid=(t // tm,),
        in_specs=[dspec] * n_g + [pl.BlockSpec((n_g, C_GROUP_DIM, C_GROUP_DIM), lambda i: (0, 0, 0)),
                                  pl.BlockSpec((1, D), lambda i: (0, 0)), pl.BlockSpec((D, D), lambda i: (0, 0)), row],
        out_specs=[row, row],
        out_shape=[jax.ShapeDtypeStruct((t, D), BF16), jax.ShapeDtypeStruct((t, D), F32)],
        compiler_params=_params("arbitrary"),
    )(*d, wgrp, scale, wo, x)


def c_out_bwd(dxm, d, wgrp, scale, wo, name):
    t = dxm.shape[0]
    tm = _row_tile(t, 512)
    n_g = len(POOL_WINDOWS)

    def body(dxm_ref, d0, d1, d2, d3, wg_ref, sc_ref, wo_ref, dyp_ref, dd0, dd1, dd2, dd3, dsc_ref):
        @pl.when(pl.program_id(0) == 0)
        def _():
            dsc_ref[...] = jnp.zeros_like(dsc_ref)

        dyo = _dot_nt(dxm_ref[...].astype(BF16), wo_ref[...])
        ypre = jnp.concatenate([_dot(dr[...], wg_ref[grp]) for grp, dr in enumerate((d0, d1, d2, d3))], axis=1)
        dsc_ref[0:1, :] += jnp.sum(dyo * ypre, axis=0, keepdims=True)
        dyp = (dyo * sc_ref[...]).astype(BF16)
        dyp_ref[...] = dyp
        for grp, ddr in enumerate((dd0, dd1, dd2, dd3)):
            ddr[...] = _dot_nt(dyp[:, grp * C_GROUP_DIM:(grp + 1) * C_GROUP_DIM], wg_ref[grp])

    dspec = pl.BlockSpec((tm, C_GROUP_DIM), lambda i: (i, 0))
    row = pl.BlockSpec((tm, D), lambda i: (i, 0))
    dshape = jax.ShapeDtypeStruct((t, C_GROUP_DIM), F32)
    return pl.pallas_call(
        body, name=name, grid=(t // tm,),
        in_specs=[row] + [dspec] * n_g + [pl.BlockSpec((n_g, C_GROUP_DIM, C_GROUP_DIM), lambda i: (0, 0, 0)),
                                          pl.BlockSpec((1, D), lambda i: (0, 0)), pl.BlockSpec((D, D), lambda i: (0, 0))],
        out_specs=[row] + [dspec] * n_g + [pl.BlockSpec((8, D), lambda i: (0, 0))],
        out_shape=[jax.ShapeDtypeStruct((t, D), BF16)] + [dshape] * n_g + [jax.ShapeDtypeStruct((8, D), F32)],
        compiler_params=_params("arbitrary"),
    )(dxm, *d, wgrp, scale, wo)


def loss_head(x, tgt, g, name):
    t = x.shape[0]
    tm = _row_tile(t, 512)

    def body(x_ref, t_ref, g_ref, dx_ref, dg_ref, loss_ref):
        @pl.when(pl.program_id(0) == 0)
        def _():
            dg_ref[...] = jnp.zeros_like(dg_ref)
            loss_ref[...] = jnp.zeros_like(loss_ref)

        xv, gvv = x_ref[...], g_ref[...]
        xh, _ = _rms(xv)
        diff = xh * gvv - t_ref[...]
        loss_ref[...] += 0.5 * jnp.sum(jnp.mean(diff * diff, axis=-1, keepdims=True))
        dx, dg = _rms_bwd(xv, gvv, diff * (1.0 / D))
        dx_ref[...] = dx
        dg_ref[0:1, :] += dg

    row = pl.BlockSpec((tm, D), lambda i: (i, 0))
    return pl.pallas_call(
        body, name=name, grid=(t // tm,),
        in_specs=[row, row, pl.BlockSpec((1, D), lambda i: (0, 0))],
        out_specs=[row, pl.BlockSpec((8, D), lambda i: (0, 0)), pl.BlockSpec((8, 128), lambda i: (0, 0))],
        out_shape=[jax.ShapeDtypeStruct((t, D), F32), jax.ShapeDtypeStruct((8, D), F32), jax.ShapeDtypeStruct((8, 128), F32)],
        compiler_params=_params("arbitrary"),
    )(x, tgt, g)


def adamw(w, g, m, v, name):
    rows, cols = w.shape
    tr = rows
    for cand in (512, 256, 128, 64, 32, 16, 8):
        if rows % cand == 0 and rows > cand:
            tr = cand
            break

    def body(w_ref, g_ref, m_ref, v_ref, d_ref, mo_ref, vo_ref):
        gv = g_ref[...]
        mn = ADAM_B1 * m_ref[...] + (1.0 - ADAM_B1) * gv
        vn = ADAM_B2 * v_ref[...] + (1.0 - ADAM_B2) * (gv * gv)
        m_hat = mn / (1.0 - ADAM_B1 ** ADAM_STEP)
        v_hat = vn / (1.0 - ADAM_B2 ** ADAM_STEP)
        d_ref[...] = -ADAM_LR * (m_hat / (jnp.sqrt(v_hat) + ADAM_EPS) + ADAM_WD * w_ref[...])
        mo_ref[...] = mn
        vo_ref[...] = vn

    spec = pl.BlockSpec((tr, cols), lambda i: (i, 0))
    shape = jax.ShapeDtypeStruct((rows, cols), F32)
    return pl.pallas_call(
        body, name=name, grid=(rows // tr,),
        in_specs=[spec] * 4, out_specs=[spec] * 3, out_shape=[shape] * 3,
        compiler_params=_params("arbitrary"),
    )(w, g, m, v)


def adamw_layer(w, m, v, own, recv, core, layer, carried, name):
    n_layers, rows, cols = w.shape
    h = rows // 2

    def body(core_ref, w_ref, m_ref, v_ref, own_ref, recv_ref, *rest):
        g_ref, d_ref, mo_ref, vo_ref = rest[-4:]
        gv = jnp.where(pl.program_id(0) == core_ref[0], own_ref[...], recv_ref[...])
        mn = ADAM_B1 * m_ref[0] + (1.0 - ADAM_B1) * gv
        vn = ADAM_B2 * v_ref[0] + (1.0 - ADAM_B2) * (gv * gv)
        m_hat = mn / (1.0 - ADAM_B1 ** ADAM_STEP)
        v_hat = vn / (1.0 - ADAM_B2 ** ADAM_STEP)
        g_ref[0] = gv
        d_ref[0] = -ADAM_LR * (m_hat / (jnp.sqrt(v_hat) + ADAM_EPS) + ADAM_WD * w_ref[0])
        mo_ref[0] = mn
        vo_ref[0] = vn

    stacked = pl.BlockSpec((1, h, cols), lambda half, core_ref: (layer, half, 0))
    halfspec = pl.BlockSpec((h, cols), lambda half, core_ref: (0, 0))
    n_carried = 0 if carried is None else 4
    shape = jax.ShapeDtypeStruct(w.shape, F32)
    return pl.pallas_call(
        body, name=name,
        grid_spec=pltpu.PrefetchScalarGridSpec(
            num_scalar_prefetch=1, grid=(2,),
            in_specs=[stacked] * 3 + [halfspec] * 2 + [ANY] * n_carried, out_specs=[stacked] * 4),
        out_shape=[shape] * 4,
        input_output_aliases={6 + i: i for i in range(n_carried)},
        compiler_params=_params("arbitrary"),
    )(core, w, m, v, own, recv, *(carried or ()))


def add_halves(gs, ps, core, name, deps=()):
    n = len(gs)

    def body(core_ref, *refs):
        for i in range(n):
            refs[2 * n + i][...] = (refs[i][...].astype(F32) + refs[n + i][...].astype(F32)).astype(BF16)

    in_specs, out_specs, out_shape = [], [], []
    for gt in gs:
        h, c = gt.shape[1] // 2, gt.shape[2]
        in_specs.append(pl.BlockSpec((1, h, c), lambda b, core_ref: (b, core_ref[0], 0)))
    for gt in gs:
        h, c = gt.shape[1] // 2, gt.shape[2]
        in_specs.append(pl.BlockSpec((1, h, c), lambda b, core_ref: (b, 0, 0)))
        out_specs.append(pl.BlockSpec((1, h, c), lambda b, core_ref: (b, 0, 0)))
        out_shape.append(jax.ShapeDtypeStruct((N_CHIPS, h, c), BF16))
    in_specs += [ANY] * len(deps)
    return pl.pallas_call(
        _after(body, 1 + 2 * n, deps), name=name,
        grid_spec=pltpu.PrefetchScalarGridSpec(num_scalar_prefetch=1, grid=(N_CHIPS,), in_specs=in_specs, out_specs=out_specs),
        out_shape=out_shape, compiler_params=_params("arbitrary"),
    )(core, *gs, *ps, *deps)


def add_final(hs, qs, chip, name, deps=()):
    n = len(hs)

    def body(chip_ref, *refs):
        for i in range(n):
            q = refs[n + i]
            refs[2 * n + i][...] = ((refs[i][0].astype(F32) + q[0].astype(F32)) + q[1].astype(F32)) + q[2].astype(F32)

    in_specs, out_specs, out_shape = [], [], []
    for ht in hs:
        h, c = ht.shape[1], ht.shape[2]
        in_specs.append(pl.BlockSpec((1, h, c), lambda i, chip_ref: (chip_ref[0], 0, 0)))
    for ht in hs:
        h, c = ht.shape[1], ht.shape[2]
        in_specs.append(pl.BlockSpec((N_CHIPS - 1, h, c), lambda i, chip_ref: (0, 0, 0)))
        out_specs.append(pl.BlockSpec((h, c), lambda i, chip_ref: (0, 0)))
        out_shape.append(jax.ShapeDtypeStruct((h, c), F32))
    in_specs += [ANY] * len(deps)
    return pl.pallas_call(
        _after(body, 1 + 2 * n, deps), name=name,
        grid_spec=pltpu.PrefetchScalarGridSpec(num_scalar_prefetch=1, grid=(1,), in_specs=in_specs, out_specs=out_specs),
        out_shape=out_shape, compiler_params=_params("arbitrary"),
    )(chip, *hs, *qs, *deps)


def sum_devices(own, gathered, me, name, deps=()):
    rows = own.shape[0]

    def body(me_ref, own_ref, g_ref, o_ref):
        me_dev = me_ref[0]
        acc = None
        for dev in range(8):
            slot = jnp.maximum((me_dev ^ dev) - 1, 0)
            term = jnp.where(me_dev == dev, own_ref[...], g_ref[slot])
            acc = term if acc is None else acc + term
        o_ref[...] = acc

    return pl.pallas_call(
        _after(body, 3, deps), name=name,
        grid_spec=pltpu.PrefetchScalarGridSpec(
            num_scalar_prefetch=1, grid=(rows // 8,),
            in_specs=[pl.BlockSpec((8, D), lambda i, me_ref: (i, 0)), pl.BlockSpec((7, 8, D), lambda i, me_ref: (0, i, 0))]
            + [ANY] * len(deps),
            out_specs=pl.BlockSpec((8, D), lambda i, me_ref: (i, 0))),
        out_shape=jax.ShapeDtypeStruct((rows, D), F32),
        compiler_params=_params("arbitrary"),
    )(me, own, gathered, *deps)


def _mesh_pos():
    return lax.axis_index("x"), lax.axis_index("y"), lax.axis_index("c")


def _other_chips(x, y):
    return [(1 - x, y), (x, 1 - y), (1 - x, 1 - y)]


def _sibling():
    x, y, c = _mesh_pos()
    return [(x, y, 1 - c)]


def _same_core_of_other_chips():
    x, y, c = _mesh_pos()
    return [(cx, cy, c) for (cx, cy) in _other_chips(x, y)]


def _on_sequencer(body, name, operands, out_shapes, sems, peers, collective_id, deps=()):
    ordered = _after(body, len(operands), deps)

    def seq_body(*refs):
        barrier = pltpu.get_barrier_semaphore()
        with_whom = peers()
        for peer in with_whom:
            pl.semaphore_signal(barrier, inc=1, device_id=peer, device_id_type=MESH)
        pl.semaphore_wait(barrier, len(with_whom))
        ordered(*refs)

    return pl.kernel(
        seq_body, name=name, out_type=out_shapes,
        mesh=plsc.ScalarSubcoreMesh(axis_name="seq", num_cores=1),
        scratch_types=sems, compiler_params=pltpu.CompilerParams(collective_id=collective_id, has_side_effects=True),
    )(*operands, *deps)


def all_gather_weights(shards, name, deps=()):
    n = len(shards)

    def body(*refs):
        ins, outs = refs[:n], refs[n:2 * n]
        send, recv, fsend, frecv = refs[2 * n:]
        x, y, c = _mesh_pos()
        k = 2 * x + y
        chips = _other_chips(x, y)

        def half(ref, i, rows_half):
            h = shards[i].shape[0] // 2
            return ref.at[pl.ds(pl.multiple_of(rows_half * h, 8), h), :]

        first = []
        for i in range(n):
            for j, (cx, cy) in enumerate(chips):
                first.append(pltpu.make_async_remote_copy(
                    src_ref=half(ins[i], i, c), dst_ref=half(outs[i].at[k], i, c),
                    send_sem=send.at[i, j], recv_sem=recv.at[i, j], device_id=(cx, cy, c), device_id_type=MESH))
        for cp in first:
            cp.start()
        passed = []
        for i in range(n):
            for j, (cx, cy) in enumerate(chips):
                blk = half(outs[i].at[2 * cx + cy], i, c)
                pltpu.make_async_remote_copy(src_ref=blk, dst_ref=blk, send_sem=send.at[i, j], recv_sem=recv.at[i, j],
                                             device_id=(cx, cy, c), device_id_type=MESH).wait_recv()
                fw = pltpu.make_async_remote_copy(src_ref=blk, dst_ref=blk, send_sem=fsend.at[i, j], recv_sem=frecv.at[i, j],
                                                  device_id=(x, y, 1 - c), device_id_type=MESH)
                fw.start()
                passed.append(fw)
        for i in range(n):
            for j, (cx, cy) in enumerate(chips):
                blk = half(outs[i].at[2 * cx + cy], i, 1 - c)
                pltpu.make_async_remote_copy(src_ref=blk, dst_ref=blk, send_sem=fsend.at[i, j], recv_sem=frecv.at[i, j],
                                             device_id=(x, y, 1 - c), device_id_type=MESH).wait_recv()
        for cp in first + passed:
            cp.wait_send()

    def peers():
        x, y, c = _mesh_pos()
        return [(cx, cy, c) for (cx, cy) in _other_chips(x, y)] + [(x, y, 1 - c)]

    return _on_sequencer(
        body, name, shards, [jax.ShapeDtypeStruct((N_CHIPS,) + s.shape, s.dtype) for s in shards],
        [pltpu.SemaphoreType.DMA((n, 3))] * 4, peers, GATHER_COLLECTIVE_ID, deps)


def place_own(gathered, shards, chip, name):
    n = len(shards)

    def body(chip_ref, *refs):
        for i in range(n):
            refs[2 * n + i][0] = refs[i][...]

    in_specs = [pl.BlockSpec(s.shape, lambda i, chip_ref: (0, 0)) for s in shards] + [ANY] * n
    out_specs = [pl.BlockSpec((1,) + s.shape, lambda i, chip_ref: (chip_ref[0], 0, 0)) for s in shards]
    return pl.pallas_call(
        body, name=name,
        grid_spec=pltpu.PrefetchScalarGridSpec(num_scalar_prefetch=1, grid=(1,), in_specs=in_specs, out_specs=out_specs),
        out_shape=[jax.ShapeDtypeStruct(g.shape, g.dtype) for g in gathered],
        input_output_aliases={1 + n + i: i for i in range(n)},
        compiler_params=_params("arbitrary"),
    )(chip, *shards, *gathered)


def all_gather_rows(shard, name):
    def body(in_ref, out_ref, send, recv, lsem):
        x, y, c = _mesh_pos()
        k = 2 * x + y
        chips = _other_chips(x, y)
        local = pltpu.make_async_copy(in_ref, out_ref.at[k], lsem)
        local.start()
        sent = [pltpu.make_async_remote_copy(src_ref=in_ref, dst_ref=out_ref.at[k], send_sem=send.at[j], recv_sem=recv.at[j],
                                             device_id=(cx, cy, c), device_id_type=MESH) for j, (cx, cy) in enumerate(chips)]
        for cp in sent:
            cp.start()
        for j, (cx, cy) in enumerate(chips):
            blk = out_ref.at[2 * cx + cy]
            pltpu.make_async_remote_copy(src_ref=blk, dst_ref=blk, send_sem=send.at[j], recv_sem=recv.at[j],
                                         device_id=(cx, cy, c), device_id_type=MESH).wait_recv()
        for cp in sent:
            cp.wait_send()
        local.wait()

    return pl.pallas_call(
        body, name=name, in_specs=[ANY], out_specs=ANY,
        out_shape=jax.ShapeDtypeStruct((N_CHIPS,) + shard.shape, shard.dtype),
        scratch_shapes=[pltpu.SemaphoreType.DMA((3,)), pltpu.SemaphoreType.DMA((3,)), pltpu.SemaphoreType.DMA],
    )(shard)


def swap_halves(gs, name):
    n = len(gs)

    def body(*refs):
        ins, outs = refs[:n], refs[n:2 * n]
        send, recv = refs[2 * n:]
        x, y, c = _mesh_pos()
        sent = []
        for i in range(n):
            h = gs[i].shape[1] // 2
            src = ins[i].at[:, pl.ds(pl.multiple_of((1 - c) * h, 8), h), :]
            cp = pltpu.make_async_remote_copy(src_ref=src, dst_ref=outs[i], send_sem=send.at[i], recv_sem=recv.at[i],
                                              device_id=(x, y, 1 - c), device_id_type=MESH)
            cp.start()
            sent.append(cp)
        for cp in sent:
            cp.wait()

    return _on_sequencer(
        body, name, gs, [jax.ShapeDtypeStruct((N_CHIPS, g.shape[1] // 2, g.shape[2]), g.dtype) for g in gs],
        [pltpu.SemaphoreType.DMA((n,)), pltpu.SemaphoreType.DMA((n,))], _sibling, SIBLING_COLLECTIVE_ID)


def scatter_chips(hs, name):
    n = len(hs)

    def body(*refs):
        ins, outs = refs[:n], refs[n:2 * n]
        send, recv = refs[2 * n:]
        x, y, c = _mesh_pos()
        chips = _other_chips(x, y)
        sent = []
        for i in range(n):
            for j, (cx, cy) in enumerate(chips):
                cp = pltpu.make_async_remote_copy(src_ref=ins[i].at[2 * cx + cy], dst_ref=outs[i].at[j],
                                                  send_sem=send.at[i, j], recv_sem=recv.at[i, j],
                                                  device_id=(cx, cy, c), device_id_type=MESH)
                cp.start()
                sent.append(cp)
        for cp in sent:
            cp.wait()

    return _on_sequencer(
        body, name, hs, [jax.ShapeDtypeStruct((N_CHIPS - 1,) + h.shape[1:], h.dtype) for h in hs],
        [pltpu.SemaphoreType.DMA((n, 3)), pltpu.SemaphoreType.DMA((n, 3))], _same_core_of_other_chips, CHIPS_COLLECTIVE_ID)


def swap_reduced(rs, name):
    n = len(rs)

    def body(*refs):
        ins, outs = refs[:n], refs[n:2 * n]
        send, recv = refs[2 * n:]
        x, y, c = _mesh_pos()
        sent = []
        for i in range(n):
            cp = pltpu.make_async_remote_copy(src_ref=ins[i], dst_ref=outs[i], send_sem=send.at[i], recv_sem=recv.at[i],
                                              device_id=(x, y, 1 - c), device_id_type=MESH)
            cp.start()
            sent.append(cp)
        for cp in sent:
            cp.wait()

    return _on_sequencer(
        body, name, rs, [jax.ShapeDtypeStruct(r.shape, r.dtype) for r in rs],
        [pltpu.SemaphoreType.DMA((n,)), pltpu.SemaphoreType.DMA((n,))], _sibling, SIBLING_COLLECTIVE_ID)


def all_gather_devices(part, name):
    def everyone_else():
        x, y, c = _mesh_pos()
        return [(1 - x if (rel >> 2) & 1 else x, 1 - y if (rel >> 1) & 1 else y, 1 - c if rel & 1 else c) for rel in range(1, 8)]

    def body(in_ref, out_ref, send, recv):
        sent = []
        for slot, peer in enumerate(everyone_else()):
            cp = pltpu.make_async_remote_copy(src_ref=in_ref, dst_ref=out_ref.at[slot], send_sem=send.at[slot],
                                              recv_sem=recv.at[slot], device_id=peer, device_id_type=MESH)
            cp.start()
            sent.append(cp)
        for cp in sent:
            cp.wait()

    return _on_sequencer(
        body, name, [part], jax.ShapeDtypeStruct((7,) + part.shape, part.dtype),
        [pltpu.SemaphoreType.DMA((7,)), pltpu.SemaphoreType.DMA((7,))], everyone_else, ALL_COLLECTIVE_ID)


class ReduceScatter:
    def __init__(self, grads, names, layer, core, chip, name):
        self.grads, self.names, self.layer, self.core, self.chip, self.name = grads, names, layer, core, chip, name
        self.from_sibling = swap_halves(grads, name + "_swap")

    def sum_cores(self, deps=()):
        self.core_sums = add_halves(self.grads, self.from_sibling, self.core, self.name + "_add2", deps)
        self.from_chips = scatter_chips(self.core_sums, self.name + "_scatter")
        return self.core_sums[0]

    def sum_chips(self, deps=()):
        self.half_sums = add_final(self.core_sums, self.from_chips, self.chip, self.name + "_add4", deps)
        self.other_half = swap_reduced(self.half_sums, self.name + "_join")
        return self.half_sums[0]


def _blocked(w):
    return w.reshape(w.shape[0] * w.shape[1], w.shape[2])


def _grp_from_blocks(w):
    return w.reshape(N_CHIPS, 4, 64, C_GROUP_DIM).transpose(1, 0, 2, 3).reshape(4, C_GROUP_DIM, C_GROUP_DIM)


def _grp_to_blocks(w):
    return w.reshape(4, N_CHIPS, 64, C_GROUP_DIM).transpose(1, 0, 2, 3).reshape(N_CHIPS, C_GROUP_DIM, C_GROUP_DIM)


def _dw_cols(h, dact, hs, name, deps):
    tm = 512
    return mm_tn(h[None], dact, (N_CHIPS, D, hs), tm, hs, N_CHIPS * (D // tm),
                 lambda j: (0, j % 2), lambda j: (0, j // 2), lambda j: (j // 2, j % 2, 0), name, deps)


def _dw_rows(y, dxm, name, deps):
    tm = 512
    out = mm_tn(y[None], dxm[None], (1, D, D), tm, D, D // tm, lambda j: (0, j), lambda j: (0, 0), lambda j: (0, j, 0), name, deps)
    return out.reshape(N_CHIPS, D // N_CHIPS, D)


def _dw_hidden(act, other, name, deps):
    tm, tn = FFN_HIDDEN // 2, 256
    n_n = D // tn
    out = mm_tn(act[None], other[None], (1, FFN_HIDDEN, D), tm, tn, 2 * n_n,
                lambda j: (0, j // n_n), lambda j: (0, j % n_n), lambda j: (0, j // n_n, j % n_n), name, deps)
    return out.reshape(N_CHIPS, FFN_SHARD, D)


def kernel(x, norm_mix_g, norm_ffn_g, final_norm_g, a_w_in, a_v_norm_g, a_w_s, a_b_s, a_w_out, b_w_in, b_conv_w, b_w_out, c_w_in, c_w_grp, c_scale, c_w_out, f_w_gate, f_w_up, f_w_down, loss_target, m_norm_mix_g, m_norm_ffn_g, m_final_norm_g, m_a_w_in, m_a_v_norm_g, m_a_w_s, m_a_b_s, m_a_w_out, m_b_w_in, m_b_conv_w, m_b_w_out, m_c_w_in, m_c_w_grp, m_c_scale, m_c_w_out, m_f_w_gate, m_f_w_up, m_f_w_down, v_norm_mix_g, v_norm_ffn_g, v_final_norm_g, v_a_w_in, v_a_v_norm_g, v_a_w_s, v_a_b_s, v_a_w_out, v_b_w_in, v_b_conv_w, v_b_w_out, v_c_w_in, v_c_w_grp, v_c_scale, v_c_w_out, v_f_w_gate, v_f_w_up, v_f_w_down):
    n_ex, seq, _ = x.shape
    t = n_ex * seq
    xi, yi, ci = lax.axis_index("x"), lax.axis_index("y"), lax.axis_index("c")
    chip = (2 * xi + yi).astype(jnp.int32)
    core_arr = ci.astype(jnp.int32).reshape(1)
    chip_arr = chip.reshape(1)
    me_arr = (4 * xi + 2 * yi + ci).astype(jnp.int32).reshape(1)
    bf = lambda w: w.astype(BF16)

    pad8 = lambda v: jnp.pad(v, ((0, 8 - v.shape[0]), (0, 0)))
    small_rows = jnp.concatenate([pad8(a_v_norm_g), pad8(b_conv_w[0]), pad8(c_scale)], axis=0)
    small_gathered = all_gather_rows(small_rows, "ag_small")
    small_full = small_gathered.transpose(1, 0, 2).reshape(24, D)
    gv_full = [small_full[0:1], small_full[1:2]]
    cw_full = small_full[8:11]
    scale_full = small_full[16:17]

    mixer_shards = [
        [bf(a_w_in[0]), bf(a_w_out[0])],
        [bf(b_w_in[0]), bf(b_w_out[0])],
        [bf(c_w_in[0]), bf(c_w_grp[0]).reshape(C_GROUP_DIM, C_GROUP_DIM), bf(c_w_out[0])],
        [bf(a_w_in[1]), bf(a_w_out[1])],
    ]
    hidden_major = lambda w: jnp.swapaxes(w, 1, 2)
    gate_t, up_t = hidden_major(f_w_gate), hidden_major(f_w_up)
    gathered = []
    for i in range(4):
        ffn_shards = [bf(gate_t[i]), bf(up_t[i]), bf(f_w_down[i])]
        if i == 0:
            parts = [(mixer_shards[0], "ag_l0_mixer", [small_gathered]), (ffn_shards, "ag_l0_ffn", [])]
        else:
            parts = [(mixer_shards[i] + ffn_shards, f"ag_l{i}", [])]
        layer = []
        for shards, name, deps in parts:
            layer += place_own(all_gather_weights(shards, name, deps), shards, chip_arr, name.replace("ag", "own"))
        gathered.append(layer)

    mask = (jnp.arange(GMLP_BLOCK)[None, :] // 64) <= (jnp.arange(GMLP_BLOCK)[:, None] // 64)
    gmix = [norm_mix_g[i:i + 1] for i in range(4)]
    gffn = [norm_ffn_g[i:i + 1] for i in range(4)]
    a_chunks = [(s, 0, 512, 0, s * 512) for s in range(N_CHIPS)]
    b_chunks = [(j // 3, (j % 3) * 256, 256, j // 4, (j % 4) * 256) for j in range(12)]
    c_chunks = [(0, 0, D, 0, 0)]

    xs = [x.reshape(t, D)]
    saved = []
    for i in range(4):
        ws = gathered[i]
        wg, wu, wd = (w.reshape(FFN_HIDDEN, D) for w in ws[-3:])
        xin = xs[-1]
        if i in (0, 3):
            j = 0 if i == 0 else 1
            win, wout = ws[0], _blocked(ws[1])
            wm32 = jnp.where(mask[None], a_w_s[j], 0.0)
            wm, wmt = bf(wm32), bf(wm32.transpose(0, 2, 1))
            bs = jnp.broadcast_to(a_b_s[j][:, :, None], (A_GROUPS, GMLP_BLOCK, 128))
            h, z = norm_mm(xin, gmix[i], win, a_chunks, 1, 2 * D, f"a_in_l{i}")
            y, vn = a_mid_fwd(z[0], gv_full[j], wm, bs, f"a_mid_l{i}")
            xmid = mm_res(y, wout, xin, f"a_out_l{i}")
            saved.append(dict(h=h, z=z, y=y, vn=vn, win=win, wout=wout, wm=wm, wmt=wmt, bs=bs, gv=gv_full[j]))
        elif i == 1:
            win, wout = ws[0], _blocked(ws[1])
            h, p3 = norm_mm(xin, gmix[i], win, b_chunks, 3, D, "b_in")
            y = b_conv_fwd(p3, cw_full, seq, "b_conv")
            xmid = mm_res(y, wout, xin, "b_out")
            saved.append(dict(h=h, p3=p3, y=y, win=win, wout=wout))
        else:
            win, wgrp, wout = _blocked(ws[0])[None], _grp_from_blocks(ws[1]), _blocked(ws[2])
            h, p = norm_mm(xin, gmix[i], win, c_chunks, 1, D, "c_in")
            dpool = c_pool_fwd(p[0], seq, "c_pool")
            y, xmid = c_out_fwd(dpool, wgrp, scale_full, wout, xin, "c_out")
            saved.append(dict(h=h, d=dpool, y=y, win=win, wgrp=wgrp, wout=wout))
        h2, fa, fb, fs, xout = ffn_fwd(xmid, gffn[i], wg, wu, wd, f"ffn_l{i}")
        saved[-1].update(h2=h2, fa=fa, fb=fb, fs=fs, xmid=xmid, wg=wg, wu=wu, wd=wd)
        xs.append(xout)

    dx, dg_final, loss_part = loss_head(xs[4], loss_target.reshape(t, D), final_norm_g[None], "loss_head")
    loss = lax.psum(loss_part[0, 0], ("x", "y", "c"))

    weights = dict(norm_mix_g=norm_mix_g, norm_ffn_g=norm_ffn_g, final_norm_g=final_norm_g, a_w_in=a_w_in, a_v_norm_g=a_v_norm_g,
                   a_w_s=a_w_s, a_b_s=a_b_s, a_w_out=a_w_out, b_w_in=b_w_in, b_conv_w=b_conv_w, b_w_out=b_w_out, c_w_in=c_w_in,
                   c_w_grp=c_w_grp, c_scale=c_scale, c_w_out=c_w_out, f_w_gate=f_w_gate, f_w_up=f_w_up, f_w_down=f_w_down)
    m_in = dict(norm_mix_g=m_norm_mix_g, norm_ffn_g=m_norm_ffn_g, final_norm_g=m_final_norm_g, a_w_in=m_a_w_in, a_v_norm_g=m_a_v_norm_g,
                a_w_s=m_a_w_s, a_b_s=m_a_b_s, a_w_out=m_a_w_out, b_w_in=m_b_w_in, b_conv_w=m_b_conv_w, b_w_out=m_b_w_out, c_w_in=m_c_w_in,
                c_w_grp=m_c_w_grp, c_scale=m_c_scale, c_w_out=m_c_w_out, f_w_gate=m_f_w_gate, f_w_up=m_f_w_up, f_w_down=m_f_w_down)
    v_in = dict(norm_mix_g=v_norm_mix_g, norm_ffn_g=v_norm_ffn_g, final_norm_g=v_final_norm_g, a_w_in=v_a_w_in, a_v_norm_g=v_a_v_norm_g,
                a_w_s=v_a_w_s, a_b_s=v_a_b_s, a_w_out=v_a_w_out, b_w_in=v_b_w_in, b_conv_w=v_b_conv_w, b_w_out=v_b_w_out, c_w_in=v_c_w_in,
                c_w_grp=v_c_w_grp, c_scale=v_c_scale, c_w_out=v_c_w_out, f_w_gate=v_f_w_gate, f_w_up=v_f_w_up, f_w_down=v_f_w_down)
    grp_rows = lambda a: a.reshape(1, C_GROUP_DIM, C_GROUP_DIM)
    same = lambda a: a
    to_stacked = {nme: same for nme in ("a_w_in", "a_w_out", "b_w_in", "b_w_out", "c_w_in", "c_w_out", "f_w_down")}
    to_stacked.update(f_w_gate=hidden_major, f_w_up=hidden_major, c_w_grp=grp_rows)
    from_stacked = dict(to_stacked, c_w_grp=lambda a: a.reshape(c_w_grp.shape))
    layer_tensors = {0: ["a_w_out", "a_w_in"], 1: ["b_w_out", "b_w_in"], 2: ["c_w_out", "c_w_in", "c_w_grp"], 3: ["a_w_out", "a_w_in"]}
    carried = {}

    def bs_rows(v):
        return jnp.pad(v[:, :, 0].reshape(1, D), ((0, 7), (0, 0)))

    def update(unit):
        done = []
        for pos, nme in enumerate(unit.names):
            stacked_layer = unit.layer if nme.startswith("f_") else (unit.layer // 3 if nme.startswith("a_") else 0)
            view = to_stacked[nme]
            carried[nme] = adamw_layer(view(weights[nme]), view(m_in[nme]), view(v_in[nme]), unit.half_sums[pos], unit.other_half[pos],
                                       core_arr, stacked_layer, carried.get(nme), f"adamw_{nme}_l{unit.layer}")
            done.append(carried[nme][0])
        return done

    ffn_names = ["f_w_gate", "f_w_up", "f_w_down"]
    dg_mix, dg_ffn = [None] * 4, [None] * 4
    small = {}
    newer = older = None
    for i in (3, 2, 1, 0):
        sv = saved[i]
        xin = xs[i]
        deps = ([newer.grads[0]] if newer else []) + ([older.half_sums[0]] if older else [])
        dxm, da, db, dg_ffn[i] = ffn_bwd(dx, sv["fa"], sv["fb"], sv["xmid"], gffn[i], sv["wg"], sv["wu"], sv["wd"], f"ffn_bwd_l{i}", deps)
        last = [newer.sum_cores([dxm])] if newer else []
        g_gate = _dw_hidden(da, sv["h2"], f"dw_gate_l{i}", last)
        g_up = _dw_hidden(db, sv["h2"], f"dw_up_l{i}", [g_gate])
        g_down = _dw_hidden(sv["fs"], dx, f"dw_down_l{i}", [g_up])
        last_ffn = [g_down]
        if i == 0:
            ffn_unit = ReduceScatter([g_down, g_gate, g_up], ["f_w_down", "f_w_gate", "f_w_up"], 0, core_arr, chip_arr, "rs_l0_ffn")
        if i in (0, 3):
            j = 0 if i == 0 else 1
            dy = mm_nt(dxm, sv["wout"], f"a_dy_l{i}")
            dz, dwm, dbs, dgv = a_mid_bwd(dy, sv["z"][0], sv["vn"], sv["gv"], sv["wm"], sv["wmt"], sv["bs"], f"a_mid_bwd_l{i}")
            if i == 0:
                early = sum_devices(early_rows, early_gathered, me_arr, "sum_small_grads_l123")
                newer.sum_chips([dz] + update(older) + [early])
                last_ffn.append(ffn_unit.sum_cores([newer.half_sums[0]]))
            dz = dz[None]
            dx, dg_mix[i] = bwd_in(dz, sv["win"], a_chunks, xin, gmix[i], dxm, f"a_bwd_in_l{i}")
            g_in = _dw_cols(sv["h"], dz, 512, f"dw_a_in_l{i}", last_ffn)
            g_out = _dw_rows(sv["y"], dxm, f"dw_a_out_l{i}", [g_in])
            small[f"wm{j}"], small[f"bs{j}"], small[f"gv{j}"] = dwm, dbs, dgv
            mixer_grads = [g_out, g_in]
        elif i == 1:
            dy = mm_nt(dxm, sv["wout"], "b_dy")
            dp3, small["cw"] = b_conv_bwd(dy, sv["p3"], cw_full, seq, "b_conv_bwd")
            dx, dg_mix[i] = bwd_in(dp3, sv["win"], b_chunks, xin, gmix[i], dxm, "b_bwd_in")
            g_in = mm_tn(sv["h"][None], dp3, (N_CHIPS, D, 768), 512, 256, 24,
                         lambda j: (0, j % 2), lambda j: ((j // 2) // 4, (j // 2) % 4),
                         lambda j: ((j // 2) // 3, j % 2, (j // 2) % 3), "dw_b_in", [g_down])
            g_out = _dw_rows(sv["y"], dxm, "dw_b_out", [g_in])
            mixer_grads = [g_out, g_in]
        else:
            outs = c_out_bwd(dxm, sv["d"], sv["wgrp"], scale_full, sv["wout"], "c_out_bwd")
            dyp, dd, small["scale"] = outs[0], list(outs[1:5]), outs[5]
            dpool = c_pool_bwd(dd, seq, "c_pool_bwd")
            dp = jnp.concatenate(dpool, axis=1)[None]
            dx, dg_mix[i] = bwd_in(dp, sv["win"], c_chunks, xin, gmix[i], dxm, "c_bwd_in")
            g_in = _dw_rows(sv["h"], dp[0], "dw_c_in", [g_down])
            dcat = jnp.concatenate(sv["d"], axis=1)
            g_grp = mm_tn(dcat[None], dyp[None], (4, C_GROUP_DIM, C_GROUP_DIM), C_GROUP_DIM, C_GROUP_DIM, 4,
                          lambda j: (0, j), lambda j: (0, j), lambda j: (j, 0, 0), "dw_c_grp", [g_in])
            g_out = _dw_rows(sv["y"], dxm, "dw_c_out", [g_grp])
            mixer_grads = [g_out, g_in, _grp_to_blocks(g_grp)]
        if i > 0:
            unit = ReduceScatter(mixer_grads + [g_gate, g_up, g_down], layer_tensors[i] + ffn_names, i, core_arr, chip_arr, f"rs_l{i}")
        else:
            unit = ReduceScatter(mixer_grads, layer_tensors[0], 0, core_arr, chip_arr, "rs_l0_mixer")
        if newer and i > 0:
            newer.sum_chips([mixer_grads[0]] + (update(older) if older else []))
        if i == 1:
            early_rows = jnp.concatenate(dg_mix[1:] + dg_ffn[1:] + [dg_final, small["gv1"], small["cw"], small["scale"],
                                                                   small["wm1"].reshape(128, D), bs_rows(small["bs1"])], axis=0)
            early_gathered = all_gather_devices(early_rows, "ag_small_grads_l123")
        older, newer = newer, unit
    grad_x = dx.reshape(n_ex, seq, D)
    mixer_unit = newer
    ffn_unit.sum_chips(update(older) + [mixer_unit.grads[0]])
    mixer_unit.sum_cores([ffn_unit.half_sums[0]])
    mixer_unit.sum_chips(update(ffn_unit))
    updated = update(mixer_unit)

    late_rows = jnp.concatenate([dg_mix[0], dg_ffn[0], small["gv0"], small["wm0"].reshape(128, D), bs_rows(small["bs0"])], axis=0)
    late = sum_devices(late_rows, all_gather_devices(late_rows, "ag_small_grads_l0"), me_arr, "sum_small_grads_l0", updated)
    first_rows = lambda a, b, n: jnp.concatenate([a, b], axis=0).reshape(n, 8, D)[:, 0]
    g_norm_mix = first_rows(late[0:8], early[0:24], 4)
    g_norm_ffn = first_rows(late[8:16], early[24:48], 4)
    g_final = early[48]
    g_gv = first_rows(late[16:24], early[56:64], 2)
    g_cw = early[64:67]
    g_scale = early[72:73]
    g_ws = jnp.where(mask[None, None], jnp.concatenate([late[24:152], early[80:208]], axis=0).reshape(2, A_GROUPS, 128, 128), 0.0)
    g_bs = first_rows(late[152:160], early[208:216], 2).reshape(2, A_GROUPS, 128)
    col0 = chip * (D // N_CHIPS)
    cols = lambda v: lax.dynamic_slice_in_dim(v, col0, D // N_CHIPS, axis=1)

    small_grads = {
        "norm_mix_g": g_norm_mix, "norm_ffn_g": g_norm_ffn, "final_norm_g": g_final, "a_v_norm_g": cols(g_gv), "a_w_s": g_ws,
        "a_b_s": g_bs, "b_conv_w": cols(g_cw)[None], "c_scale": cols(g_scale),
    }
    results = {}
    for nme, g in small_grads.items():
        w = weights[nme]
        flat = lambda a: a.reshape(-1, w.shape[-1])
        dl, mn, vn = adamw(flat(w), flat(g), flat(m_in[nme]), flat(v_in[nme]), f"adamw_{nme}")
        results[nme] = tuple(o.reshape(w.shape) for o in (g, dl, mn, vn))
    for nme, outs in carried.items():
        results[nme] = tuple(from_stacked[nme](o) for o in outs)

    names = list(weights)
    return (loss, grad_x, *[results[n][0] for n in names], *[results[n][1] for n in names],
            *[results[n][2] for n in names], *[results[n][3] for n in names])
```

```python
import jax
import jax.numpy as jnp
from jax import lax
from jax.experimental import pallas as pl
from jax.experimental.pallas import tpu as pltpu
from jax.experimental.pallas import tpu_sc as plsc

F32 = jnp.float32
BF16 = jnp.bfloat16
D = 1024
FFN_SHARD = 704
GMLP_BLOCK = 128
A_GROUPS = 8
POOL_WINDOWS = (2, 4, 8, 16)
C_GROUP_DIM = 256
N_CHIPS = 4
EPS = 1e-6
ADAM_LR, ADAM_B1, ADAM_B2, ADAM_EPS, ADAM_WD, ADAM_STEP = 0.001, 0.9, 0.999, 1e-08, 0.01, 10
VMEM_LIMIT_BYTES = 56 * 1024 * 1024
FFN_HIDDEN = N_CHIPS * FFN_SHARD
FFN_CHUNKS = ((0, 768), (768, 768), (1536, 768), (2304, 512))
FFN_FWD_ROWS = 512
FFN_BWD_ROWS = 256
A_ROWS = 256
MESH = pl.DeviceIdType.MESH
GATHER_COLLECTIVE_ID = 1
SIBLING_COLLECTIVE_ID = 2
CHIPS_COLLECTIVE_ID = 3
ALL_COLLECTIVE_ID = 4
ANY = pl.BlockSpec(memory_space=pl.ANY)
NT_DIMS = (((1,), (1,)), ((), ()))
TN_DIMS = (((0,), (0,)), ((), ()))
INV_SQRT2 = 0.7071067811865476
INV_SQRT_2PI = 0.3989422804014327


def _params(*semantics):
    return pltpu.CompilerParams(dimension_semantics=semantics, vmem_limit_bytes=VMEM_LIMIT_BYTES)


def _dot(a, b):
    return jnp.dot(a, b, preferred_element_type=F32)


def _dot_nt(a, b):
    return lax.dot_general(a, b, NT_DIMS, preferred_element_type=F32)


def _rms(x):
    r = lax.rsqrt(jnp.mean(x * x, axis=-1, keepdims=True) + EPS)
    return x * r, r


def _rms_bwd(x, g, dh):
    xh, r = _rms(x)
    dxh = dh * g
    dx = r * (dxh - xh * jnp.mean(dxh * xh, axis=-1, keepdims=True))
    return dx, jnp.sum(dh * xh, axis=0, keepdims=True)


def _gelu(x):
    return 0.5 * x * (1.0 + lax.erf(x * INV_SQRT2))


def _gelu_grad(x):
    return 0.5 * (1.0 + lax.erf(x * INV_SQRT2)) + x * jnp.exp(-0.5 * x * x) * INV_SQRT_2PI


def _shift_down(v, s, row):
    return jnp.where(row >= s, pltpu.roll(v, s, 0), 0.0)


def _shift_up(v, s, row):
    n = v.shape[0]
    return jnp.where(row < n - s, pltpu.roll(v, n - s, 0), 0.0)


def _row_tile(t, want):
    return want if t % want == 0 else t


def _after(body, first, deps):
    if not deps:
        return body

    def ordered(*refs):
        return body(*refs[:first], *refs[first + len(deps):])

    return ordered


def norm_mm(x, g, w, chunks, n_parts, part_width, name):
    t = x.shape[0]
    tm = _row_tile(t, 512)
    n_shards, _, hs = w.shape

    def body(x_ref, g_ref, w_ref, h_ref, p_ref):
        xh, _ = _rms(x_ref[...])
        h = (xh * g_ref[...]).astype(BF16)
        h_ref[...] = h
        for s in range(n_shards):
            res = _dot(h, w_ref[s]).astype(BF16)
            for (cs, wc, width, part, pc) in chunks:
                if cs == s:
                    p_ref[part, :, pc:pc + width] = res[:, wc:wc + width]

    return pl.pallas_call(
        body, name=name, grid=(t // tm,),
        in_specs=[pl.BlockSpec((tm, D), lambda i: (i, 0)), pl.BlockSpec((1, D), lambda i: (0, 0)),
                  pl.BlockSpec((n_shards, D, hs), lambda i: (0, 0, 0))],
        out_specs=[pl.BlockSpec((tm, D), lambda i: (i, 0)), pl.BlockSpec((n_parts, tm, part_width), lambda i: (0, i, 0))],
        out_shape=[jax.ShapeDtypeStruct((t, D), BF16), jax.ShapeDtypeStruct((n_parts, t, part_width), BF16)],
        compiler_params=_params("arbitrary"),
    )(x, g, w)


def mm_res(a, w, res, name):
    t, k = a.shape
    n = w.shape[1]
    tm = _row_tile(t, 512)

    def body(a_ref, w_ref, r_ref, o_ref):
        o_ref[...] = r_ref[...] + _dot(a_ref[...], w_ref[...])

    return pl.pallas_call(
        body, name=name, grid=(t // tm,),
        in_specs=[pl.BlockSpec((tm, k), lambda i: (i, 0)), pl.BlockSpec((k, n), lambda i: (0, 0)),
                  pl.BlockSpec((tm, n), lambda i: (i, 0))],
        out_specs=pl.BlockSpec((tm, n), lambda i: (i, 0)),
        out_shape=jax.ShapeDtypeStruct((t, n), F32),
        compiler_params=_params("arbitrary"),
    )(a, w, res)


def mm_nt(a, w, name):
    t, n = a.shape
    k = w.shape[0]
    tm = _row_tile(t, 512)

    def body(a_ref, w_ref, o_ref):
        o_ref[...] = _dot_nt(a_ref[...].astype(BF16), w_ref[...]).astype(BF16)

    return pl.pallas_call(
        body, name=name, grid=(t // tm,),
        in_specs=[pl.BlockSpec((tm, n), lambda i: (i, 0)), pl.BlockSpec((k, n), lambda i: (0, 0))],
        out_specs=pl.BlockSpec((tm, k), lambda i: (i, 0)),
        out_shape=jax.ShapeDtypeStruct((t, k), BF16),
        compiler_params=_params("arbitrary"),
    )(a, w)


def bwd_in(dp, w, chunks, x, g, dres, name):
    n_parts, t, part_width = dp.shape
    n_shards, _, hs = w.shape
    tm = _row_tile(t, 512)

    def body(dp_ref, w_ref, x_ref, g_ref, dres_ref, dx_ref, dg_ref):
        acc = jnp.zeros((tm, D), F32)
        for (cs, wc, width, part, pc) in chunks:
            acc = acc + _dot_nt(dp_ref[part, :, pc:pc + width], w_ref[cs, :, wc:wc + width])
        dx, dg = _rms_bwd(x_ref[...], g_ref[...], acc)
        dx_ref[...] = dres_ref[...] + dx

        @pl.when(pl.program_id(0) == 0)
        def _():
            dg_ref[...] = jnp.zeros_like(dg_ref)

        dg_ref[0:1, :] += dg

    return pl.pallas_call(
        body, name=name, grid=(t // tm,),
        in_specs=[pl.BlockSpec((n_parts, tm, part_width), lambda i: (0, i, 0)),
                  pl.BlockSpec((n_shards, D, hs), lambda i: (0, 0, 0)),
                  pl.BlockSpec((tm, D), lambda i: (i, 0)), pl.BlockSpec((1, D), lambda i: (0, 0)),
                  pl.BlockSpec((tm, D), lambda i: (i, 0))],
        out_specs=[pl.BlockSpec((tm, D), lambda i: (i, 0)), pl.BlockSpec((8, D), lambda i: (0, 0))],
        out_shape=[jax.ShapeDtypeStruct((t, D), F32), jax.ShapeDtypeStruct((8, D), F32)],
        compiler_params=_params("arbitrary"),
    )(dp, w, x, g, dres)


def mm_tn(a, b, out_shape, tm, tn, n_tiles, a_idx, b_idx, o_idx, name, deps=()):
    t = a.shape[1]

    def body(a_ref, b_ref, o_ref):
        o_ref[0] = lax.dot_general(a_ref[0].astype(BF16), b_ref[0].astype(BF16), TN_DIMS, preferred_element_type=F32).astype(BF16)

    return pl.pallas_call(
        _after(body, 2, deps), name=name, grid=(n_tiles,),
        in_specs=[pl.BlockSpec((1, t, tm), lambda j: (a_idx(j)[0], 0, a_idx(j)[1])),
                  pl.BlockSpec((1, t, tn), lambda j: (b_idx(j)[0], 0, b_idx(j)[1]))] + [ANY] * len(deps),
        out_specs=pl.BlockSpec((1, tm, tn), lambda j: o_idx(j)),
        out_shape=jax.ShapeDtypeStruct(out_shape, BF16),
        compiler_params=_params("arbitrary"),
    )(a, b, *deps)


def ffn_fwd(x, g, wg, wu, wd, name):
    t = x.shape[0]
    tm = _row_tile(t, FFN_FWD_ROWS)
    hidden = wg.shape[0]

    def body(x_ref, g_ref, wg_ref, wu_ref, wd_ref, h_ref, a_ref, b_ref, s_ref, o_ref):
        xv = x_ref[...]
        xh, _ = _rms(xv)
        h = (xh * g_ref[...]).astype(BF16)
        h_ref[...] = h
        acc = xv
        for c0, cw in FFN_CHUNKS:
            cols = slice(c0, c0 + cw)
            a = _dot_nt(h, wg_ref[cols, :])
            b = _dot_nt(h, wu_ref[cols, :])
            sig = jax.nn.sigmoid(a)
            silu = a * sig
            s = (silu * b).astype(BF16)
            a_ref[:, cols] = (b * (sig * (1.0 + a * (1.0 - sig)))).astype(BF16)
            b_ref[:, cols] = silu.astype(BF16)
            s_ref[:, cols] = s
            acc = acc + _dot(s, wd_ref[cols, :])
        o_ref[...] = acc

    act = pl.BlockSpec((tm, hidden), lambda i: (i, 0))
    act_shape = jax.ShapeDtypeStruct((t, hidden), BF16)
    wspec = pl.BlockSpec((hidden, D), lambda i: (0, 0), pipeline_mode=pl.Buffered(1))
    row = pl.BlockSpec((tm, D), lambda i: (i, 0))
    return pl.pallas_call(
        body, name=name, grid=(t // tm,),
        in_specs=[row, pl.BlockSpec((1, D), lambda i: (0, 0)), wspec, wspec, wspec],
        out_specs=[row, act, act, act, row],
        out_shape=[jax.ShapeDtypeStruct((t, D), BF16), act_shape, act_shape, act_shape, jax.ShapeDtypeStruct((t, D), F32)],
        compiler_params=_params("arbitrary"),
    )(x, g, wg, wu, wd)


def ffn_bwd(dxo, a, b, x, g, wg, wu, wd, name, deps=()):
    t = x.shape[0]
    tm = _row_tile(t, FFN_BWD_ROWS)
    hidden = wg.shape[0]

    def body(dxo_ref, a_ref, b_ref, x_ref, g_ref, wg_ref, wu_ref, wd_ref, dx_ref, da_ref, db_ref, dg_ref):
        @pl.when(pl.program_id(0) == 0)
        def _():
            dg_ref[...] = jnp.zeros_like(dg_ref)

        dxo = dxo_ref[...]
        dxb = dxo.astype(BF16)
        acc = None
        for c0, cw in FFN_CHUNKS:
            cols = slice(c0, c0 + cw)
            ds = _dot_nt(dxb, wd_ref[cols, :])
            da = (ds * a_ref[:, cols].astype(F32)).astype(BF16)
            db = (ds * b_ref[:, cols].astype(F32)).astype(BF16)
            da_ref[:, cols] = da
            db_ref[:, cols] = db
            part = _dot(da, wg_ref[cols, :]) + _dot(db, wu_ref[cols, :])
            acc = part if acc is None else acc + part
        dx, dg = _rms_bwd(x_ref[...], g_ref[...], acc)
        dx_ref[...] = dxo + dx
        dg_ref[0:1, :] += dg

    act = pl.BlockSpec((tm, hidden), lambda i: (i, 0))
    act_shape = jax.ShapeDtypeStruct((t, hidden), BF16)
    row = pl.BlockSpec((tm, D), lambda i: (i, 0))
    wspec = pl.BlockSpec((hidden, D), lambda i: (0, 0), pipeline_mode=pl.Buffered(1))
    return pl.pallas_call(
        _after(body, 8, deps), name=name, grid=(t // tm,),
        in_specs=[row, act, act, row, pl.BlockSpec((1, D), lambda i: (0, 0)), wspec, wspec, wspec] + [ANY] * len(deps),
        out_specs=[row, act, act, pl.BlockSpec((8, D), lambda i: (0, 0))],
        out_shape=[jax.ShapeDtypeStruct((t, D), F32), act_shape, act_shape, jax.ShapeDtypeStruct((8, D), F32)],
        compiler_params=_params("arbitrary"),
    )(dxo, a, b, x, g, wg, wu, wd, *deps)


def _layer_norm_stats(v):
    mu = jnp.mean(v, axis=-1, keepdims=True)
    vc = v - mu
    rstd = lax.rsqrt(jnp.mean(vc * vc, axis=-1, keepdims=True) + EPS)
    return vc * rstd, rstd


def a_fwd(x, g, win, gv, wm, bs, wout, name):
    t = x.shape[0]
    tm = _row_tile(t, A_ROWS)
    n_shards, _, hs = win.shape

    def body(x_ref, g_ref, win_ref, gv_ref, wm_ref, bs_ref, wout_ref, h_ref, z_ref, vn_ref, y_ref, o_ref):
        xv = x_ref[...]
        xh, _ = _rms(xv)
        h = (xh * g_ref[...]).astype(BF16)
        h_ref[...] = h
        zs = []
        for s in range(n_shards):
            zb = _dot(h, win_ref[s]).astype(BF16)
            z_ref[:, s * hs:(s + 1) * hs] = zb
            zs.append(zb.astype(F32))
        half = n_shards // 2
        u = _gelu(jnp.concatenate(zs[:half], axis=1))
        vhat, _ = _layer_norm_stats(_gelu(jnp.concatenate(zs[half:], axis=1)))
        vnb = (vhat * gv_ref[...]).astype(BF16)
        vn_ref[...] = vnb
        for n in range(tm // GMLP_BLOCK):
            rows = slice(n * GMLP_BLOCK, (n + 1) * GMLP_BLOCK)
            for grp in range(A_GROUPS):
                cols = slice(grp * 128, (grp + 1) * 128)
                sv = _dot(wm_ref[grp], vnb[rows, cols]) + bs_ref[grp]
                y_ref[rows, cols] = (u[rows, cols] * sv).astype(BF16)
        o_ref[...] = xv + _dot(y_ref[...], wout_ref[...])

    small = pl.BlockSpec((A_GROUPS, 128, 128), lambda i: (0, 0, 0))
    row = pl.BlockSpec((tm, D), lambda i: (i, 0))
    wide = pl.BlockSpec((tm, 2 * D), lambda i: (i, 0))
    gain = pl.BlockSpec((1, D), lambda i: (0, 0))
    return pl.pallas_call(
        body, name=name, grid=(t // tm,),
        in_specs=[row, gain, pl.BlockSpec((n_shards, D, hs), lambda i: (0, 0, 0)), gain, small, small,
                  pl.BlockSpec((D, D), lambda i: (0, 0))],
        out_specs=[row, wide, row, row, row],
        out_shape=[jax.ShapeDtypeStruct((t, D), BF16), jax.ShapeDtypeStruct((t, 2 * D), BF16), jax.ShapeDtypeStruct((t, D), BF16),
                   jax.ShapeDtypeStruct((t, D), BF16), jax.ShapeDtypeStruct((t, D), F32)],
        compiler_params=_params("arbitrary"),
    )(x, g, win, gv, wm, bs, wout)


def a_bwd(dxm, x, g, z, vn, gv, wm, wmt, bs, wout, win, name):
    t = x.shape[0]
    tm = _row_tile(t, A_ROWS)
    n_shards, _, hs = win.shape

    def body(dxm_ref, x_ref, g_ref, z_ref, vn_ref, gv_ref, wm_ref, wmt_ref, bs_ref, wout_ref, win_ref,
             dx_ref, dz_ref, dwm_ref, dbs_ref, dgv_ref, dg_ref, du_ref, dvn_ref):
        @pl.when(pl.program_id(0) == 0)
        def _():
            dwm_ref[...] = jnp.zeros_like(dwm_ref)
            dbs_ref[...] = jnp.zeros_like(dbs_ref)
            dgv_ref[...] = jnp.zeros_like(dgv_ref)
            dg_ref[...] = jnp.zeros_like(dg_ref)

        dxm = dxm_ref[...]
        dyv = _dot_nt(dxm.astype(BF16), wout_ref[...])
        zz = z_ref[...].astype(F32)
        zu, zv = zz[:, :D], zz[:, D:]
        u = _gelu(zu)
        vhat, rstd = _layer_norm_stats(_gelu(zv))
        vnb = vn_ref[...]
        ones = jnp.ones((128, 128), BF16)
        for n in range(tm // GMLP_BLOCK):
            rows = slice(n * GMLP_BLOCK, (n + 1) * GMLP_BLOCK)
            for grp in range(A_GROUPS):
                cols = slice(grp * 128, (grp + 1) * 128)
                blk = vnb[rows, cols]
                sv = _dot(wm_ref[grp], blk) + bs_ref[grp]
                dyb = dyv[rows, cols]
                du_ref[rows, cols] = dyb * sv
                dsv = (dyb * u[rows, cols]).astype(BF16)
                dvn_ref[rows, cols] = _dot(wmt_ref[grp], dsv)
                dwm_ref[grp] += _dot_nt(dsv, blk)
                dbs_ref[grp] += _dot(dsv, ones)
        dvn = dvn_ref[...]
        dgv_ref[0:1, :] += jnp.sum(dvn * vhat, axis=0, keepdims=True)
        dvh = dvn * gv_ref[...]
        dv = rstd * (dvh - jnp.mean(dvh, axis=-1, keepdims=True) - vhat * jnp.mean(dvh * vhat, axis=-1, keepdims=True))
        dz_ref[:, :D] = (du_ref[...] * _gelu_grad(zu)).astype(BF16)
        dz_ref[:, D:] = (dv * _gelu_grad(zv)).astype(BF16)
        dh = None
        for s in range(n_shards):
            part = _dot_nt(dz_ref[:, s * hs:(s + 1) * hs], win_ref[s])
            dh = part if dh is None else dh + part
        dx, dg = _rms_bwd(x_ref[...], g_ref[...], dh)
        dx_ref[...] = dxm + dx
        dg_ref[0:1, :] += dg

    small = pl.BlockSpec((A_GROUPS, 128, 128), lambda i: (0, 0, 0))
    row = pl.BlockSpec((tm, D), lambda i: (i, 0))
    wide = pl.BlockSpec((tm, 2 * D), lambda i: (i, 0))
    gain = pl.BlockSpec((1, D), lambda i: (0, 0))
    sums = pl.BlockSpec((8, D), lambda i: (0, 0))
    small_shape = jax.ShapeDtypeStruct((A_GROUPS, 128, 128), F32)
    sums_shape = jax.ShapeDtypeStruct((8, D), F32)
    return pl.pallas_call(
        body, name=name, grid=(t // tm,),
        in_specs=[row, row, gain, wide, row, gain, small, small, small, pl.BlockSpec((D, D), lambda i: (0, 0)),
                  pl.BlockSpec((n_shards, D, hs), lambda i: (0, 0, 0))],
        out_specs=[row, wide, small, small, sums, sums],
        out_shape=[jax.ShapeDtypeStruct((t, D), F32), jax.ShapeDtypeStruct((t, 2 * D), BF16), small_shape, small_shape,
                   sums_shape, sums_shape],
        scratch_shapes=[pltpu.VMEM((tm, D), F32), pltpu.VMEM((tm, D), F32)],
        compiler_params=_params("arbitrary"),
    )(dxm, x, g, z, vn, gv, wm, wmt, bs, wout, win)


def _conv_terms(p_ref, row):
    gb = p_ref[0].astype(F32)
    gc = p_ref[1].astype(F32)
    xt = p_ref[2].astype(F32)
    q = gc * xt
    return gb, gc, xt, q, _shift_down(q, 1, row), _shift_down(q, 2, row)


def b_conv_fwd(p3, cw, seq, name):
    t = p3.shape[1]
    cb = 256

    def body(p_ref, cw_ref, y_ref):
        row = lax.broadcasted_iota(jnp.int32, (seq, cb), 0)
        gb, _, _, q, q1, q2 = _conv_terms(p_ref, row)
        y_ref[...] = (gb * (cw_ref[2:3, :] * q + cw_ref[1:2, :] * q1 + cw_ref[0:1, :] * q2)).astype(BF16)

    return pl.pallas_call(
        body, name=name, grid=(t // seq, D // cb),
        in_specs=[pl.BlockSpec((3, seq, cb), lambda e, c: (0, e, c)), pl.BlockSpec((3, cb), lambda e, c: (0, c))],
        out_specs=pl.BlockSpec((seq, cb), lambda e, c: (e, c)),
        out_shape=jax.ShapeDtypeStruct((t, D), BF16),
        compiler_params=_params("arbitrary", "arbitrary"),
    )(p3, cw)


def b_conv_bwd(dy, p3, cw, seq, name):
    t = p3.shape[1]
    cb = 256

    def body(dy_ref, p_ref, cw_ref, dp_ref, dcw_ref):
        @pl.when(pl.program_id(1) == 0)
        def _():
            dcw_ref[...] = jnp.zeros_like(dcw_ref)

        row = lax.broadcasted_iota(jnp.int32, (seq, cb), 0)
        gb, gc, xt, q, q1, q2 = _conv_terms(p_ref, row)
        dyv = dy_ref[...].astype(F32)
        conv = cw_ref[2:3, :] * q + cw_ref[1:2, :] * q1 + cw_ref[0:1, :] * q2
        dyc = dyv * gb
        dq = cw_ref[2:3, :] * dyc + cw_ref[1:2, :] * _shift_up(dyc, 1, row) + cw_ref[0:1, :] * _shift_up(dyc, 2, row)
        dp_ref[0] = (dyv * conv).astype(BF16)
        dp_ref[1] = (dq * xt).astype(BF16)
        dp_ref[2] = (dq * gc).astype(BF16)
        dcw_ref[0:1, :] += jnp.sum(dyc * q2, axis=0, keepdims=True)
        dcw_ref[1:2, :] += jnp.sum(dyc * q1, axis=0, keepdims=True)
        dcw_ref[2:3, :] += jnp.sum(dyc * q, axis=0, keepdims=True)

    return pl.pallas_call(
        body, name=name, grid=(D // cb, t // seq),
        in_specs=[pl.BlockSpec((seq, cb), lambda c, e: (e, c)), pl.BlockSpec((3, seq, cb), lambda c, e: (0, e, c)),
                  pl.BlockSpec((3, cb), lambda c, e: (0, c))],
        out_specs=[pl.BlockSpec((3, seq, cb), lambda c, e: (0, e, c)), pl.BlockSpec((8, cb), lambda c, e: (0, c))],
        out_shape=[jax.ShapeDtypeStruct((3, t, D), BF16), jax.ShapeDtypeStruct((8, D), F32)],
        compiler_params=_params("arbitrary", "arbitrary"),
    )(dy, p3, cw)


def c_pool_fwd(p, seq, name):
    t = p.shape[0]

    def make(grp):
        w = POOL_WINDOWS[grp]

        def body_g(p_ref, d_ref):
            row = lax.broadcasted_iota(jnp.int32, (seq, C_GROUP_DIM), 0)
            pv = p_ref[...].astype(F32)
            acc = pv
            sh = 1
            while sh < w:
                acc = acc + _shift_down(acc, sh, row)
                sh *= 2
            d_ref[...] = (acc / jnp.minimum(row + 1, w).astype(F32) - pv).astype(BF16)

        return body_g

    outs = []
    for grp in range(len(POOL_WINDOWS)):
        outs.append(pl.pallas_call(
            make(grp), name=f"{name}_g{grp}", grid=(t // seq,),
            in_specs=[pl.BlockSpec((seq, C_GROUP_DIM), lambda e, grp=grp: (e, grp))],
            out_specs=pl.BlockSpec((seq, C_GROUP_DIM), lambda e: (e, 0)),
            out_shape=jax.ShapeDtypeStruct((t, C_GROUP_DIM), BF16),
            compiler_params=_params("arbitrary"),
        )(p))
    return outs


def c_pool_bwd(dd, seq, name):
    t = dd[0].shape[0]

    def make(w):
        def body_g(dd_ref, dp_ref):
            row = lax.broadcasted_iota(jnp.int32, (seq, C_GROUP_DIM), 0)
            ddv = dd_ref[...]
            acc = ddv / jnp.minimum(row + 1, w).astype(F32)
            sh = 1
            while sh < w:
                acc = acc + _shift_up(acc, sh, row)
                sh *= 2
            dp_ref[...] = (acc - ddv).astype(BF16)

        return body_g

    outs = []
    for grp, w in enumerate(POOL_WINDOWS):
        outs.append(pl.pallas_call(
            make(w), name=f"{name}_g{grp}", grid=(t // seq,),
            in_specs=[pl.BlockSpec((seq, C_GROUP_DIM), lambda e: (e, 0))],
            out_specs=pl.BlockSpec((seq, C_GROUP_DIM), lambda e: (e, 0)),
            out_shape=jax.ShapeDtypeStruct((t, C_GROUP_DIM), BF16),
            compiler_params=_params("arbitrary"),
        )(dd[grp]))
    return outs


def c_out_fwd(d, wgrp, scale, wo, x, name):
    t = x.shape[0]
    tm = _row_tile(t, 512)
    n_g = len(POOL_WINDOWS)

    def body(d0, d1, d2, d3, wg_ref, sc_ref, wo_ref, x_ref, y_ref, o_ref):
        parts = [_dot(dr[...], wg_ref[grp]) for grp, dr in enumerate((d0, d1, d2, d3))]
        y = (jnp.concatenate(parts, axis=1) * sc_ref[...]).astype(BF16)
        y_ref[...] = y
        o_ref[...] = x_ref[...] + _dot(y, wo_ref[...])

    dspec = pl.BlockSpec((tm, C_GROUP_DIM), lambda i: (i, 0))
    row = pl.BlockSpec((tm, D), lambda i: (i, 0))
    return pl.pallas_call(
        body, name=name, grid=(t // tm,),
        in_specs=[dspec] * n_g + [pl.BlockSpec((n_g, C_GROUP_DIM, C_GROUP_DIM), lambda i: (0, 0, 0)),
                                  pl.BlockSpec((1, D), lambda i: (0, 0)), pl.BlockSpec((D, D), lambda i: (0, 0)), row],
        out_specs=[row, row],
        out_shape=[jax.ShapeDtypeStruct((t, D), BF16), jax.ShapeDtypeStruct((t, D), F32)],
        compiler_params=_params("arbitrary"),
    )(*d, wgrp, scale, wo, x)


def c_out_bwd(dxm, d, wgrp, scale, wo, name):
    t = dxm.shape[0]
    tm = _row_tile(t, 512)
    n_g = len(POOL_WINDOWS)

    def body(dxm_ref, d0, d1, d2, d3, wg_ref, sc_ref, wo_ref, dyp_ref, dd0, dd1, dd2, dd3, dsc_ref):
        @pl.when(pl.program_id(0) == 0)
        def _():
            dsc_ref[...] = jnp.zeros_like(dsc_ref)

        dyo = _dot_nt(dxm_ref[...].astype(BF16), wo_ref[...])
        ypre = jnp.concatenate([_dot(dr[...], wg_ref[grp]) for grp, dr in enumerate((d0, d1, d2, d3))], axis=1)
        dsc_ref[0:1, :] += jnp.sum(dyo * ypre, axis=0, keepdims=True)
        dyp = (dyo * sc_ref[...]).astype(BF16)
        dyp_ref[...] = dyp
        for grp, ddr in enumerate((dd0, dd1, dd2, dd3)):
            ddr[...] = _dot_nt(dyp[:, grp * C_GROUP_DIM:(grp + 1) * C_GROUP_DIM], wg_ref[grp])

    dspec = pl.BlockSpec((tm, C_GROUP_DIM), lambda i: (i, 0))
    row = pl.BlockSpec((tm, D), lambda i: (i, 0))
    dshape = jax.ShapeDtypeStruct((t, C_GROUP_DIM), F32)
    return pl.pallas_call(
        body, name=name, grid=(t // tm,),
        in_specs=[row] + [dspec] * n_g + [pl.BlockSpec((n_g, C_GROUP_DIM, C_GROUP_DIM), lambda i: (0, 0, 0)),
                                          pl.BlockSpec((1, D), lambda i: (0, 0)), pl.BlockSpec((D, D), lambda i: (0, 0))],
        out_specs=[row] + [dspec] * n_g + [pl.BlockSpec((8, D), lambda i: (0, 0))],
        out_shape=[jax.ShapeDtypeStruct((t, D), BF16)] + [dshape] * n_g + [jax.ShapeDtypeStruct((8, D), F32)],
        compiler_params=_params("arbitrary"),
    )(dxm, *d, wgrp, scale, wo)


def loss_head(x, tgt, g, name):
    t = x.shape[0]
    tm = _row_tile(t, 512)

    def body(x_ref, t_ref, g_ref, dx_ref, dg_ref, loss_ref):
        @pl.when(pl.program_id(0) == 0)
        def _():
            dg_ref[...] = jnp.zeros_like(dg_ref)
            loss_ref[...] = jnp.zeros_like(loss_ref)

        xv, gvv = x_ref[...], g_ref[...]
        xh, _ = _rms(xv)
        diff = xh * gvv - t_ref[...]
        loss_ref[...] += 0.5 * jnp.sum(jnp.mean(diff * diff, axis=-1, keepdims=True))
        dx, dg = _rms_bwd(xv, gvv, diff * (1.0 / D))
        dx_ref[...] = dx
        dg_ref[0:1, :] += dg

    row = pl.BlockSpec((tm, D), lambda i: (i, 0))
    return pl.pallas_call(
        body, name=name, grid=(t // tm,),
        in_specs=[row, row, pl.BlockSpec((1, D), lambda i: (0, 0))],
        out_specs=[row, pl.BlockSpec((8, D), lambda i: (0, 0)), pl.BlockSpec((8, 128), lambda i: (0, 0))],
        out_shape=[jax.ShapeDtypeStruct((t, D), F32), jax.ShapeDtypeStruct((8, D), F32), jax.ShapeDtypeStruct((8, 128), F32)],
        compiler_params=_params("arbitrary"),
    )(x, tgt, g)


def adamw(w, g, m, v, name):
    rows, cols = w.shape
    tr = rows
    for cand in (512, 256, 128, 64, 32, 16, 8):
        if rows % cand == 0 and rows > cand:
            tr = cand
            break

    def body(w_ref, g_ref, m_ref, v_ref, d_ref, mo_ref, vo_ref):
        gv = g_ref[...]
        mn = ADAM_B1 * m_ref[...] + (1.0 - ADAM_B1) * gv
        vn = ADAM_B2 * v_ref[...] + (1.0 - ADAM_B2) * (gv * gv)
        m_hat = mn / (1.0 - ADAM_B1 ** ADAM_STEP)
        v_hat = vn / (1.0 - ADAM_B2 ** ADAM_STEP)
        d_ref[...] = -ADAM_LR * (m_hat / (jnp.sqrt(v_hat) + ADAM_EPS) + ADAM_WD * w_ref[...])
        mo_ref[...] = mn
        vo_ref[...] = vn

    spec = pl.BlockSpec((tr, cols), lambda i: (i, 0))
    shape = jax.ShapeDtypeStruct((rows, cols), F32)
    return pl.pallas_call(
        body, name=name, grid=(rows // tr,),
        in_specs=[spec] * 4, out_specs=[spec] * 3, out_shape=[shape] * 3,
        compiler_params=_params("arbitrary"),
    )(w, g, m, v)


def adamw_layer(w, m, v, own, recv, core, layer, carried, name):
    n_layers, rows, cols = w.shape
    h = rows // 2

    def body(core_ref, w_ref, m_ref, v_ref, own_ref, recv_ref, *rest):
        g_ref, d_ref, mo_ref, vo_ref = rest[-4:]
        gv = jnp.where(pl.program_id(0) == core_ref[0], own_ref[...], recv_ref[...])
        mn = ADAM_B1 * m_ref[0] + (1.0 - ADAM_B1) * gv
        vn = ADAM_B2 * v_ref[0] + (1.0 - ADAM_B2) * (gv * gv)
        m_hat = mn / (1.0 - ADAM_B1 ** ADAM_STEP)
        v_hat = vn / (1.0 - ADAM_B2 ** ADAM_STEP)
        g_ref[0] = gv
        d_ref[0] = -ADAM_LR * (m_hat / (jnp.sqrt(v_hat) + ADAM_EPS) + ADAM_WD * w_ref[0])
        mo_ref[0] = mn
        vo_ref[0] = vn

    steps = 4
    tr = h // steps
    stacked = pl.BlockSpec((1, tr, cols), lambda half, j, core_ref: (layer, half * steps + j, 0))
    halfspec = pl.BlockSpec((tr, cols), lambda half, j, core_ref: (j, 0))
    n_carried = 0 if carried is None else 4
    shape = jax.ShapeDtypeStruct(w.shape, F32)
    return pl.pallas_call(
        body, name=name,
        grid_spec=pltpu.PrefetchScalarGridSpec(
            num_scalar_prefetch=1, grid=(2, steps),
            in_specs=[stacked] * 3 + [halfspec] * 2 + [ANY] * n_carried, out_specs=[stacked] * 4),
        out_shape=[shape] * 4,
        input_output_aliases={6 + i: i for i in range(n_carried)},
        compiler_params=_params("arbitrary", "arbitrary"),
    )(core, w, m, v, own, recv, *(carried or ()))


def add_halves(gs, ps, core, name, deps=()):
    n = len(gs)

    def body(core_ref, *refs):
        for i in range(n):
            refs[2 * n + i][...] = (refs[i][...].astype(F32) + refs[n + i][...].astype(F32)).astype(BF16)

    in_specs, out_specs, out_shape = [], [], []
    for gt in gs:
        h, c = gt.shape[1] // 2, gt.shape[2]
        in_specs.append(pl.BlockSpec((1, h, c), lambda b, core_ref: (b, core_ref[0], 0)))
    for gt in gs:
        h, c = gt.shape[1] // 2, gt.shape[2]
        in_specs.append(pl.BlockSpec((1, h, c), lambda b, core_ref: (b, 0, 0)))
        out_specs.append(pl.BlockSpec((1, h, c), lambda b, core_ref: (b, 0, 0)))
        out_shape.append(jax.ShapeDtypeStruct((N_CHIPS, h, c), BF16))
    in_specs += [ANY] * len(deps)
    return pl.pallas_call(
        _after(body, 1 + 2 * n, deps), name=name,
        grid_spec=pltpu.PrefetchScalarGridSpec(num_scalar_prefetch=1, grid=(N_CHIPS,), in_specs=in_specs, out_specs=out_specs),
        out_shape=out_shape, compiler_params=_params("arbitrary"),
    )(core, *gs, *ps, *deps)


def add_final(hs, qs, chip, name, deps=()):
    n = len(hs)

    def body(chip_ref, *refs):
        for i in range(n):
            q = refs[n + i]
            refs[2 * n + i][...] = ((refs[i][0].astype(F32) + q[0].astype(F32)) + q[1].astype(F32)) + q[2].astype(F32)

    steps = 2
    in_specs, out_specs, out_shape = [], [], []
    for ht in hs:
        h, c = ht.shape[1], ht.shape[2]
        in_specs.append(pl.BlockSpec((1, h // steps, c), lambda i, chip_ref: (chip_ref[0], i, 0)))
    for ht in hs:
        h, c = ht.shape[1], ht.shape[2]
        in_specs.append(pl.BlockSpec((N_CHIPS - 1, h // steps, c), lambda i, chip_ref: (0, i, 0)))
        out_specs.append(pl.BlockSpec((h // steps, c), lambda i, chip_ref: (i, 0)))
        out_shape.append(jax.ShapeDtypeStruct((h, c), F32))
    in_specs += [ANY] * len(deps)
    return pl.pallas_call(
        _after(body, 1 + 2 * n, deps), name=name,
        grid_spec=pltpu.PrefetchScalarGridSpec(num_scalar_prefetch=1, grid=(steps,), in_specs=in_specs, out_specs=out_specs),
        out_shape=out_shape, compiler_params=_params("arbitrary"),
    )(chip, *hs, *qs, *deps)


def sum_devices(own, gathered, me, name, deps=()):
    rows = own.shape[0]

    def body(me_ref, own_ref, g_ref, o_ref):
        me_dev = me_ref[0]
        acc = None
        for dev in range(8):
            slot = jnp.maximum((me_dev ^ dev) - 1, 0)
            term = jnp.where(me_dev == dev, own_ref[...], g_ref[slot])
            acc = term if acc is None else acc + term
        o_ref[...] = acc

    tr = next(c for c in range(80, 0, -8) if rows % c == 0)
    return pl.pallas_call(
        _after(body, 3, deps), name=name,
        grid_spec=pltpu.PrefetchScalarGridSpec(
            num_scalar_prefetch=1, grid=(rows // tr,),
            in_specs=[pl.BlockSpec((tr, D), lambda i, me_ref: (i, 0)), pl.BlockSpec((7, tr, D), lambda i, me_ref: (0, i, 0))]
            + [ANY] * len(deps),
            out_specs=pl.BlockSpec((tr, D), lambda i, me_ref: (i, 0))),
        out_shape=jax.ShapeDtypeStruct((rows, D), F32),
        compiler_params=_params("arbitrary"),
    )(me, own, gathered, *deps)


def _mesh_pos():
    return lax.axis_index("x"), lax.axis_index("y"), lax.axis_index("c")


def _other_chips(x, y):
    return [(1 - x, y), (x, 1 - y), (1 - x, 1 - y)]


def _sibling():
    x, y, c = _mesh_pos()
    return [(x, y, 1 - c)]


def _same_core_of_other_chips():
    x, y, c = _mesh_pos()
    return [(cx, cy, c) for (cx, cy) in _other_chips(x, y)]


def _on_sequencer(body, name, operands, out_shapes, sems, peers, collective_id, deps=()):
    ordered = _after(body, len(operands), deps)

    def seq_body(*refs):
        barrier = pltpu.get_barrier_semaphore()
        with_whom = peers()
        for peer in with_whom:
            pl.semaphore_signal(barrier, inc=1, device_id=peer, device_id_type=MESH)
        pl.semaphore_wait(barrier, len(with_whom))
        ordered(*refs)

    return pl.kernel(
        seq_body, name=name, out_type=out_shapes,
        mesh=plsc.ScalarSubcoreMesh(axis_name="seq", num_cores=1),
        scratch_types=sems, compiler_params=pltpu.CompilerParams(collective_id=collective_id, has_side_effects=True),
    )(*operands, *deps)


def all_gather_weights(shards, name, deps=()):
    n = len(shards)

    def body(*refs):
        ins, outs = refs[:n], refs[n:2 * n]
        send, recv, fsend, frecv = refs[2 * n:]
        x, y, c = _mesh_pos()
        k = 2 * x + y
        chips = _other_chips(x, y)

        def half(ref, i, rows_half):
            h = shards[i].shape[0] // 2
            return ref.at[pl.ds(pl.multiple_of(rows_half * h, 8), h), :]

        first = []
        for i in range(n):
            for j, (cx, cy) in enumerate(chips):
                first.append(pltpu.make_async_remote_copy(
                    src_ref=half(ins[i], i, c), dst_ref=half(outs[i].at[k], i, c),
                    send_sem=send.at[i, j], recv_sem=recv.at[i, j], device_id=(cx, cy, c), device_id_type=MESH))
        for cp in first:
            cp.start()
        passed = []
        for i in range(n):
            for j, (cx, cy) in enumerate(chips):
                blk = half(outs[i].at[2 * cx + cy], i, c)
                pltpu.make_async_remote_copy(src_ref=blk, dst_ref=blk, send_sem=send.at[i, j], recv_sem=recv.at[i, j],
                                             device_id=(cx, cy, c), device_id_type=MESH).wait_recv()
                fw = pltpu.make_async_remote_copy(src_ref=blk, dst_ref=blk, send_sem=fsend.at[i, j], recv_sem=frecv.at[i, j],
                                                  device_id=(x, y, 1 - c), device_id_type=MESH)
                fw.start()
                passed.append(fw)
        for i in range(n):
            for j, (cx, cy) in enumerate(chips):
                blk = half(outs[i].at[2 * cx + cy], i, 1 - c)
                pltpu.make_async_remote_copy(src_ref=blk, dst_ref=blk, send_sem=fsend.at[i, j], recv_sem=frecv.at[i, j],
                                             device_id=(x, y, 1 - c), device_id_type=MESH).wait_recv()
        for cp in first + passed:
            cp.wait_send()

    def peers():
        x, y, c = _mesh_pos()
        return [(cx, cy, c) for (cx, cy) in _other_chips(x, y)] + [(x, y, 1 - c)]

    return _on_sequencer(
        body, name, shards, [jax.ShapeDtypeStruct((N_CHIPS,) + s.shape, s.dtype) for s in shards],
        [pltpu.SemaphoreType.DMA((n, 3))] * 4, peers, GATHER_COLLECTIVE_ID, deps)


def place_own(gathered, shards, chip, name):
    n = len(shards)

    def body(chip_ref, *refs):
        for i in range(n):
            refs[2 * n + i][0] = refs[i][...]

    in_specs = [pl.BlockSpec(s.shape, lambda i, chip_ref: (0, 0)) for s in shards] + [ANY] * n
    out_specs = [pl.BlockSpec((1,) + s.shape, lambda i, chip_ref: (chip_ref[0], 0, 0)) for s in shards]
    return pl.pallas_call(
        body, name=name,
        grid_spec=pltpu.PrefetchScalarGridSpec(num_scalar_prefetch=1, grid=(1,), in_specs=in_specs, out_specs=out_specs),
        out_shape=[jax.ShapeDtypeStruct(g.shape, g.dtype) for g in gathered],
        input_output_aliases={1 + n + i: i for i in range(n)},
        compiler_params=_params("arbitrary"),
    )(chip, *shards, *gathered)


def all_gather_rows(shard, name):
    def body(in_ref, out_ref, send, recv, lsem):
        x, y, c = _mesh_pos()
        k = 2 * x + y
        chips = _other_chips(x, y)
        local = pltpu.make_async_copy(in_ref, out_ref.at[k], lsem)
        local.start()
        sent = [pltpu.make_async_remote_copy(src_ref=in_ref, dst_ref=out_ref.at[k], send_sem=send.at[j], recv_sem=recv.at[j],
                                             device_id=(cx, cy, c), device_id_type=MESH) for j, (cx, cy) in enumerate(chips)]
        for cp in sent:
            cp.start()
        for j, (cx, cy) in enumerate(chips):
            blk = out_ref.at[2 * cx + cy]
            pltpu.make_async_remote_copy(src_ref=blk, dst_ref=blk, send_sem=send.at[j], recv_sem=recv.at[j],
                                         device_id=(cx, cy, c), device_id_type=MESH).wait_recv()
        for cp in sent:
            cp.wait_send()
        local.wait()

    return pl.pallas_call(
        body, name=name, in_specs=[ANY], out_specs=ANY,
        out_shape=jax.ShapeDtypeStruct((N_CHIPS,) + shard.shape, shard.dtype),
        scratch_shapes=[pltpu.SemaphoreType.DMA((3,)), pltpu.SemaphoreType.DMA((3,)), pltpu.SemaphoreType.DMA],
    )(shard)


def swap_halves(gs, name):
    n = len(gs)

    def body(*refs):
        ins, outs = refs[:n], refs[n:2 * n]
        send, recv = refs[2 * n:]
        x, y, c = _mesh_pos()
        sent = []
        for i in range(n):
            h = gs[i].shape[1] // 2
            src = ins[i].at[:, pl.ds(pl.multiple_of((1 - c) * h, 8), h), :]
            cp = pltpu.make_async_remote_copy(src_ref=src, dst_ref=outs[i], send_sem=send.at[i], recv_sem=recv.at[i],
                                              device_id=(x, y, 1 - c), device_id_type=MESH)
            cp.start()
            sent.append(cp)
        for cp in sent:
            cp.wait()

    return _on_sequencer(
        body, name, gs, [jax.ShapeDtypeStruct((N_CHIPS, g.shape[1] // 2, g.shape[2]), g.dtype) for g in gs],
        [pltpu.SemaphoreType.DMA((n,)), pltpu.SemaphoreType.DMA((n,))], _sibling, SIBLING_COLLECTIVE_ID)


def scatter_chips(hs, name):
    n = len(hs)

    def body(*refs):
        ins, outs = refs[:n], refs[n:2 * n]
        send, recv = refs[2 * n:]
        x, y, c = _mesh_pos()
        chips = _other_chips(x, y)
        sent = []
        for i in range(n):
            for j, (cx, cy) in enumerate(chips):
                cp = pltpu.make_async_remote_copy(src_ref=ins[i].at[2 * cx + cy], dst_ref=outs[i].at[j],
                                                  send_sem=send.at[i, j], recv_sem=recv.at[i, j],
                                                  device_id=(cx, cy, c), device_id_type=MESH)
                cp.start()
                sent.append(cp)
        for cp in sent:
            cp.wait()

    return _on_sequencer(
        body, name, hs, [jax.ShapeDtypeStruct((N_CHIPS - 1,) + h.shape[1:], h.dtype) for h in hs],
        [pltpu.SemaphoreType.DMA((n, 3)), pltpu.SemaphoreType.DMA((n, 3))], _same_core_of_other_chips, CHIPS_COLLECTIVE_ID)


def swap_reduced(rs, name):
    n = len(rs)

    def body(*refs):
        ins, outs = refs[:n], refs[n:2 * n]
        send, recv = refs[2 * n:]
        x, y, c = _mesh_pos()
        sent = []
        for i in range(n):
            cp = pltpu.make_async_remote_copy(src_ref=ins[i], dst_ref=outs[i], send_sem=send.at[i], recv_sem=recv.at[i],
                                              device_id=(x, y, 1 - c), device_id_type=MESH)
            cp.start()
            sent.append(cp)
        for cp in sent:
            cp.wait()

    return _on_sequencer(
        body, name, rs, [jax.ShapeDtypeStruct(r.shape, r.dtype) for r in rs],
        [pltpu.SemaphoreType.DMA((n,)), pltpu.SemaphoreType.DMA((n,))], _sibling, SIBLING_COLLECTIVE_ID)


def all_gather_devices(part, name):
    def everyone_else():
        x, y, c = _mesh_pos()
        return [(1 - x if (rel >> 2) & 1 else x, 1 - y if (rel >> 1) & 1 else y, 1 - c if rel & 1 else c) for rel in range(1, 8)]

    def body(in_ref, out_ref, send, recv):
        sent = []
        for slot, peer in enumerate(everyone_else()):
            cp = pltpu.make_async_remote_copy(src_ref=in_ref, dst_ref=out_ref.at[slot], send_sem=send.at[slot],
                                              recv_sem=recv.at[slot], device_id=peer, device_id_type=MESH)
            cp.start()
            sent.append(cp)
        for cp in sent:
            cp.wait()

    return _on_sequencer(
        body, name, [part], jax.ShapeDtypeStruct((7,) + part.shape, part.dtype),
        [pltpu.SemaphoreType.DMA((7,)), pltpu.SemaphoreType.DMA((7,))], everyone_else, ALL_COLLECTIVE_ID)


class ReduceScatter:
    def __init__(self, grads, names, layer, core, chip, name):
        self.grads, self.names, self.layer, self.core, self.chip, self.name = grads, names, layer, core, chip, name
        self.from_sibling = swap_halves(grads, name + "_swap")

    def sum_cores(self, deps=()):
        self.core_sums = add_halves(self.grads, self.from_sibling, self.core, self.name + "_add2", deps)
        self.from_chips = scatter_chips(self.core_sums, self.name + "_scatter")
        return self.core_sums[0]

    def sum_chips(self, deps=()):
        self.half_sums = add_final(self.core_sums, self.from_chips, self.chip, self.name + "_add4", deps)
        self.other_half = swap_reduced(self.half_sums, self.name + "_join")
        return self.half_sums[0]


def _blocked(w):
    return w.reshape(w.shape[0] * w.shape[1], w.shape[2])


def _grp_from_blocks(w):
    return w.reshape(N_CHIPS, 4, 64, C_GROUP_DIM).transpose(1, 0, 2, 3).reshape(4, C_GROUP_DIM, C_GROUP_DIM)


def _grp_to_blocks(w):
    return w.reshape(4, N_CHIPS, 64, C_GROUP_DIM).transpose(1, 0, 2, 3).reshape(N_CHIPS, C_GROUP_DIM, C_GROUP_DIM)


def _dw_cols(h, dact, hs, name, deps):
    tm = 512
    return mm_tn(h[None], dact, (N_CHIPS, D, hs), tm, hs, N_CHIPS * (D // tm),
                 lambda j: (0, j % 2), lambda j: (0, j // 2), lambda j: (j // 2, j % 2, 0), name, deps)


def _dw_rows(y, dxm, name, deps):
    tm = 512
    out = mm_tn(y[None], dxm[None], (1, D, D), tm, D, D // tm, lambda j: (0, j), lambda j: (0, 0), lambda j: (0, j, 0), name, deps)
    return out.reshape(N_CHIPS, D // N_CHIPS, D)


def _dw_hidden(act, other, name, deps):
    tm, tn = FFN_HIDDEN // 2, 256
    n_n = D // tn
    out = mm_tn(act[None], other[None], (1, FFN_HIDDEN, D), tm, tn, 2 * n_n,
                lambda j: (0, j // n_n), lambda j: (0, j % n_n), lambda j: (0, j // n_n, j % n_n), name, deps)
    return out.reshape(N_CHIPS, FFN_SHARD, D)


def kernel(x, norm_mix_g, norm_ffn_g, final_norm_g, a_w_in, a_v_norm_g, a_w_s, a_b_s, a_w_out, b_w_in, b_conv_w, b_w_out, c_w_in, c_w_grp, c_scale, c_w_out, f_w_gate, f_w_up, f_w_down, loss_target, m_norm_mix_g, m_norm_ffn_g, m_final_norm_g, m_a_w_in, m_a_v_norm_g, m_a_w_s, m_a_b_s, m_a_w_out, m_b_w_in, m_b_conv_w, m_b_w_out, m_c_w_in, m_c_w_grp, m_c_scale, m_c_w_out, m_f_w_gate, m_f_w_up, m_f_w_down, v_norm_mix_g, v_norm_ffn_g, v_final_norm_g, v_a_w_in, v_a_v_norm_g, v_a_w_s, v_a_b_s, v_a_w_out, v_b_w_in, v_b_conv_w, v_b_w_out, v_c_w_in, v_c_w_grp, v_c_scale, v_c_w_out, v_f_w_gate, v_f_w_up, v_f_w_down):
    n_ex, seq, _ = x.shape
    t = n_ex * seq
    xi, yi, ci = lax.axis_index("x"), lax.axis_index("y"), lax.axis_index("c")
    chip = (2 * xi + yi).astype(jnp.int32)
    core_arr = ci.astype(jnp.int32).reshape(1)
    chip_arr = chip.reshape(1)
    me_arr = (4 * xi + 2 * yi + ci).astype(jnp.int32).reshape(1)
    bf = lambda w: w.astype(BF16)

    pad8 = lambda v: jnp.pad(v, ((0, 8 - v.shape[0]), (0, 0)))
    small_rows = jnp.concatenate([pad8(a_v_norm_g), pad8(b_conv_w[0]), pad8(c_scale)], axis=0)
    small_gathered = all_gather_rows(small_rows, "ag_small")
    small_full = small_gathered.transpose(1, 0, 2).reshape(24, D)
    gv_full = [small_full[0:1], small_full[1:2]]
    cw_full = small_full[8:11]
    scale_full = small_full[16:17]

    mixer_shards = [
        [bf(a_w_in[0]), bf(a_w_out[0])],
        [bf(b_w_in[0]), bf(b_w_out[0])],
        [bf(c_w_in[0]), bf(c_w_grp[0]).reshape(C_GROUP_DIM, C_GROUP_DIM), bf(c_w_out[0])],
        [bf(a_w_in[1]), bf(a_w_out[1])],
    ]
    hidden_major = lambda w: jnp.swapaxes(w, 1, 2)
    gate_t, up_t = hidden_major(f_w_gate), hidden_major(f_w_up)
    gathered = []
    for i in range(4):
        ffn_shards = [bf(gate_t[i]), bf(up_t[i]), bf(f_w_down[i])]
        if i == 0:
            parts = [(mixer_shards[0], "ag_l0_mixer", [small_gathered]), (ffn_shards, "ag_l0_ffn", [])]
        else:
            parts = [(mixer_shards[i] + ffn_shards, f"ag_l{i}", [])]
        layer = []
        for shards, name, deps in parts:
            layer += place_own(all_gather_weights(shards, name, deps), shards, chip_arr, name.replace("ag", "own"))
        gathered.append(layer)

    mask = (jnp.arange(GMLP_BLOCK)[None, :] // 64) <= (jnp.arange(GMLP_BLOCK)[:, None] // 64)
    gmix = [norm_mix_g[i:i + 1] for i in range(4)]
    gffn = [norm_ffn_g[i:i + 1] for i in range(4)]
    a_chunks = [(s, 0, 512, 0, s * 512) for s in range(N_CHIPS)]
    b_chunks = [(j // 3, (j % 3) * 256, 256, j // 4, (j % 4) * 256) for j in range(12)]
    c_chunks = [(0, 0, D, 0, 0)]

    xs = [x.reshape(t, D)]
    saved = []
    for i in range(4):
        ws = gathered[i]
        wg, wu, wd = (w.reshape(FFN_HIDDEN, D) for w in ws[-3:])
        xin = xs[-1]
        if i in (0, 3):
            j = 0 if i == 0 else 1
            win, wout = ws[0], _blocked(ws[1])
            wm32 = jnp.where(mask[None], a_w_s[j], 0.0)
            wm, wmt = bf(wm32), bf(wm32.transpose(0, 2, 1))
            bs = jnp.broadcast_to(a_b_s[j][:, :, None], (A_GROUPS, GMLP_BLOCK, 128))
            h, z, vn, y, xmid = a_fwd(xin, gmix[i], win, gv_full[j], wm, bs, wout, f"a_fwd_l{i}")
            saved.append(dict(h=h, z=z, y=y, vn=vn, win=win, wout=wout, wm=wm, wmt=wmt, bs=bs, gv=gv_full[j]))
        elif i == 1:
            win, wout = ws[0], _blocked(ws[1])
            h, p3 = norm_mm(xin, gmix[i], win, b_chunks, 3, D, "b_in")
            y = b_conv_fwd(p3, cw_full, seq, "b_conv")
            xmid = mm_res(y, wout, xin, "b_out")
            saved.append(dict(h=h, p3=p3, y=y, win=win, wout=wout))
        else:
            win, wgrp, wout = _blocked(ws[0])[None], _grp_from_blocks(ws[1]), _blocked(ws[2])
            h, p = norm_mm(xin, gmix[i], win, c_chunks, 1, D, "c_in")
            dpool = c_pool_fwd(p[0], seq, "c_pool")
            y, xmid = c_out_fwd(dpool, wgrp, scale_full, wout, xin, "c_out")
            saved.append(dict(h=h, d=dpool, y=y, win=win, wgrp=wgrp, wout=wout))
        h2, fa, fb, fs, xout = ffn_fwd(xmid, gffn[i], wg, wu, wd, f"ffn_l{i}")
        saved[-1].update(h2=h2, fa=fa, fb=fb, fs=fs, xmid=xmid, wg=wg, wu=wu, wd=wd)
        xs.append(xout)

    dx, dg_final, loss_part = loss_head(xs[4], loss_target.reshape(t, D), final_norm_g[None], "loss_head")
    loss = lax.psum(loss_part[0, 0], ("x", "y", "c"))

    weights = dict(norm_mix_g=norm_mix_g, norm_ffn_g=norm_ffn_g, final_norm_g=final_norm_g, a_w_in=a_w_in, a_v_norm_g=a_v_norm_g,
                   a_w_s=a_w_s, a_b_s=a_b_s, a_w_out=a_w_out, b_w_in=b_w_in, b_conv_w=b_conv_w, b_w_out=b_w_out, c_w_in=c_w_in,
                   c_w_grp=c_w_grp, c_scale=c_scale, c_w_out=c_w_out, f_w_gate=f_w_gate, f_w_up=f_w_up, f_w_down=f_w_down)
    m_in = dict(norm_mix_g=m_norm_mix_g, norm_ffn_g=m_norm_ffn_g, final_norm_g=m_final_norm_g, a_w_in=m_a_w_in, a_v_norm_g=m_a_v_norm_g,
                a_w_s=m_a_w_s, a_b_s=m_a_b_s, a_w_out=m_a_w_out, b_w_in=m_b_w_in, b_conv_w=m_b_conv_w, b_w_out=m_b_w_out, c_w_in=m_c_w_in,
                c_w_grp=m_c_w_grp, c_scale=m_c_scale, c_w_out=m_c_w_out, f_w_gate=m_f_w_gate, f_w_up=m_f_w_up, f_w_down=m_f_w_down)
    v_in = dict(norm_mix_g=v_norm_mix_g, norm_ffn_g=v_norm_ffn_g, final_norm_g=v_final_norm_g, a_w_in=v_a_w_in, a_v_norm_g=v_a_v_norm_g,
                a_w_s=v_a_w_s, a_b_s=v_a_b_s, a_w_out=v_a_w_out, b_w_in=v_b_w_in, b_conv_w=v_b_conv_w, b_w_out=v_b_w_out, c_w_in=v_c_w_in,
                c_w_grp=v_c_w_grp, c_scale=v_c_scale, c_w_out=v_c_w_out, f_w_gate=v_f_w_gate, f_w_up=v_f_w_up, f_w_down=v_f_w_down)
    grp_rows = lambda a: a.reshape(1, C_GROUP_DIM, C_GROUP_DIM)
    same = lambda a: a
    to_stacked = {nme: same for nme in ("a_w_in", "a_w_out", "b_w_in", "b_w_out", "c_w_in", "c_w_out", "f_w_down")}
    to_stacked.update(f_w_gate=hidden_major, f_w_up=hidden_major, c_w_grp=grp_rows)
    from_stacked = dict(to_stacked, c_w_grp=lambda a: a.reshape(c_w_grp.shape))
    layer_tensors = {0: ["a_w_out", "a_w_in"], 1: ["b_w_out", "b_w_in"], 2: ["c_w_out", "c_w_in", "c_w_grp"], 3: ["a_w_out", "a_w_in"]}
    carried = {}

    def bs_rows(v):
        return jnp.pad(v[:, :, 0].reshape(1, D), ((0, 7), (0, 0)))

    def update(unit):
        done = []
        for pos, nme in enumerate(unit.names):
            stacked_layer = unit.layer if nme.startswith("f_") else (unit.layer // 3 if nme.startswith("a_") else 0)
            view = to_stacked[nme]
            carried[nme] = adamw_layer(view(weights[nme]), view(m_in[nme]), view(v_in[nme]), unit.half_sums[pos], unit.other_half[pos],
                                       core_arr, stacked_layer, carried.get(nme), f"adamw_{nme}_l{unit.layer}")
            done.append(carried[nme][0])
        return done

    ffn_names = ["f_w_gate", "f_w_up", "f_w_down"]
    dg_mix, dg_ffn = [None] * 4, [None] * 4
    small = {}
    newer = older = None
    for i in (3, 2, 1, 0):
        sv = saved[i]
        xin = xs[i]
        deps = ([newer.grads[0]] if newer else []) + ([older.half_sums[0]] if older else [])
        dxm, da, db, dg_ffn[i] = ffn_bwd(dx, sv["fa"], sv["fb"], sv["xmid"], gffn[i], sv["wg"], sv["wu"], sv["wd"], f"ffn_bwd_l{i}", deps)
        last = [newer.sum_cores([dxm])] if newer else []
        g_gate = _dw_hidden(da, sv["h2"], f"dw_gate_l{i}", last)
        g_up = _dw_hidden(db, sv["h2"], f"dw_up_l{i}", [g_gate])
        g_down = _dw_hidden(sv["fs"], dx, f"dw_down_l{i}", [g_up])
        last_ffn = [g_down]
        if i == 0:
            ffn_unit = ReduceScatter([g_down, g_gate, g_up], ["f_w_down", "f_w_gate", "f_w_up"], 0, core_arr, chip_arr, "rs_l0_ffn")
        if i in (0, 3):
            j = 0 if i == 0 else 1
            dx, dz, dwm, dbs, dgv, dg_mix[i] = a_bwd(dxm, xin, gmix[i], sv["z"], sv["vn"], sv["gv"], sv["wm"], sv["wmt"], sv["bs"],
                                                     sv["wout"], sv["win"], f"a_bwd_l{i}")
            if i == 0:
                early = sum_devices(early_rows, early_gathered, me_arr, "sum_small_grads_l123")
                newer.sum_chips([dz] + update(older) + [early])
                last_ffn.append(ffn_unit.sum_cores([newer.half_sums[0]]))
            dz = dz[None]
            g_in = _dw_cols(sv["h"], dz, 512, f"dw_a_in_l{i}", last_ffn)
            g_out = _dw_rows(sv["y"], dxm, f"dw_a_out_l{i}", [g_in])
            small[f"wm{j}"], small[f"bs{j}"], small[f"gv{j}"] = dwm, dbs, dgv
            mixer_grads = [g_out, g_in]
        elif i == 1:
            dy = mm_nt(dxm, sv["wout"], "b_dy")
            dp3, small["cw"] = b_conv_bwd(dy, sv["p3"], cw_full, seq, "b_conv_bwd")
            dx, dg_mix[i] = bwd_in(dp3, sv["win"], b_chunks, xin, gmix[i], dxm, "b_bwd_in")
            g_in = mm_tn(sv["h"][None], dp3, (N_CHIPS, D, 768), 512, 256, 24,
                         lambda j: (0, j % 2), lambda j: ((j // 2) // 4, (j // 2) % 4),
                         lambda j: ((j // 2) // 3, j % 2, (j // 2) % 3), "dw_b_in", [g_down])
            g_out = _dw_rows(sv["y"], dxm, "dw_b_out", [g_in])
            mixer_grads = [g_out, g_in]
        else:
            outs = c_out_bwd(dxm, sv["d"], sv["wgrp"], scale_full, sv["wout"], "c_out_bwd")
            dyp, dd, small["scale"] = outs[0], list(outs[1:5]), outs[5]
            dpool = c_pool_bwd(dd, seq, "c_pool_bwd")
            dp = jnp.concatenate(dpool, axis=1)[None]
            dx, dg_mix[i] = bwd_in(dp, sv["win"], c_chunks, xin, gmix[i], dxm, "c_bwd_in")
            g_in = _dw_rows(sv["h"], dp[0], "dw_c_in", [g_down])
            dcat = jnp.concatenate(sv["d"], axis=1)
            g_grp = mm_tn(dcat[None], dyp[None], (4, C_GROUP_DIM, C_GROUP_DIM), C_GROUP_DIM, C_GROUP_DIM, 4,
                          lambda j: (0, j), lambda j: (0, j), lambda j: (j, 0, 0), "dw_c_grp", [g_in])
            g_out = _dw_rows(sv["y"], dxm, "dw_c_out", [g_grp])
            mixer_grads = [g_out, g_in, _grp_to_blocks(g_grp)]
        if i > 0:
            unit = ReduceScatter(mixer_grads + [g_gate, g_up, g_down], layer_tensors[i] + ffn_names, i, core_arr, chip_arr, f"rs_l{i}")
        else:
            unit = ReduceScatter(mixer_grads, layer_tensors[0], 0, core_arr, chip_arr, "rs_l0_mixer")
        if newer and i > 0:
            newer.sum_chips([mixer_grads[0]] + (update(older) if older else []))
        if i == 1:
            early_rows = jnp.concatenate(dg_mix[1:] + dg_ffn[1:] + [dg_final, small["gv1"], small["cw"], small["scale"],
                                                                   small["wm1"].reshape(128, D), bs_rows(small["bs1"])], axis=0)
            early_gathered = all_gather_devices(early_rows, "ag_small_grads_l123")
        older, newer = newer, unit
    grad_x = dx.reshape(n_ex, seq, D)
    mixer_unit = newer
    ffn_unit.sum_chips(update(older) + [mixer_unit.grads[0]])
    mixer_unit.sum_cores([ffn_unit.half_sums[0]])
    mixer_unit.sum_chips(update(ffn_unit))
    updated = update(mixer_unit)

    late_rows = jnp.concatenate([dg_mix[0], dg_ffn[0], small["gv0"], small["wm0"].reshape(128, D), bs_rows(small["bs0"])], axis=0)
    late = sum_devices(late_rows, all_gather_devices(late_rows, "ag_small_grads_l0"), me_arr, "sum_small_grads_l0", updated)
    first_rows = lambda a, b, n: jnp.concatenate([a, b], axis=0).reshape(n, 8, D)[:, 0]
    g_norm_mix = first_rows(late[0:8], early[0:24], 4)
    g_norm_ffn = first_rows(late[8:16], early[24:48], 4)
    g_final = early[48]
    g_gv = first_rows(late[16:24], early[56:64], 2)
    g_cw = early[64:67]
    g_scale = early[72:73]
    g_ws = jnp.where(mask[None, None], jnp.concatenate([late[24:152], early[80:208]], axis=0).reshape(2, A_GROUPS, 128, 128), 0.0)
    g_bs = first_rows(late[152:160], early[208:216], 2).reshape(2, A_GROUPS, 128)
    col0 = chip * (D // N_CHIPS)
    cols = lambda v: lax.dynamic_slice_in_dim(v, col0, D // N_CHIPS, axis=1)

    small_grads = {
        "norm_mix_g": g_norm_mix, "norm_ffn_g": g_norm_ffn, "final_norm_g": g_final, "a_v_norm_g": cols(g_gv), "a_w_s": g_ws,
        "a_b_s": g_bs, "b_conv_w": cols(g_cw)[None], "c_scale": cols(g_scale),
    }
    results = {}
    for nme, g in small_grads.items():
        w = weights[nme]
        flat = lambda a: a.reshape(-1, w.shape[-1])
        dl, mn, vn = adamw(flat(w), flat(g), flat(m_in[nme]), flat(v_in[nme]), f"adamw_{nme}")
        results[nme] = tuple(o.reshape(w.shape) for o in (g, dl, mn, vn))
    for nme, outs in carried.items():
        results[nme] = tuple(from_stacked[nme](o) for o in outs)

    names = list(weights)
    return (loss, grad_x, *[results[n][0] for n in names], *[results[n][1] for n in names],
            *[results[n][2] for n in names], *[results[n][3] for n in names])
```

```python
import jax
import jax.numpy as jnp
from jax import lax
from jax.experimental import pallas as pl
from jax.experimental.pallas import tpu as pltpu
from jax.experimental.pallas import tpu_sc as plsc

F32 = jnp.float32
BF16 = jnp.bfloat16
D = 1024
FFN_SHARD = 704
GMLP_BLOCK = 128
A_GROUPS = 8
POOL_WINDOWS = (2, 4, 8, 16)
C_GROUP_DIM = 256
N_CHIPS = 4
EPS = 1e-6
ADAM_LR, ADAM_B1, ADAM_B2, ADAM_EPS, ADAM_WD, ADAM_STEP = 0.001, 0.9, 0.999, 1e-08, 0.01, 10
VMEM_LIMIT_BYTES = 56 * 1024 * 1024
FFN_HIDDEN = N_CHIPS * FFN_SHARD
FFN_CHUNKS = ((0, 768), (768, 768), (1536, 768), (2304, 512))
FFN_FWD_ROWS = 512
FFN_BWD_ROWS = 256
A_ROWS = 256
MESH = pl.DeviceIdType.MESH
GATHER_COLLECTIVE_ID = 1
SIBLING_COLLECTIVE_ID = 2
CHIPS_COLLECTIVE_ID = 3
ALL_COLLECTIVE_ID = 4
ANY = pl.BlockSpec(memory_space=pl.ANY)
NT_DIMS = (((1,), (1,)), ((), ()))
TN_DIMS = (((0,), (0,)), ((), ()))
INV_SQRT2 = 0.7071067811865476
INV_SQRT_2PI = 0.3989422804014327


def _params(*semantics):
    return pltpu.CompilerParams(dimension_semantics=semantics, vmem_limit_bytes=VMEM_LIMIT_BYTES)


def _dot(a, b):
    return jnp.dot(a, b, preferred_element_type=F32)


def _dot_nt(a, b):
    return lax.dot_general(a, b, NT_DIMS, preferred_element_type=F32)


def _rms(x):
    r = lax.rsqrt(jnp.mean(x * x, axis=-1, keepdims=True) + EPS)
    return x * r, r


def _rms_bwd(x, g, dh):
    xh, r = _rms(x)
    dxh = dh * g
    dx = r * (dxh - xh * jnp.mean(dxh * xh, axis=-1, keepdims=True))
    return dx, jnp.sum(dh * xh, axis=0, keepdims=True)


def _gelu(x):
    return 0.5 * x * (1.0 + lax.erf(x * INV_SQRT2))


def _gelu_grad(x):
    return 0.5 * (1.0 + lax.erf(x * INV_SQRT2)) + x * jnp.exp(-0.5 * x * x) * INV_SQRT_2PI


def _shift_down(v, s, row):
    return jnp.where(row >= s, pltpu.roll(v, s, 0), 0.0)


def _shift_up(v, s, row):
    n = v.shape[0]
    return jnp.where(row < n - s, pltpu.roll(v, n - s, 0), 0.0)


def _row_tile(t, want):
    return want if t % want == 0 else t


def _after(body, first, deps):
    if not deps:
        return body

    def ordered(*refs):
        return body(*refs[:first], *refs[first + len(deps):])

    return ordered


def norm_mm(x, g, w, chunks, n_parts, part_width, name):
    t = x.shape[0]
    tm = _row_tile(t, 512)
    n_shards, _, hs = w.shape

    def body(x_ref, g_ref, w_ref, h_ref, p_ref):
        xh, _ = _rms(x_ref[...])
        h = (xh * g_ref[...]).astype(BF16)
        h_ref[...] = h
        for s in range(n_shards):
            res = _dot(h, w_ref[s]).astype(BF16)
            for (cs, wc, width, part, pc) in chunks:
                if cs == s:
                    p_ref[part, :, pc:pc + width] = res[:, wc:wc + width]

    return pl.pallas_call(
        body, name=name, grid=(t // tm,),
        in_specs=[pl.BlockSpec((tm, D), lambda i: (i, 0)), pl.BlockSpec((1, D), lambda i: (0, 0)),
                  pl.BlockSpec((n_shards, D, hs), lambda i: (0, 0, 0))],
        out_specs=[pl.BlockSpec((tm, D), lambda i: (i, 0)), pl.BlockSpec((n_parts, tm, part_width), lambda i: (0, i, 0))],
        out_shape=[jax.ShapeDtypeStruct((t, D), BF16), jax.ShapeDtypeStruct((n_parts, t, part_width), BF16)],
        compiler_params=_params("arbitrary"),
    )(x, g, w)


def mm_res(a, w, res, name):
    t, k = a.shape
    n = w.shape[1]
    tm = _row_tile(t, 512)

    def body(a_ref, w_ref, r_ref, o_ref):
        o_ref[...] = r_ref[...] + _dot(a_ref[...], w_ref[...])

    return pl.pallas_call(
        body, name=name, grid=(t // tm,),
        in_specs=[pl.BlockSpec((tm, k), lambda i: (i, 0)), pl.BlockSpec((k, n), lambda i: (0, 0)),
                  pl.BlockSpec((tm, n), lambda i: (i, 0))],
        out_specs=pl.BlockSpec((tm, n), lambda i: (i, 0)),
        out_shape=jax.ShapeDtypeStruct((t, n), F32),
        compiler_params=_params("arbitrary"),
    )(a, w, res)


def mm_nt(a, w, name):
    t, n = a.shape
    k = w.shape[0]
    tm = _row_tile(t, 512)

    def body(a_ref, w_ref, o_ref):
        o_ref[...] = _dot_nt(a_ref[...].astype(BF16), w_ref[...]).astype(BF16)

    return pl.pallas_call(
        body, name=name, grid=(t // tm,),
        in_specs=[pl.BlockSpec((tm, n), lambda i: (i, 0)), pl.BlockSpec((k, n), lambda i: (0, 0))],
        out_specs=pl.BlockSpec((tm, k), lambda i: (i, 0)),
        out_shape=jax.ShapeDtypeStruct((t, k), BF16),
        compiler_params=_params("arbitrary"),
    )(a, w)


def bwd_in(dp, w, chunks, x, g, dres, name):
    n_parts, t, part_width = dp.shape
    n_shards, _, hs = w.shape
    tm = _row_tile(t, 512)

    def body(dp_ref, w_ref, x_ref, g_ref, dres_ref, dx_ref, dg_ref):
        acc = jnp.zeros((tm, D), F32)
        for (cs, wc, width, part, pc) in chunks:
            acc = acc + _dot_nt(dp_ref[part, :, pc:pc + width], w_ref[cs, :, wc:wc + width])
        dx, dg = _rms_bwd(x_ref[...], g_ref[...], acc)
        dx_ref[...] = dres_ref[...] + dx

        @pl.when(pl.program_id(0) == 0)
        def _():
            dg_ref[...] = jnp.zeros_like(dg_ref)

        dg_ref[0:1, :] += dg

    return pl.pallas_call(
        body, name=name, grid=(t // tm,),
        in_specs=[pl.BlockSpec((n_parts, tm, part_width), lambda i: (0, i, 0)),
                  pl.BlockSpec((n_shards, D, hs), lambda i: (0, 0, 0)),
                  pl.BlockSpec((tm, D), lambda i: (i, 0)), pl.BlockSpec((1, D), lambda i: (0, 0)),
                  pl.BlockSpec((tm, D), lambda i: (i, 0))],
        out_specs=[pl.BlockSpec((tm, D), lambda i: (i, 0)), pl.BlockSpec((8, D), lambda i: (0, 0))],
        out_shape=[jax.ShapeDtypeStruct((t, D), F32), jax.ShapeDtypeStruct((8, D), F32)],
        compiler_params=_params("arbitrary"),
    )(dp, w, x, g, dres)


def mm_tn(a, b, out_shape, tm, tn, n_tiles, a_idx, b_idx, o_idx, name, deps=()):
    t = a.shape[1]

    def body(a_ref, b_ref, o_ref):
        o_ref[0] = lax.dot_general(a_ref[0].astype(BF16), b_ref[0].astype(BF16), TN_DIMS, preferred_element_type=F32).astype(BF16)

    return pl.pallas_call(
        _after(body, 2, deps), name=name, grid=(n_tiles,),
        in_specs=[pl.BlockSpec((1, t, tm), lambda j: (a_idx(j)[0], 0, a_idx(j)[1])),
                  pl.BlockSpec((1, t, tn), lambda j: (b_idx(j)[0], 0, b_idx(j)[1]))] + [ANY] * len(deps),
        out_specs=pl.BlockSpec((1, tm, tn), lambda j: o_idx(j)),
        out_shape=jax.ShapeDtypeStruct(out_shape, BF16),
        compiler_params=_params("arbitrary"),
    )(a, b, *deps)


def ffn_fwd(x, g, wg, wu, wd, name):
    t = x.shape[0]
    tm = _row_tile(t, FFN_FWD_ROWS)
    hidden = wg.shape[0]

    def body(x_ref, g_ref, wg_ref, wu_ref, wd_ref, h_ref, a_ref, b_ref, s_ref, o_ref):
        xv = x_ref[...]
        xh, _ = _rms(xv)
        h = (xh * g_ref[...]).astype(BF16)
        h_ref[...] = h
        acc = xv
        for c0, cw in FFN_CHUNKS:
            cols = slice(c0, c0 + cw)
            a = _dot_nt(h, wg_ref[cols, :])
            b = _dot_nt(h, wu_ref[cols, :])
            sig = jax.nn.sigmoid(a)
            silu = a * sig
            s = (silu * b).astype(BF16)
            a_ref[:, cols] = (b * (sig * (1.0 + a * (1.0 - sig)))).astype(BF16)
            b_ref[:, cols] = silu.astype(BF16)
            s_ref[:, cols] = s
            acc = acc + _dot(s, wd_ref[cols, :])
        o_ref[...] = acc

    act = pl.BlockSpec((tm, hidden), lambda i: (i, 0))
    act_shape = jax.ShapeDtypeStruct((t, hidden), BF16)
    wspec = pl.BlockSpec((hidden, D), lambda i: (0, 0), pipeline_mode=pl.Buffered(1))
    row = pl.BlockSpec((tm, D), lambda i: (i, 0))
    return pl.pallas_call(
        body, name=name, grid=(t // tm,),
        in_specs=[row, pl.BlockSpec((1, D), lambda i: (0, 0)), wspec, wspec, wspec],
        out_specs=[row, act, act, act, row],
        out_shape=[jax.ShapeDtypeStruct((t, D), BF16), act_shape, act_shape, act_shape, jax.ShapeDtypeStruct((t, D), F32)],
        compiler_params=_params("arbitrary"),
    )(x, g, wg, wu, wd)


def ffn_bwd(dxo, a, b, x, g, wg, wu, wd, name, deps=()):
    t = x.shape[0]
    tm = _row_tile(t, FFN_BWD_ROWS)
    hidden = wg.shape[0]

    def body(dxo_ref, a_ref, b_ref, x_ref, g_ref, wg_ref, wu_ref, wd_ref, dx_ref, da_ref, db_ref, dg_ref):
        @pl.when(pl.program_id(0) == 0)
        def _():
            dg_ref[...] = jnp.zeros_like(dg_ref)

        dxo = dxo_ref[...]
        dxb = dxo.astype(BF16)
        acc = None
        for c0, cw in FFN_CHUNKS:
            cols = slice(c0, c0 + cw)
            ds = _dot_nt(dxb, wd_ref[cols, :])
            da = (ds * a_ref[:, cols].astype(F32)).astype(BF16)
            db = (ds * b_ref[:, cols].astype(F32)).astype(BF16)
            da_ref[:, cols] = da
            db_ref[:, cols] = db
            part = _dot(da, wg_ref[cols, :]) + _dot(db, wu_ref[cols, :])
            acc = part if acc is None else acc + part
        dx, dg = _rms_bwd(x_ref[...], g_ref[...], acc)
        dx_ref[...] = dxo + dx
        dg_ref[0:1, :] += dg

    act = pl.BlockSpec((tm, hidden), lambda i: (i, 0))
    act_shape = jax.ShapeDtypeStruct((t, hidden), BF16)
    row = pl.BlockSpec((tm, D), lambda i: (i, 0))
    wspec = pl.BlockSpec((hidden, D), lambda i: (0, 0), pipeline_mode=pl.Buffered(1))
    return pl.pallas_call(
        _after(body, 8, deps), name=name, grid=(t // tm,),
        in_specs=[row, act, act, row, pl.BlockSpec((1, D), lambda i: (0, 0)), wspec, wspec, wspec] + [ANY] * len(deps),
        out_specs=[row, act, act, pl.BlockSpec((8, D), lambda i: (0, 0))],
        out_shape=[jax.ShapeDtypeStruct((t, D), F32), act_shape, act_shape, jax.ShapeDtypeStruct((8, D), F32)],
        compiler_params=_params("arbitrary"),
    )(dxo, a, b, x, g, wg, wu, wd, *deps)


def _layer_norm_stats(v):
    mu = jnp.mean(v, axis=-1, keepdims=True)
    vc = v - mu
    rstd = lax.rsqrt(jnp.mean(vc * vc, axis=-1, keepdims=True) + EPS)
    return vc * rstd, rstd


def a_fwd(x, g, win, gv, wm, bs, wout, name):
    t = x.shape[0]
    tm = _row_tile(t, A_ROWS)
    n_shards, _, hs = win.shape

    def body(x_ref, g_ref, win_ref, gv_ref, wm_ref, bs_ref, wout_ref, h_ref, z_ref, vn_ref, y_ref, o_ref):
        xv = x_ref[...]
        xh, _ = _rms(xv)
        h = (xh * g_ref[...]).astype(BF16)
        h_ref[...] = h
        zs = []
        for s in range(n_shards):
            zb = _dot(h, win_ref[s]).astype(BF16)
            z_ref[:, s * hs:(s + 1) * hs] = zb
            zs.append(zb.astype(F32))
        half = n_shards // 2
        u = _gelu(jnp.concatenate(zs[:half], axis=1))
        vhat, _ = _layer_norm_stats(_gelu(jnp.concatenate(zs[half:], axis=1)))
        vnb = (vhat * gv_ref[...]).astype(BF16)
        vn_ref[...] = vnb
        for n in range(tm // GMLP_BLOCK):
            rows = slice(n * GMLP_BLOCK, (n + 1) * GMLP_BLOCK)
            for grp in range(A_GROUPS):
                cols = slice(grp * 128, (grp + 1) * 128)
                sv = _dot(wm_ref[grp], vnb[rows, cols]) + bs_ref[grp]
                y_ref[rows, cols] = (u[rows, cols] * sv).astype(BF16)
        o_ref[...] = xv + _dot(y_ref[...], wout_ref[...])

    small = pl.BlockSpec((A_GROUPS, 128, 128), lambda i: (0, 0, 0))
    row = pl.BlockSpec((tm, D), lambda i: (i, 0))
    wide = pl.BlockSpec((tm, 2 * D), lambda i: (i, 0))
    gain = pl.BlockSpec((1, D), lambda i: (0, 0))
    return pl.pallas_call(
        body, name=name, grid=(t // tm,),
        in_specs=[row, gain, pl.BlockSpec((n_shards, D, hs), lambda i: (0, 0, 0)), gain, small, small,
                  pl.BlockSpec((D, D), lambda i: (0, 0))],
        out_specs=[row, wide, row, row, row],
        out_shape=[jax.ShapeDtypeStruct((t, D), BF16), jax.ShapeDtypeStruct((t, 2 * D), BF16), jax.ShapeDtypeStruct((t, D), BF16),
                   jax.ShapeDtypeStruct((t, D), BF16), jax.ShapeDtypeStruct((t, D), F32)],
        compiler_params=_params("arbitrary"),
    )(x, g, win, gv, wm, bs, wout)


def a_bwd(dxm, x, g, z, vn, gv, wm, wmt, bs, wout, win, name):
    t = x.shape[0]
    tm = _row_tile(t, A_ROWS)
    n_shards, _, hs = win.shape

    def body(dxm_ref, x_ref, g_ref, z_ref, vn_ref, gv_ref, wm_ref, wmt_ref, bs_ref, wout_ref, win_ref,
             dx_ref, dz_ref, dwm_ref, dbs_ref, dgv_ref, dg_ref, du_ref, dvn_ref):
        @pl.when(pl.program_id(0) == 0)
        def _():
            dwm_ref[...] = jnp.zeros_like(dwm_ref)
            dbs_ref[...] = jnp.zeros_like(dbs_ref)
            dgv_ref[...] = jnp.zeros_like(dgv_ref)
            dg_ref[...] = jnp.zeros_like(dg_ref)

        dxm = dxm_ref[...]
        dyv = _dot_nt(dxm.astype(BF16), wout_ref[...])
        zz = z_ref[...].astype(F32)
        zu, zv = zz[:, :D], zz[:, D:]
        u = _gelu(zu)
        vhat, rstd = _layer_norm_stats(_gelu(zv))
        vnb = vn_ref[...]
        ones = jnp.ones((128, 128), BF16)
        for n in range(tm // GMLP_BLOCK):
            rows = slice(n * GMLP_BLOCK, (n + 1) * GMLP_BLOCK)
            for grp in range(A_GROUPS):
                cols = slice(grp * 128, (grp + 1) * 128)
                blk = vnb[rows, cols]
                sv = _dot(wm_ref[grp], blk) + bs_ref[grp]
                dyb = dyv[rows, cols]
                du_ref[rows, cols] = dyb * sv
                dsv = (dyb * u[rows, cols]).astype(BF16)
                dvn_ref[rows, cols] = _dot(wmt_ref[grp], dsv)
                dwm_ref[grp] += _dot_nt(dsv, blk)
                dbs_ref[grp] += _dot(dsv, ones)
        dvn = dvn_ref[...]
        dgv_ref[0:1, :] += jnp.sum(dvn * vhat, axis=0, keepdims=True)
        dvh = dvn * gv_ref[...]
        dv = rstd * (dvh - jnp.mean(dvh, axis=-1, keepdims=True) - vhat * jnp.mean(dvh * vhat, axis=-1, keepdims=True))
        dz_ref[:, :D] = (du_ref[...] * _gelu_grad(zu)).astype(BF16)
        dz_ref[:, D:] = (dv * _gelu_grad(zv)).astype(BF16)
        dh = None
        for s in range(n_shards):
            part = _dot_nt(dz_ref[:, s * hs:(s + 1) * hs], win_ref[s])
            dh = part if dh is None else dh + part
        dx, dg = _rms_bwd(x_ref[...], g_ref[...], dh)
        dx_ref[...] = dxm + dx
        dg_ref[0:1, :] += dg

    small = pl.BlockSpec((A_GROUPS, 128, 128), lambda i: (0, 0, 0))
    row = pl.BlockSpec((tm, D), lambda i: (i, 0))
    wide = pl.BlockSpec((tm, 2 * D), lambda i: (i, 0))
    gain = pl.BlockSpec((1, D), lambda i: (0, 0))
    sums = pl.BlockSpec((8, D), lambda i: (0, 0))
    small_shape = jax.ShapeDtypeStruct((A_GROUPS, 128, 128), F32)
    sums_shape = jax.ShapeDtypeStruct((8, D), F32)
    return pl.pallas_call(
        body, name=name, grid=(t // tm,),
        in_specs=[row, row, gain, wide, row, gain, small, small, small, pl.BlockSpec((D, D), lambda i: (0, 0)),
                  pl.BlockSpec((n_shards, D, hs), lambda i: (0, 0, 0))],
        out_specs=[row, wide, small, small, sums, sums],
        out_shape=[jax.ShapeDtypeStruct((t, D), F32), jax.ShapeDtypeStruct((t, 2 * D), BF16), small_shape, small_shape,
                   sums_shape, sums_shape],
        scratch_shapes=[pltpu.VMEM((tm, D), F32), pltpu.VMEM((tm, D), F32)],
        compiler_params=_params("arbitrary"),
    )(dxm, x, g, z, vn, gv, wm, wmt, bs, wout, win)


def _conv_terms(p_ref, row):
    gb = p_ref[0].astype(F32)
    gc = p_ref[1].astype(F32)
    xt = p_ref[2].astype(F32)
    q = gc * xt
    return gb, gc, xt, q, _shift_down(q, 1, row), _shift_down(q, 2, row)


def b_conv_fwd(p3, cw, seq, name):
    t = p3.shape[1]
    cb = 256

    def body(p_ref, cw_ref, y_ref):
        row = lax.broadcasted_iota(jnp.int32, (seq, cb), 0)
        gb, _, _, q, q1, q2 = _conv_terms(p_ref, row)
        y_ref[...] = (gb * (cw_ref[2:3, :] * q + cw_ref[1:2, :] * q1 + cw_ref[0:1, :] * q2)).astype(BF16)

    return pl.pallas_call(
        body, name=name, grid=(t // seq, D // cb),
        in_specs=[pl.BlockSpec((3, seq, cb), lambda e, c: (0, e, c)), pl.BlockSpec((3, cb), lambda e, c: (0, c))],
        out_specs=pl.BlockSpec((seq, cb), lambda e, c: (e, c)),
        out_shape=jax.ShapeDtypeStruct((t, D), BF16),
        compiler_params=_params("arbitrary", "arbitrary"),
    )(p3, cw)


def b_conv_bwd(dy, p3, cw, seq, name):
    t = p3.shape[1]
    cb = 256

    def body(dy_ref, p_ref, cw_ref, dp_ref, dcw_ref):
        @pl.when(pl.program_id(1) == 0)
        def _():
            dcw_ref[...] = jnp.zeros_like(dcw_ref)

        row = lax.broadcasted_iota(jnp.int32, (seq, cb), 0)
        gb, gc, xt, q, q1, q2 = _conv_terms(p_ref, row)
        dyv = dy_ref[...].astype(F32)
        conv = cw_ref[2:3, :] * q + cw_ref[1:2, :] * q1 + cw_ref[0:1, :] * q2
        dyc = dyv * gb
        dq = cw_ref[2:3, :] * dyc + cw_ref[1:2, :] * _shift_up(dyc, 1, row) + cw_ref[0:1, :] * _shift_up(dyc, 2, row)
        dp_ref[0] = (dyv * conv).astype(BF16)
        dp_ref[1] = (dq * xt).astype(BF16)
        dp_ref[2] = (dq * gc).astype(BF16)
        dcw_ref[0:1, :] += jnp.sum(dyc * q2, axis=0, keepdims=True)
        dcw_ref[1:2, :] += jnp.sum(dyc * q1, axis=0, keepdims=True)
        dcw_ref[2:3, :] += jnp.sum(dyc * q, axis=0, keepdims=True)

    return pl.pallas_call(
        body, name=name, grid=(D // cb, t // seq),
        in_specs=[pl.BlockSpec((seq, cb), lambda c, e: (e, c)), pl.BlockSpec((3, seq, cb), lambda c, e: (0, e, c)),
                  pl.BlockSpec((3, cb), lambda c, e: (0, c))],
        out_specs=[pl.BlockSpec((3, seq, cb), lambda c, e: (0, e, c)), pl.BlockSpec((8, cb), lambda c, e: (0, c))],
        out_shape=[jax.ShapeDtypeStruct((3, t, D), BF16), jax.ShapeDtypeStruct((8, D), F32)],
        compiler_params=_params("arbitrary", "arbitrary"),
    )(dy, p3, cw)


def c_pool_fwd(p, seq, name):
    t = p.shape[0]

    def make(grp):
        w = POOL_WINDOWS[grp]

        def body_g(p_ref, d_ref):
            row = lax.broadcasted_iota(jnp.int32, (seq, C_GROUP_DIM), 0)
            pv = p_ref[...].astype(F32)
            acc = pv
            sh = 1
            while sh < w:
                acc = acc + _shift_down(acc, sh, row)
                sh *= 2
            d_ref[...] = (acc / jnp.minimum(row + 1, w).astype(F32) - pv).astype(BF16)

        return body_g

    outs = []
    for grp in range(len(POOL_WINDOWS)):
        outs.append(pl.pallas_call(
            make(grp), name=f"{name}_g{grp}", grid=(t // seq,),
            in_specs=[pl.BlockSpec((seq, C_GROUP_DIM), lambda e, grp=grp: (e, grp))],
            out_specs=pl.BlockSpec((seq, C_GROUP_DIM), lambda e: (e, 0)),
            out_shape=jax.ShapeDtypeStruct((t, C_GROUP_DIM), BF16),
            compiler_params=_params("arbitrary"),
        )(p))
    return outs


def c_pool_bwd(dd, seq, name):
    t = dd[0].shape[0]

    def make(w):
        def body_g(dd_ref, dp_ref):
            row = lax.broadcasted_iota(jnp.int32, (seq, C_GROUP_DIM), 0)
            ddv = dd_ref[...]
            acc = ddv / jnp.minimum(row + 1, w).astype(F32)
            sh = 1
            while sh < w:
                acc = acc + _shift_up(acc, sh, row)
                sh *= 2
            dp_ref[...] = (acc - ddv).astype(BF16)

        return body_g

    outs = []
    for grp, w in enumerate(POOL_WINDOWS):
        outs.append(pl.pallas_call(
            make(w), name=f"{name}_g{grp}", grid=(t // seq,),
            in_specs=[pl.BlockSpec((seq, C_GROUP_DIM), lambda e: (e, 0))],
            out_specs=pl.BlockSpec((seq, C_GROUP_DIM), lambda e: (e, 0)),
            out_shape=jax.ShapeDtypeStruct((t, C_GROUP_DIM), BF16),
            compiler_params=_params("arbitrary"),
        )(dd[grp]))
    return outs


def c_out_fwd(d, wgrp, scale, wo, x, name):
    t = x.shape[0]
    tm = _row_tile(t, 512)
    n_g = len(POOL_WINDOWS)

    def body(d0, d1, d2, d3, wg_ref, sc_ref, wo_ref, x_ref, y_ref, o_ref):
        parts = [_dot(dr[...], wg_ref[grp]) for grp, dr in enumerate((d0, d1, d2, d3))]
        y = (jnp.concatenate(parts, axis=1) * sc_ref[...]).astype(BF16)
        y_ref[...] = y
        o_ref[...] = x_ref[...] + _dot(y, wo_ref[...])

    dspec = pl.BlockSpec((tm, C_GROUP_DIM), lambda i: (i, 0))
    row = pl.BlockSpec((tm, D), lambda i: (i, 0))
    return pl.pallas_call(
        body, name=name, grid=(t // tm,),
        in_specs=[dspec] * n_g + [pl.BlockSpec((n_g, C_GROUP_DIM, C_GROUP_DIM), lambda i: (0, 0, 0)),
                                  pl.BlockSpec((1, D), lambda i: (0, 0)), pl.BlockSpec((D, D), lambda i: (0, 0)), row],
        out_specs=[row, row],
        out_shape=[jax.ShapeDtypeStruct((t, D), BF16), jax.ShapeDtypeStruct((t, D), F32)],
        compiler_params=_params("arbitrary"),
    )(*d, wgrp, scale, wo, x)


def c_out_bwd(dxm, d, wgrp, scale, wo, name):
    t = dxm.shape[0]
    tm = _row_tile(t, 512)
    n_g = len(POOL_WINDOWS)

    def body(dxm_ref, d0, d1, d2, d3, wg_ref, sc_ref, wo_ref, dyp_ref, dd0, dd1, dd2, dd3, dsc_ref):
        @pl.when(pl.program_id(0) == 0)
        def _():
            dsc_ref[...] = jnp.zeros_like(dsc_ref)

        dyo = _dot_nt(dxm_ref[...].astype(BF16), wo_ref[...])
        ypre = jnp.concatenate([_dot(dr[...], wg_ref[grp]) for grp, dr in enumerate((d0, d1, d2, d3))], axis=1)
        dsc_ref[0:1, :] += jnp.sum(dyo * ypre, axis=0, keepdims=True)
        dyp = (dyo * sc_ref[...]).astype(BF16)
        dyp_ref[...] = dyp
        for grp, ddr in enumerate((dd0, dd1, dd2, dd3)):
            ddr[...] = _dot_nt(dyp[:, grp * C_GROUP_DIM:(grp + 1) * C_GROUP_DIM], wg_ref[grp])

    dspec = pl.BlockSpec((tm, C_GROUP_DIM), lambda i: (i, 0))
    row = pl.BlockSpec((tm, D), lambda i: (i, 0))
    dshape = jax.ShapeDtypeStruct((t, C_GROUP_DIM), F32)
    return pl.pallas_call(
        body, name=name, grid=(t // tm,),
        in_specs=[row] + [dspec] * n_g + [pl.BlockSpec((n_g, C_GROUP_DIM, C_GROUP_DIM), lambda i: (0, 0, 0)),
                                          pl.BlockSpec((1, D), lambda i: (0, 0)), pl.BlockSpec((D, D), lambda i: (0, 0))],
        out_specs=[row] + [dspec] * n_g + [pl.BlockSpec((8, D), lambda i: (0, 0))],
        out_shape=[jax.ShapeDtypeStruct((t, D), BF16)] + [dshape] * n_g + [jax.ShapeDtypeStruct((8, D), F32)],
        compiler_params=_params("arbitrary"),
    )(dxm, *d, wgrp, scale, wo)


def loss_head(x, tgt, g, name):
    t = x.shape[0]
    tm = _row_tile(t, 512)

    def body(x_ref, t_ref, g_ref, dx_ref, dg_ref, loss_ref):
        @pl.when(pl.program_id(0) == 0)
        def _():
            dg_ref[...] = jnp.zeros_like(dg_ref)
            loss_ref[...] = jnp.zeros_like(loss_ref)

        xv, gvv = x_ref[...], g_ref[...]
        xh, _ = _rms(xv)
        diff = xh * gvv - t_ref[...]
        loss_ref[...] += 0.5 * jnp.sum(jnp.mean(diff * diff, axis=-1, keepdims=True))
        dx, dg = _rms_bwd(xv, gvv, diff * (1.0 / D))
        dx_ref[...] = dx
        dg_ref[0:1, :] += dg

    row = pl.BlockSpec((tm, D), lambda i: (i, 0))
    return pl.pallas_call(
        body, name=name, grid=(t // tm,),
        in_specs=[row, row, pl.BlockSpec((1, D), lambda i: (0, 0))],
        out_specs=[row, pl.BlockSpec((8, D), lambda i: (0, 0)), pl.BlockSpec((8, 128), lambda i: (0, 0))],
        out_shape=[jax.ShapeDtypeStruct((t, D), F32), jax.ShapeDtypeStruct((8, D), F32), jax.ShapeDtypeStruct((8, 128), F32)],
        compiler_params=_params("arbitrary"),
    )(x, tgt, g)


def adamw(w, g, m, v, name):
    rows, cols = w.shape
    tr = rows
    for cand in (512, 256, 128, 64, 32, 16, 8):
        if rows % cand == 0 and rows > cand:
            tr = cand
            break

    def body(w_ref, g_ref, m_ref, v_ref, d_ref, mo_ref, vo_ref):
        gv = g_ref[...]
        mn = ADAM_B1 * m_ref[...] + (1.0 - ADAM_B1) * gv
        vn = ADAM_B2 * v_ref[...] + (1.0 - ADAM_B2) * (gv * gv)
        m_hat = mn / (1.0 - ADAM_B1 ** ADAM_STEP)
        v_hat = vn / (1.0 - ADAM_B2 ** ADAM_STEP)
        d_ref[...] = -ADAM_LR * (m_hat / (jnp.sqrt(v_hat) + ADAM_EPS) + ADAM_WD * w_ref[...])
        mo_ref[...] = mn
        vo_ref[...] = vn

    spec = pl.BlockSpec((tr, cols), lambda i: (i, 0))
    shape = jax.ShapeDtypeStruct((rows, cols), F32)
    return pl.pallas_call(
        body, name=name, grid=(rows // tr,),
        in_specs=[spec] * 4, out_specs=[spec] * 3, out_shape=[shape] * 3,
        compiler_params=_params("arbitrary"),
    )(w, g, m, v)


def adamw_layer(w, m, v, own, recv, core, layer, carried, name):
    n_layers, rows, cols = w.shape
    h = rows // 2

    def body(core_ref, w_ref, m_ref, v_ref, own_ref, recv_ref, *rest):
        g_ref, d_ref, mo_ref, vo_ref = rest[-4:]
        gv = jnp.where(pl.program_id(0) == core_ref[0], own_ref[...], recv_ref[...])
        mn = ADAM_B1 * m_ref[0] + (1.0 - ADAM_B1) * gv
        vn = ADAM_B2 * v_ref[0] + (1.0 - ADAM_B2) * (gv * gv)
        m_hat = mn / (1.0 - ADAM_B1 ** ADAM_STEP)
        v_hat = vn / (1.0 - ADAM_B2 ** ADAM_STEP)
        g_ref[0] = gv
        d_ref[0] = -ADAM_LR * (m_hat / (jnp.sqrt(v_hat) + ADAM_EPS) + ADAM_WD * w_ref[0])
        mo_ref[0] = mn
        vo_ref[0] = vn

    steps = 1
    tr = h // steps
    stacked = pl.BlockSpec((1, tr, cols), lambda half, j, core_ref: (layer, half * steps + j, 0))
    halfspec = pl.BlockSpec((tr, cols), lambda half, j, core_ref: (j, 0))
    n_carried = 0 if carried is None else 4
    shape = jax.ShapeDtypeStruct(w.shape, F32)
    return pl.pallas_call(
        body, name=name,
        grid_spec=pltpu.PrefetchScalarGridSpec(
            num_scalar_prefetch=1, grid=(2, steps),
            in_specs=[stacked] * 3 + [halfspec] * 2 + [ANY] * n_carried, out_specs=[stacked] * 4),
        out_shape=[shape] * 4,
        input_output_aliases={6 + i: i for i in range(n_carried)},
        compiler_params=_params("arbitrary", "arbitrary"),
    )(core, w, m, v, own, recv, *(carried or ()))


def add_halves(gs, ps, core, name, deps=()):
    n = len(gs)

    def body(core_ref, *refs):
        for i in range(n):
            refs[2 * n + i][...] = (refs[i][...].astype(F32) + refs[n + i][...].astype(F32)).astype(BF16)

    in_specs, out_specs, out_shape = [], [], []
    for gt in gs:
        h, c = gt.shape[1] // 2, gt.shape[2]
        in_specs.append(pl.BlockSpec((1, h, c), lambda b, core_ref: (b, core_ref[0], 0)))
    for gt in gs:
        h, c = gt.shape[1] // 2, gt.shape[2]
        in_specs.append(pl.BlockSpec((1, h, c), lambda b, core_ref: (b, 0, 0)))
        out_specs.append(pl.BlockSpec((1, h, c), lambda b, core_ref: (b, 0, 0)))
        out_shape.append(jax.ShapeDtypeStruct((N_CHIPS, h, c), BF16))
    in_specs += [ANY] * len(deps)
    return pl.pallas_call(
        _after(body, 1 + 2 * n, deps), name=name,
        grid_spec=pltpu.PrefetchScalarGridSpec(num_scalar_prefetch=1, grid=(N_CHIPS,), in_specs=in_specs, out_specs=out_specs),
        out_shape=out_shape, compiler_params=_params("arbitrary"),
    )(core, *gs, *ps, *deps)


def add_final(hs, qs, chip, name, deps=()):
    n = len(hs)

    def body(chip_ref, *refs):
        for i in range(n):
            q = refs[n + i]
            refs[2 * n + i][...] = ((refs[i][0].astype(F32) + q[0].astype(F32)) + q[1].astype(F32)) + q[2].astype(F32)

    steps = 2
    in_specs, out_specs, out_shape = [], [], []
    for ht in hs:
        h, c = ht.shape[1], ht.shape[2]
        in_specs.append(pl.BlockSpec((1, h // steps, c), lambda i, chip_ref: (chip_ref[0], i, 0)))
    for ht in hs:
        h, c = ht.shape[1], ht.shape[2]
        in_specs.append(pl.BlockSpec((N_CHIPS - 1, h // steps, c), lambda i, chip_ref: (0, i, 0)))
        out_specs.append(pl.BlockSpec((h // steps, c), lambda i, chip_ref: (i, 0)))
        out_shape.append(jax.ShapeDtypeStruct((h, c), F32))
    in_specs += [ANY] * len(deps)
    return pl.pallas_call(
        _after(body, 1 + 2 * n, deps), name=name,
        grid_spec=pltpu.PrefetchScalarGridSpec(num_scalar_prefetch=1, grid=(steps,), in_specs=in_specs, out_specs=out_specs),
        out_shape=out_shape, compiler_params=_params("arbitrary"),
    )(chip, *hs, *qs, *deps)


def sum_devices(own, gathered, me, name, deps=()):
    rows = own.shape[0]

    def body(me_ref, own_ref, g_ref, o_ref):
        me_dev = me_ref[0]
        acc = None
        for dev in range(8):
            slot = jnp.maximum((me_dev ^ dev) - 1, 0)
            term = jnp.where(me_dev == dev, own_ref[...], g_ref[slot])
            acc = term if acc is None else acc + term
        o_ref[...] = acc

    tr = next(c for c in range(80, 0, -8) if rows % c == 0)
    return pl.pallas_call(
        _after(body, 3, deps), name=name,
        grid_spec=pltpu.PrefetchScalarGridSpec(
            num_scalar_prefetch=1, grid=(rows // tr,),
            in_specs=[pl.BlockSpec((tr, D), lambda i, me_ref: (i, 0)), pl.BlockSpec((7, tr, D), lambda i, me_ref: (0, i, 0))]
            + [ANY] * len(deps),
            out_specs=pl.BlockSpec((tr, D), lambda i, me_ref: (i, 0))),
        out_shape=jax.ShapeDtypeStruct((rows, D), F32),
        compiler_params=_params("arbitrary"),
    )(me, own, gathered, *deps)


def _mesh_pos():
    return lax.axis_index("x"), lax.axis_index("y"), lax.axis_index("c")


def _other_chips(x, y):
    return [(1 - x, y), (x, 1 - y), (1 - x, 1 - y)]


def _sibling():
    x, y, c = _mesh_pos()
    return [(x, y, 1 - c)]


def _same_core_of_other_chips():
    x, y, c = _mesh_pos()
    return [(cx, cy, c) for (cx, cy) in _other_chips(x, y)]


def _on_sequencer(body, name, operands, out_shapes, sems, peers, collective_id, deps=()):
    ordered = _after(body, len(operands), deps)

    def seq_body(*refs):
        barrier = pltpu.get_barrier_semaphore()
        with_whom = peers()
        for peer in with_whom:
            pl.semaphore_signal(barrier, inc=1, device_id=peer, device_id_type=MESH)
        pl.semaphore_wait(barrier, len(with_whom))
        ordered(*refs)

    return pl.kernel(
        seq_body, name=name, out_type=out_shapes,
        mesh=plsc.ScalarSubcoreMesh(axis_name="seq", num_cores=1),
        scratch_types=sems, compiler_params=pltpu.CompilerParams(collective_id=collective_id, has_side_effects=True),
    )(*operands, *deps)


def all_gather_weights(shards, name, deps=()):
    n = len(shards)

    def body(*refs):
        ins, outs = refs[:n], refs[n:2 * n]
        send, recv, fsend, frecv = refs[2 * n:]
        x, y, c = _mesh_pos()
        k = 2 * x + y
        chips = _other_chips(x, y)

        def half(ref, i, rows_half):
            h = shards[i].shape[0] // 2
            return ref.at[pl.ds(pl.multiple_of(rows_half * h, 8), h), :]

        first = []
        for i in range(n):
            for j, (cx, cy) in enumerate(chips):
                first.append(pltpu.make_async_remote_copy(
                    src_ref=half(ins[i], i, c), dst_ref=half(outs[i].at[k], i, c),
                    send_sem=send.at[i, j], recv_sem=recv.at[i, j], device_id=(cx, cy, c), device_id_type=MESH))
        for cp in first:
            cp.start()
        passed = []
        for i in range(n):
            for j, (cx, cy) in enumerate(chips):
                blk = half(outs[i].at[2 * cx + cy], i, c)
                pltpu.make_async_remote_copy(src_ref=blk, dst_ref=blk, send_sem=send.at[i, j], recv_sem=recv.at[i, j],
                                             device_id=(cx, cy, c), device_id_type=MESH).wait_recv()
                fw = pltpu.make_async_remote_copy(src_ref=blk, dst_ref=blk, send_sem=fsend.at[i, j], recv_sem=frecv.at[i, j],
                                                  device_id=(x, y, 1 - c), device_id_type=MESH)
                fw.start()
                passed.append(fw)
        for i in range(n):
            for j, (cx, cy) in enumerate(chips):
                blk = half(outs[i].at[2 * cx + cy], i, 1 - c)
                pltpu.make_async_remote_copy(src_ref=blk, dst_ref=blk, send_sem=fsend.at[i, j], recv_sem=frecv.at[i, j],
                                             device_id=(x, y, 1 - c), device_id_type=MESH).wait_recv()
        for cp in first + passed:
            cp.wait_send()

    def peers():
        x, y, c = _mesh_pos()
        return [(cx, cy, c) for (cx, cy) in _other_chips(x, y)] + [(x, y, 1 - c)]

    return _on_sequencer(
        body, name, shards, [jax.ShapeDtypeStruct((N_CHIPS,) + s.shape, s.dtype) for s in shards],
        [pltpu.SemaphoreType.DMA((n, 3))] * 4, peers, GATHER_COLLECTIVE_ID, deps)


def place_own(gathered, shards, chip, name):
    n = len(shards)

    def body(chip_ref, *refs):
        for i in range(n):
            refs[2 * n + i][0] = refs[i][...]

    in_specs = [pl.BlockSpec(s.shape, lambda i, chip_ref: (0, 0)) for s in shards] + [ANY] * n
    out_specs = [pl.BlockSpec((1,) + s.shape, lambda i, chip_ref: (chip_ref[0], 0, 0)) for s in shards]
    return pl.pallas_call(
        body, name=name,
        grid_spec=pltpu.PrefetchScalarGridSpec(num_scalar_prefetch=1, grid=(1,), in_specs=in_specs, out_specs=out_specs),
        out_shape=[jax.ShapeDtypeStruct(g.shape, g.dtype) for g in gathered],
        input_output_aliases={1 + n + i: i for i in range(n)},
        compiler_params=_params("arbitrary"),
    )(chip, *shards, *gathered)


def all_gather_rows(shard, name):
    def body(in_ref, out_ref, send, recv, lsem):
        x, y, c = _mesh_pos()
        k = 2 * x + y
        chips = _other_chips(x, y)
        local = pltpu.make_async_copy(in_ref, out_ref.at[k], lsem)
        local.start()
        sent = [pltpu.make_async_remote_copy(src_ref=in_ref, dst_ref=out_ref.at[k], send_sem=send.at[j], recv_sem=recv.at[j],
                                             device_id=(cx, cy, c), device_id_type=MESH) for j, (cx, cy) in enumerate(chips)]
        for cp in sent:
            cp.start()
        for j, (cx, cy) in enumerate(chips):
            blk = out_ref.at[2 * cx + cy]
            pltpu.make_async_remote_copy(src_ref=blk, dst_ref=blk, send_sem=send.at[j], recv_sem=recv.at[j],
                                         device_id=(cx, cy, c), device_id_type=MESH).wait_recv()
        for cp in sent:
            cp.wait_send()
        local.wait()

    return pl.pallas_call(
        body, name=name, in_specs=[ANY], out_specs=ANY,
        out_shape=jax.ShapeDtypeStruct((N_CHIPS,) + shard.shape, shard.dtype),
        scratch_shapes=[pltpu.SemaphoreType.DMA((3,)), pltpu.SemaphoreType.DMA((3,)), pltpu.SemaphoreType.DMA],
    )(shard)


def swap_halves(gs, name):
    n = len(gs)

    def body(*refs):
        ins, outs = refs[:n], refs[n:2 * n]
        send, recv = refs[2 * n:]
        x, y, c = _mesh_pos()
        sent = []
        for i in range(n):
            h = gs[i].shape[1] // 2
            src = ins[i].at[:, pl.ds(pl.multiple_of((1 - c) * h, 8), h), :]
            cp = pltpu.make_async_remote_copy(src_ref=src, dst_ref=outs[i], send_sem=send.at[i], recv_sem=recv.at[i],
                                              device_id=(x, y, 1 - c), device_id_type=MESH)
            cp.start()
            sent.append(cp)
        for cp in sent:
            cp.wait()

    return _on_sequencer(
        body, name, gs, [jax.ShapeDtypeStruct((N_CHIPS, g.shape[1] // 2, g.shape[2]), g.dtype) for g in gs],
        [pltpu.SemaphoreType.DMA((n,)), pltpu.SemaphoreType.DMA((n,))], _sibling, SIBLING_COLLECTIVE_ID)


def scatter_chips(hs, name):
    n = len(hs)

    def body(*refs):
        ins, outs = refs[:n], refs[n:2 * n]
        send, recv = refs[2 * n:]
        x, y, c = _mesh_pos()
        chips = _other_chips(x, y)
        sent = []
        for i in range(n):
            for j, (cx, cy) in enumerate(chips):
                cp = pltpu.make_async_remote_copy(src_ref=ins[i].at[2 * cx + cy], dst_ref=outs[i].at[j],
                                                  send_sem=send.at[i, j], recv_sem=recv.at[i, j],
                                                  device_id=(cx, cy, c), device_id_type=MESH)
                cp.start()
                sent.append(cp)
        for cp in sent:
            cp.wait()

    return _on_sequencer(
        body, name, hs, [jax.ShapeDtypeStruct((N_CHIPS - 1,) + h.shape[1:], h.dtype) for h in hs],
        [pltpu.SemaphoreType.DMA((n, 3)), pltpu.SemaphoreType.DMA((n, 3))], _same_core_of_other_chips, CHIPS_COLLECTIVE_ID)


def swap_reduced(rs, name):
    n = len(rs)

    def body(*refs):
        ins, outs = refs[:n], refs[n:2 * n]
        send, recv = refs[2 * n:]
        x, y, c = _mesh_pos()
        sent = []
        for i in range(n):
            cp = pltpu.make_async_remote_copy(src_ref=ins[i], dst_ref=outs[i], send_sem=send.at[i], recv_sem=recv.at[i],
                                              device_id=(x, y, 1 - c), device_id_type=MESH)
            cp.start()
            sent.append(cp)
        for cp in sent:
            cp.wait()

    return _on_sequencer(
        body, name, rs, [jax.ShapeDtypeStruct(r.shape, r.dtype) for r in rs],
        [pltpu.SemaphoreType.DMA((n,)), pltpu.SemaphoreType.DMA((n,))], _sibling, SIBLING_COLLECTIVE_ID)


def all_gather_devices(part, name):
    def everyone_else():
        x, y, c = _mesh_pos()
        return [(1 - x if (rel >> 2) & 1 else x, 1 - y if (rel >> 1) & 1 else y, 1 - c if rel & 1 else c) for rel in range(1, 8)]

    def body(in_ref, out_ref, send, recv):
        sent = []
        for slot, peer in enumerate(everyone_else()):
            cp = pltpu.make_async_remote_copy(src_ref=in_ref, dst_ref=out_ref.at[slot], send_sem=send.at[slot],
                                              recv_sem=recv.at[slot], device_id=peer, device_id_type=MESH)
            cp.start()
            sent.append(cp)
        for cp in sent:
            cp.wait()

    return _on_sequencer(
        body, name, [part], jax.ShapeDtypeStruct((7,) + part.shape, part.dtype),
        [pltpu.SemaphoreType.DMA((7,)), pltpu.SemaphoreType.DMA((7,))], everyone_else, ALL_COLLECTIVE_ID)


class ReduceScatter:
    def __init__(self, grads, names, layer, core, chip, name):
        self.grads, self.names, self.layer, self.core, self.chip, self.name = grads, names, layer, core, chip, name
        self.from_sibling = swap_halves(grads, name + "_swap")

    def sum_cores(self, deps=()):
        self.core_sums = add_halves(self.grads, self.from_sibling, self.core, self.name + "_add2", deps)
        self.from_chips = scatter_chips(self.core_sums, self.name + "_scatter")
        return self.core_sums[0]

    def sum_chips(self, deps=()):
        self.half_sums = add_final(self.core_sums, self.from_chips, self.chip, self.name + "_add4", deps)
        self.other_half = swap_reduced(self.half_sums, self.name + "_join")
        return self.half_sums[0]


def _blocked(w):
    return w.reshape(w.shape[0] * w.shape[1], w.shape[2])


def _grp_from_blocks(w):
    return w.reshape(N_CHIPS, 4, 64, C_GROUP_DIM).transpose(1, 0, 2, 3).reshape(4, C_GROUP_DIM, C_GROUP_DIM)


def _grp_to_blocks(w):
    return w.reshape(4, N_CHIPS, 64, C_GROUP_DIM).transpose(1, 0, 2, 3).reshape(N_CHIPS, C_GROUP_DIM, C_GROUP_DIM)


def _dw_cols(h, dact, hs, name, deps):
    tm = 512
    return mm_tn(h[None], dact, (N_CHIPS, D, hs), tm, hs, N_CHIPS * (D // tm),
                 lambda j: (0, j % 2), lambda j: (0, j // 2), lambda j: (j // 2, j % 2, 0), name, deps)


def _dw_rows(y, dxm, name, deps):
    tm = 512
    out = mm_tn(y[None], dxm[None], (1, D, D), tm, D, D // tm, lambda j: (0, j), lambda j: (0, 0), lambda j: (0, j, 0), name, deps)
    return out.reshape(N_CHIPS, D // N_CHIPS, D)


def _dw_hidden(act, other, name, deps):
    tm, tn = FFN_HIDDEN // 2, (D if other.dtype == BF16 else 256)
    n_n = D // tn
    out = mm_tn(act[None], other[None], (1, FFN_HIDDEN, D), tm, tn, 2 * n_n,
                lambda j: (0, j // n_n), lambda j: (0, j % n_n), lambda j: (0, j // n_n, j % n_n), name, deps)
    return out.reshape(N_CHIPS, FFN_SHARD, D)


def kernel(x, norm_mix_g, norm_ffn_g, final_norm_g, a_w_in, a_v_norm_g, a_w_s, a_b_s, a_w_out, b_w_in, b_conv_w, b_w_out, c_w_in, c_w_grp, c_scale, c_w_out, f_w_gate, f_w_up, f_w_down, loss_target, m_norm_mix_g, m_norm_ffn_g, m_final_norm_g, m_a_w_in, m_a_v_norm_g, m_a_w_s, m_a_b_s, m_a_w_out, m_b_w_in, m_b_conv_w, m_b_w_out, m_c_w_in, m_c_w_grp, m_c_scale, m_c_w_out, m_f_w_gate, m_f_w_up, m_f_w_down, v_norm_mix_g, v_norm_ffn_g, v_final_norm_g, v_a_w_in, v_a_v_norm_g, v_a_w_s, v_a_b_s, v_a_w_out, v_b_w_in, v_b_conv_w, v_b_w_out, v_c_w_in, v_c_w_grp, v_c_scale, v_c_w_out, v_f_w_gate, v_f_w_up, v_f_w_down):
    n_ex, seq, _ = x.shape
    t = n_ex * seq
    xi, yi, ci = lax.axis_index("x"), lax.axis_index("y"), lax.axis_index("c")
    chip = (2 * xi + yi).astype(jnp.int32)
    core_arr = ci.astype(jnp.int32).reshape(1)
    chip_arr = chip.reshape(1)
    me_arr = (4 * xi + 2 * yi + ci).astype(jnp.int32).reshape(1)
    bf = lambda w: w.astype(BF16)

    pad8 = lambda v: jnp.pad(v, ((0, 8 - v.shape[0]), (0, 0)))
    small_rows = jnp.concatenate([pad8(a_v_norm_g), pad8(b_conv_w[0]), pad8(c_scale)], axis=0)
    small_gathered = all_gather_rows(small_rows, "ag_small")
    small_full = small_gathered.transpose(1, 0, 2).reshape(24, D)
    gv_full = [small_full[0:1], small_full[1:2]]
    cw_full = small_full[8:11]
    scale_full = small_full[16:17]

    mixer_shards = [
        [bf(a_w_in[0]), bf(a_w_out[0])],
        [bf(b_w_in[0]), bf(b_w_out[0])],
        [bf(c_w_in[0]), bf(c_w_grp[0]).reshape(C_GROUP_DIM, C_GROUP_DIM), bf(c_w_out[0])],
        [bf(a_w_in[1]), bf(a_w_out[1])],
    ]
    hidden_major = lambda w: jnp.swapaxes(w, 1, 2)
    gate_t, up_t = hidden_major(f_w_gate), hidden_major(f_w_up)
    gathered = []
    for i in range(4):
        ffn_shards = [bf(gate_t[i]), bf(up_t[i]), bf(f_w_down[i])]
        if i == 0:
            parts = [(mixer_shards[0], "ag_l0_mixer", [small_gathered]), (ffn_shards, "ag_l0_ffn", [])]
        else:
            parts = [(mixer_shards[i] + ffn_shards, f"ag_l{i}", [])]
        layer = []
        for shards, name, deps in parts:
            layer += place_own(all_gather_weights(shards, name, deps), shards, chip_arr, name.replace("ag", "own"))
        gathered.append(layer)

    mask = (jnp.arange(GMLP_BLOCK)[None, :] // 64) <= (jnp.arange(GMLP_BLOCK)[:, None] // 64)
    gmix = [norm_mix_g[i:i + 1] for i in range(4)]
    gffn = [norm_ffn_g[i:i + 1] for i in range(4)]
    a_chunks = [(s, 0, 512, 0, s * 512) for s in range(N_CHIPS)]
    b_chunks = [(j // 3, (j % 3) * 256, 256, j // 4, (j % 4) * 256) for j in range(12)]
    c_chunks = [(0, 0, D, 0, 0)]

    xs = [x.reshape(t, D)]
    saved = []
    for i in range(4):
        ws = gathered[i]
        wg, wu, wd = (w.reshape(FFN_HIDDEN, D) for w in ws[-3:])
        xin = xs[-1]
        if i in (0, 3):
            j = 0 if i == 0 else 1
            win, wout = ws[0], _blocked(ws[1])
            wm32 = jnp.where(mask[None], a_w_s[j], 0.0)
            wm, wmt = bf(wm32), bf(wm32.transpose(0, 2, 1))
            bs = jnp.broadcast_to(a_b_s[j][:, :, None], (A_GROUPS, GMLP_BLOCK, 128))
            h, z, vn, y, xmid = a_fwd(xin, gmix[i], win, gv_full[j], wm, bs, wout, f"a_fwd_l{i}")
            saved.append(dict(h=h, z=z, y=y, vn=vn, win=win, wout=wout, wm=wm, wmt=wmt, bs=bs, gv=gv_full[j]))
        elif i == 1:
            win, wout = ws[0], _blocked(ws[1])
            h, p3 = norm_mm(xin, gmix[i], win, b_chunks, 3, D, "b_in")
            y = b_conv_fwd(p3, cw_full, seq, "b_conv")
            xmid = mm_res(y, wout, xin, "b_out")
            saved.append(dict(h=h, p3=p3, y=y, win=win, wout=wout))
        else:
            win, wgrp, wout = _blocked(ws[0])[None], _grp_from_blocks(ws[1]), _blocked(ws[2])
            h, p = norm_mm(xin, gmix[i], win, c_chunks, 1, D, "c_in")
            dpool = c_pool_fwd(p[0], seq, "c_pool")
            y, xmid = c_out_fwd(dpool, wgrp, scale_full, wout, xin, "c_out")
            saved.append(dict(h=h, d=dpool, y=y, win=win, wgrp=wgrp, wout=wout))
        h2, fa, fb, fs, xout = ffn_fwd(xmid, gffn[i], wg, wu, wd, f"ffn_l{i}")
        saved[-1].update(h2=h2, fa=fa, fb=fb, fs=fs, xmid=xmid, wg=wg, wu=wu, wd=wd)
        xs.append(xout)

    dx, dg_final, loss_part = loss_head(xs[4], loss_target.reshape(t, D), final_norm_g[None], "loss_head")
    loss = lax.psum(loss_part[0, 0], ("x", "y", "c"))

    weights = dict(norm_mix_g=norm_mix_g, norm_ffn_g=norm_ffn_g, final_norm_g=final_norm_g, a_w_in=a_w_in, a_v_norm_g=a_v_norm_g,
                   a_w_s=a_w_s, a_b_s=a_b_s, a_w_out=a_w_out, b_w_in=b_w_in, b_conv_w=b_conv_w, b_w_out=b_w_out, c_w_in=c_w_in,
                   c_w_grp=c_w_grp, c_scale=c_scale, c_w_out=c_w_out, f_w_gate=f_w_gate, f_w_up=f_w_up, f_w_down=f_w_down)
    m_in = dict(norm_mix_g=m_norm_mix_g, norm_ffn_g=m_norm_ffn_g, final_norm_g=m_final_norm_g, a_w_in=m_a_w_in, a_v_norm_g=m_a_v_norm_g,
                a_w_s=m_a_w_s, a_b_s=m_a_b_s, a_w_out=m_a_w_out, b_w_in=m_b_w_in, b_conv_w=m_b_conv_w, b_w_out=m_b_w_out, c_w_in=m_c_w_in,
                c_w_grp=m_c_w_grp, c_scale=m_c_scale, c_w_out=m_c_w_out, f_w_gate=m_f_w_gate, f_w_up=m_f_w_up, f_w_down=m_f_w_down)
    v_in = dict(norm_mix_g=v_norm_mix_g, norm_ffn_g=v_norm_ffn_g, final_norm_g=v_final_norm_g, a_w_in=v_a_w_in, a_v_norm_g=v_a_v_norm_g,
                a_w_s=v_a_w_s, a_b_s=v_a_b_s, a_w_out=v_a_w_out, b_w_in=v_b_w_in, b_conv_w=v_b_conv_w, b_w_out=v_b_w_out, c_w_in=v_c_w_in,
                c_w_grp=v_c_w_grp, c_scale=v_c_scale, c_w_out=v_c_w_out, f_w_gate=v_f_w_gate, f_w_up=v_f_w_up, f_w_down=v_f_w_down)
    grp_rows = lambda a: a.reshape(1, C_GROUP_DIM, C_GROUP_DIM)
    same = lambda a: a
    to_stacked = {nme: same for nme in ("a_w_in", "a_w_out", "b_w_in", "b_w_out", "c_w_in", "c_w_out", "f_w_down")}
    to_stacked.update(f_w_gate=hidden_major, f_w_up=hidden_major, c_w_grp=grp_rows)
    from_stacked = dict(to_stacked, c_w_grp=lambda a: a.reshape(c_w_grp.shape))
    layer_tensors = {0: ["a_w_out", "a_w_in"], 1: ["b_w_out", "b_w_in"], 2: ["c_w_out", "c_w_in", "c_w_grp"], 3: ["a_w_out", "a_w_in"]}
    carried = {}

    def bs_rows(v):
        return jnp.pad(v[:, :, 0].reshape(1, D), ((0, 7), (0, 0)))

    def update(unit):
        done = []
        for pos, nme in enumerate(unit.names):
            stacked_layer = unit.layer if nme.startswith("f_") else (unit.layer // 3 if nme.startswith("a_") else 0)
            view = to_stacked[nme]
            carried[nme] = adamw_layer(view(weights[nme]), view(m_in[nme]), view(v_in[nme]), unit.half_sums[pos], unit.other_half[pos],
                                       core_arr, stacked_layer, carried.get(nme), f"adamw_{nme}_l{unit.layer}")
            done.append(carried[nme][0])
        return done

    ffn_names = ["f_w_gate", "f_w_up", "f_w_down"]
    dg_mix, dg_ffn = [None] * 4, [None] * 4
    small = {}
    newer = older = None
    for i in (3, 2, 1, 0):
        sv = saved[i]
        xin = xs[i]
        deps = ([newer.grads[0]] if newer else []) + ([older.half_sums[0]] if older else [])
        dxm, da, db, dg_ffn[i] = ffn_bwd(dx, sv["fa"], sv["fb"], sv["xmid"], gffn[i], sv["wg"], sv["wu"], sv["wd"], f"ffn_bwd_l{i}", deps)
        last = [newer.sum_cores([dxm])] if newer else []
        g_gate = _dw_hidden(da, sv["h2"], f"dw_gate_l{i}", last)
        g_up = _dw_hidden(db, sv["h2"], f"dw_up_l{i}", [g_gate])
        g_down = _dw_hidden(sv["fs"], dx, f"dw_down_l{i}", [g_up])
        last_ffn = [g_down]
        if i == 0:
            ffn_unit = ReduceScatter([g_down, g_gate, g_up], ["f_w_down", "f_w_gate", "f_w_up"], 0, core_arr, chip_arr, "rs_l0_ffn")
        if i in (0, 3):
            j = 0 if i == 0 else 1
            dx, dz, dwm, dbs, dgv, dg_mix[i] = a_bwd(dxm, xin, gmix[i], sv["z"], sv["vn"], sv["gv"], sv["wm"], sv["wmt"], sv["bs"],
                                                     sv["wout"], sv["win"], f"a_bwd_l{i}")
            if i == 0:
                early = sum_devices(early_rows, early_gathered, me_arr, "sum_small_grads_l123")
                newer.sum_chips([dz] + update(older) + [early])
                last_ffn.append(ffn_unit.sum_cores([newer.half_sums[0]]))
            dz = dz[None]
            g_in = _dw_cols(sv["h"], dz, 512, f"dw_a_in_l{i}", last_ffn)
            g_out = _dw_rows(sv["y"], dxm, f"dw_a_out_l{i}", [g_in])
            small[f"wm{j}"], small[f"bs{j}"], small[f"gv{j}"] = dwm, dbs, dgv
            mixer_grads = [g_out, g_in]
        elif i == 1:
            dy = mm_nt(dxm, sv["wout"], "b_dy")
            dp3, small["cw"] = b_conv_bwd(dy, sv["p3"], cw_full, seq, "b_conv_bwd")
            dx, dg_mix[i] = bwd_in(dp3, sv["win"], b_chunks, xin, gmix[i], dxm, "b_bwd_in")
            g_in = mm_tn(sv["h"][None], dp3, (N_CHIPS, D, 768), 512, 256, 24,
                         lambda j: (0, j % 2), lambda j: ((j // 2) // 4, (j // 2) % 4),
                         lambda j: ((j // 2) // 3, j % 2, (j // 2) % 3), "dw_b_in", [g_down])
            g_out = _dw_rows(sv["y"], dxm, "dw_b_out", [g_in])
            mixer_grads = [g_out, g_in]
        else:
            outs = c_out_bwd(dxm, sv["d"], sv["wgrp"], scale_full, sv["wout"], "c_out_bwd")
            dyp, dd, small["scale"] = outs[0], list(outs[1:5]), outs[5]
            dpool = c_pool_bwd(dd, seq, "c_pool_bwd")
            dp = jnp.concatenate(dpool, axis=1)[None]
            dx, dg_mix[i] = bwd_in(dp, sv["win"], c_chunks, xin, gmix[i], dxm, "c_bwd_in")
            g_in = _dw_rows(sv["h"], dp[0], "dw_c_in", [g_down])
            dcat = jnp.concatenate(sv["d"], axis=1)
            g_grp = mm_tn(dcat[None], dyp[None], (4, C_GROUP_DIM, C_GROUP_DIM), C_GROUP_DIM, C_GROUP_DIM, 4,
                          lambda j: (0, j), lambda j: (0, j), lambda j: (j, 0, 0), "dw_c_grp", [g_in])
            g_out = _dw_rows(sv["y"], dxm, "dw_c_out", [g_grp])
            mixer_grads = [g_out, g_in, _grp_to_blocks(g_grp)]
        if i > 0:
            unit = ReduceScatter(mixer_grads + [g_gate, g_up, g_down], layer_tensors[i] + ffn_names, i, core_arr, chip_arr, f"rs_l{i}")
        else:
            unit = ReduceScatter(mixer_grads, layer_tensors[0], 0, core_arr, chip_arr, "rs_l0_mixer")
        if newer and i > 0:
            newer.sum_chips([mixer_grads[0]] + (update(older) if older else []))
        if i == 1:
            early_rows = jnp.concatenate(dg_mix[1:] + dg_ffn[1:] + [dg_final, small["gv1"], small["cw"], small["scale"],
                                                                   small["wm1"].reshape(128, D), bs_rows(small["bs1"])], axis=0)
            early_gathered = all_gather_devices(early_rows, "ag_small_grads_l123")
        older, newer = newer, unit
    grad_x = dx.reshape(n_ex, seq, D)
    mixer_unit = newer
    ffn_unit.sum_chips(update(older) + [mixer_unit.grads[0]])
    mixer_unit.sum_cores([ffn_unit.half_sums[0]])
    mixer_unit.sum_chips(update(ffn_unit))
    updated = update(mixer_unit)

    late_rows = jnp.concatenate([dg_mix[0], dg_ffn[0], small["gv0"], small["wm0"].reshape(128, D), bs_rows(small["bs0"])], axis=0)
    late = sum_devices(late_rows, all_gather_devices(late_rows, "ag_small_grads_l0"), me_arr, "sum_small_grads_l0", updated)
    first_rows = lambda a, b, n: jnp.concatenate([a, b], axis=0).reshape(n, 8, D)[:, 0]
    g_norm_mix = first_rows(late[0:8], early[0:24], 4)
    g_norm_ffn = first_rows(late[8:16], early[24:48], 4)
    g_final = early[48]
    g_gv = first_rows(late[16:24], early[56:64], 2)
    g_cw = early[64:67]
    g_scale = early[72:73]
    g_ws = jnp.where(mask[None, None], jnp.concatenate([late[24:152], early[80:208]], axis=0).reshape(2, A_GROUPS, 128, 128), 0.0)
    g_bs = first_rows(late[152:160], early[208:216], 2).reshape(2, A_GROUPS, 128)
    col0 = chip * (D // N_CHIPS)
    cols = lambda v: lax.dynamic_slice_in_dim(v, col0, D // N_CHIPS, axis=1)

    small_grads = {
        "norm_mix_g": g_norm_mix, "norm_ffn_g": g_norm_ffn, "final_norm_g": g_final, "a_v_norm_g": cols(g_gv), "a_w_s": g_ws,
        "a_b_s": g_bs, "b_conv_w": cols(g_cw)[None], "c_scale": cols(g_scale),
    }
    results = {}
    for nme, g in small_grads.items():
        w = weights[nme]
        flat = lambda a: a.reshape(-1, w.shape[-1])
        dl, mn, vn = adamw(flat(w), flat(g), flat(m_in[nme]), flat(v_in[nme]), f"adamw_{nme}")
        results[nme] = tuple(o.reshape(w.shape) for o in (g, dl, mn, vn))
    for nme, outs in carried.items():
        results[nme] = tuple(from_stacked[nme](o) for o in outs)

    names = list(weights)
    return (loss, grad_x, *[results[n][0] for n in names], *[results[n][1] for n in names],
            *[results[n][2] for n in names], *[results[n][3] for n in names])
```

```python
import jax
import jax.numpy as jnp
from jax import lax
from jax.experimental import pallas as pl
from jax.experimental.pallas import tpu as pltpu
from jax.experimental.pallas import tpu_sc as plsc

F32 = jnp.float32
BF16 = jnp.bfloat16
D = 1024
FFN_SHARD = 704
GMLP_BLOCK = 128
A_GROUPS = 8
POOL_WINDOWS = (2, 4, 8, 16)
C_GROUP_DIM = 256
N_CHIPS = 4
EPS = 1e-6
ADAM_LR, ADAM_B1, ADAM_B2, ADAM_EPS, ADAM_WD, ADAM_STEP = 0.001, 0.9, 0.999, 1e-08, 0.01, 10
VMEM_LIMIT_BYTES = 56 * 1024 * 1024
FFN_HIDDEN = N_CHIPS * FFN_SHARD
FFN_CHUNKS = ((0, 768), (768, 768), (1536, 768), (2304, 512))
FFN_FWD_ROWS = 512
FFN_BWD_ROWS = 256
A_ROWS = 256
MESH = pl.DeviceIdType.MESH
GATHER_COLLECTIVE_ID = 1
SIBLING_COLLECTIVE_ID = 2
CHIPS_COLLECTIVE_ID = 3
ALL_COLLECTIVE_ID = 4
ANY = pl.BlockSpec(memory_space=pl.ANY)
NT_DIMS = (((1,), (1,)), ((), ()))
TN_DIMS = (((0,), (0,)), ((), ()))
INV_SQRT2 = 0.7071067811865476
INV_SQRT_2PI = 0.3989422804014327


def _params(*semantics):
    return pltpu.CompilerParams(dimension_semantics=semantics, vmem_limit_bytes=VMEM_LIMIT_BYTES)


def _dot(a, b):
    return jnp.dot(a, b, preferred_element_type=F32)


def _dot_nt(a, b):
    return lax.dot_general(a, b, NT_DIMS, preferred_element_type=F32)


def _rms(x):
    r = lax.rsqrt(jnp.mean(x * x, axis=-1, keepdims=True) + EPS)
    return x * r, r


def _rms_bwd(x, g, dh):
    xh, r = _rms(x)
    dxh = dh * g
    dx = r * (dxh - xh * jnp.mean(dxh * xh, axis=-1, keepdims=True))
    return dx, jnp.sum(dh * xh, axis=0, keepdims=True)


def _gelu(x):
    return 0.5 * x * (1.0 + lax.erf(x * INV_SQRT2))


def _gelu_grad(x):
    return 0.5 * (1.0 + lax.erf(x * INV_SQRT2)) + x * jnp.exp(-0.5 * x * x) * INV_SQRT_2PI


def _shift_down(v, s, row):
    return jnp.where(row >= s, pltpu.roll(v, s, 0), 0.0)


def _shift_up(v, s, row):
    n = v.shape[0]
    return jnp.where(row < n - s, pltpu.roll(v, n - s, 0), 0.0)


def _row_tile(t, want):
    return want if t % want == 0 else t


def _after(body, first, deps):
    if not deps:
        return body

    def ordered(*refs):
        return body(*refs[:first], *refs[first + len(deps):])

    return ordered


def norm_mm(x, g, w, chunks, n_parts, part_width, name):
    t = x.shape[0]
    tm = _row_tile(t, 512)
    n_shards, _, hs = w.shape

    def body(x_ref, g_ref, w_ref, h_ref, p_ref):
        xh, _ = _rms(x_ref[...])
        h = (xh * g_ref[...]).astype(BF16)
        h_ref[...] = h
        for s in range(n_shards):
            res = _dot(h, w_ref[s]).astype(BF16)
            for (cs, wc, width, part, pc) in chunks:
                if cs == s:
                    p_ref[part, :, pc:pc + width] = res[:, wc:wc + width]

    return pl.pallas_call(
        body, name=name, grid=(t // tm,),
        in_specs=[pl.BlockSpec((tm, D), lambda i: (i, 0)), pl.BlockSpec((1, D), lambda i: (0, 0)),
                  pl.BlockSpec((n_shards, D, hs), lambda i: (0, 0, 0))],
        out_specs=[pl.BlockSpec((tm, D), lambda i: (i, 0)), pl.BlockSpec((n_parts, tm, part_width), lambda i: (0, i, 0))],
        out_shape=[jax.ShapeDtypeStruct((t, D), BF16), jax.ShapeDtypeStruct((n_parts, t, part_width), BF16)],
        compiler_params=_params("arbitrary"),
    )(x, g, w)


def mm_res(a, w, res, name):
    t, k = a.shape
    n = w.shape[1]
    tm = _row_tile(t, 512)

    def body(a_ref, w_ref, r_ref, o_ref):
        o_ref[...] = r_ref[...] + _dot(a_ref[...], w_ref[...])

    return pl.pallas_call(
        body, name=name, grid=(t // tm,),
        in_specs=[pl.BlockSpec((tm, k), lambda i: (i, 0)), pl.BlockSpec((k, n), lambda i: (0, 0)),
                  pl.BlockSpec((tm, n), lambda i: (i, 0))],
        out_specs=pl.BlockSpec((tm, n), lambda i: (i, 0)),
        out_shape=jax.ShapeDtypeStruct((t, n), F32),
        compiler_params=_params("arbitrary"),
    )(a, w, res)


def mm_nt(a, w, name):
    t, n = a.shape
    k = w.shape[0]
    tm = _row_tile(t, 512)

    def body(a_ref, w_ref, o_ref):
        o_ref[...] = _dot_nt(a_ref[...].astype(BF16), w_ref[...]).astype(BF16)

    return pl.pallas_call(
        body, name=name, grid=(t // tm,),
        in_specs=[pl.BlockSpec((tm, n), lambda i: (i, 0)), pl.BlockSpec((k, n), lambda i: (0, 0))],
        out_specs=pl.BlockSpec((tm, k), lambda i: (i, 0)),
        out_shape=jax.ShapeDtypeStruct((t, k), BF16),
        compiler_params=_params("arbitrary"),
    )(a, w)


def bwd_in(dp, w, chunks, x, g, dres, name):
    n_parts, t, part_width = dp.shape
    n_shards, _, hs = w.shape
    tm = _row_tile(t, 512)

    def body(dp_ref, w_ref, x_ref, g_ref, dres_ref, dx_ref, dg_ref):
        acc = jnp.zeros((tm, D), F32)
        for (cs, wc, width, part, pc) in chunks:
            acc = acc + _dot_nt(dp_ref[part, :, pc:pc + width], w_ref[cs, :, wc:wc + width])
        dx, dg = _rms_bwd(x_ref[...], g_ref[...], acc)
        dx_ref[...] = dres_ref[...] + dx

        @pl.when(pl.program_id(0) == 0)
        def _():
            dg_ref[...] = jnp.zeros_like(dg_ref)

        dg_ref[0:1, :] += dg

    return pl.pallas_call(
        body, name=name, grid=(t // tm,),
        in_specs=[pl.BlockSpec((n_parts, tm, part_width), lambda i: (0, i, 0)),
                  pl.BlockSpec((n_shards, D, hs), lambda i: (0, 0, 0)),
                  pl.BlockSpec((tm, D), lambda i: (i, 0)), pl.BlockSpec((1, D), lambda i: (0, 0)),
                  pl.BlockSpec((tm, D), lambda i: (i, 0))],
        out_specs=[pl.BlockSpec((tm, D), lambda i: (i, 0)), pl.BlockSpec((8, D), lambda i: (0, 0))],
        out_shape=[jax.ShapeDtypeStruct((t, D), F32), jax.ShapeDtypeStruct((8, D), F32)],
        compiler_params=_params("arbitrary"),
    )(dp, w, x, g, dres)


def mm_tn(a, b, out_shape, tm, tn, n_tiles, a_idx, b_idx, o_idx, name, deps=()):
    t = a.shape[1]

    def body(a_ref, b_ref, o_ref):
        o_ref[0] = lax.dot_general(a_ref[0].astype(BF16), b_ref[0].astype(BF16), TN_DIMS, preferred_element_type=F32).astype(BF16)

    return pl.pallas_call(
        _after(body, 2, deps), name=name, grid=(n_tiles,),
        in_specs=[pl.BlockSpec((1, t, tm), lambda j: (a_idx(j)[0], 0, a_idx(j)[1])),
                  pl.BlockSpec((1, t, tn), lambda j: (b_idx(j)[0], 0, b_idx(j)[1]))] + [ANY] * len(deps),
        out_specs=pl.BlockSpec((1, tm, tn), lambda j: o_idx(j)),
        out_shape=jax.ShapeDtypeStruct(out_shape, BF16),
        compiler_params=_params("arbitrary"),
    )(a, b, *deps)


def ffn_fwd(x, g, wg, wu, wd, name):
    t = x.shape[0]
    tm = _row_tile(t, FFN_FWD_ROWS)
    hidden = wg.shape[0]

    def body(x_ref, g_ref, wg_ref, wu_ref, wd_ref, h_ref, a_ref, b_ref, s_ref, o_ref):
        xv = x_ref[...]
        xh, _ = _rms(xv)
        h = (xh * g_ref[...]).astype(BF16)
        h_ref[...] = h
        acc = xv
        for c0, cw in FFN_CHUNKS:
            cols = slice(c0, c0 + cw)
            a = _dot_nt(h, wg_ref[cols, :])
            b = _dot_nt(h, wu_ref[cols, :])
            sig = jax.nn.sigmoid(a)
            silu = a * sig
            s = (silu * b).astype(BF16)
            a_ref[:, cols] = (b * (sig * (1.0 + a * (1.0 - sig)))).astype(BF16)
            b_ref[:, cols] = silu.astype(BF16)
            s_ref[:, cols] = s
            acc = acc + _dot(s, wd_ref[cols, :])
        o_ref[...] = acc

    act = pl.BlockSpec((tm, hidden), lambda i: (i, 0))
    act_shape = jax.ShapeDtypeStruct((t, hidden), BF16)
    wspec = pl.BlockSpec((hidden, D), lambda i: (0, 0), pipeline_mode=pl.Buffered(1))
    row = pl.BlockSpec((tm, D), lambda i: (i, 0))
    return pl.pallas_call(
        body, name=name, grid=(t // tm,),
        in_specs=[row, pl.BlockSpec((1, D), lambda i: (0, 0)), wspec, wspec, wspec],
        out_specs=[row, act, act, act, row],
        out_shape=[jax.ShapeDtypeStruct((t, D), BF16), act_shape, act_shape, act_shape, jax.ShapeDtypeStruct((t, D), F32)],
        compiler_params=_params("arbitrary"),
    )(x, g, wg, wu, wd)


def ffn_bwd(dxo, a, b, x, g, wg, wu, wd, name, deps=()):
    t = x.shape[0]
    tm = _row_tile(t, FFN_BWD_ROWS)
    hidden = wg.shape[0]

    def body(dxo_ref, a_ref, b_ref, x_ref, g_ref, wg_ref, wu_ref, wd_ref, dx_ref, da_ref, db_ref, dg_ref):
        @pl.when(pl.program_id(0) == 0)
        def _():
            dg_ref[...] = jnp.zeros_like(dg_ref)

        dxo = dxo_ref[...]
        dxb = dxo.astype(BF16)
        acc = None
        for c0, cw in FFN_CHUNKS:
            cols = slice(c0, c0 + cw)
            ds = _dot_nt(dxb, wd_ref[cols, :])
            da = (ds * a_ref[:, cols].astype(F32)).astype(BF16)
            db = (ds * b_ref[:, cols].astype(F32)).astype(BF16)
            da_ref[:, cols] = da
            db_ref[:, cols] = db
            part = _dot(da, wg_ref[cols, :]) + _dot(db, wu_ref[cols, :])
            acc = part if acc is None else acc + part
        dx, dg = _rms_bwd(x_ref[...], g_ref[...], acc)
        dx_ref[...] = dxo + dx
        dg_ref[0:1, :] += dg

    act = pl.BlockSpec((tm, hidden), lambda i: (i, 0))
    act_shape = jax.ShapeDtypeStruct((t, hidden), BF16)
    row = pl.BlockSpec((tm, D), lambda i: (i, 0))
    wspec = pl.BlockSpec((hidden, D), lambda i: (0, 0), pipeline_mode=pl.Buffered(1))
    return pl.pallas_call(
        _after(body, 8, deps), name=name, grid=(t // tm,),
        in_specs=[row, act, act, row, pl.BlockSpec((1, D), lambda i: (0, 0)), wspec, wspec, wspec] + [ANY] * len(deps),
        out_specs=[row, act, act, pl.BlockSpec((8, D), lambda i: (0, 0))],
        out_shape=[jax.ShapeDtypeStruct((t, D), F32), act_shape, act_shape, jax.ShapeDtypeStruct((8, D), F32)],
        compiler_params=_params("arbitrary"),
    )(dxo, a, b, x, g, wg, wu, wd, *deps)


def _layer_norm_stats(v):
    mu = jnp.mean(v, axis=-1, keepdims=True)
    vc = v - mu
    rstd = lax.rsqrt(jnp.mean(vc * vc, axis=-1, keepdims=True) + EPS)
    return vc * rstd, rstd


def a_fwd(x, g, win, gv, wm, bs, wout, name):
    t = x.shape[0]
    tm = _row_tile(t, A_ROWS)
    n_shards, _, hs = win.shape

    def body(x_ref, g_ref, win_ref, gv_ref, wm_ref, bs_ref, wout_ref, h_ref, z_ref, vn_ref, y_ref, o_ref):
        xv = x_ref[...]
        xh, _ = _rms(xv)
        h = (xh * g_ref[...]).astype(BF16)
        h_ref[...] = h
        zs = []
        for s in range(n_shards):
            zb = _dot(h, win_ref[s]).astype(BF16)
            z_ref[:, s * hs:(s + 1) * hs] = zb
            zs.append(zb.astype(F32))
        half = n_shards // 2
        u = _gelu(jnp.concatenate(zs[:half], axis=1))
        vhat, _ = _layer_norm_stats(_gelu(jnp.concatenate(zs[half:], axis=1)))
        vnb = (vhat * gv_ref[...]).astype(BF16)
        vn_ref[...] = vnb
        for n in range(tm // GMLP_BLOCK):
            rows = slice(n * GMLP_BLOCK, (n + 1) * GMLP_BLOCK)
            for grp in range(A_GROUPS):
                cols = slice(grp * 128, (grp + 1) * 128)
                sv = _dot(wm_ref[grp], vnb[rows, cols]) + bs_ref[grp]
                y_ref[rows, cols] = (u[rows, cols] * sv).astype(BF16)
        o_ref[...] = xv + _dot(y_ref[...], wout_ref[...])

    small = pl.BlockSpec((A_GROUPS, 128, 128), lambda i: (0, 0, 0))
    row = pl.BlockSpec((tm, D), lambda i: (i, 0))
    wide = pl.BlockSpec((tm, 2 * D), lambda i: (i, 0))
    gain = pl.BlockSpec((1, D), lambda i: (0, 0))
    return pl.pallas_call(
        body, name=name, grid=(t // tm,),
        in_specs=[row, gain, pl.BlockSpec((n_shards, D, hs), lambda i: (0, 0, 0)), gain, small, small,
                  pl.BlockSpec((D, D), lambda i: (0, 0))],
        out_specs=[row, wide, row, row, row],
        out_shape=[jax.ShapeDtypeStruct((t, D), BF16), jax.ShapeDtypeStruct((t, 2 * D), BF16), jax.ShapeDtypeStruct((t, D), BF16),
                   jax.ShapeDtypeStruct((t, D), BF16), jax.ShapeDtypeStruct((t, D), F32)],
        compiler_params=_params("arbitrary"),
    )(x, g, win, gv, wm, bs, wout)


def a_bwd(dxm, x, g, z, vn, gv, wm, wmt, bs, wout, win, name):
    t = x.shape[0]
    tm = _row_tile(t, A_ROWS)
    n_shards, _, hs = win.shape

    def body(dxm_ref, x_ref, g_ref, z_ref, vn_ref, gv_ref, wm_ref, wmt_ref, bs_ref, wout_ref, win_ref,
             dx_ref, dz_ref, dwm_ref, dbs_ref, dgv_ref, dg_ref, du_ref, dvn_ref):
        @pl.when(pl.program_id(0) == 0)
        def _():
            dwm_ref[...] = jnp.zeros_like(dwm_ref)
            dbs_ref[...] = jnp.zeros_like(dbs_ref)
            dgv_ref[...] = jnp.zeros_like(dgv_ref)
            dg_ref[...] = jnp.zeros_like(dg_ref)

        dxm = dxm_ref[...]
        dyv = _dot_nt(dxm.astype(BF16), wout_ref[...])
        zz = z_ref[...].astype(F32)
        zu, zv = zz[:, :D], zz[:, D:]
        u = _gelu(zu)
        vhat, rstd = _layer_norm_stats(_gelu(zv))
        vnb = vn_ref[...]
        ones = jnp.ones((128, 128), BF16)
        for n in range(tm // GMLP_BLOCK):
            rows = slice(n * GMLP_BLOCK, (n + 1) * GMLP_BLOCK)
            for grp in range(A_GROUPS):
                cols = slice(grp * 128, (grp + 1) * 128)
                blk = vnb[rows, cols]
                sv = _dot(wm_ref[grp], blk) + bs_ref[grp]
                dyb = dyv[rows, cols]
                du_ref[rows, cols] = dyb * sv
                dsv = (dyb * u[rows, cols]).astype(BF16)
                dvn_ref[rows, cols] = _dot(wmt_ref[grp], dsv)
                dwm_ref[grp] += _dot_nt(dsv, blk)
                dbs_ref[grp] += _dot(dsv, ones)
        dvn = dvn_ref[...]
        dgv_ref[0:1, :] += jnp.sum(dvn * vhat, axis=0, keepdims=True)
        dvh = dvn * gv_ref[...]
        dv = rstd * (dvh - jnp.mean(dvh, axis=-1, keepdims=True) - vhat * jnp.mean(dvh * vhat, axis=-1, keepdims=True))
        dz_ref[:, :D] = (du_ref[...] * _gelu_grad(zu)).astype(BF16)
        dz_ref[:, D:] = (dv * _gelu_grad(zv)).astype(BF16)
        dh = None
        for s in range(n_shards):
            part = _dot_nt(dz_ref[:, s * hs:(s + 1) * hs], win_ref[s])
            dh = part if dh is None else dh + part
        dx, dg = _rms_bwd(x_ref[...], g_ref[...], dh)
        dx_ref[...] = dxm + dx
        dg_ref[0:1, :] += dg

    small = pl.BlockSpec((A_GROUPS, 128, 128), lambda i: (0, 0, 0))
    row = pl.BlockSpec((tm, D), lambda i: (i, 0))
    wide = pl.BlockSpec((tm, 2 * D), lambda i: (i, 0))
    gain = pl.BlockSpec((1, D), lambda i: (0, 0))
    sums = pl.BlockSpec((8, D), lambda i: (0, 0))
    small_shape = jax.ShapeDtypeStruct((A_GROUPS, 128, 128), F32)
    sums_shape = jax.ShapeDtypeStruct((8, D), F32)
    return pl.pallas_call(
        body, name=name, grid=(t // tm,),
        in_specs=[row, row, gain, wide, row, gain, small, small, small, pl.BlockSpec((D, D), lambda i: (0, 0)),
                  pl.BlockSpec((n_shards, D, hs), lambda i: (0, 0, 0))],
        out_specs=[row, wide, small, small, sums, sums],
        out_shape=[jax.ShapeDtypeStruct((t, D), F32), jax.ShapeDtypeStruct((t, 2 * D), BF16), small_shape, small_shape,
                   sums_shape, sums_shape],
        scratch_shapes=[pltpu.VMEM((tm, D), F32), pltpu.VMEM((tm, D), F32)],
        compiler_params=_params("arbitrary"),
    )(dxm, x, g, z, vn, gv, wm, wmt, bs, wout, win)


def _conv_terms(p_ref, row):
    gb = p_ref[0].astype(F32)
    gc = p_ref[1].astype(F32)
    xt = p_ref[2].astype(F32)
    q = gc * xt
    return gb, gc, xt, q, _shift_down(q, 1, row), _shift_down(q, 2, row)


def b_conv_fwd(p3, cw, seq, name):
    t = p3.shape[1]
    cb = 256

    def body(p_ref, cw_ref, y_ref):
        row = lax.broadcasted_iota(jnp.int32, (seq, cb), 0)
        gb, _, _, q, q1, q2 = _conv_terms(p_ref, row)
        y_ref[...] = (gb * (cw_ref[2:3, :] * q + cw_ref[1:2, :] * q1 + cw_ref[0:1, :] * q2)).astype(BF16)

    return pl.pallas_call(
        body, name=name, grid=(t // seq, D // cb),
        in_specs=[pl.BlockSpec((3, seq, cb), lambda e, c: (0, e, c)), pl.BlockSpec((3, cb), lambda e, c: (0, c))],
        out_specs=pl.BlockSpec((seq, cb), lambda e, c: (e, c)),
        out_shape=jax.ShapeDtypeStruct((t, D), BF16),
        compiler_params=_params("arbitrary", "arbitrary"),
    )(p3, cw)


def b_conv_bwd(dy, p3, cw, seq, name):
    t = p3.shape[1]
    cb = 256

    def body(dy_ref, p_ref, cw_ref, dp_ref, dcw_ref):
        @pl.when(pl.program_id(1) == 0)
        def _():
            dcw_ref[...] = jnp.zeros_like(dcw_ref)

        row = lax.broadcasted_iota(jnp.int32, (seq, cb), 0)
        gb, gc, xt, q, q1, q2 = _conv_terms(p_ref, row)
        dyv = dy_ref[...].astype(F32)
        conv = cw_ref[2:3, :] * q + cw_ref[1:2, :] * q1 + cw_ref[0:1, :] * q2
        dyc = dyv * gb
        dq = cw_ref[2:3, :] * dyc + cw_ref[1:2, :] * _shift_up(dyc, 1, row) + cw_ref[0:1, :] * _shift_up(dyc, 2, row)
        dp_ref[0] = (dyv * conv).astype(BF16)
        dp_ref[1] = (dq * xt).astype(BF16)
        dp_ref[2] = (dq * gc).astype(BF16)
        dcw_ref[0:1, :] += jnp.sum(dyc * q2, axis=0, keepdims=True)
        dcw_ref[1:2, :] += jnp.sum(dyc * q1, axis=0, keepdims=True)
        dcw_ref[2:3, :] += jnp.sum(dyc * q, axis=0, keepdims=True)

    return pl.pallas_call(
        body, name=name, grid=(D // cb, t // seq),
        in_specs=[pl.BlockSpec((seq, cb), lambda c, e: (e, c)), pl.BlockSpec((3, seq, cb), lambda c, e: (0, e, c)),
                  pl.BlockSpec((3, cb), lambda c, e: (0, c))],
        out_specs=[pl.BlockSpec((3, seq, cb), lambda c, e: (0, e, c)), pl.BlockSpec((8, cb), lambda c, e: (0, c))],
        out_shape=[jax.ShapeDtypeStruct((3, t, D), BF16), jax.ShapeDtypeStruct((8, D), F32)],
        compiler_params=_params("arbitrary", "arbitrary"),
    )(dy, p3, cw)


def c_pool_fwd(p, seq, name):
    t = p.shape[0]

    def make(grp):
        w = POOL_WINDOWS[grp]

        def body_g(p_ref, d_ref):
            row = lax.broadcasted_iota(jnp.int32, (seq, C_GROUP_DIM), 0)
            pv = p_ref[...].astype(F32)
            acc = pv
            sh = 1
            while sh < w:
                acc = acc + _shift_down(acc, sh, row)
                sh *= 2
            d_ref[...] = (acc / jnp.minimum(row + 1, w).astype(F32) - pv).astype(BF16)

        return body_g

    outs = []
    for grp in range(len(POOL_WINDOWS)):
        outs.append(pl.pallas_call(
            make(grp), name=f"{name}_g{grp}", grid=(t // seq,),
            in_specs=[pl.BlockSpec((seq, C_GROUP_DIM), lambda e, grp=grp: (e, grp))],
            out_specs=pl.BlockSpec((seq, C_GROUP_DIM), lambda e: (e, 0)),
            out_shape=jax.ShapeDtypeStruct((t, C_GROUP_DIM), BF16),
            compiler_params=_params("arbitrary"),
        )(p))
    return outs


def c_pool_bwd(dd, seq, name):
    t = dd[0].shape[0]

    def make(w):
        def body_g(dd_ref, dp_ref):
            row = lax.broadcasted_iota(jnp.int32, (seq, C_GROUP_DIM), 0)
            ddv = dd_ref[...]
            acc = ddv / jnp.minimum(row + 1, w).astype(F32)
            sh = 1
            while sh < w:
                acc = acc + _shift_up(acc, sh, row)
                sh *= 2
            dp_ref[...] = (acc - ddv).astype(BF16)

        return body_g

    outs = []
    for grp, w in enumerate(POOL_WINDOWS):
        outs.append(pl.pallas_call(
            make(w), name=f"{name}_g{grp}", grid=(t // seq,),
            in_specs=[pl.BlockSpec((seq, C_GROUP_DIM), lambda e: (e, 0))],
            out_specs=pl.BlockSpec((seq, C_GROUP_DIM), lambda e: (e, 0)),
            out_shape=jax.ShapeDtypeStruct((t, C_GROUP_DIM), BF16),
            compiler_params=_params("arbitrary"),
        )(dd[grp]))
    return outs


def c_out_fwd(d, wgrp, scale, wo, x, name):
    t = x.shape[0]
    tm = _row_tile(t, 512)
    n_g = len(POOL_WINDOWS)

    def body(d0, d1, d2, d3, wg_ref, sc_ref, wo_ref, x_ref, y_ref, o_ref):
        parts = [_dot(dr[...], wg_ref[grp]) for grp, dr in enumerate((d0, d1, d2, d3))]
        y = (jnp.concatenate(parts, axis=1) * sc_ref[...]).astype(BF16)
        y_ref[...] = y
        o_ref[...] = x_ref[...] + _dot(y, wo_ref[...])

    dspec = pl.BlockSpec((tm, C_GROUP_DIM), lambda i: (i, 0))
    row = pl.BlockSpec((tm, D), lambda i: (i, 0))
    return pl.pallas_call(
        body, name=name, grid=(t // tm,),
        in_specs=[dspec] * n_g + [pl.BlockSpec((n_g, C_GROUP_DIM, C_GROUP_DIM), lambda i: (0, 0, 0)),
                                  pl.BlockSpec((1, D), lambda i: (0, 0)), pl.BlockSpec((D, D), lambda i: (0, 0)), row],
        out_specs=[row, row],
        out_shape=[jax.ShapeDtypeStruct((t, D), BF16), jax.ShapeDtypeStruct((t, D), F32)],
        compiler_params=_params("arbitrary"),
    )(*d, wgrp, scale, wo, x)


def c_out_bwd(dxm, d, wgrp, scale, wo, name):
    t = dxm.shape[0]
    tm = _row_tile(t, 512)
    n_g = len(POOL_WINDOWS)

    def body(dxm_ref, d0, d1, d2, d3, wg_ref, sc_ref, wo_ref, dyp_ref, dd0, dd1, dd2, dd3, dsc_ref):
        @pl.when(pl.program_id(0) == 0)
        def _():
            dsc_ref[...] = jnp.zeros_like(dsc_ref)

        dyo = _dot_nt(dxm_ref[...].astype(BF16), wo_ref[...])
        ypre = jnp.concatenate([_dot(dr[...], wg_ref[grp]) for grp, dr in enumerate((d0, d1, d2, d3))], axis=1)
        dsc_ref[0:1, :] += jnp.sum(dyo * ypre, axis=0, keepdims=True)
        dyp = (dyo * sc_ref[...]).astype(BF16)
        dyp_ref[...] = dyp
        for grp, ddr in enumerate((dd0, dd1, dd2, dd3)):
            ddr[...] = _dot_nt(dyp[:, grp * C_GROUP_DIM:(grp + 1) * C_GROUP_DIM], wg_ref[grp])

    dspec = pl.BlockSpec((tm, C_GROUP_DIM), lambda i: (i, 0))
    row = pl.BlockSpec((tm, D), lambda i: (i, 0))
    dshape = jax.ShapeDtypeStruct((t, C_GROUP_DIM), F32)
    return pl.pallas_call(
        body, name=name, grid=(t // tm,),
        in_specs=[row] + [dspec] * n_g + [pl.BlockSpec((n_g, C_GROUP_DIM, C_GROUP_DIM), lambda i: (0, 0, 0)),
                                          pl.BlockSpec((1, D), lambda i: (0, 0)), pl.BlockSpec((D, D), lambda i: (0, 0))],
        out_specs=[row] + [dspec] * n_g + [pl.BlockSpec((8, D), lambda i: (0, 0))],
        out_shape=[jax.ShapeDtypeStruct((t, D), BF16)] + [dshape] * n_g + [jax.ShapeDtypeStruct((8, D), F32)],
        compiler_params=_params("arbitrary"),
    )(dxm, *d, wgrp, scale, wo)


def loss_head(x, tgt, g, name):
    t = x.shape[0]
    tm = _row_tile(t, 512)

    def body(x_ref, t_ref, g_ref, dx_ref, dg_ref, loss_ref):
        @pl.when(pl.program_id(0) == 0)
        def _():
            dg_ref[...] = jnp.zeros_like(dg_ref)
            loss_ref[...] = jnp.zeros_like(loss_ref)

        xv, gvv = x_ref[...], g_ref[...]
        xh, _ = _rms(xv)
        diff = xh * gvv - t_ref[...]
        loss_ref[...] += 0.5 * jnp.sum(jnp.mean(diff * diff, axis=-1, keepdims=True))
        dx, dg = _rms_bwd(xv, gvv, diff * (1.0 / D))
        dx_ref[...] = dx
        dg_ref[0:1, :] += dg

    row = pl.BlockSpec((tm, D), lambda i: (i, 0))
    return pl.pallas_call(
        body, name=name, grid=(t // tm,),
        in_specs=[row, row, pl.BlockSpec((1, D), lambda i: (0, 0))],
        out_specs=[row, pl.BlockSpec((8, D), lambda i: (0, 0)), pl.BlockSpec((8, 128), lambda i: (0, 0))],
        out_shape=[jax.ShapeDtypeStruct((t, D), F32), jax.ShapeDtypeStruct((8, D), F32), jax.ShapeDtypeStruct((8, 128), F32)],
        compiler_params=_params("arbitrary"),
    )(x, tgt, g)


def adamw(w, g, m, v, name):
    rows, cols = w.shape
    tr = rows
    for cand in (512, 256, 128, 64, 32, 16, 8):
        if rows % cand == 0 and rows > cand:
            tr = cand
            break

    def body(w_ref, g_ref, m_ref, v_ref, d_ref, mo_ref, vo_ref):
        gv = g_ref[...]
        mn = ADAM_B1 * m_ref[...] + (1.0 - ADAM_B1) * gv
        vn = ADAM_B2 * v_ref[...] + (1.0 - ADAM_B2) * (gv * gv)
        m_hat = mn / (1.0 - ADAM_B1 ** ADAM_STEP)
        v_hat = vn / (1.0 - ADAM_B2 ** ADAM_STEP)
        d_ref[...] = -ADAM_LR * (m_hat / (jnp.sqrt(v_hat) + ADAM_EPS) + ADAM_WD * w_ref[...])
        mo_ref[...] = mn
        vo_ref[...] = vn

    spec = pl.BlockSpec((tr, cols), lambda i: (i, 0))
    shape = jax.ShapeDtypeStruct((rows, cols), F32)
    return pl.pallas_call(
        body, name=name, grid=(rows // tr,),
        in_specs=[spec] * 4, out_specs=[spec] * 3, out_shape=[shape] * 3,
        compiler_params=_params("arbitrary"),
    )(w, g, m, v)


def adamw_layer(w, m, v, own, recv, core, layer, carried, name):
    n_layers, rows, cols = w.shape
    h = rows // 2

    def body(core_ref, w_ref, m_ref, v_ref, own_ref, recv_ref, *rest):
        g_ref, d_ref, mo_ref, vo_ref = rest[-4:]
        gv = jnp.where(pl.program_id(0) == core_ref[0], own_ref[...], recv_ref[...])
        mn = ADAM_B1 * m_ref[0] + (1.0 - ADAM_B1) * gv
        vn = ADAM_B2 * v_ref[0] + (1.0 - ADAM_B2) * (gv * gv)
        m_hat = mn / (1.0 - ADAM_B1 ** ADAM_STEP)
        v_hat = vn / (1.0 - ADAM_B2 ** ADAM_STEP)
        g_ref[0] = gv
        d_ref[0] = -ADAM_LR * (m_hat / (jnp.sqrt(v_hat) + ADAM_EPS) + ADAM_WD * w_ref[0])
        mo_ref[0] = mn
        vo_ref[0] = vn

    steps = 1
    tr = h // steps
    stacked = pl.BlockSpec((1, tr, cols), lambda half, j, core_ref: (layer, half * steps + j, 0))
    halfspec = pl.BlockSpec((tr, cols), lambda half, j, core_ref: (j, 0))
    n_carried = 0 if carried is None else 4
    shape = jax.ShapeDtypeStruct(w.shape, F32)
    return pl.pallas_call(
        body, name=name,
        grid_spec=pltpu.PrefetchScalarGridSpec(
            num_scalar_prefetch=1, grid=(2, steps),
            in_specs=[stacked] * 3 + [halfspec] * 2 + [ANY] * n_carried, out_specs=[stacked] * 4),
        out_shape=[shape] * 4,
        input_output_aliases={6 + i: i for i in range(n_carried)},
        compiler_params=_params("arbitrary", "arbitrary"),
    )(core, w, m, v, own, recv, *(carried or ()))


def add_halves(gs, ps, core, name, deps=()):
    n = len(gs)

    def body(core_ref, *refs):
        for i in range(n):
            refs[2 * n + i][...] = (refs[i][...].astype(F32) + refs[n + i][...].astype(F32)).astype(BF16)

    in_specs, out_specs, out_shape = [], [], []
    for gt in gs:
        h, c = gt.shape[1] // 2, gt.shape[2]
        in_specs.append(pl.BlockSpec((1, h, c), lambda b, core_ref: (b, core_ref[0], 0)))
    for gt in gs:
        h, c = gt.shape[1] // 2, gt.shape[2]
        in_specs.append(pl.BlockSpec((1, h, c), lambda b, core_ref: (b, 0, 0)))
        out_specs.append(pl.BlockSpec((1, h, c), lambda b, core_ref: (b, 0, 0)))
        out_shape.append(jax.ShapeDtypeStruct((N_CHIPS, h, c), BF16))
    in_specs += [ANY] * len(deps)
    return pl.pallas_call(
        _after(body, 1 + 2 * n, deps), name=name,
        grid_spec=pltpu.PrefetchScalarGridSpec(num_scalar_prefetch=1, grid=(N_CHIPS,), in_specs=in_specs, out_specs=out_specs),
        out_shape=out_shape, compiler_params=_params("arbitrary"),
    )(core, *gs, *ps, *deps)


def add_final(hs, qs, chip, name, deps=()):
    n = len(hs)

    def body(chip_ref, *refs):
        for i in range(n):
            q = refs[n + i]
            refs[2 * n + i][...] = ((refs[i][0].astype(F32) + q[0].astype(F32)) + q[1].astype(F32)) + q[2].astype(F32)

    steps = 2
    in_specs, out_specs, out_shape = [], [], []
    for ht in hs:
        h, c = ht.shape[1], ht.shape[2]
        in_specs.append(pl.BlockSpec((1, h // steps, c), lambda i, chip_ref: (chip_ref[0], i, 0)))
    for ht in hs:
        h, c = ht.shape[1], ht.shape[2]
        in_specs.append(pl.BlockSpec((N_CHIPS - 1, h // steps, c), lambda i, chip_ref: (0, i, 0)))
        out_specs.append(pl.BlockSpec((h // steps, c), lambda i, chip_ref: (i, 0)))
        out_shape.append(jax.ShapeDtypeStruct((h, c), F32))
    in_specs += [ANY] * len(deps)
    return pl.pallas_call(
        _after(body, 1 + 2 * n, deps), name=name,
        grid_spec=pltpu.PrefetchScalarGridSpec(num_scalar_prefetch=1, grid=(steps,), in_specs=in_specs, out_specs=out_specs),
        out_shape=out_shape, compiler_params=_params("arbitrary"),
    )(chip, *hs, *qs, *deps)


def sum_devices(own, gathered, me, name, deps=()):
    rows = own.shape[0]

    def body(me_ref, own_ref, g_ref, o_ref):
        me_dev = me_ref[0]
        acc = None
        for dev in range(8):
            slot = jnp.maximum((me_dev ^ dev) - 1, 0)
            term = jnp.where(me_dev == dev, own_ref[...], g_ref[slot]).astype(F32)
            acc = term if acc is None else acc + term
        o_ref[...] = acc

    tr = ([c for c in range(80, 0, -16) if rows % c == 0] or [rows])[0]
    return pl.pallas_call(
        _after(body, 3, deps), name=name,
        grid_spec=pltpu.PrefetchScalarGridSpec(
            num_scalar_prefetch=1, grid=(rows // tr,),
            in_specs=[pl.BlockSpec((tr, D), lambda i, me_ref: (i, 0)), pl.BlockSpec((7, tr, D), lambda i, me_ref: (0, i, 0))]
            + [ANY] * len(deps),
            out_specs=pl.BlockSpec((tr, D), lambda i, me_ref: (i, 0))),
        out_shape=jax.ShapeDtypeStruct((rows, D), F32),
        compiler_params=_params("arbitrary"),
    )(me, own, gathered, *deps)


def _mesh_pos():
    return lax.axis_index("x"), lax.axis_index("y"), lax.axis_index("c")


def _other_chips(x, y):
    return [(1 - x, y), (x, 1 - y), (1 - x, 1 - y)]


def _sibling():
    x, y, c = _mesh_pos()
    return [(x, y, 1 - c)]


def _same_core_of_other_chips():
    x, y, c = _mesh_pos()
    return [(cx, cy, c) for (cx, cy) in _other_chips(x, y)]


def _on_sequencer(body, name, operands, out_shapes, sems, peers, collective_id, deps=()):
    ordered = _after(body, len(operands), deps)

    def seq_body(*refs):
        barrier = pltpu.get_barrier_semaphore()
        with_whom = peers()
        for peer in with_whom:
            pl.semaphore_signal(barrier, inc=1, device_id=peer, device_id_type=MESH)
        pl.semaphore_wait(barrier, len(with_whom))
        ordered(*refs)

    return pl.kernel(
        seq_body, name=name, out_type=out_shapes,
        mesh=plsc.ScalarSubcoreMesh(axis_name="seq", num_cores=1),
        scratch_types=sems, compiler_params=pltpu.CompilerParams(collective_id=collective_id, has_side_effects=True),
    )(*operands, *deps)


def all_gather_weights(shards, name, deps=()):
    n = len(shards)

    def body(*refs):
        ins, outs = refs[:n], refs[n:2 * n]
        send, recv, fsend, frecv = refs[2 * n:]
        x, y, c = _mesh_pos()
        k = 2 * x + y
        chips = _other_chips(x, y)

        def half(ref, i, rows_half):
            h = shards[i].shape[0] // 2
            return ref.at[pl.ds(pl.multiple_of(rows_half * h, 8), h), :]

        first = []
        for i in range(n):
            for j, (cx, cy) in enumerate(chips):
                first.append(pltpu.make_async_remote_copy(
                    src_ref=half(ins[i], i, c), dst_ref=half(outs[i].at[k], i, c),
                    send_sem=send.at[i, j], recv_sem=recv.at[i, j], device_id=(cx, cy, c), device_id_type=MESH))
        for cp in first:
            cp.start()
        passed = []
        for i in range(n):
            for j, (cx, cy) in enumerate(chips):
                blk = half(outs[i].at[2 * cx + cy], i, c)
                pltpu.make_async_remote_copy(src_ref=blk, dst_ref=blk, send_sem=send.at[i, j], recv_sem=recv.at[i, j],
                                             device_id=(cx, cy, c), device_id_type=MESH).wait_recv()
                fw = pltpu.make_async_remote_copy(src_ref=blk, dst_ref=blk, send_sem=fsend.at[i, j], recv_sem=frecv.at[i, j],
                                                  device_id=(x, y, 1 - c), device_id_type=MESH)
                fw.start()
                passed.append(fw)
        for i in range(n):
            for j, (cx, cy) in enumerate(chips):
                blk = half(outs[i].at[2 * cx + cy], i, 1 - c)
                pltpu.make_async_remote_copy(src_ref=blk, dst_ref=blk, send_sem=fsend.at[i, j], recv_sem=frecv.at[i, j],
                                             device_id=(x, y, 1 - c), device_id_type=MESH).wait_recv()
        for cp in first + passed:
            cp.wait_send()

    def peers():
        x, y, c = _mesh_pos()
        return [(cx, cy, c) for (cx, cy) in _other_chips(x, y)] + [(x, y, 1 - c)]

    return _on_sequencer(
        body, name, shards, [jax.ShapeDtypeStruct((N_CHIPS,) + s.shape, s.dtype) for s in shards],
        [pltpu.SemaphoreType.DMA((n, 3))] * 4, peers, GATHER_COLLECTIVE_ID, deps)


def place_own(gathered, shards, chip, name):
    n = len(shards)

    def body(chip_ref, *refs):
        for i in range(n):
            refs[2 * n + i][0] = refs[i][...]

    in_specs = [pl.BlockSpec(s.shape, lambda i, chip_ref: (0, 0)) for s in shards] + [ANY] * n
    out_specs = [pl.BlockSpec((1,) + s.shape, lambda i, chip_ref: (chip_ref[0], 0, 0)) for s in shards]
    return pl.pallas_call(
        body, name=name,
        grid_spec=pltpu.PrefetchScalarGridSpec(num_scalar_prefetch=1, grid=(1,), in_specs=in_specs, out_specs=out_specs),
        out_shape=[jax.ShapeDtypeStruct(g.shape, g.dtype) for g in gathered],
        input_output_aliases={1 + n + i: i for i in range(n)},
        compiler_params=_params("arbitrary"),
    )(chip, *shards, *gathered)


def all_gather_rows(shard, name):
    def body(in_ref, out_ref, send, recv, lsem):
        x, y, c = _mesh_pos()
        k = 2 * x + y
        chips = _other_chips(x, y)
        local = pltpu.make_async_copy(in_ref, out_ref.at[k], lsem)
        local.start()
        sent = [pltpu.make_async_remote_copy(src_ref=in_ref, dst_ref=out_ref.at[k], send_sem=send.at[j], recv_sem=recv.at[j],
                                             device_id=(cx, cy, c), device_id_type=MESH) for j, (cx, cy) in enumerate(chips)]
        for cp in sent:
            cp.start()
        for j, (cx, cy) in enumerate(chips):
            blk = out_ref.at[2 * cx + cy]
            pltpu.make_async_remote_copy(src_ref=blk, dst_ref=blk, send_sem=send.at[j], recv_sem=recv.at[j],
                                         device_id=(cx, cy, c), device_id_type=MESH).wait_recv()
        for cp in sent:
            cp.wait_send()
        local.wait()

    return pl.pallas_call(
        body, name=name, in_specs=[ANY], out_specs=ANY,
        out_shape=jax.ShapeDtypeStruct((N_CHIPS,) + shard.shape, shard.dtype),
        scratch_shapes=[pltpu.SemaphoreType.DMA((3,)), pltpu.SemaphoreType.DMA((3,)), pltpu.SemaphoreType.DMA],
    )(shard)


def swap_halves(gs, name):
    n = len(gs)

    def body(*refs):
        ins, outs = refs[:n], refs[n:2 * n]
        send, recv = refs[2 * n:]
        x, y, c = _mesh_pos()
        sent = []
        for i in range(n):
            h = gs[i].shape[1] // 2
            src = ins[i].at[:, pl.ds(pl.multiple_of((1 - c) * h, 8), h), :]
            cp = pltpu.make_async_remote_copy(src_ref=src, dst_ref=outs[i], send_sem=send.at[i], recv_sem=recv.at[i],
                                              device_id=(x, y, 1 - c), device_id_type=MESH)
            cp.start()
            sent.append(cp)
        for cp in sent:
            cp.wait()

    return _on_sequencer(
        body, name, gs, [jax.ShapeDtypeStruct((N_CHIPS, g.shape[1] // 2, g.shape[2]), g.dtype) for g in gs],
        [pltpu.SemaphoreType.DMA((n,)), pltpu.SemaphoreType.DMA((n,))], _sibling, SIBLING_COLLECTIVE_ID)


def scatter_chips(hs, name):
    n = len(hs)

    def body(*refs):
        ins, outs = refs[:n], refs[n:2 * n]
        send, recv = refs[2 * n:]
        x, y, c = _mesh_pos()
        chips = _other_chips(x, y)
        sent = []
        for i in range(n):
            for j, (cx, cy) in enumerate(chips):
                cp = pltpu.make_async_remote_copy(src_ref=ins[i].at[2 * cx + cy], dst_ref=outs[i].at[j],
                                                  send_sem=send.at[i, j], recv_sem=recv.at[i, j],
                                                  device_id=(cx, cy, c), device_id_type=MESH)
                cp.start()
                sent.append(cp)
        for cp in sent:
            cp.wait()

    return _on_sequencer(
        body, name, hs, [jax.ShapeDtypeStruct((N_CHIPS - 1,) + h.shape[1:], h.dtype) for h in hs],
        [pltpu.SemaphoreType.DMA((n, 3)), pltpu.SemaphoreType.DMA((n, 3))], _same_core_of_other_chips, CHIPS_COLLECTIVE_ID)


def swap_reduced(rs, name):
    n = len(rs)

    def body(*refs):
        ins, outs = refs[:n], refs[n:2 * n]
        send, recv = refs[2 * n:]
        x, y, c = _mesh_pos()
        sent = []
        for i in range(n):
            cp = pltpu.make_async_remote_copy(src_ref=ins[i], dst_ref=outs[i], send_sem=send.at[i], recv_sem=recv.at[i],
                                              device_id=(x, y, 1 - c), device_id_type=MESH)
            cp.start()
            sent.append(cp)
        for cp in sent:
            cp.wait()

    return _on_sequencer(
        body, name, rs, [jax.ShapeDtypeStruct(r.shape, r.dtype) for r in rs],
        [pltpu.SemaphoreType.DMA((n,)), pltpu.SemaphoreType.DMA((n,))], _sibling, SIBLING_COLLECTIVE_ID)


def all_gather_devices(parts, name):
    n = len(parts)

    def everyone_else():
        x, y, c = _mesh_pos()
        return [(1 - x if (rel >> 2) & 1 else x, 1 - y if (rel >> 1) & 1 else y, 1 - c if rel & 1 else c) for rel in range(1, 8)]

    def body(*refs):
        ins, outs = refs[:n], refs[n:2 * n]
        send, recv = refs[2 * n:]
        sent = []
        for slot, peer in enumerate(everyone_else()):
            for i in range(n):
                cp = pltpu.make_async_remote_copy(src_ref=ins[i], dst_ref=outs[i].at[slot], send_sem=send.at[i, slot],
                                                  recv_sem=recv.at[i, slot], device_id=peer, device_id_type=MESH)
                cp.start()
                sent.append(cp)
        for cp in sent:
            cp.wait()

    return _on_sequencer(
        body, name, parts, [jax.ShapeDtypeStruct((7,) + p.shape, p.dtype) for p in parts],
        [pltpu.SemaphoreType.DMA((n, 7)), pltpu.SemaphoreType.DMA((n, 7))], everyone_else, ALL_COLLECTIVE_ID)


class ReduceScatter:
    def __init__(self, grads, names, layer, core, chip, name):
        self.grads, self.names, self.layer, self.core, self.chip, self.name = grads, names, layer, core, chip, name
        self.from_sibling = swap_halves(grads, name + "_swap")

    def sum_cores(self, deps=()):
        self.core_sums = add_halves(self.grads, self.from_sibling, self.core, self.name + "_add2", deps)
        self.from_chips = scatter_chips(self.core_sums, self.name + "_scatter")
        return self.core_sums[0]

    def sum_chips(self, deps=()):
        self.half_sums = add_final(self.core_sums, self.from_chips, self.chip, self.name + "_add4", deps)
        self.other_half = swap_reduced(self.half_sums, self.name + "_join")
        return self.half_sums[0]


def _blocked(w):
    return w.reshape(w.shape[0] * w.shape[1], w.shape[2])


def _grp_from_blocks(w):
    return w.reshape(N_CHIPS, 4, 64, C_GROUP_DIM).transpose(1, 0, 2, 3).reshape(4, C_GROUP_DIM, C_GROUP_DIM)


def _grp_to_blocks(w):
    return w.reshape(4, N_CHIPS, 64, C_GROUP_DIM).transpose(1, 0, 2, 3).reshape(N_CHIPS, C_GROUP_DIM, C_GROUP_DIM)


def _dw_cols(h, dact, hs, name, deps):
    return mm_tn(h[None], dact, (N_CHIPS, D, hs), D, hs, N_CHIPS, lambda j: (0, 0), lambda j: (0, j), lambda j: (j, 0, 0), name, deps)


def _dw_rows(y, dxm, name, deps):
    tm = 512
    out = mm_tn(y[None], dxm[None], (1, D, D), tm, D, D // tm, lambda j: (0, j), lambda j: (0, 0), lambda j: (0, j, 0), name, deps)
    return out.reshape(N_CHIPS, D // N_CHIPS, D)


def _dw_hidden(act, other, name, deps):
    tm, tn = FFN_HIDDEN // 2, (D if other.dtype == BF16 else 256)
    n_n = D // tn
    out = mm_tn(act[None], other[None], (1, FFN_HIDDEN, D), tm, tn, 2 * n_n,
                lambda j: (0, j // n_n), lambda j: (0, j % n_n), lambda j: (0, j // n_n, j % n_n), name, deps)
    return out.reshape(N_CHIPS, FFN_SHARD, D)


def kernel(x, norm_mix_g, norm_ffn_g, final_norm_g, a_w_in, a_v_norm_g, a_w_s, a_b_s, a_w_out, b_w_in, b_conv_w, b_w_out, c_w_in, c_w_grp, c_scale, c_w_out, f_w_gate, f_w_up, f_w_down, loss_target, m_norm_mix_g, m_norm_ffn_g, m_final_norm_g, m_a_w_in, m_a_v_norm_g, m_a_w_s, m_a_b_s, m_a_w_out, m_b_w_in, m_b_conv_w, m_b_w_out, m_c_w_in, m_c_w_grp, m_c_scale, m_c_w_out, m_f_w_gate, m_f_w_up, m_f_w_down, v_norm_mix_g, v_norm_ffn_g, v_final_norm_g, v_a_w_in, v_a_v_norm_g, v_a_w_s, v_a_b_s, v_a_w_out, v_b_w_in, v_b_conv_w, v_b_w_out, v_c_w_in, v_c_w_grp, v_c_scale, v_c_w_out, v_f_w_gate, v_f_w_up, v_f_w_down):
    n_ex, seq, _ = x.shape
    t = n_ex * seq
    xi, yi, ci = lax.axis_index("x"), lax.axis_index("y"), lax.axis_index("c")
    chip = (2 * xi + yi).astype(jnp.int32)
    core_arr = ci.astype(jnp.int32).reshape(1)
    chip_arr = chip.reshape(1)
    me_arr = (4 * xi + 2 * yi + ci).astype(jnp.int32).reshape(1)
    bf = lambda w: w.astype(BF16)

    pad8 = lambda v: jnp.pad(v, ((0, 8 - v.shape[0]), (0, 0)))
    small_rows = jnp.concatenate([pad8(a_v_norm_g), pad8(b_conv_w[0]), pad8(c_scale)], axis=0)
    small_gathered = all_gather_rows(small_rows, "ag_small")
    small_full = small_gathered.transpose(1, 0, 2).reshape(24, D)
    gv_full = [small_full[0:1], small_full[1:2]]
    cw_full = small_full[8:11]
    scale_full = small_full[16:17]

    mixer_shards = [
        [bf(a_w_in[0]), bf(a_w_out[0])],
        [bf(b_w_in[0]), bf(b_w_out[0])],
        [bf(c_w_in[0]), bf(c_w_grp[0]).reshape(C_GROUP_DIM, C_GROUP_DIM), bf(c_w_out[0])],
        [bf(a_w_in[1]), bf(a_w_out[1])],
    ]
    hidden_major = lambda w: jnp.swapaxes(w, 1, 2)
    gate_t, up_t = hidden_major(f_w_gate), hidden_major(f_w_up)
    gathered = []
    for i in range(4):
        ffn_shards = [bf(gate_t[i]), bf(up_t[i]), bf(f_w_down[i])]
        if i == 0:
            parts = [(mixer_shards[0], "ag_l0_mixer", [small_gathered]), (ffn_shards, "ag_l0_ffn", [])]
        else:
            parts = [(mixer_shards[i] + ffn_shards, f"ag_l{i}", [])]
        layer = []
        for shards, name, deps in parts:
            layer += place_own(all_gather_weights(shards, name, deps), shards, chip_arr, name.replace("ag", "own"))
        gathered.append(layer)

    mask = (jnp.arange(GMLP_BLOCK)[None, :] // 64) <= (jnp.arange(GMLP_BLOCK)[:, None] // 64)
    gmix = [norm_mix_g[i:i + 1] for i in range(4)]
    gffn = [norm_ffn_g[i:i + 1] for i in range(4)]
    b_chunks = [(j // 3, (j % 3) * 256, 256, j // 4, (j % 4) * 256) for j in range(12)]
    c_chunks = [(0, 0, D, 0, 0)]

    xs = [x.reshape(t, D)]
    saved = []
    for i in range(4):
        ws = gathered[i]
        wg, wu, wd = (w.reshape(FFN_HIDDEN, D) for w in ws[-3:])
        xin = xs[-1]
        if i in (0, 3):
            j = 0 if i == 0 else 1
            win, wout = ws[0], _blocked(ws[1])
            wm32 = jnp.where(mask[None], a_w_s[j], 0.0)
            wm, wmt = bf(wm32), bf(wm32.transpose(0, 2, 1))
            bs = jnp.broadcast_to(a_b_s[j][:, :, None], (A_GROUPS, GMLP_BLOCK, 128))
            h, z, vn, y, xmid = a_fwd(xin, gmix[i], win, gv_full[j], wm, bs, wout, f"a_fwd_l{i}")
            saved.append(dict(h=h, z=z, y=y, vn=vn, win=win, wout=wout, wm=wm, wmt=wmt, bs=bs, gv=gv_full[j]))
        elif i == 1:
            win, wout = ws[0], _blocked(ws[1])
            h, p3 = norm_mm(xin, gmix[i], win, b_chunks, 3, D, "b_in")
            y = b_conv_fwd(p3, cw_full, seq, "b_conv")
            xmid = mm_res(y, wout, xin, "b_out")
            saved.append(dict(h=h, p3=p3, y=y, win=win, wout=wout))
        else:
            win, wgrp, wout = _blocked(ws[0])[None], _grp_from_blocks(ws[1]), _blocked(ws[2])
            h, p = norm_mm(xin, gmix[i], win, c_chunks, 1, D, "c_in")
            dpool = c_pool_fwd(p[0], seq, "c_pool")
            y, xmid = c_out_fwd(dpool, wgrp, scale_full, wout, xin, "c_out")
            saved.append(dict(h=h, d=dpool, y=y, win=win, wgrp=wgrp, wout=wout))
        h2, fa, fb, fs, xout = ffn_fwd(xmid, gffn[i], wg, wu, wd, f"ffn_l{i}")
        saved[-1].update(h2=h2, fa=fa, fb=fb, fs=fs, xmid=xmid, wg=wg, wu=wu, wd=wd)
        xs.append(xout)

    dx, dg_final, loss_part = loss_head(xs[4], loss_target.reshape(t, D), final_norm_g[None], "loss_head")
    loss = lax.psum(loss_part[0, 0], ("x", "y", "c"))

    weights = dict(norm_mix_g=norm_mix_g, norm_ffn_g=norm_ffn_g, final_norm_g=final_norm_g, a_w_in=a_w_in, a_v_norm_g=a_v_norm_g,
                   a_w_s=a_w_s, a_b_s=a_b_s, a_w_out=a_w_out, b_w_in=b_w_in, b_conv_w=b_conv_w, b_w_out=b_w_out, c_w_in=c_w_in,
                   c_w_grp=c_w_grp, c_scale=c_scale, c_w_out=c_w_out, f_w_gate=f_w_gate, f_w_up=f_w_up, f_w_down=f_w_down)
    m_in = dict(norm_mix_g=m_norm_mix_g, norm_ffn_g=m_norm_ffn_g, final_norm_g=m_final_norm_g, a_w_in=m_a_w_in, a_v_norm_g=m_a_v_norm_g,
                a_w_s=m_a_w_s, a_b_s=m_a_b_s, a_w_out=m_a_w_out, b_w_in=m_b_w_in, b_conv_w=m_b_conv_w, b_w_out=m_b_w_out, c_w_in=m_c_w_in,
                c_w_grp=m_c_w_grp, c_scale=m_c_scale, c_w_out=m_c_w_out, f_w_gate=m_f_w_gate, f_w_up=m_f_w_up, f_w_down=m_f_w_down)
    v_in = dict(norm_mix_g=v_norm_mix_g, norm_ffn_g=v_norm_ffn_g, final_norm_g=v_final_norm_g, a_w_in=v_a_w_in, a_v_norm_g=v_a_v_norm_g,
                a_w_s=v_a_w_s, a_b_s=v_a_b_s, a_w_out=v_a_w_out, b_w_in=v_b_w_in, b_conv_w=v_b_conv_w, b_w_out=v_b_w_out, c_w_in=v_c_w_in,
                c_w_grp=v_c_w_grp, c_scale=v_c_scale, c_w_out=v_c_w_out, f_w_gate=v_f_w_gate, f_w_up=v_f_w_up, f_w_down=v_f_w_down)
    grp_rows = lambda a: a.reshape(1, C_GROUP_DIM, C_GROUP_DIM)
    same = lambda a: a
    to_stacked = {nme: same for nme in ("a_w_in", "a_w_out", "b_w_in", "b_w_out", "c_w_in", "c_w_out", "f_w_down")}
    to_stacked.update(f_w_gate=hidden_major, f_w_up=hidden_major, c_w_grp=grp_rows)
    from_stacked = dict(to_stacked, c_w_grp=lambda a: a.reshape(c_w_grp.shape))
    layer_tensors = {0: ["a_w_out", "a_w_in"], 1: ["b_w_out", "b_w_in"], 2: ["c_w_out", "c_w_in", "c_w_grp"], 3: ["a_w_out", "a_w_in"]}
    carried = {}

    def bs_rows(v):
        return jnp.pad(v[:, :, 0].reshape(1, D), ((0, 7), (0, 0)))

    def update(unit):
        done = []
        for pos, nme in enumerate(unit.names):
            stacked_layer = unit.layer if nme.startswith("f_") else (unit.layer // 3 if nme.startswith("a_") else 0)
            view = to_stacked[nme]
            carried[nme] = adamw_layer(view(weights[nme]), view(m_in[nme]), view(v_in[nme]), unit.half_sums[pos], unit.other_half[pos],
                                       core_arr, stacked_layer, carried.get(nme), f"adamw_{nme}_l{unit.layer}")
            done.append(carried[nme][0])
        return done

    ffn_names = ["f_w_gate", "f_w_up", "f_w_down"]
    dg_mix, dg_ffn = [None] * 4, [None] * 4
    small = {}
    newer = older = None
    for i in (3, 2, 1, 0):
        sv = saved[i]
        xin = xs[i]
        deps = ([newer.grads[0]] if newer else []) + ([older.half_sums[0]] if older else [])
        dxm, da, db, dg_ffn[i] = ffn_bwd(dx, sv["fa"], sv["fb"], sv["xmid"], gffn[i], sv["wg"], sv["wu"], sv["wd"], f"ffn_bwd_l{i}", deps)
        last = [newer.sum_cores([dxm])] if newer else []
        g_gate = _dw_hidden(da, sv["h2"], f"dw_gate_l{i}", last)
        g_up = _dw_hidden(db, sv["h2"], f"dw_up_l{i}", [g_gate])
        g_down = _dw_hidden(sv["fs"], dx, f"dw_down_l{i}", [g_up])
        last_ffn = [g_down]
        if i == 0:
            ffn_unit = ReduceScatter([g_down, g_gate, g_up], ["f_w_down", "f_w_gate", "f_w_up"], 0, core_arr, chip_arr, "rs_l0_ffn")
        if i in (0, 3):
            j = 0 if i == 0 else 1
            dx, dz, dwm, dbs, dgv, dg_mix[i] = a_bwd(dxm, xin, gmix[i], sv["z"], sv["vn"], sv["gv"], sv["wm"], sv["wmt"], sv["bs"],
                                                     sv["wout"], sv["win"], f"a_bwd_l{i}")
            if i == 0:
                early = [sum_devices(p, gth, me_arr, f"sum_small_grads_l123_{n}")
                         for n, (p, gth) in enumerate(zip(early_parts, early_gathered))]
                newer.sum_chips([dz] + update(older) + early)
                last_ffn.append(ffn_unit.sum_cores([newer.half_sums[0]]))
            dz = dz[None]
            g_in = _dw_cols(sv["h"], dz, 512, f"dw_a_in_l{i}", last_ffn)
            g_out = _dw_rows(sv["y"], dxm, f"dw_a_out_l{i}", [g_in])
            small[f"wm{j}"], small[f"bs{j}"], small[f"gv{j}"] = dwm, dbs, dgv
            mixer_grads = [g_out, g_in]
        elif i == 1:
            dy = mm_nt(dxm, sv["wout"], "b_dy")
            dp3, small["cw"] = b_conv_bwd(dy, sv["p3"], cw_full, seq, "b_conv_bwd")
            dx, dg_mix[i] = bwd_in(dp3, sv["win"], b_chunks, xin, gmix[i], dxm, "b_bwd_in")
            g_in = mm_tn(sv["h"][None], dp3, (N_CHIPS, D, 768), D, 256, 12,
                         lambda j: (0, 0), lambda j: (j // 4, j % 4), lambda j: (j // 3, 0, j % 3), "dw_b_in", [g_down])
            g_out = _dw_rows(sv["y"], dxm, "dw_b_out", [g_in])
            mixer_grads = [g_out, g_in]
        else:
            outs = c_out_bwd(dxm, sv["d"], sv["wgrp"], scale_full, sv["wout"], "c_out_bwd")
            dyp, dd, small["scale"] = outs[0], list(outs[1:5]), outs[5]
            dpool = c_pool_bwd(dd, seq, "c_pool_bwd")
            dp = jnp.concatenate(dpool, axis=1)[None]
            dx, dg_mix[i] = bwd_in(dp, sv["win"], c_chunks, xin, gmix[i], dxm, "c_bwd_in")
            g_in = _dw_rows(sv["h"], dp[0], "dw_c_in", [g_down])
            dcat = jnp.concatenate(sv["d"], axis=1)
            g_grp = mm_tn(dcat[None], dyp[None], (4, C_GROUP_DIM, C_GROUP_DIM), C_GROUP_DIM, C_GROUP_DIM, 4,
                          lambda j: (0, j), lambda j: (0, j), lambda j: (j, 0, 0), "dw_c_grp", [g_in])
            g_out = _dw_rows(sv["y"], dxm, "dw_c_out", [g_grp])
            mixer_grads = [g_out, g_in, _grp_to_blocks(g_grp)]
        if i > 0:
            unit = ReduceScatter(mixer_grads + [g_gate, g_up, g_down], layer_tensors[i] + ffn_names, i, core_arr, chip_arr, f"rs_l{i}")
        else:
            unit = ReduceScatter(mixer_grads, layer_tensors[0], 0, core_arr, chip_arr, "rs_l0_mixer")
        if newer and i > 0:
            newer.sum_chips([mixer_grads[0]] + (update(older) if older else []))
        if i == 1:
            early_parts = [jnp.concatenate(dg_mix[1:] + dg_ffn[1:] + [dg_final, small["gv1"], small["cw"], small["scale"],
                                                                     bs_rows(small["bs1"])], axis=0),
                           bf(small["wm1"].reshape(128, D))]
            early_gathered = all_gather_devices(early_parts, "ag_small_grads_l123")
        older, newer = newer, unit
    grad_x = dx.reshape(n_ex, seq, D)
    mixer_unit = newer
    ffn_unit.sum_chips(update(older) + [mixer_unit.grads[0]])
    mixer_unit.sum_cores([ffn_unit.half_sums[0]])
    mixer_unit.sum_chips(update(ffn_unit))
    updated = update(mixer_unit)

    late_parts = [jnp.concatenate([dg_mix[0], dg_ffn[0], small["gv0"], bs_rows(small["bs0"])], axis=0), bf(small["wm0"].reshape(128, D))]
    late_gathered = all_gather_devices(late_parts, "ag_small_grads_l0")
    late = [sum_devices(p, gth, me_arr, f"sum_small_grads_l0_{n}", updated) for n, (p, gth) in enumerate(zip(late_parts, late_gathered))]
    (early_rows, early_ws), (late_rows, late_ws) = early, late
    first_rows = lambda a, b, n: jnp.concatenate([a, b], axis=0).reshape(n, 8, D)[:, 0]
    g_norm_mix = first_rows(late_rows[0:8], early_rows[0:24], 4)
    g_norm_ffn = first_rows(late_rows[8:16], early_rows[24:48], 4)
    g_final = early_rows[48]
    g_gv = first_rows(late_rows[16:24], early_rows[56:64], 2)
    g_cw = early_rows[64:67]
    g_scale = early_rows[72:73]
    g_ws = jnp.where(mask[None, None], jnp.concatenate([late_ws, early_ws], axis=0).reshape(2, A_GROUPS, 128, 128), 0.0)
    g_bs = first_rows(late_rows[24:32], early_rows[80:88], 2).reshape(2, A_GROUPS, 128)
    col0 = chip * (D // N_CHIPS)
    cols = lambda v: lax.dynamic_slice_in_dim(v, col0, D // N_CHIPS, axis=1)

    small_grads = {
        "norm_mix_g": g_norm_mix, "norm_ffn_g": g_norm_ffn, "final_norm_g": g_final, "a_v_norm_g": cols(g_gv), "a_w_s": g_ws,
        "a_b_s": g_bs, "b_conv_w": cols(g_cw)[None], "c_scale": cols(g_scale),
    }
    results = {}
    for nme, g in small_grads.items():
        w = weights[nme]
        flat = lambda a: a.reshape(-1, w.shape[-1])
        dl, mn, vn = adamw(flat(w), flat(g), flat(m_in[nme]), flat(v_in[nme]), f"adamw_{nme}")
        results[nme] = tuple(o.reshape(w.shape) for o in (g, dl, mn, vn))
    for nme, outs in carried.items():
        results[nme] = tuple(from_stacked[nme](o) for o in outs)

    names = list(weights)
    return (loss, grad_x, *[results[n][0] for n in names], *[results[n][1] for n in names],
            *[results[n][2] for n in names], *[results[n][3] for n in names])
```

```python
import jax
import jax.numpy as jnp
from jax import lax
from jax.experimental import pallas as pl
from jax.experimental.pallas import tpu as pltpu
from jax.experimental.pallas import tpu_sc as plsc

F32 = jnp.float32
BF16 = jnp.bfloat16
D = 1024
FFN_SHARD = 704
GMLP_BLOCK = 128
A_GROUPS = 8
POOL_WINDOWS = (2, 4, 8, 16)
C_GROUP_DIM = 256
N_CHIPS = 4
EPS = 1e-6
ADAM_LR, ADAM_B1, ADAM_B2, ADAM_EPS, ADAM_WD, ADAM_STEP = 0.001, 0.9, 0.999, 1e-08, 0.01, 10
VMEM_LIMIT_BYTES = 56 * 1024 * 1024
FFN_HIDDEN = N_CHIPS * FFN_SHARD
FFN_FWD_CHUNKS = ((0, 768), (768, 768), (1536, 768), (2304, 512))
FFN_BWD_CHUNKS = ((0, 1024), (1024, 1024), (2048, 768))
FFN_FWD_ROWS = 512
FFN_BWD_ROWS = 256
A_ROWS = 256
MESH = pl.DeviceIdType.MESH
GATHER_COLLECTIVE_ID = 1
SIBLING_COLLECTIVE_ID = 2
CHIPS_COLLECTIVE_ID = 3
ALL_COLLECTIVE_ID = 4
ANY = pl.BlockSpec(memory_space=pl.ANY)
NT_DIMS = (((1,), (1,)), ((), ()))
TN_DIMS = (((0,), (0,)), ((), ()))
INV_SQRT2 = 0.7071067811865476
INV_SQRT_2PI = 0.3989422804014327


def _params(*semantics):
    return pltpu.CompilerParams(dimension_semantics=semantics, vmem_limit_bytes=VMEM_LIMIT_BYTES)


def _dot(a, b):
    return jnp.dot(a, b, preferred_element_type=F32)


def _dot_nt(a, b):
    return lax.dot_general(a, b, NT_DIMS, preferred_element_type=F32)


def _rms(x):
    r = lax.rsqrt(jnp.mean(x * x, axis=-1, keepdims=True) + EPS)
    return x * r, r


def _rms_bwd(x, g, dh):
    xh, r = _rms(x)
    dxh = dh * g
    dx = r * (dxh - xh * jnp.mean(dxh * xh, axis=-1, keepdims=True))
    return dx, jnp.sum(dh * xh, axis=0, keepdims=True)


def _gelu(x):
    return 0.5 * x * (1.0 + lax.erf(x * INV_SQRT2))


def _gelu_grad(x):
    return 0.5 * (1.0 + lax.erf(x * INV_SQRT2)) + x * jnp.exp(-0.5 * x * x) * INV_SQRT_2PI


def _shift_down(v, s, row):
    return jnp.where(row >= s, pltpu.roll(v, s, 0), 0.0)


def _shift_up(v, s, row):
    n = v.shape[0]
    return jnp.where(row < n - s, pltpu.roll(v, n - s, 0), 0.0)


def _row_tile(t, want):
    return want if t % want == 0 else t


def _after(body, first, deps):
    if not deps:
        return body

    def ordered(*refs):
        return body(*refs[:first], *refs[first + len(deps):])

    return ordered


def norm_mm(x, g, w, chunks, n_parts, part_width, name):
    t = x.shape[0]
    tm = _row_tile(t, 512)
    n_shards, _, hs = w.shape

    def body(x_ref, g_ref, w_ref, h_ref, p_ref):
        xh, _ = _rms(x_ref[...])
        h = (xh * g_ref[...]).astype(BF16)
        h_ref[...] = h
        for s in range(n_shards):
            res = _dot(h, w_ref[s]).astype(BF16)
            for (cs, wc, width, part, pc) in chunks:
                if cs == s:
                    p_ref[part, :, pc:pc + width] = res[:, wc:wc + width]

    return pl.pallas_call(
        body, name=name, grid=(t // tm,),
        in_specs=[pl.BlockSpec((tm, D), lambda i: (i, 0)), pl.BlockSpec((1, D), lambda i: (0, 0)),
                  pl.BlockSpec((n_shards, D, hs), lambda i: (0, 0, 0))],
        out_specs=[pl.BlockSpec((tm, D), lambda i: (i, 0)), pl.BlockSpec((n_parts, tm, part_width), lambda i: (0, i, 0))],
        out_shape=[jax.ShapeDtypeStruct((t, D), BF16), jax.ShapeDtypeStruct((n_parts, t, part_width), BF16)],
        compiler_params=_params("arbitrary"),
    )(x, g, w)


def mm_res(a, w, res, name):
    t, k = a.shape
    n = w.shape[1]
    tm = _row_tile(t, 512)

    def body(a_ref, w_ref, r_ref, o_ref):
        o_ref[...] = r_ref[...] + _dot(a_ref[...], w_ref[...])

    return pl.pallas_call(
        body, name=name, grid=(t // tm,),
        in_specs=[pl.BlockSpec((tm, k), lambda i: (i, 0)), pl.BlockSpec((k, n), lambda i: (0, 0)),
                  pl.BlockSpec((tm, n), lambda i: (i, 0))],
        out_specs=pl.BlockSpec((tm, n), lambda i: (i, 0)),
        out_shape=jax.ShapeDtypeStruct((t, n), F32),
        compiler_params=_params("arbitrary"),
    )(a, w, res)


def mm_nt(a, w, name):
    t, n = a.shape
    k = w.shape[0]
    tm = _row_tile(t, 512)

    def body(a_ref, w_ref, o_ref):
        o_ref[...] = _dot_nt(a_ref[...].astype(BF16), w_ref[...]).astype(BF16)

    return pl.pallas_call(
        body, name=name, grid=(t // tm,),
        in_specs=[pl.BlockSpec((tm, n), lambda i: (i, 0)), pl.BlockSpec((k, n), lambda i: (0, 0))],
        out_specs=pl.BlockSpec((tm, k), lambda i: (i, 0)),
        out_shape=jax.ShapeDtypeStruct((t, k), BF16),
        compiler_params=_params("arbitrary"),
    )(a, w)


def bwd_in(dp, w, chunks, x, g, dres, name):
    n_parts, t, part_width = dp.shape
    n_shards, _, hs = w.shape
    tm = _row_tile(t, 512)

    def body(dp_ref, w_ref, x_ref, g_ref, dres_ref, dx_ref, dg_ref):
        acc = jnp.zeros((tm, D), F32)
        for (cs, wc, width, part, pc) in chunks:
            acc = acc + _dot_nt(dp_ref[part, :, pc:pc + width], w_ref[cs, :, wc:wc + width])
        dx, dg = _rms_bwd(x_ref[...], g_ref[...], acc)
        dx_ref[...] = dres_ref[...] + dx

        @pl.when(pl.program_id(0) == 0)
        def _():
            dg_ref[...] = jnp.zeros_like(dg_ref)

        dg_ref[0:1, :] += dg

    return pl.pallas_call(
        body, name=name, grid=(t // tm,),
        in_specs=[pl.BlockSpec((n_parts, tm, part_width), lambda i: (0, i, 0)),
                  pl.BlockSpec((n_shards, D, hs), lambda i: (0, 0, 0)),
                  pl.BlockSpec((tm, D), lambda i: (i, 0)), pl.BlockSpec((1, D), lambda i: (0, 0)),
                  pl.BlockSpec((tm, D), lambda i: (i, 0))],
        out_specs=[pl.BlockSpec((tm, D), lambda i: (i, 0)), pl.BlockSpec((8, D), lambda i: (0, 0))],
        out_shape=[jax.ShapeDtypeStruct((t, D), F32), jax.ShapeDtypeStruct((8, D), F32)],
        compiler_params=_params("arbitrary"),
    )(dp, w, x, g, dres)


def mm_tn(a, b, out_shape, tm, tn, n_tiles, a_idx, b_idx, o_idx, name, deps=()):
    t = a.shape[1]

    def body(a_ref, b_ref, o_ref):
        o_ref[0] = lax.dot_general(a_ref[0].astype(BF16), b_ref[0].astype(BF16), TN_DIMS, preferred_element_type=F32).astype(BF16)

    return pl.pallas_call(
        _after(body, 2, deps), name=name, grid=(n_tiles,),
        in_specs=[pl.BlockSpec((1, t, tm), lambda j: (a_idx(j)[0], 0, a_idx(j)[1])),
                  pl.BlockSpec((1, t, tn), lambda j: (b_idx(j)[0], 0, b_idx(j)[1]))] + [ANY] * len(deps),
        out_specs=pl.BlockSpec((1, tm, tn), lambda j: o_idx(j)),
        out_shape=jax.ShapeDtypeStruct(out_shape, BF16),
        compiler_params=_params("arbitrary"),
    )(a, b, *deps)


def ffn_fwd(x, g, wg, wu, wd, name):
    t = x.shape[0]
    tm = _row_tile(t, FFN_FWD_ROWS)
    hidden = wg.shape[0]

    def body(x_ref, g_ref, wg_ref, wu_ref, wd_ref, h_ref, a_ref, b_ref, s_ref, o_ref):
        xv = x_ref[...]
        xh, _ = _rms(xv)
        h = (xh * g_ref[...]).astype(BF16)
        h_ref[...] = h
        acc = xv
        for c0, cw in FFN_FWD_CHUNKS:
            cols = slice(c0, c0 + cw)
            a = _dot_nt(h, wg_ref[cols, :])
            b = _dot_nt(h, wu_ref[cols, :])
            sig = jax.nn.sigmoid(a)
            silu = a * sig
            s = (silu * b).astype(BF16)
            a_ref[:, cols] = (b * (sig * (1.0 + a * (1.0 - sig)))).astype(BF16)
            b_ref[:, cols] = silu.astype(BF16)
            s_ref[:, cols] = s
            acc = acc + _dot(s, wd_ref[cols, :])
        o_ref[...] = acc

    act = pl.BlockSpec((tm, hidden), lambda i: (i, 0))
    act_shape = jax.ShapeDtypeStruct((t, hidden), BF16)
    wspec = pl.BlockSpec((hidden, D), lambda i: (0, 0), pipeline_mode=pl.Buffered(1))
    row = pl.BlockSpec((tm, D), lambda i: (i, 0))
    return pl.pallas_call(
        body, name=name, grid=(t // tm,),
        in_specs=[row, pl.BlockSpec((1, D), lambda i: (0, 0)), wspec, wspec, wspec],
        out_specs=[row, act, act, act, row],
        out_shape=[jax.ShapeDtypeStruct((t, D), BF16), act_shape, act_shape, act_shape, jax.ShapeDtypeStruct((t, D), F32)],
        compiler_params=_params("arbitrary"),
    )(x, g, wg, wu, wd)


def ffn_bwd(dxo, a, b, x, g, wg, wu, wd, name, deps=()):
    t = x.shape[0]
    tm = _row_tile(t, FFN_BWD_ROWS)
    hidden = wg.shape[0]

    def body(dxo_ref, a_ref, b_ref, x_ref, g_ref, wg_ref, wu_ref, wd_ref, dx_ref, da_ref, db_ref, dg_ref):
        @pl.when(pl.program_id(0) == 0)
        def _():
            dg_ref[...] = jnp.zeros_like(dg_ref)

        dxo = dxo_ref[...]
        dxb = dxo.astype(BF16)
        acc = None
        for c0, cw in FFN_BWD_CHUNKS:
            cols = slice(c0, c0 + cw)
            ds = _dot_nt(dxb, wd_ref[cols, :])
            da = (ds * a_ref[:, cols].astype(F32)).astype(BF16)
            db = (ds * b_ref[:, cols].astype(F32)).astype(BF16)
            da_ref[:, cols] = da
            db_ref[:, cols] = db
            part = _dot(da, wg_ref[cols, :]) + _dot(db, wu_ref[cols, :])
            acc = part if acc is None else acc + part
        dx, dg = _rms_bwd(x_ref[...], g_ref[...], acc)
        dx_ref[...] = dxo + dx
        dg_ref[0:1, :] += dg

    act = pl.BlockSpec((tm, hidden), lambda i: (i, 0))
    act_shape = jax.ShapeDtypeStruct((t, hidden), BF16)
    row = pl.BlockSpec((tm, D), lambda i: (i, 0))
    wspec = pl.BlockSpec((hidden, D), lambda i: (0, 0), pipeline_mode=pl.Buffered(1))
    return pl.pallas_call(
        _after(body, 8, deps), name=name, grid=(t // tm,),
        in_specs=[row, act, act, row, pl.BlockSpec((1, D), lambda i: (0, 0)), wspec, wspec, wspec] + [ANY] * len(deps),
        out_specs=[row, act, act, pl.BlockSpec((8, D), lambda i: (0, 0))],
        out_shape=[jax.ShapeDtypeStruct((t, D), F32), act_shape, act_shape, jax.ShapeDtypeStruct((8, D), F32)],
        compiler_params=_params("arbitrary"),
    )(dxo, a, b, x, g, wg, wu, wd, *deps)


def _layer_norm_stats(v):
    mu = jnp.mean(v, axis=-1, keepdims=True)
    vc = v - mu
    rstd = lax.rsqrt(jnp.mean(vc * vc, axis=-1, keepdims=True) + EPS)
    return vc * rstd, rstd


def a_fwd(x, g, win, gv, wm, bs, wout, name):
    t = x.shape[0]
    tm = _row_tile(t, A_ROWS)
    n_shards, _, hs = win.shape

    def body(x_ref, g_ref, win_ref, gv_ref, wm_ref, bs_ref, wout_ref, h_ref, z_ref, vn_ref, y_ref, o_ref):
        xv = x_ref[...]
        xh, _ = _rms(xv)
        h = (xh * g_ref[...]).astype(BF16)
        h_ref[...] = h
        zs = []
        for s in range(n_shards):
            zb = _dot(h, win_ref[s]).astype(BF16)
            z_ref[:, s * hs:(s + 1) * hs] = zb
            zs.append(zb.astype(F32))
        half = n_shards // 2
        u = _gelu(jnp.concatenate(zs[:half], axis=1))
        vhat, _ = _layer_norm_stats(_gelu(jnp.concatenate(zs[half:], axis=1)))
        vnb = (vhat * gv_ref[...]).astype(BF16)
        vn_ref[...] = vnb
        for n in range(tm // GMLP_BLOCK):
            rows = slice(n * GMLP_BLOCK, (n + 1) * GMLP_BLOCK)
            for grp in range(A_GROUPS):
                cols = slice(grp * 128, (grp + 1) * 128)
                sv = _dot(wm_ref[grp], vnb[rows, cols]) + bs_ref[grp]
                y_ref[rows, cols] = (u[rows, cols] * sv).astype(BF16)
        o_ref[...] = xv + _dot(y_ref[...], wout_ref[...])

    small = pl.BlockSpec((A_GROUPS, 128, 128), lambda i: (0, 0, 0))
    row = pl.BlockSpec((tm, D), lambda i: (i, 0))
    wide = pl.BlockSpec((tm, 2 * D), lambda i: (i, 0))
    gain = pl.BlockSpec((1, D), lambda i: (0, 0))
    return pl.pallas_call(
        body, name=name, grid=(t // tm,),
        in_specs=[row, gain, pl.BlockSpec((n_shards, D, hs), lambda i: (0, 0, 0)), gain, small, small,
                  pl.BlockSpec((D, D), lambda i: (0, 0))],
        out_specs=[row, wide, row, row, row],
        out_shape=[jax.ShapeDtypeStruct((t, D), BF16), jax.ShapeDtypeStruct((t, 2 * D), BF16), jax.ShapeDtypeStruct((t, D), BF16),
                   jax.ShapeDtypeStruct((t, D), BF16), jax.ShapeDtypeStruct((t, D), F32)],
        compiler_params=_params("arbitrary"),
    )(x, g, win, gv, wm, bs, wout)


def a_bwd(dxm, x, g, z, vn, gv, wm, wmt, bs, wout, win, name):
    t = x.shape[0]
    tm = _row_tile(t, A_ROWS)
    n_shards, _, hs = win.shape

    def body(dxm_ref, x_ref, g_ref, z_ref, vn_ref, gv_ref, wm_ref, wmt_ref, bs_ref, wout_ref, win_ref,
             dx_ref, dz_ref, dwm_ref, dbs_ref, dgv_ref, dg_ref, du_ref, dvn_ref):
        @pl.when(pl.program_id(0) == 0)
        def _():
            dwm_ref[...] = jnp.zeros_like(dwm_ref)
            dbs_ref[...] = jnp.zeros_like(dbs_ref)
            dgv_ref[...] = jnp.zeros_like(dgv_ref)
            dg_ref[...] = jnp.zeros_like(dg_ref)

        dxm = dxm_ref[...]
        dyv = _dot_nt(dxm.astype(BF16), wout_ref[...])
        zz = z_ref[...].astype(F32)
        zu, zv = zz[:, :D], zz[:, D:]
        u = _gelu(zu)
        vhat, rstd = _layer_norm_stats(_gelu(zv))
        vnb = vn_ref[...]
        ones = jnp.ones((128, 128), BF16)
        for n in range(tm // GMLP_BLOCK):
            rows = slice(n * GMLP_BLOCK, (n + 1) * GMLP_BLOCK)
            for grp in range(A_GROUPS):
                cols = slice(grp * 128, (grp + 1) * 128)
                blk = vnb[rows, cols]
                sv = _dot(wm_ref[grp], blk) + bs_ref[grp]
                dyb = dyv[rows, cols]
                du_ref[rows, cols] = dyb * sv
                dsv = (dyb * u[rows, cols]).astype(BF16)
                dvn_ref[rows, cols] = _dot(wmt_ref[grp], dsv)
                dwm_ref[grp] += _dot_nt(dsv, blk)
                dbs_ref[grp] += _dot(dsv, ones)
        dvn = dvn_ref[...]
        dgv_ref[0:1, :] += jnp.sum(dvn * vhat, axis=0, keepdims=True)
        dvh = dvn * gv_ref[...]
        dv = rstd * (dvh - jnp.mean(dvh, axis=-1, keepdims=True) - vhat * jnp.mean(dvh * vhat, axis=-1, keepdims=True))
        dz_ref[:, :D] = (du_ref[...] * _gelu_grad(zu)).astype(BF16)
        dz_ref[:, D:] = (dv * _gelu_grad(zv)).astype(BF16)
        dh = None
        for s in range(n_shards):
            part = _dot_nt(dz_ref[:, s * hs:(s + 1) * hs], win_ref[s])
            dh = part if dh is None else dh + part
        dx, dg = _rms_bwd(x_ref[...], g_ref[...], dh)
        dx_ref[...] = dxm + dx
        dg_ref[0:1, :] += dg

    small = pl.BlockSpec((A_GROUPS, 128, 128), lambda i: (0, 0, 0))
    row = pl.BlockSpec((tm, D), lambda i: (i, 0))
    wide = pl.BlockSpec((tm, 2 * D), lambda i: (i, 0))
    gain = pl.BlockSpec((1, D), lambda i: (0, 0))
    sums = pl.BlockSpec((8, D), lambda i: (0, 0))
    small_shape = jax.ShapeDtypeStruct((A_GROUPS, 128, 128), F32)
    sums_shape = jax.ShapeDtypeStruct((8, D), F32)
    return pl.pallas_call(
        body, name=name, grid=(t // tm,),
        in_specs=[row, row, gain, wide, row, gain, small, small, small, pl.BlockSpec((D, D), lambda i: (0, 0)),
                  pl.BlockSpec((n_shards, D, hs), lambda i: (0, 0, 0))],
        out_specs=[row, wide, small, small, sums, sums],
        out_shape=[jax.ShapeDtypeStruct((t, D), F32), jax.ShapeDtypeStruct((t, 2 * D), BF16), small_shape, small_shape,
                   sums_shape, sums_shape],
        scratch_shapes=[pltpu.VMEM((tm, D), F32), pltpu.VMEM((tm, D), F32)],
        compiler_params=_params("arbitrary"),
    )(dxm, x, g, z, vn, gv, wm, wmt, bs, wout, win)


def _conv_terms(p_ref, row):
    gb = p_ref[0].astype(F32)
    gc = p_ref[1].astype(F32)
    xt = p_ref[2].astype(F32)
    q = gc * xt
    return gb, gc, xt, q, _shift_down(q, 1, row), _shift_down(q, 2, row)


def b_conv_fwd(p3, cw, seq, name):
    t = p3.shape[1]
    cb = 256

    def body(p_ref, cw_ref, y_ref):
        row = lax.broadcasted_iota(jnp.int32, (seq, cb), 0)
        gb, _, _, q, q1, q2 = _conv_terms(p_ref, row)
        y_ref[...] = (gb * (cw_ref[2:3, :] * q + cw_ref[1:2, :] * q1 + cw_ref[0:1, :] * q2)).astype(BF16)

    return pl.pallas_call(
        body, name=name, grid=(t // seq, D // cb),
        in_specs=[pl.BlockSpec((3, seq, cb), lambda e, c: (0, e, c)), pl.BlockSpec((3, cb), lambda e, c: (0, c))],
        out_specs=pl.BlockSpec((seq, cb), lambda e, c: (e, c)),
        out_shape=jax.ShapeDtypeStruct((t, D), BF16),
        compiler_params=_params("arbitrary", "arbitrary"),
    )(p3, cw)


def b_conv_bwd(dy, p3, cw, seq, name):
    t = p3.shape[1]
    cb = 256

    def body(dy_ref, p_ref, cw_ref, dp_ref, dcw_ref):
        @pl.when(pl.program_id(1) == 0)
        def _():
            dcw_ref[...] = jnp.zeros_like(dcw_ref)

        row = lax.broadcasted_iota(jnp.int32, (seq, cb), 0)
        gb, gc, xt, q, q1, q2 = _conv_terms(p_ref, row)
        dyv = dy_ref[...].astype(F32)
        conv = cw_ref[2:3, :] * q + cw_ref[1:2, :] * q1 + cw_ref[0:1, :] * q2
        dyc = dyv * gb
        dq = cw_ref[2:3, :] * dyc + cw_ref[1:2, :] * _shift_up(dyc, 1, row) + cw_ref[0:1, :] * _shift_up(dyc, 2, row)
        dp_ref[0] = (dyv * conv).astype(BF16)
        dp_ref[1] = (dq * xt).astype(BF16)
        dp_ref[2] = (dq * gc).astype(BF16)
        dcw_ref[0:1, :] += jnp.sum(dyc * q2, axis=0, keepdims=True)
        dcw_ref[1:2, :] += jnp.sum(dyc * q1, axis=0, keepdims=True)
        dcw_ref[2:3, :] += jnp.sum(dyc * q, axis=0, keepdims=True)

    return pl.pallas_call(
        body, name=name, grid=(D // cb, t // seq),
        in_specs=[pl.BlockSpec((seq, cb), lambda c, e: (e, c)), pl.BlockSpec((3, seq, cb), lambda c, e: (0, e, c)),
                  pl.BlockSpec((3, cb), lambda c, e: (0, c))],
        out_specs=[pl.BlockSpec((3, seq, cb), lambda c, e: (0, e, c)), pl.BlockSpec((8, cb), lambda c, e: (0, c))],
        out_shape=[jax.ShapeDtypeStruct((3, t, D), BF16), jax.ShapeDtypeStruct((8, D), F32)],
        compiler_params=_params("arbitrary", "arbitrary"),
    )(dy, p3, cw)


def c_pool_fwd(p, seq, name):
    t = p.shape[0]

    def make(grp):
        w = POOL_WINDOWS[grp]

        def body_g(p_ref, d_ref):
            row = lax.broadcasted_iota(jnp.int32, (seq, C_GROUP_DIM), 0)
            pv = p_ref[...].astype(F32)
            acc = pv
            sh = 1
            while sh < w:
                acc = acc + _shift_down(acc, sh, row)
                sh *= 2
            d_ref[...] = (acc / jnp.minimum(row + 1, w).astype(F32) - pv).astype(BF16)

        return body_g

    outs = []
    for grp in range(len(POOL_WINDOWS)):
        outs.append(pl.pallas_call(
            make(grp), name=f"{name}_g{grp}", grid=(t // seq,),
            in_specs=[pl.BlockSpec((seq, C_GROUP_DIM), lambda e, grp=grp: (e, grp))],
            out_specs=pl.BlockSpec((seq, C_GROUP_DIM), lambda e: (e, 0)),
            out_shape=jax.ShapeDtypeStruct((t, C_GROUP_DIM), BF16),
            compiler_params=_params("arbitrary"),
        )(p))
    return outs


def c_pool_bwd(dd, seq, name):
    t = dd[0].shape[0]

    def make(w):
        def body_g(dd_ref, dp_ref):
            row = lax.broadcasted_iota(jnp.int32, (seq, C_GROUP_DIM), 0)
            ddv = dd_ref[...]
            acc = ddv / jnp.minimum(row + 1, w).astype(F32)
            sh = 1
            while sh < w:
                acc = acc + _shift_up(acc, sh, row)
                sh *= 2
            dp_ref[...] = (acc - ddv).astype(BF16)

        return body_g

    outs = []
    for grp, w in enumerate(POOL_WINDOWS):
        outs.append(pl.pallas_call(
            make(w), name=f"{name}_g{grp}", grid=(t // seq,),
            in_specs=[pl.BlockSpec((seq, C_GROUP_DIM), lambda e: (e, 0))],
            out_specs=pl.BlockSpec((seq, C_GROUP_DIM), lambda e: (e, 0)),
            out_shape=jax.ShapeDtypeStruct((t, C_GROUP_DIM), BF16),
            compiler_params=_params("arbitrary"),
        )(dd[grp]))
    return outs


def c_out_fwd(d, wgrp, scale, wo, x, name):
    t = x.shape[0]
    tm = _row_tile(t, 512)
    n_g = len(POOL_WINDOWS)

    def body(d0, d1, d2, d3, wg_ref, sc_ref, wo_ref, x_ref, y_ref, o_ref):
        parts = [_dot(dr[...], wg_ref[grp]) for grp, dr in enumerate((d0, d1, d2, d3))]
        y = (jnp.concatenate(parts, axis=1) * sc_ref[...]).astype(BF16)
        y_ref[...] = y
        o_ref[...] = x_ref[...] + _dot(y, wo_ref[...])

    dspec = pl.BlockSpec((tm, C_GROUP_DIM), lambda i: (i, 0))
    row = pl.BlockSpec((tm, D), lambda i: (i, 0))
    return pl.pallas_call(
        body, name=name, grid=(t // tm,),
        in_specs=[dspec] * n_g + [pl.BlockSpec((n_g, C_GROUP_DIM, C_GROUP_DIM), lambda i: (0, 0, 0)),
                                  pl.BlockSpec((1, D), lambda i: (0, 0)), pl.BlockSpec((D, D), lambda i: (0, 0)), row],
        out_specs=[row, row],
        out_shape=[jax.ShapeDtypeStruct((t, D), BF16), jax.ShapeDtypeStruct((t, D), F32)],
        compiler_params=_params("arbitrary"),
    )(*d, wgrp, scale, wo, x)


def c_out_bwd(dxm, d, wgrp, scale, wo, name):
    t = dxm.shape[0]
    tm = _row_tile(t, 512)
    n_g = len(POOL_WINDOWS)

    def body(dxm_ref, d0, d1, d2, d3, wg_ref, sc_ref, wo_ref, dyp_ref, dd0, dd1, dd2, dd3, dsc_ref):
        @pl.when(pl.program_id(0) == 0)
        def _():
            dsc_ref[...] = jnp.zeros_like(dsc_ref)

        dyo = _dot_nt(dxm_ref[...].astype(BF16), wo_ref[...])
        ypre = jnp.concatenate([_dot(dr[...], wg_ref[grp]) for grp, dr in enumerate((d0, d1, d2, d3))], axis=1)
        dsc_ref[0:1, :] += jnp.sum(dyo * ypre, axis=0, keepdims=True)
        dyp = (dyo * sc_ref[...]).astype(BF16)
        dyp_ref[...] = dyp
        for grp, ddr in enumerate((dd0, dd1, dd2, dd3)):
            ddr[...] = _dot_nt(dyp[:, grp * C_GROUP_DIM:(grp + 1) * C_GROUP_DIM], wg_ref[grp])

    dspec = pl.BlockSpec((tm, C_GROUP_DIM), lambda i: (i, 0))
    row = pl.BlockSpec((tm, D), lambda i: (i, 0))
    dshape = jax.ShapeDtypeStruct((t, C_GROUP_DIM), F32)
    return pl.pallas_call(
        body, name=name, grid=(t // tm,),
        in_specs=[row] + [dspec] * n_g + [pl.BlockSpec((n_g, C_GROUP_DIM, C_GROUP_DIM), lambda i: (0, 0, 0)),
                                          pl.BlockSpec((1, D), lambda i: (0, 0)), pl.BlockSpec((D, D), lambda i: (0, 0))],
        out_specs=[row] + [dspec] * n_g + [pl.BlockSpec((8, D), lambda i: (0, 0))],
        out_shape=[jax.ShapeDtypeStruct((t, D), BF16)] + [dshape] * n_g + [jax.ShapeDtypeStruct((8, D), F32)],
        compiler_params=_params("arbitrary"),
    )(dxm, *d, wgrp, scale, wo)


def loss_head(x, tgt, g, name):
    t = x.shape[0]
    tm = _row_tile(t, 512)

    def body(x_ref, t_ref, g_ref, dx_ref, dg_ref, loss_ref):
        @pl.when(pl.program_id(0) == 0)
        def _():
            dg_ref[...] = jnp.zeros_like(dg_ref)
            loss_ref[...] = jnp.zeros_like(loss_ref)

        xv, gvv = x_ref[...], g_ref[...]
        xh, _ = _rms(xv)
        diff = xh * gvv - t_ref[...]
        loss_ref[...] += 0.5 * jnp.sum(jnp.mean(diff * diff, axis=-1, keepdims=True))
        dx, dg = _rms_bwd(xv, gvv, diff * (1.0 / D))
        dx_ref[...] = dx
        dg_ref[0:1, :] += dg

    row = pl.BlockSpec((tm, D), lambda i: (i, 0))
    return pl.pallas_call(
        body, name=name, grid=(t // tm,),
        in_specs=[row, row, pl.BlockSpec((1, D), lambda i: (0, 0))],
        out_specs=[row, pl.BlockSpec((8, D), lambda i: (0, 0)), pl.BlockSpec((8, 128), lambda i: (0, 0))],
        out_shape=[jax.ShapeDtypeStruct((t, D), F32), jax.ShapeDtypeStruct((8, D), F32), jax.ShapeDtypeStruct((8, 128), F32)],
        compiler_params=_params("arbitrary"),
    )(x, tgt, g)


def adamw(w, g, m, v, name):
    rows, cols = w.shape
    tr = rows
    for cand in (512, 256, 128, 64, 32, 16, 8):
        if rows % cand == 0 and rows > cand:
            tr = cand
            break

    def body(w_ref, g_ref, m_ref, v_ref, d_ref, mo_ref, vo_ref):
        gv = g_ref[...]
        mn = ADAM_B1 * m_ref[...] + (1.0 - ADAM_B1) * gv
        vn = ADAM_B2 * v_ref[...] + (1.0 - ADAM_B2) * (gv * gv)
        m_hat = mn / (1.0 - ADAM_B1 ** ADAM_STEP)
        v_hat = vn / (1.0 - ADAM_B2 ** ADAM_STEP)
        d_ref[...] = -ADAM_LR * (m_hat / (jnp.sqrt(v_hat) + ADAM_EPS) + ADAM_WD * w_ref[...])
        mo_ref[...] = mn
        vo_ref[...] = vn

    spec = pl.BlockSpec((tr, cols), lambda i: (i, 0))
    shape = jax.ShapeDtypeStruct((rows, cols), F32)
    return pl.pallas_call(
        body, name=name, grid=(rows // tr,),
        in_specs=[spec] * 4, out_specs=[spec] * 3, out_shape=[shape] * 3,
        compiler_params=_params("arbitrary"),
    )(w, g, m, v)


def adamw_layer(w, m, v, own, recv, core, layer, carried, name):
    n_layers, rows, cols = w.shape
    h = rows // 2

    def body(core_ref, w_ref, m_ref, v_ref, own_ref, recv_ref, *rest):
        g_ref, d_ref, mo_ref, vo_ref = rest[-4:]
        gv = jnp.where(pl.program_id(0) == core_ref[0], own_ref[...], recv_ref[...])
        mn = ADAM_B1 * m_ref[0] + (1.0 - ADAM_B1) * gv
        vn = ADAM_B2 * v_ref[0] + (1.0 - ADAM_B2) * (gv * gv)
        m_hat = mn / (1.0 - ADAM_B1 ** ADAM_STEP)
        v_hat = vn / (1.0 - ADAM_B2 ** ADAM_STEP)
        g_ref[0] = gv
        d_ref[0] = -ADAM_LR * (m_hat / (jnp.sqrt(v_hat) + ADAM_EPS) + ADAM_WD * w_ref[0])
        mo_ref[0] = mn
        vo_ref[0] = vn

    steps = 1
    tr = h // steps
    stacked = pl.BlockSpec((1, tr, cols), lambda half, j, core_ref: (layer, half * steps + j, 0))
    halfspec = pl.BlockSpec((tr, cols), lambda half, j, core_ref: (j, 0))
    n_carried = 0 if carried is None else 4
    shape = jax.ShapeDtypeStruct(w.shape, F32)
    return pl.pallas_call(
        body, name=name,
        grid_spec=pltpu.PrefetchScalarGridSpec(
            num_scalar_prefetch=1, grid=(2, steps),
            in_specs=[stacked] * 3 + [halfspec] * 2 + [ANY] * n_carried, out_specs=[stacked] * 4),
        out_shape=[shape] * 4,
        input_output_aliases={6 + i: i for i in range(n_carried)},
        compiler_params=_params("arbitrary", "arbitrary"),
    )(core, w, m, v, own, recv, *(carried or ()))


def add_halves(gs, ps, core, name, deps=()):
    n = len(gs)

    def body(core_ref, *refs):
        for i in range(n):
            refs[2 * n + i][...] = (refs[i][...].astype(F32) + refs[n + i][...].astype(F32)).astype(BF16)

    in_specs, out_specs, out_shape = [], [], []
    for gt in gs:
        h, c = gt.shape[1] // 2, gt.shape[2]
        in_specs.append(pl.BlockSpec((1, h, c), lambda b, core_ref: (b, core_ref[0], 0)))
    for gt in gs:
        h, c = gt.shape[1] // 2, gt.shape[2]
        in_specs.append(pl.BlockSpec((1, h, c), lambda b, core_ref: (b, 0, 0)))
        out_specs.append(pl.BlockSpec((1, h, c), lambda b, core_ref: (b, 0, 0)))
        out_shape.append(jax.ShapeDtypeStruct((N_CHIPS, h, c), BF16))
    in_specs += [ANY] * len(deps)
    return pl.pallas_call(
        _after(body, 1 + 2 * n, deps), name=name,
        grid_spec=pltpu.PrefetchScalarGridSpec(num_scalar_prefetch=1, grid=(N_CHIPS,), in_specs=in_specs, out_specs=out_specs),
        out_shape=out_shape, compiler_params=_params("arbitrary"),
    )(core, *gs, *ps, *deps)


def add_final(hs, qs, chip, name, deps=()):
    n = len(hs)

    def body(chip_ref, *refs):
        for i in range(n):
            q = refs[n + i]
            refs[2 * n + i][...] = ((refs[i][0].astype(F32) + q[0].astype(F32)) + q[1].astype(F32)) + q[2].astype(F32)

    steps = 2
    in_specs, out_specs, out_shape = [], [], []
    for ht in hs:
        h, c = ht.shape[1], ht.shape[2]
        in_specs.append(pl.BlockSpec((1, h // steps, c), lambda i, chip_ref: (chip_ref[0], i, 0)))
    for ht in hs:
        h, c = ht.shape[1], ht.shape[2]
        in_specs.append(pl.BlockSpec((N_CHIPS - 1, h // steps, c), lambda i, chip_ref: (0, i, 0)))
        out_specs.append(pl.BlockSpec((h // steps, c), lambda i, chip_ref: (i, 0)))
        out_shape.append(jax.ShapeDtypeStruct((h, c), F32))
    in_specs += [ANY] * len(deps)
    return pl.pallas_call(
        _after(body, 1 + 2 * n, deps), name=name,
        grid_spec=pltpu.PrefetchScalarGridSpec(num_scalar_prefetch=1, grid=(steps,), in_specs=in_specs, out_specs=out_specs),
        out_shape=out_shape, compiler_params=_params("arbitrary"),
    )(chip, *hs, *qs, *deps)


def sum_devices(own, gathered, me, name, deps=()):
    rows = own.shape[0]

    def body(me_ref, own_ref, g_ref, o_ref):
        me_dev = me_ref[0]
        acc = None
        for dev in range(8):
            slot = jnp.maximum((me_dev ^ dev) - 1, 0)
            term = jnp.where(me_dev == dev, own_ref[...], g_ref[slot]).astype(F32)
            acc = term if acc is None else acc + term
        o_ref[...] = acc

    tr = ([c for c in range(80, 0, -16) if rows % c == 0] or [rows])[0]
    return pl.pallas_call(
        _after(body, 3, deps), name=name,
        grid_spec=pltpu.PrefetchScalarGridSpec(
            num_scalar_prefetch=1, grid=(rows // tr,),
            in_specs=[pl.BlockSpec((tr, D), lambda i, me_ref: (i, 0)), pl.BlockSpec((7, tr, D), lambda i, me_ref: (0, i, 0))]
            + [ANY] * len(deps),
            out_specs=pl.BlockSpec((tr, D), lambda i, me_ref: (i, 0))),
        out_shape=jax.ShapeDtypeStruct((rows, D), F32),
        compiler_params=_params("arbitrary"),
    )(me, own, gathered, *deps)


def _mesh_pos():
    return lax.axis_index("x"), lax.axis_index("y"), lax.axis_index("c")


def _other_chips(x, y):
    return [(1 - x, y), (x, 1 - y), (1 - x, 1 - y)]


def _sibling():
    x, y, c = _mesh_pos()
    return [(x, y, 1 - c)]


def _same_core_of_other_chips():
    x, y, c = _mesh_pos()
    return [(cx, cy, c) for (cx, cy) in _other_chips(x, y)]


def _on_sequencer(body, name, operands, out_shapes, sems, peers, collective_id, deps=()):
    ordered = _after(body, len(operands), deps)

    def seq_body(*refs):
        barrier = pltpu.get_barrier_semaphore()
        with_whom = peers()
        for peer in with_whom:
            pl.semaphore_signal(barrier, inc=1, device_id=peer, device_id_type=MESH)
        pl.semaphore_wait(barrier, len(with_whom))
        ordered(*refs)

    return pl.kernel(
        seq_body, name=name, out_type=out_shapes,
        mesh=plsc.ScalarSubcoreMesh(axis_name="seq", num_cores=1),
        scratch_types=sems, compiler_params=pltpu.CompilerParams(collective_id=collective_id, has_side_effects=True),
    )(*operands, *deps)


def all_gather_weights(shards, name, deps=()):
    n = len(shards)

    def body(*refs):
        ins, outs = refs[:n], refs[n:2 * n]
        send, recv, fsend, frecv = refs[2 * n:]
        x, y, c = _mesh_pos()
        k = 2 * x + y
        chips = _other_chips(x, y)

        def half(ref, i, rows_half):
            h = shards[i].shape[0] // 2
            return ref.at[pl.ds(pl.multiple_of(rows_half * h, 8), h), :]

        first = []
        for i in range(n):
            for j, (cx, cy) in enumerate(chips):
                first.append(pltpu.make_async_remote_copy(
                    src_ref=half(ins[i], i, c), dst_ref=half(outs[i].at[k], i, c),
                    send_sem=send.at[i, j], recv_sem=recv.at[i, j], device_id=(cx, cy, c), device_id_type=MESH))
        for cp in first:
            cp.start()
        passed = []
        for i in range(n):
            for j, (cx, cy) in enumerate(chips):
                blk = half(outs[i].at[2 * cx + cy], i, c)
                pltpu.make_async_remote_copy(src_ref=blk, dst_ref=blk, send_sem=send.at[i, j], recv_sem=recv.at[i, j],
                                             device_id=(cx, cy, c), device_id_type=MESH).wait_recv()
                fw = pltpu.make_async_remote_copy(src_ref=blk, dst_ref=blk, send_sem=fsend.at[i, j], recv_sem=frecv.at[i, j],
                                                  device_id=(x, y, 1 - c), device_id_type=MESH)
                fw.start()
                passed.append(fw)
        for i in range(n):
            for j, (cx, cy) in enumerate(chips):
                blk = half(outs[i].at[2 * cx + cy], i, 1 - c)
                pltpu.make_async_remote_copy(src_ref=blk, dst_ref=blk, send_sem=fsend.at[i, j], recv_sem=frecv.at[i, j],
                                             device_id=(x, y, 1 - c), device_id_type=MESH).wait_recv()
        for cp in first + passed:
            cp.wait_send()

    def peers():
        x, y, c = _mesh_pos()
        return [(cx, cy, c) for (cx, cy) in _other_chips(x, y)] + [(x, y, 1 - c)]

    return _on_sequencer(
        body, name, shards, [jax.ShapeDtypeStruct((N_CHIPS,) + s.shape, s.dtype) for s in shards],
        [pltpu.SemaphoreType.DMA((n, 3))] * 4, peers, GATHER_COLLECTIVE_ID, deps)


def place_own(gathered, shards, chip, name):
    n = len(shards)

    def body(chip_ref, *refs):
        for i in range(n):
            refs[2 * n + i][0] = refs[i][...]

    in_specs = [pl.BlockSpec(s.shape, lambda i, chip_ref: (0, 0)) for s in shards] + [ANY] * n
    out_specs = [pl.BlockSpec((1,) + s.shape, lambda i, chip_ref: (chip_ref[0], 0, 0)) for s in shards]
    return pl.pallas_call(
        body, name=name,
        grid_spec=pltpu.PrefetchScalarGridSpec(num_scalar_prefetch=1, grid=(1,), in_specs=in_specs, out_specs=out_specs),
        out_shape=[jax.ShapeDtypeStruct(g.shape, g.dtype) for g in gathered],
        input_output_aliases={1 + n + i: i for i in range(n)},
        compiler_params=_params("arbitrary"),
    )(chip, *shards, *gathered)


def all_gather_rows(shard, name):
    def body(in_ref, out_ref, send, recv, lsem):
        x, y, c = _mesh_pos()
        k = 2 * x + y
        chips = _other_chips(x, y)
        local = pltpu.make_async_copy(in_ref, out_ref.at[k], lsem)
        local.start()
        sent = [pltpu.make_async_remote_copy(src_ref=in_ref, dst_ref=out_ref.at[k], send_sem=send.at[j], recv_sem=recv.at[j],
                                             device_id=(cx, cy, c), device_id_type=MESH) for j, (cx, cy) in enumerate(chips)]
        for cp in sent:
            cp.start()
        for j, (cx, cy) in enumerate(chips):
            blk = out_ref.at[2 * cx + cy]
            pltpu.make_async_remote_copy(src_ref=blk, dst_ref=blk, send_sem=send.at[j], recv_sem=recv.at[j],
                                         device_id=(cx, cy, c), device_id_type=MESH).wait_recv()
        for cp in sent:
            cp.wait_send()
        local.wait()

    return pl.pallas_call(
        body, name=name, in_specs=[ANY], out_specs=ANY,
        out_shape=jax.ShapeDtypeStruct((N_CHIPS,) + shard.shape, shard.dtype),
        scratch_shapes=[pltpu.SemaphoreType.DMA((3,)), pltpu.SemaphoreType.DMA((3,)), pltpu.SemaphoreType.DMA],
    )(shard)


def swap_halves(gs, name):
    n = len(gs)

    def body(*refs):
        ins, outs = refs[:n], refs[n:2 * n]
        send, recv = refs[2 * n:]
        x, y, c = _mesh_pos()
        sent = []
        for i in range(n):
            h = gs[i].shape[1] // 2
            src = ins[i].at[:, pl.ds(pl.multiple_of((1 - c) * h, 8), h), :]
            cp = pltpu.make_async_remote_copy(src_ref=src, dst_ref=outs[i], send_sem=send.at[i], recv_sem=recv.at[i],
                                              device_id=(x, y, 1 - c), device_id_type=MESH)
            cp.start()
            sent.append(cp)
        for cp in sent:
            cp.wait()

    return _on_sequencer(
        body, name, gs, [jax.ShapeDtypeStruct((N_CHIPS, g.shape[1] // 2, g.shape[2]), g.dtype) for g in gs],
        [pltpu.SemaphoreType.DMA((n,)), pltpu.SemaphoreType.DMA((n,))], _sibling, SIBLING_COLLECTIVE_ID)


def scatter_chips(hs, name):
    n = len(hs)

    def body(*refs):
        ins, outs = refs[:n], refs[n:2 * n]
        send, recv = refs[2 * n:]
        x, y, c = _mesh_pos()
        chips = _other_chips(x, y)
        sent = []
        for i in range(n):
            for j, (cx, cy) in enumerate(chips):
                cp = pltpu.make_async_remote_copy(src_ref=ins[i].at[2 * cx + cy], dst_ref=outs[i].at[j],
                                                  send_sem=send.at[i, j], recv_sem=recv.at[i, j],
                                                  device_id=(cx, cy, c), device_id_type=MESH)
                cp.start()
                sent.append(cp)
        for cp in sent:
            cp.wait()

    return _on_sequencer(
        body, name, hs, [jax.ShapeDtypeStruct((N_CHIPS - 1,) + h.shape[1:], h.dtype) for h in hs],
        [pltpu.SemaphoreType.DMA((n, 3)), pltpu.SemaphoreType.DMA((n, 3))], _same_core_of_other_chips, CHIPS_COLLECTIVE_ID)


def swap_reduced(rs, name):
    n = len(rs)

    def body(*refs):
        ins, outs = refs[:n], refs[n:2 * n]
        send, recv = refs[2 * n:]
        x, y, c = _mesh_pos()
        sent = []
        for i in range(n):
            cp = pltpu.make_async_remote_copy(src_ref=ins[i], dst_ref=outs[i], send_sem=send.at[i], recv_sem=recv.at[i],
                                              device_id=(x, y, 1 - c), device_id_type=MESH)
            cp.start()
            sent.append(cp)
        for cp in sent:
            cp.wait()

    return _on_sequencer(
        body, name, rs, [jax.ShapeDtypeStruct(r.shape, r.dtype) for r in rs],
        [pltpu.SemaphoreType.DMA((n,)), pltpu.SemaphoreType.DMA((n,))], _sibling, SIBLING_COLLECTIVE_ID)


def all_gather_devices(parts, name):
    n = len(parts)

    def everyone_else():
        x, y, c = _mesh_pos()
        return [(1 - x if (rel >> 2) & 1 else x, 1 - y if (rel >> 1) & 1 else y, 1 - c if rel & 1 else c) for rel in range(1, 8)]

    def body(*refs):
        ins, outs = refs[:n], refs[n:2 * n]
        send, recv = refs[2 * n:]
        sent = []
        for slot, peer in enumerate(everyone_else()):
            for i in range(n):
                cp = pltpu.make_async_remote_copy(src_ref=ins[i], dst_ref=outs[i].at[slot], send_sem=send.at[i, slot],
                                                  recv_sem=recv.at[i, slot], device_id=peer, device_id_type=MESH)
                cp.start()
                sent.append(cp)
        for cp in sent:
            cp.wait()

    return _on_sequencer(
        body, name, parts, [jax.ShapeDtypeStruct((7,) + p.shape, p.dtype) for p in parts],
        [pltpu.SemaphoreType.DMA((n, 7)), pltpu.SemaphoreType.DMA((n, 7))], everyone_else, ALL_COLLECTIVE_ID)


class ReduceScatter:
    def __init__(self, grads, names, layer, core, chip, name):
        self.grads, self.names, self.layer, self.core, self.chip, self.name = grads, names, layer, core, chip, name
        self.from_sibling = swap_halves(grads, name + "_swap")

    def sum_cores(self, deps=()):
        self.core_sums = add_halves(self.grads, self.from_sibling, self.core, self.name + "_add2", deps)
        self.from_chips = scatter_chips(self.core_sums, self.name + "_scatter")
        return self.core_sums[0]

    def sum_chips(self, deps=()):
        self.half_sums = add_final(self.core_sums, self.from_chips, self.chip, self.name + "_add4", deps)
        self.other_half = swap_reduced(self.half_sums, self.name + "_join")
        return self.half_sums[0]


def _blocked(w):
    return w.reshape(w.shape[0] * w.shape[1], w.shape[2])


def _grp_from_blocks(w):
    return w.reshape(N_CHIPS, 4, 64, C_GROUP_DIM).transpose(1, 0, 2, 3).reshape(4, C_GROUP_DIM, C_GROUP_DIM)


def _grp_to_blocks(w):
    return w.reshape(4, N_CHIPS, 64, C_GROUP_DIM).transpose(1, 0, 2, 3).reshape(N_CHIPS, C_GROUP_DIM, C_GROUP_DIM)


def _dw_cols(h, dact, hs, name, deps):
    return mm_tn(h[None], dact, (N_CHIPS, D, hs), D, hs, N_CHIPS, lambda j: (0, 0), lambda j: (0, j), lambda j: (j, 0, 0), name, deps)


def _dw_rows(y, dxm, name, deps):
    tm = 512
    out = mm_tn(y[None], dxm[None], (1, D, D), tm, D, D // tm, lambda j: (0, j), lambda j: (0, 0), lambda j: (0, j, 0), name, deps)
    return out.reshape(N_CHIPS, D // N_CHIPS, D)


def _dw_hidden(act, other, name, deps):
    tm, tn = FFN_HIDDEN // 2, (D if other.dtype == BF16 else 256)
    n_n = D // tn
    out = mm_tn(act[None], other[None], (1, FFN_HIDDEN, D), tm, tn, 2 * n_n,
                lambda j: (0, j // n_n), lambda j: (0, j % n_n), lambda j: (0, j // n_n, j % n_n), name, deps)
    return out.reshape(N_CHIPS, FFN_SHARD, D)


def kernel(x, norm_mix_g, norm_ffn_g, final_norm_g, a_w_in, a_v_norm_g, a_w_s, a_b_s, a_w_out, b_w_in, b_conv_w, b_w_out, c_w_in, c_w_grp, c_scale, c_w_out, f_w_gate, f_w_up, f_w_down, loss_target, m_norm_mix_g, m_norm_ffn_g, m_final_norm_g, m_a_w_in, m_a_v_norm_g, m_a_w_s, m_a_b_s, m_a_w_out, m_b_w_in, m_b_conv_w, m_b_w_out, m_c_w_in, m_c_w_grp, m_c_scale, m_c_w_out, m_f_w_gate, m_f_w_up, m_f_w_down, v_norm_mix_g, v_norm_ffn_g, v_final_norm_g, v_a_w_in, v_a_v_norm_g, v_a_w_s, v_a_b_s, v_a_w_out, v_b_w_in, v_b_conv_w, v_b_w_out, v_c_w_in, v_c_w_grp, v_c_scale, v_c_w_out, v_f_w_gate, v_f_w_up, v_f_w_down):
    n_ex, seq, _ = x.shape
    t = n_ex * seq
    xi, yi, ci = lax.axis_index("x"), lax.axis_index("y"), lax.axis_index("c")
    chip = (2 * xi + yi).astype(jnp.int32)
    core_arr = ci.astype(jnp.int32).reshape(1)
    chip_arr = chip.reshape(1)
    me_arr = (4 * xi + 2 * yi + ci).astype(jnp.int32).reshape(1)
    bf = lambda w: w.astype(BF16)

    pad8 = lambda v: jnp.pad(v, ((0, 8 - v.shape[0]), (0, 0)))
    small_rows = jnp.concatenate([pad8(a_v_norm_g), pad8(b_conv_w[0]), pad8(c_scale)], axis=0)
    small_gathered = all_gather_rows(small_rows, "ag_small")
    small_full = small_gathered.transpose(1, 0, 2).reshape(24, D)
    gv_full = [small_full[0:1], small_full[1:2]]
    cw_full = small_full[8:11]
    scale_full = small_full[16:17]

    mixer_shards = [
        [bf(a_w_in[0]), bf(a_w_out[0])],
        [bf(b_w_in[0]), bf(b_w_out[0])],
        [bf(c_w_in[0]), bf(c_w_grp[0]).reshape(C_GROUP_DIM, C_GROUP_DIM), bf(c_w_out[0])],
        [bf(a_w_in[1]), bf(a_w_out[1])],
    ]
    hidden_major = lambda w: jnp.swapaxes(w, 1, 2)
    gate_t, up_t = hidden_major(f_w_gate), hidden_major(f_w_up)
    gathered = []
    for i in range(4):
        ffn_shards = [bf(gate_t[i]), bf(up_t[i]), bf(f_w_down[i])]
        if i == 0:
            parts = [(mixer_shards[0], "ag_l0_mixer", [small_gathered]), (ffn_shards, "ag_l0_ffn", [])]
        else:
            parts = [(mixer_shards[i] + ffn_shards, f"ag_l{i}", [])]
        layer = []
        for shards, name, deps in parts:
            layer += place_own(all_gather_weights(shards, name, deps), shards, chip_arr, name.replace("ag", "own"))
        gathered.append(layer)

    mask = (jnp.arange(GMLP_BLOCK)[None, :] // 64) <= (jnp.arange(GMLP_BLOCK)[:, None] // 64)
    gmix = [norm_mix_g[i:i + 1] for i in range(4)]
    gffn = [norm_ffn_g[i:i + 1] for i in range(4)]
    b_chunks = [(j // 3, (j % 3) * 256, 256, j // 4, (j % 4) * 256) for j in range(12)]
    c_chunks = [(0, 0, D, 0, 0)]

    xs = [x.reshape(t, D)]
    saved = []
    for i in range(4):
        ws = gathered[i]
        wg, wu, wd = (w.reshape(FFN_HIDDEN, D) for w in ws[-3:])
        xin = xs[-1]
        if i in (0, 3):
            j = 0 if i == 0 else 1
            win, wout = ws[0], _blocked(ws[1])
            wm32 = jnp.where(mask[None], a_w_s[j], 0.0)
            wm, wmt = bf(wm32), bf(wm32.transpose(0, 2, 1))
            bs = jnp.broadcast_to(a_b_s[j][:, :, None], (A_GROUPS, GMLP_BLOCK, 128))
            h, z, vn, y, xmid = a_fwd(xin, gmix[i], win, gv_full[j], wm, bs, wout, f"a_fwd_l{i}")
            saved.append(dict(h=h, z=z, y=y, vn=vn, win=win, wout=wout, wm=wm, wmt=wmt, bs=bs, gv=gv_full[j]))
        elif i == 1:
            win, wout = ws[0], _blocked(ws[1])
            h, p3 = norm_mm(xin, gmix[i], win, b_chunks, 3, D, "b_in")
            y = b_conv_fwd(p3, cw_full, seq, "b_conv")
            xmid = mm_res(y, wout, xin, "b_out")
            saved.append(dict(h=h, p3=p3, y=y, win=win, wout=wout))
        else:
            win, wgrp, wout = _blocked(ws[0])[None], _grp_from_blocks(ws[1]), _blocked(ws[2])
            h, p = norm_mm(xin, gmix[i], win, c_chunks, 1, D, "c_in")
            dpool = c_pool_fwd(p[0], seq, "c_pool")
            y, xmid = c_out_fwd(dpool, wgrp, scale_full, wout, xin, "c_out")
            saved.append(dict(h=h, d=dpool, y=y, win=win, wgrp=wgrp, wout=wout))
        h2, fa, fb, fs, xout = ffn_fwd(xmid, gffn[i], wg, wu, wd, f"ffn_l{i}")
        saved[-1].update(h2=h2, fa=fa, fb=fb, fs=fs, xmid=xmid, wg=wg, wu=wu, wd=wd)
        xs.append(xout)

    dx, dg_final, loss_part = loss_head(xs[4], loss_target.reshape(t, D), final_norm_g[None], "loss_head")
    loss = lax.psum(loss_part[0, 0], ("x", "y", "c"))

    weights = dict(norm_mix_g=norm_mix_g, norm_ffn_g=norm_ffn_g, final_norm_g=final_norm_g, a_w_in=a_w_in, a_v_norm_g=a_v_norm_g,
                   a_w_s=a_w_s, a_b_s=a_b_s, a_w_out=a_w_out, b_w_in=b_w_in, b_conv_w=b_conv_w, b_w_out=b_w_out, c_w_in=c_w_in,
                   c_w_grp=c_w_grp, c_scale=c_scale, c_w_out=c_w_out, f_w_gate=f_w_gate, f_w_up=f_w_up, f_w_down=f_w_down)
    m_in = dict(norm_mix_g=m_norm_mix_g, norm_ffn_g=m_norm_ffn_g, final_norm_g=m_final_norm_g, a_w_in=m_a_w_in, a_v_norm_g=m_a_v_norm_g,
                a_w_s=m_a_w_s, a_b_s=m_a_b_s, a_w_out=m_a_w_out, b_w_in=m_b_w_in, b_conv_w=m_b_conv_w, b_w_out=m_b_w_out, c_w_in=m_c_w_in,
                c_w_grp=m_c_w_grp, c_scale=m_c_scale, c_w_out=m_c_w_out, f_w_gate=m_f_w_gate, f_w_up=m_f_w_up, f_w_down=m_f_w_down)
    v_in = dict(norm_mix_g=v_norm_mix_g, norm_ffn_g=v_norm_ffn_g, final_norm_g=v_final_norm_g, a_w_in=v_a_w_in, a_v_norm_g=v_a_v_norm_g,
                a_w_s=v_a_w_s, a_b_s=v_a_b_s, a_w_out=v_a_w_out, b_w_in=v_b_w_in, b_conv_w=v_b_conv_w, b_w_out=v_b_w_out, c_w_in=v_c_w_in,
                c_w_grp=v_c_w_grp, c_scale=v_c_scale, c_w_out=v_c_w_out, f_w_gate=v_f_w_gate, f_w_up=v_f_w_up, f_w_down=v_f_w_down)
    grp_rows = lambda a: a.reshape(1, C_GROUP_DIM, C_GROUP_DIM)
    same = lambda a: a
    to_stacked = {nme: same for nme in ("a_w_in", "a_w_out", "b_w_in", "b_w_out", "c_w_in", "c_w_out", "f_w_down")}
    to_stacked.update(f_w_gate=hidden_major, f_w_up=hidden_major, c_w_grp=grp_rows)
    from_stacked = dict(to_stacked, c_w_grp=lambda a: a.reshape(c_w_grp.shape))
    layer_tensors = {0: ["a_w_out", "a_w_in"], 1: ["b_w_out", "b_w_in"], 2: ["c_w_out", "c_w_in", "c_w_grp"], 3: ["a_w_out", "a_w_in"]}
    carried = {}

    def bs_rows(v):
        return jnp.pad(v[:, :, 0].reshape(1, D), ((0, 7), (0, 0)))

    def update(unit):
        done = []
        for pos, nme in enumerate(unit.names):
            stacked_layer = unit.layer if nme.startswith("f_") else (unit.layer // 3 if nme.startswith("a_") else 0)
            view = to_stacked[nme]
            carried[nme] = adamw_layer(view(weights[nme]), view(m_in[nme]), view(v_in[nme]), unit.half_sums[pos], unit.other_half[pos],
                                       core_arr, stacked_layer, carried.get(nme), f"adamw_{nme}_l{unit.layer}")
            done.append(carried[nme][0])
        return done

    ffn_names = ["f_w_gate", "f_w_up", "f_w_down"]
    dg_mix, dg_ffn = [None] * 4, [None] * 4
    small = {}
    newer = older = None
    for i in (3, 2, 1, 0):
        sv = saved[i]
        xin = xs[i]
        deps = ([newer.grads[0]] if newer else []) + ([older.half_sums[0]] if older else [])
        dxm, da, db, dg_ffn[i] = ffn_bwd(dx, sv["fa"], sv["fb"], sv["xmid"], gffn[i], sv["wg"], sv["wu"], sv["wd"], f"ffn_bwd_l{i}", deps)
        last = [newer.sum_cores([dxm])] if newer else []
        g_gate = _dw_hidden(da, sv["h2"], f"dw_gate_l{i}", last)
        g_up = _dw_hidden(db, sv["h2"], f"dw_up_l{i}", [g_gate])
        g_down = _dw_hidden(sv["fs"], dx, f"dw_down_l{i}", [g_up])
        last_ffn = [g_down]
        if i == 0:
            ffn_unit = ReduceScatter([g_down, g_gate, g_up], ["f_w_down", "f_w_gate", "f_w_up"], 0, core_arr, chip_arr, "rs_l0_ffn")
        if i in (0, 3):
            j = 0 if i == 0 else 1
            dx, dz, dwm, dbs, dgv, dg_mix[i] = a_bwd(dxm, xin, gmix[i], sv["z"], sv["vn"], sv["gv"], sv["wm"], sv["wmt"], sv["bs"],
                                                     sv["wout"], sv["win"], f"a_bwd_l{i}")
            if i == 0:
                early = [sum_devices(p, gth, me_arr, f"sum_small_grads_l123_{n}")
                         for n, (p, gth) in enumerate(zip(early_parts, early_gathered))]
                newer.sum_chips([dz] + update(older) + early)
                last_ffn.append(ffn_unit.sum_cores([newer.half_sums[0]]))
            dz = dz[None]
            g_in = _dw_cols(sv["h"], dz, 512, f"dw_a_in_l{i}", last_ffn)
            g_out = _dw_rows(sv["y"], dxm, f"dw_a_out_l{i}", [g_in])
            small[f"wm{j}"], small[f"bs{j}"], small[f"gv{j}"] = dwm, dbs, dgv
            mixer_grads = [g_out, g_in]
        elif i == 1:
            dy = mm_nt(dxm, sv["wout"], "b_dy")
            dp3, small["cw"] = b_conv_bwd(dy, sv["p3"], cw_full, seq, "b_conv_bwd")
            dx, dg_mix[i] = bwd_in(dp3, sv["win"], b_chunks, xin, gmix[i], dxm, "b_bwd_in")
            g_in = mm_tn(sv["h"][None], dp3, (N_CHIPS, D, 768), D, 256, 12,
                         lambda j: (0, 0), lambda j: (j // 4, j % 4), lambda j: (j // 3, 0, j % 3), "dw_b_in", [g_down])
            g_out = _dw_rows(sv["y"], dxm, "dw_b_out", [g_in])
            mixer_grads = [g_out, g_in]
        else:
            outs = c_out_bwd(dxm, sv["d"], sv["wgrp"], scale_full, sv["wout"], "c_out_bwd")
            dyp, dd, small["scale"] = outs[0], list(outs[1:5]), outs[5]
            dpool = c_pool_bwd(dd, seq, "c_pool_bwd")
            dp = jnp.concatenate(dpool, axis=1)[None]
            dx, dg_mix[i] = bwd_in(dp, sv["win"], c_chunks, xin, gmix[i], dxm, "c_bwd_in")
            g_in = _dw_rows(sv["h"], dp[0], "dw_c_in", [g_down])
            dcat = jnp.concatenate(sv["d"], axis=1)
            g_grp = mm_tn(dcat[None], dyp[None], (4, C_GROUP_DIM, C_GROUP_DIM), C_GROUP_DIM, C_GROUP_DIM, 4,
                          lambda j: (0, j), lambda j: (0, j), lambda j: (j, 0, 0), "dw_c_grp", [g_in])
            g_out = _dw_rows(sv["y"], dxm, "dw_c_out", [g_grp])
            mixer_grads = [g_out, g_in, _grp_to_blocks(g_grp)]
        if i > 0:
            unit = ReduceScatter(mixer_grads + [g_gate, g_up, g_down], layer_tensors[i] + ffn_names, i, core_arr, chip_arr, f"rs_l{i}")
        else:
            unit = ReduceScatter(mixer_grads, layer_tensors[0], 0, core_arr, chip_arr, "rs_l0_mixer")
        if newer and i > 0:
            newer.sum_chips([mixer_grads[0]] + (update(older) if older else []))
        if i == 1:
            early_parts = [jnp.concatenate(dg_mix[1:] + dg_ffn[1:] + [dg_final, small["gv1"], small["cw"], small["scale"],
                                                                     bs_rows(small["bs1"])], axis=0),
                           bf(small["wm1"].reshape(128, D))]
            early_gathered = all_gather_devices(early_parts, "ag_small_grads_l123")
        older, newer = newer, unit
    grad_x = dx.reshape(n_ex, seq, D)
    mixer_unit = newer
    ffn_unit.sum_chips(update(older) + [mixer_unit.grads[0]])
    mixer_unit.sum_cores([ffn_unit.half_sums[0]])
    mixer_unit.sum_chips(update(ffn_unit))
    updated = update(mixer_unit)

    late_parts = [jnp.concatenate([dg_mix[0], dg_ffn[0], small["gv0"], bs_rows(small["bs0"])], axis=0), bf(small["wm0"].reshape(128, D))]
    late_gathered = all_gather_devices(late_parts, "ag_small_grads_l0")
    late = [sum_devices(p, gth, me_arr, f"sum_small_grads_l0_{n}", updated) for n, (p, gth) in enumerate(zip(late_parts, late_gathered))]
    (early_rows, early_ws), (late_rows, late_ws) = early, late
    first_rows = lambda a, b, n: jnp.concatenate([a, b], axis=0).reshape(n, 8, D)[:, 0]
    g_norm_mix = first_rows(late_rows[0:8], early_rows[0:24], 4)
    g_norm_ffn = first_rows(late_rows[8:16], early_rows[24:48], 4)
    g_final = early_rows[48]
    g_gv = first_rows(late_rows[16:24], early_rows[56:64], 2)
    g_cw = early_rows[64:67]
    g_scale = early_rows[72:73]
    g_ws = jnp.where(mask[None, None], jnp.concatenate([late_ws, early_ws], axis=0).reshape(2, A_GROUPS, 128, 128), 0.0)
    g_bs = first_rows(late_rows[24:32], early_rows[80:88], 2).reshape(2, A_GROUPS, 128)
    col0 = chip * (D // N_CHIPS)
    cols = lambda v: lax.dynamic_slice_in_dim(v, col0, D // N_CHIPS, axis=1)

    small_grads = {
        "norm_mix_g": g_norm_mix, "norm_ffn_g": g_norm_ffn, "final_norm_g": g_final, "a_v_norm_g": cols(g_gv), "a_w_s": g_ws,
        "a_b_s": g_bs, "b_conv_w": cols(g_cw)[None], "c_scale": cols(g_scale),
    }
    results = {}
    for nme, g in small_grads.items():
        w = weights[nme]
        flat = lambda a: a.reshape(-1, w.shape[-1])
        dl, mn, vn = adamw(flat(w), flat(g), flat(m_in[nme]), flat(v_in[nme]), f"adamw_{nme}")
        results[nme] = tuple(o.reshape(w.shape) for o in (g, dl, mn, vn))
    for nme, outs in carried.items():
        results[nme] = tuple(from_stacked[nme](o) for o in outs)

    names = list(weights)
    return (loss, grad_x, *[results[n][0] for n in names], *[results[n][1] for n in names],
            *[results[n][2] for n in names], *[results[n][3] for n in names])
```

```python
import jax
import jax.numpy as jnp
from jax import lax
from jax.experimental import pallas as pl
from jax.experimental.pallas import tpu as pltpu
from jax.experimental.pallas import tpu_sc as plsc

F32 = jnp.float32
BF16 = jnp.bfloat16
D = 1024
FFN_SHARD = 704
GMLP_BLOCK = 128
A_GROUPS = 8
POOL_WINDOWS = (2, 4, 8, 16)
C_GROUP_DIM = 256
N_CHIPS = 4
EPS = 1e-6
ADAM_LR, ADAM_B1, ADAM_B2, ADAM_EPS, ADAM_WD, ADAM_STEP = 0.001, 0.9, 0.999, 1e-08, 0.01, 10
VMEM_LIMIT_BYTES = 56 * 1024 * 1024
FFN_HIDDEN = N_CHIPS * FFN_SHARD
FFN_FWD_CHUNKS = ((0, 768), (768, 768), (1536, 768), (2304, 512))
FFN_BWD_CHUNKS = ((0, 1024), (1024, 1024), (2048, 768))
FFN_FWD_ROWS = 512
FFN_BWD_ROWS = 256
A_ROWS = 512
MESH = pl.DeviceIdType.MESH
GATHER_COLLECTIVE_ID = 1
SIBLING_COLLECTIVE_ID = 2
CHIPS_COLLECTIVE_ID = 3
ALL_COLLECTIVE_ID = 4
ANY = pl.BlockSpec(memory_space=pl.ANY)
NT_DIMS = (((1,), (1,)), ((), ()))
TN_DIMS = (((0,), (0,)), ((), ()))
INV_SQRT2 = 0.7071067811865476
INV_SQRT_2PI = 0.3989422804014327


def _params(*semantics):
    return pltpu.CompilerParams(dimension_semantics=semantics, vmem_limit_bytes=VMEM_LIMIT_BYTES)


def _dot(a, b):
    return jnp.dot(a, b, preferred_element_type=F32)


def _dot_nt(a, b):
    return lax.dot_general(a, b, NT_DIMS, preferred_element_type=F32)


def _rms(x):
    r = lax.rsqrt(jnp.mean(x * x, axis=-1, keepdims=True) + EPS)
    return x * r, r


def _rms_bwd(x, g, dh):
    xh, r = _rms(x)
    dxh = dh * g
    dx = r * (dxh - xh * jnp.mean(dxh * xh, axis=-1, keepdims=True))
    return dx, jnp.sum(dh * xh, axis=0, keepdims=True)


def _gelu(x):
    return 0.5 * x * (1.0 + lax.erf(x * INV_SQRT2))


def _gelu_grad(x):
    return 0.5 * (1.0 + lax.erf(x * INV_SQRT2)) + x * jnp.exp(-0.5 * x * x) * INV_SQRT_2PI


def _shift_down(v, s, row):
    return jnp.where(row >= s, pltpu.roll(v, s, 0), 0.0)


def _shift_up(v, s, row):
    n = v.shape[0]
    return jnp.where(row < n - s, pltpu.roll(v, n - s, 0), 0.0)


def _row_tile(t, want):
    return want if t % want == 0 else t


def _after(body, first, deps):
    if not deps:
        return body

    def ordered(*refs):
        return body(*refs[:first], *refs[first + len(deps):])

    return ordered


def norm_mm(x, g, w, chunks, n_parts, part_width, name):
    t = x.shape[0]
    tm = _row_tile(t, 512)
    n_shards, _, hs = w.shape

    def body(x_ref, g_ref, w_ref, h_ref, p_ref):
        xh, _ = _rms(x_ref[...])
        h = (xh * g_ref[...]).astype(BF16)
        h_ref[...] = h
        for s in range(n_shards):
            res = _dot(h, w_ref[s]).astype(BF16)
            for (cs, wc, width, part, pc) in chunks:
                if cs == s:
                    p_ref[part, :, pc:pc + width] = res[:, wc:wc + width]

    return pl.pallas_call(
        body, name=name, grid=(t // tm,),
        in_specs=[pl.BlockSpec((tm, D), lambda i: (i, 0)), pl.BlockSpec((1, D), lambda i: (0, 0)),
                  pl.BlockSpec((n_shards, D, hs), lambda i: (0, 0, 0))],
        out_specs=[pl.BlockSpec((tm, D), lambda i: (i, 0)), pl.BlockSpec((n_parts, tm, part_width), lambda i: (0, i, 0))],
        out_shape=[jax.ShapeDtypeStruct((t, D), BF16), jax.ShapeDtypeStruct((n_parts, t, part_width), BF16)],
        compiler_params=_params("arbitrary"),
    )(x, g, w)


def mm_res(a, w, res, name):
    t, k = a.shape
    n = w.shape[1]
    tm = _row_tile(t, 512)

    def body(a_ref, w_ref, r_ref, o_ref):
        o_ref[...] = r_ref[...] + _dot(a_ref[...], w_ref[...])

    return pl.pallas_call(
        body, name=name, grid=(t // tm,),
        in_specs=[pl.BlockSpec((tm, k), lambda i: (i, 0)), pl.BlockSpec((k, n), lambda i: (0, 0)),
                  pl.BlockSpec((tm, n), lambda i: (i, 0))],
        out_specs=pl.BlockSpec((tm, n), lambda i: (i, 0)),
        out_shape=jax.ShapeDtypeStruct((t, n), F32),
        compiler_params=_params("arbitrary"),
    )(a, w, res)


def mm_nt(a, w, name):
    t, n = a.shape
    k = w.shape[0]
    tm = _row_tile(t, 512)

    def body(a_ref, w_ref, o_ref):
        o_ref[...] = _dot_nt(a_ref[...].astype(BF16), w_ref[...]).astype(BF16)

    return pl.pallas_call(
        body, name=name, grid=(t // tm,),
        in_specs=[pl.BlockSpec((tm, n), lambda i: (i, 0)), pl.BlockSpec((k, n), lambda i: (0, 0))],
        out_specs=pl.BlockSpec((tm, k), lambda i: (i, 0)),
        out_shape=jax.ShapeDtypeStruct((t, k), BF16),
        compiler_params=_params("arbitrary"),
    )(a, w)


def bwd_in(dp, w, chunks, x, g, dres, name):
    n_parts, t, part_width = dp.shape
    n_shards, _, hs = w.shape
    tm = _row_tile(t, 512)

    def body(dp_ref, w_ref, x_ref, g_ref, dres_ref, dx_ref, dg_ref):
        acc = jnp.zeros((tm, D), F32)
        for (cs, wc, width, part, pc) in chunks:
            acc = acc + _dot_nt(dp_ref[part, :, pc:pc + width], w_ref[cs, :, wc:wc + width])
        dx, dg = _rms_bwd(x_ref[...], g_ref[...], acc)
        dx_ref[...] = dres_ref[...] + dx

        @pl.when(pl.program_id(0) == 0)
        def _():
            dg_ref[...] = jnp.zeros_like(dg_ref)

        dg_ref[0:1, :] += dg

    return pl.pallas_call(
        body, name=name, grid=(t // tm,),
        in_specs=[pl.BlockSpec((n_parts, tm, part_width), lambda i: (0, i, 0)),
                  pl.BlockSpec((n_shards, D, hs), lambda i: (0, 0, 0)),
                  pl.BlockSpec((tm, D), lambda i: (i, 0)), pl.BlockSpec((1, D), lambda i: (0, 0)),
                  pl.BlockSpec((tm, D), lambda i: (i, 0))],
        out_specs=[pl.BlockSpec((tm, D), lambda i: (i, 0)), pl.BlockSpec((8, D), lambda i: (0, 0))],
        out_shape=[jax.ShapeDtypeStruct((t, D), F32), jax.ShapeDtypeStruct((8, D), F32)],
        compiler_params=_params("arbitrary"),
    )(dp, w, x, g, dres)


def mm_tn(a, b, out_shape, tm, tn, n_tiles, a_idx, b_idx, o_idx, name, deps=()):
    t = a.shape[1]

    def body(a_ref, b_ref, o_ref):
        o_ref[0] = lax.dot_general(a_ref[0].astype(BF16), b_ref[0].astype(BF16), TN_DIMS, preferred_element_type=F32).astype(BF16)

    return pl.pallas_call(
        _after(body, 2, deps), name=name, grid=(n_tiles,),
        in_specs=[pl.BlockSpec((1, t, tm), lambda j: (a_idx(j)[0], 0, a_idx(j)[1])),
                  pl.BlockSpec((1, t, tn), lambda j: (b_idx(j)[0], 0, b_idx(j)[1]))] + [ANY] * len(deps),
        out_specs=pl.BlockSpec((1, tm, tn), lambda j: o_idx(j)),
        out_shape=jax.ShapeDtypeStruct(out_shape, BF16),
        compiler_params=_params("arbitrary"),
    )(a, b, *deps)


def ffn_fwd(x, g, wg, wu, wd, name):
    t = x.shape[0]
    tm = _row_tile(t, FFN_FWD_ROWS)
    hidden = wg.shape[0]

    def body(x_ref, g_ref, wg_ref, wu_ref, wd_ref, h_ref, a_ref, b_ref, s_ref, o_ref):
        xv = x_ref[...]
        xh, _ = _rms(xv)
        h = (xh * g_ref[...]).astype(BF16)
        h_ref[...] = h
        acc = xv
        for c0, cw in FFN_FWD_CHUNKS:
            cols = slice(c0, c0 + cw)
            a = _dot_nt(h, wg_ref[cols, :])
            b = _dot_nt(h, wu_ref[cols, :])
            sig = jax.nn.sigmoid(a)
            silu = a * sig
            s = (silu * b).astype(BF16)
            a_ref[:, cols] = (b * (sig * (1.0 + a * (1.0 - sig)))).astype(BF16)
            b_ref[:, cols] = silu.astype(BF16)
            s_ref[:, cols] = s
            acc = acc + _dot(s, wd_ref[cols, :])
        o_ref[...] = acc

    act = pl.BlockSpec((tm, hidden), lambda i: (i, 0))
    act_shape = jax.ShapeDtypeStruct((t, hidden), BF16)
    wspec = pl.BlockSpec((hidden, D), lambda i: (0, 0), pipeline_mode=pl.Buffered(1))
    row = pl.BlockSpec((tm, D), lambda i: (i, 0))
    return pl.pallas_call(
        body, name=name, grid=(t // tm,),
        in_specs=[row, pl.BlockSpec((1, D), lambda i: (0, 0)), wspec, wspec, wspec],
        out_specs=[row, act, act, act, row],
        out_shape=[jax.ShapeDtypeStruct((t, D), BF16), act_shape, act_shape, act_shape, jax.ShapeDtypeStruct((t, D), F32)],
        compiler_params=_params("arbitrary"),
    )(x, g, wg, wu, wd)


def ffn_bwd(dxo, a, b, x, g, wg, wu, wd, name, deps=()):
    t = x.shape[0]
    tm = _row_tile(t, FFN_BWD_ROWS)
    hidden = wg.shape[0]

    def body(dxo_ref, a_ref, b_ref, x_ref, g_ref, wg_ref, wu_ref, wd_ref, dx_ref, da_ref, db_ref, dg_ref):
        @pl.when(pl.program_id(0) == 0)
        def _():
            dg_ref[...] = jnp.zeros_like(dg_ref)

        dxo = dxo_ref[...]
        dxb = dxo.astype(BF16)
        acc = None
        for c0, cw in FFN_BWD_CHUNKS:
            cols = slice(c0, c0 + cw)
            ds = _dot_nt(dxb, wd_ref[cols, :])
            da = (ds * a_ref[:, cols].astype(F32)).astype(BF16)
            db = (ds * b_ref[:, cols].astype(F32)).astype(BF16)
            da_ref[:, cols] = da
            db_ref[:, cols] = db
            part = _dot(da, wg_ref[cols, :]) + _dot(db, wu_ref[cols, :])
            acc = part if acc is None else acc + part
        dx, dg = _rms_bwd(x_ref[...], g_ref[...], acc)
        dx_ref[...] = dxo + dx
        dg_ref[0:1, :] += dg

    act = pl.BlockSpec((tm, hidden), lambda i: (i, 0))
    act_shape = jax.ShapeDtypeStruct((t, hidden), BF16)
    row = pl.BlockSpec((tm, D), lambda i: (i, 0))
    wspec = pl.BlockSpec((hidden, D), lambda i: (0, 0), pipeline_mode=pl.Buffered(1))
    return pl.pallas_call(
        _after(body, 8, deps), name=name, grid=(t // tm,),
        in_specs=[row, act, act, row, pl.BlockSpec((1, D), lambda i: (0, 0)), wspec, wspec, wspec] + [ANY] * len(deps),
        out_specs=[row, act, act, pl.BlockSpec((8, D), lambda i: (0, 0))],
        out_shape=[jax.ShapeDtypeStruct((t, D), F32), act_shape, act_shape, jax.ShapeDtypeStruct((8, D), F32)],
        compiler_params=_params("arbitrary"),
    )(dxo, a, b, x, g, wg, wu, wd, *deps)


def _layer_norm_stats(v):
    mu = jnp.mean(v, axis=-1, keepdims=True)
    vc = v - mu
    rstd = lax.rsqrt(jnp.mean(vc * vc, axis=-1, keepdims=True) + EPS)
    return vc * rstd, rstd


def a_fwd(x, g, win, gv, wm, bs, wout, name):
    t = x.shape[0]
    tm = _row_tile(t, A_ROWS)
    n_shards, _, hs = win.shape

    def body(x_ref, g_ref, win_ref, gv_ref, wm_ref, bs_ref, wout_ref, h_ref, z_ref, vn_ref, y_ref, o_ref):
        xv = x_ref[...]
        xh, _ = _rms(xv)
        h = (xh * g_ref[...]).astype(BF16)
        h_ref[...] = h
        zs = []
        for s in range(n_shards):
            zb = _dot(h, win_ref[s]).astype(BF16)
            z_ref[:, s * hs:(s + 1) * hs] = zb
            zs.append(zb.astype(F32))
        half = n_shards // 2
        u = _gelu(jnp.concatenate(zs[:half], axis=1))
        vhat, _ = _layer_norm_stats(_gelu(jnp.concatenate(zs[half:], axis=1)))
        vnb = (vhat * gv_ref[...]).astype(BF16)
        vn_ref[...] = vnb
        for n in range(tm // GMLP_BLOCK):
            rows = slice(n * GMLP_BLOCK, (n + 1) * GMLP_BLOCK)
            for grp in range(A_GROUPS):
                cols = slice(grp * 128, (grp + 1) * 128)
                sv = _dot(wm_ref[grp], vnb[rows, cols]) + bs_ref[grp]
                y_ref[rows, cols] = (u[rows, cols] * sv).astype(BF16)
        o_ref[...] = xv + _dot(y_ref[...], wout_ref[...])

    small = pl.BlockSpec((A_GROUPS, 128, 128), lambda i: (0, 0, 0))
    row = pl.BlockSpec((tm, D), lambda i: (i, 0))
    wide = pl.BlockSpec((tm, 2 * D), lambda i: (i, 0))
    gain = pl.BlockSpec((1, D), lambda i: (0, 0))
    return pl.pallas_call(
        body, name=name, grid=(t // tm,),
        in_specs=[row, gain, pl.BlockSpec((n_shards, D, hs), lambda i: (0, 0, 0)), gain, small, small,
                  pl.BlockSpec((D, D), lambda i: (0, 0))],
        out_specs=[row, wide, row, row, row],
        out_shape=[jax.ShapeDtypeStruct((t, D), BF16), jax.ShapeDtypeStruct((t, 2 * D), BF16), jax.ShapeDtypeStruct((t, D), BF16),
                   jax.ShapeDtypeStruct((t, D), BF16), jax.ShapeDtypeStruct((t, D), F32)],
        compiler_params=_params("arbitrary"),
    )(x, g, win, gv, wm, bs, wout)


def a_bwd(dxm, x, g, z, vn, gv, wm, wmt, bs, wout, win, name):
    t = x.shape[0]
    tm = _row_tile(t, A_ROWS)
    n_shards, _, hs = win.shape

    def body(dxm_ref, x_ref, g_ref, z_ref, vn_ref, gv_ref, wm_ref, wmt_ref, bs_ref, wout_ref, win_ref,
             dx_ref, dz_ref, dwm_ref, dbs_ref, dgv_ref, dg_ref, du_ref, dvn_ref):
        @pl.when(pl.program_id(0) == 0)
        def _():
            dwm_ref[...] = jnp.zeros_like(dwm_ref)
            dbs_ref[...] = jnp.zeros_like(dbs_ref)
            dgv_ref[...] = jnp.zeros_like(dgv_ref)
            dg_ref[...] = jnp.zeros_like(dg_ref)

        dxm = dxm_ref[...]
        dyv = _dot_nt(dxm.astype(BF16), wout_ref[...])
        zz = z_ref[...].astype(F32)
        zu, zv = zz[:, :D], zz[:, D:]
        u = _gelu(zu)
        vhat, rstd = _layer_norm_stats(_gelu(zv))
        vnb = vn_ref[...]
        ones = jnp.ones((128, 128), BF16)
        for n in range(tm // GMLP_BLOCK):
            rows = slice(n * GMLP_BLOCK, (n + 1) * GMLP_BLOCK)
            for grp in range(A_GROUPS):
                cols = slice(grp * 128, (grp + 1) * 128)
                blk = vnb[rows, cols]
                sv = _dot(wm_ref[grp], blk) + bs_ref[grp]
                dyb = dyv[rows, cols]
                du_ref[rows, cols] = dyb * sv
                dsv = (dyb * u[rows, cols]).astype(BF16)
                dvn_ref[rows, cols] = _dot(wmt_ref[grp], dsv)
                dwm_ref[grp] += _dot_nt(dsv, blk)
                dbs_ref[grp] += _dot(dsv, ones)
        dvn = dvn_ref[...]
        dgv_ref[0:1, :] += jnp.sum(dvn * vhat, axis=0, keepdims=True)
        dvh = dvn * gv_ref[...]
        dv = rstd * (dvh - jnp.mean(dvh, axis=-1, keepdims=True) - vhat * jnp.mean(dvh * vhat, axis=-1, keepdims=True))
        dz_ref[:, :D] = (du_ref[...] * _gelu_grad(zu)).astype(BF16)
        dz_ref[:, D:] = (dv * _gelu_grad(zv)).astype(BF16)
        dh = None
        for s in range(n_shards):
            part = _dot_nt(dz_ref[:, s * hs:(s + 1) * hs], win_ref[s])
            dh = part if dh is None else dh + part
        dx, dg = _rms_bwd(x_ref[...], g_ref[...], dh)
        dx_ref[...] = dxm + dx
        dg_ref[0:1, :] += dg

    small = pl.BlockSpec((A_GROUPS, 128, 128), lambda i: (0, 0, 0))
    row = pl.BlockSpec((tm, D), lambda i: (i, 0))
    wide = pl.BlockSpec((tm, 2 * D), lambda i: (i, 0))
    gain = pl.BlockSpec((1, D), lambda i: (0, 0))
    sums = pl.BlockSpec((8, D), lambda i: (0, 0))
    small_shape = jax.ShapeDtypeStruct((A_GROUPS, 128, 128), F32)
    sums_shape = jax.ShapeDtypeStruct((8, D), F32)
    return pl.pallas_call(
        body, name=name, grid=(t // tm,),
        in_specs=[row, row, gain, wide, row, gain, small, small, small, pl.BlockSpec((D, D), lambda i: (0, 0)),
                  pl.BlockSpec((n_shards, D, hs), lambda i: (0, 0, 0))],
        out_specs=[row, wide, small, small, sums, sums],
        out_shape=[jax.ShapeDtypeStruct((t, D), F32), jax.ShapeDtypeStruct((t, 2 * D), BF16), small_shape, small_shape,
                   sums_shape, sums_shape],
        scratch_shapes=[pltpu.VMEM((tm, D), F32), pltpu.VMEM((tm, D), F32)],
        compiler_params=_params("arbitrary"),
    )(dxm, x, g, z, vn, gv, wm, wmt, bs, wout, win)


def _conv_terms(p_ref, row):
    gb = p_ref[0].astype(F32)
    gc = p_ref[1].astype(F32)
    xt = p_ref[2].astype(F32)
    q = gc * xt
    return gb, gc, xt, q, _shift_down(q, 1, row), _shift_down(q, 2, row)


def b_conv_fwd(p3, cw, seq, name):
    t = p3.shape[1]
    cb = 256

    def body(p_ref, cw_ref, y_ref):
        row = lax.broadcasted_iota(jnp.int32, (seq, cb), 0)
        gb, _, _, q, q1, q2 = _conv_terms(p_ref, row)
        y_ref[...] = (gb * (cw_ref[2:3, :] * q + cw_ref[1:2, :] * q1 + cw_ref[0:1, :] * q2)).astype(BF16)

    return pl.pallas_call(
        body, name=name, grid=(t // seq, D // cb),
        in_specs=[pl.BlockSpec((3, seq, cb), lambda e, c: (0, e, c)), pl.BlockSpec((3, cb), lambda e, c: (0, c))],
        out_specs=pl.BlockSpec((seq, cb), lambda e, c: (e, c)),
        out_shape=jax.ShapeDtypeStruct((t, D), BF16),
        compiler_params=_params("arbitrary", "arbitrary"),
    )(p3, cw)


def b_conv_bwd(dy, p3, cw, seq, name):
    t = p3.shape[1]
    cb = 256

    def body(dy_ref, p_ref, cw_ref, dp_ref, dcw_ref):
        @pl.when(pl.program_id(1) == 0)
        def _():
            dcw_ref[...] = jnp.zeros_like(dcw_ref)

        row = lax.broadcasted_iota(jnp.int32, (seq, cb), 0)
        gb, gc, xt, q, q1, q2 = _conv_terms(p_ref, row)
        dyv = dy_ref[...].astype(F32)
        conv = cw_ref[2:3, :] * q + cw_ref[1:2, :] * q1 + cw_ref[0:1, :] * q2
        dyc = dyv * gb
        dq = cw_ref[2:3, :] * dyc + cw_ref[1:2, :] * _shift_up(dyc, 1, row) + cw_ref[0:1, :] * _shift_up(dyc, 2, row)
        dp_ref[0] = (dyv * conv).astype(BF16)
        dp_ref[1] = (dq * xt).astype(BF16)
        dp_ref[2] = (dq * gc).astype(BF16)
        dcw_ref[0:1, :] += jnp.sum(dyc * q2, axis=0, keepdims=True)
        dcw_ref[1:2, :] += jnp.sum(dyc * q1, axis=0, keepdims=True)
        dcw_ref[2:3, :] += jnp.sum(dyc * q, axis=0, keepdims=True)

    return pl.pallas_call(
        body, name=name, grid=(D // cb, t // seq),
        in_specs=[pl.BlockSpec((seq, cb), lambda c, e: (e, c)), pl.BlockSpec((3, seq, cb), lambda c, e: (0, e, c)),
                  pl.BlockSpec((3, cb), lambda c, e: (0, c))],
        out_specs=[pl.BlockSpec((3, seq, cb), lambda c, e: (0, e, c)), pl.BlockSpec((8, cb), lambda c, e: (0, c))],
        out_shape=[jax.ShapeDtypeStruct((3, t, D), BF16), jax.ShapeDtypeStruct((8, D), F32)],
        compiler_params=_params("arbitrary", "arbitrary"),
    )(dy, p3, cw)


def c_pool_fwd(p, seq, name):
    t = p.shape[0]

    def make(grp):
        w = POOL_WINDOWS[grp]

        def body_g(p_ref, d_ref):
            row = lax.broadcasted_iota(jnp.int32, (seq, C_GROUP_DIM), 0)
            pv = p_ref[...].astype(F32)
            acc = pv
            sh = 1
            while sh < w:
                acc = acc + _shift_down(acc, sh, row)
                sh *= 2
            d_ref[...] = (acc / jnp.minimum(row + 1, w).astype(F32) - pv).astype(BF16)

        return body_g

    outs = []
    for grp in range(len(POOL_WINDOWS)):
        outs.append(pl.pallas_call(
            make(grp), name=f"{name}_g{grp}", grid=(t // seq,),
            in_specs=[pl.BlockSpec((seq, C_GROUP_DIM), lambda e, grp=grp: (e, grp))],
            out_specs=pl.BlockSpec((seq, C_GROUP_DIM), lambda e: (e, 0)),
            out_shape=jax.ShapeDtypeStruct((t, C_GROUP_DIM), BF16),
            compiler_params=_params("arbitrary"),
        )(p))
    return outs


def c_pool_bwd(dd, seq, name):
    t = dd[0].shape[0]

    def make(w):
        def body_g(dd_ref, dp_ref):
            row = lax.broadcasted_iota(jnp.int32, (seq, C_GROUP_DIM), 0)
            ddv = dd_ref[...]
            acc = ddv / jnp.minimum(row + 1, w).astype(F32)
            sh = 1
            while sh < w:
                acc = acc + _shift_up(acc, sh, row)
                sh *= 2
            dp_ref[...] = (acc - ddv).astype(BF16)

        return body_g

    outs = []
    for grp, w in enumerate(POOL_WINDOWS):
        outs.append(pl.pallas_call(
            make(w), name=f"{name}_g{grp}", grid=(t // seq,),
            in_specs=[pl.BlockSpec((seq, C_GROUP_DIM), lambda e: (e, 0))],
            out_specs=pl.BlockSpec((seq, C_GROUP_DIM), lambda e: (e, 0)),
            out_shape=jax.ShapeDtypeStruct((t, C_GROUP_DIM), BF16),
            compiler_params=_params("arbitrary"),
        )(dd[grp]))
    return outs


def c_out_fwd(d, wgrp, scale, wo, x, name):
    t = x.shape[0]
    tm = _row_tile(t, 512)
    n_g = len(POOL_WINDOWS)

    def body(d0, d1, d2, d3, wg_ref, sc_ref, wo_ref, x_ref, y_ref, o_ref):
        parts = [_dot(dr[...], wg_ref[grp]) for grp, dr in enumerate((d0, d1, d2, d3))]
        y = (jnp.concatenate(parts, axis=1) * sc_ref[...]).astype(BF16)
        y_ref[...] = y
        o_ref[...] = x_ref[...] + _dot(y, wo_ref[...])

    dspec = pl.BlockSpec((tm, C_GROUP_DIM), lambda i: (i, 0))
    row = pl.BlockSpec((tm, D), lambda i: (i, 0))
    return pl.pallas_call(
        body, name=name, grid=(t // tm,),
        in_specs=[dspec] * n_g + [pl.BlockSpec((n_g, C_GROUP_DIM, C_GROUP_DIM), lambda i: (0, 0, 0)),
                                  pl.BlockSpec((1, D), lambda i: (0, 0)), pl.BlockSpec((D, D), lambda i: (0, 0)), row],
        out_specs=[row, row],
        out_shape=[jax.ShapeDtypeStruct((t, D), BF16), jax.ShapeDtypeStruct((t, D), F32)],
        compiler_params=_params("arbitrary"),
    )(*d, wgrp, scale, wo, x)


def c_out_bwd(dxm, d, wgrp, scale, wo, name):
    t = dxm.shape[0]
    tm = _row_tile(t, 512)
    n_g = len(POOL_WINDOWS)

    def body(dxm_ref, d0, d1, d2, d3, wg_ref, sc_ref, wo_ref, dyp_ref, dd0, dd1, dd2, dd3, dsc_ref):
        @pl.when(pl.program_id(0) == 0)
        def _():
            dsc_ref[...] = jnp.zeros_like(dsc_ref)

        dyo = _dot_nt(dxm_ref[...].astype(BF16), wo_ref[...])
        ypre = jnp.concatenate([_dot(dr[...], wg_ref[grp]) for grp, dr in enumerate((d0, d1, d2, d3))], axis=1)
        dsc_ref[0:1, :] += jnp.sum(dyo * ypre, axis=0, keepdims=True)
        dyp = (dyo * sc_ref[...]).astype(BF16)
        dyp_ref[...] = dyp
        for grp, ddr in enumerate((dd0, dd1, dd2, dd3)):
            ddr[...] = _dot_nt(dyp[:, grp * C_GROUP_DIM:(grp + 1) * C_GROUP_DIM], wg_ref[grp])

    dspec = pl.BlockSpec((tm, C_GROUP_DIM), lambda i: (i, 0))
    row = pl.BlockSpec((tm, D), lambda i: (i, 0))
    dshape = jax.ShapeDtypeStruct((t, C_GROUP_DIM), F32)
    return pl.pallas_call(
        body, name=name, grid=(t // tm,),
        in_specs=[row] + [dspec] * n_g + [pl.BlockSpec((n_g, C_GROUP_DIM, C_GROUP_DIM), lambda i: (0, 0, 0)),
                                          pl.BlockSpec((1, D), lambda i: (0, 0)), pl.BlockSpec((D, D), lambda i: (0, 0))],
        out_specs=[row] + [dspec] * n_g + [pl.BlockSpec((8, D), lambda i: (0, 0))],
        out_shape=[jax.ShapeDtypeStruct((t, D), BF16)] + [dshape] * n_g + [jax.ShapeDtypeStruct((8, D), F32)],
        compiler_params=_params("arbitrary"),
    )(dxm, *d, wgrp, scale, wo)


def loss_head(x, tgt, g, name):
    t = x.shape[0]
    tm = _row_tile(t, 512)

    def body(x_ref, t_ref, g_ref, dx_ref, dg_ref, loss_ref):
        @pl.when(pl.program_id(0) == 0)
        def _():
            dg_ref[...] = jnp.zeros_like(dg_ref)
            loss_ref[...] = jnp.zeros_like(loss_ref)

        xv, gvv = x_ref[...], g_ref[...]
        xh, _ = _rms(xv)
        diff = xh * gvv - t_ref[...]
        loss_ref[...] += 0.5 * jnp.sum(jnp.mean(diff * diff, axis=-1, keepdims=True))
        dx, dg = _rms_bwd(xv, gvv, diff * (1.0 / D))
        dx_ref[...] = dx
        dg_ref[0:1, :] += dg

    row = pl.BlockSpec((tm, D), lambda i: (i, 0))
    return pl.pallas_call(
        body, name=name, grid=(t // tm,),
        in_specs=[row, row, pl.BlockSpec((1, D), lambda i: (0, 0))],
        out_specs=[row, pl.BlockSpec((8, D), lambda i: (0, 0)), pl.BlockSpec((8, 128), lambda i: (0, 0))],
        out_shape=[jax.ShapeDtypeStruct((t, D), F32), jax.ShapeDtypeStruct((8, D), F32), jax.ShapeDtypeStruct((8, 128), F32)],
        compiler_params=_params("arbitrary"),
    )(x, tgt, g)


def adamw(w, g, m, v, name):
    rows, cols = w.shape
    tr = rows
    for cand in (512, 256, 128, 64, 32, 16, 8):
        if rows % cand == 0 and rows > cand:
            tr = cand
            break

    def body(w_ref, g_ref, m_ref, v_ref, d_ref, mo_ref, vo_ref):
        gv = g_ref[...]
        mn = ADAM_B1 * m_ref[...] + (1.0 - ADAM_B1) * gv
        vn = ADAM_B2 * v_ref[...] + (1.0 - ADAM_B2) * (gv * gv)
        m_hat = mn / (1.0 - ADAM_B1 ** ADAM_STEP)
        v_hat = vn / (1.0 - ADAM_B2 ** ADAM_STEP)
        d_ref[...] = -ADAM_LR * (m_hat / (jnp.sqrt(v_hat) + ADAM_EPS) + ADAM_WD * w_ref[...])
        mo_ref[...] = mn
        vo_ref[...] = vn

    spec = pl.BlockSpec((tr, cols), lambda i: (i, 0))
    shape = jax.ShapeDtypeStruct((rows, cols), F32)
    return pl.pallas_call(
        body, name=name, grid=(rows // tr,),
        in_specs=[spec] * 4, out_specs=[spec] * 3, out_shape=[shape] * 3,
        compiler_params=_params("arbitrary"),
    )(w, g, m, v)


def adamw_layer(w, m, v, own, recv, core, layer, carried, name):
    n_layers, rows, cols = w.shape
    h = rows // 2

    def body(core_ref, w_ref, m_ref, v_ref, own_ref, recv_ref, *rest):
        g_ref, d_ref, mo_ref, vo_ref = rest[-4:]
        gv = jnp.where(pl.program_id(0) == core_ref[0], own_ref[...], recv_ref[...])
        mn = ADAM_B1 * m_ref[0] + (1.0 - ADAM_B1) * gv
        vn = ADAM_B2 * v_ref[0] + (1.0 - ADAM_B2) * (gv * gv)
        m_hat = mn / (1.0 - ADAM_B1 ** ADAM_STEP)
        v_hat = vn / (1.0 - ADAM_B2 ** ADAM_STEP)
        g_ref[0] = gv
        d_ref[0] = -ADAM_LR * (m_hat / (jnp.sqrt(v_hat) + ADAM_EPS) + ADAM_WD * w_ref[0])
        mo_ref[0] = mn
        vo_ref[0] = vn

    steps = 1
    tr = h // steps
    stacked = pl.BlockSpec((1, tr, cols), lambda half, j, core_ref: (layer, half * steps + j, 0))
    halfspec = pl.BlockSpec((tr, cols), lambda half, j, core_ref: (j, 0))
    n_carried = 0 if carried is None else 4
    shape = jax.ShapeDtypeStruct(w.shape, F32)
    return pl.pallas_call(
        body, name=name,
        grid_spec=pltpu.PrefetchScalarGridSpec(
            num_scalar_prefetch=1, grid=(2, steps),
            in_specs=[stacked] * 3 + [halfspec] * 2 + [ANY] * n_carried, out_specs=[stacked] * 4),
        out_shape=[shape] * 4,
        input_output_aliases={6 + i: i for i in range(n_carried)},
        compiler_params=_params("arbitrary", "arbitrary"),
    )(core, w, m, v, own, recv, *(carried or ()))


def add_halves(gs, ps, core, name, deps=()):
    n = len(gs)

    def body(core_ref, *refs):
        for i in range(n):
            refs[2 * n + i][...] = (refs[i][...].astype(F32) + refs[n + i][...].astype(F32)).astype(BF16)

    in_specs, out_specs, out_shape = [], [], []
    for gt in gs:
        h, c = gt.shape[1] // 2, gt.shape[2]
        in_specs.append(pl.BlockSpec((1, h, c), lambda b, core_ref: (b, core_ref[0], 0)))
    for gt in gs:
        h, c = gt.shape[1] // 2, gt.shape[2]
        in_specs.append(pl.BlockSpec((1, h, c), lambda b, core_ref: (b, 0, 0)))
        out_specs.append(pl.BlockSpec((1, h, c), lambda b, core_ref: (b, 0, 0)))
        out_shape.append(jax.ShapeDtypeStruct((N_CHIPS, h, c), BF16))
    in_specs += [ANY] * len(deps)
    return pl.pallas_call(
        _after(body, 1 + 2 * n, deps), name=name,
        grid_spec=pltpu.PrefetchScalarGridSpec(num_scalar_prefetch=1, grid=(N_CHIPS,), in_specs=in_specs, out_specs=out_specs),
        out_shape=out_shape, compiler_params=_params("arbitrary"),
    )(core, *gs, *ps, *deps)


def add_final(hs, qs, chip, name, deps=()):
    n = len(hs)

    def body(chip_ref, *refs):
        for i in range(n):
            q = refs[n + i]
            refs[2 * n + i][...] = ((refs[i][0].astype(F32) + q[0].astype(F32)) + q[1].astype(F32)) + q[2].astype(F32)

    steps = 2
    in_specs, out_specs, out_shape = [], [], []
    for ht in hs:
        h, c = ht.shape[1], ht.shape[2]
        in_specs.append(pl.BlockSpec((1, h // steps, c), lambda i, chip_ref: (chip_ref[0], i, 0)))
    for ht in hs:
        h, c = ht.shape[1], ht.shape[2]
        in_specs.append(pl.BlockSpec((N_CHIPS - 1, h // steps, c), lambda i, chip_ref: (0, i, 0)))
        out_specs.append(pl.BlockSpec((h // steps, c), lambda i, chip_ref: (i, 0)))
        out_shape.append(jax.ShapeDtypeStruct((h, c), F32))
    in_specs += [ANY] * len(deps)
    return pl.pallas_call(
        _after(body, 1 + 2 * n, deps), name=name,
        grid_spec=pltpu.PrefetchScalarGridSpec(num_scalar_prefetch=1, grid=(steps,), in_specs=in_specs, out_specs=out_specs),
        out_shape=out_shape, compiler_params=_params("arbitrary"),
    )(chip, *hs, *qs, *deps)


def sum_devices(own, gathered, me, name, deps=()):
    rows = own.shape[0]

    def body(me_ref, own_ref, g_ref, o_ref):
        me_dev = me_ref[0]
        acc = None
        for dev in range(8):
            slot = jnp.maximum((me_dev ^ dev) - 1, 0)
            term = jnp.where(me_dev == dev, own_ref[...], g_ref[slot]).astype(F32)
            acc = term if acc is None else acc + term
        o_ref[...] = acc

    tr = ([c for c in range(80, 0, -16) if rows % c == 0] or [rows])[0]
    return pl.pallas_call(
        _after(body, 3, deps), name=name,
        grid_spec=pltpu.PrefetchScalarGridSpec(
            num_scalar_prefetch=1, grid=(rows // tr,),
            in_specs=[pl.BlockSpec((tr, D), lambda i, me_ref: (i, 0)), pl.BlockSpec((7, tr, D), lambda i, me_ref: (0, i, 0))]
            + [ANY] * len(deps),
            out_specs=pl.BlockSpec((tr, D), lambda i, me_ref: (i, 0))),
        out_shape=jax.ShapeDtypeStruct((rows, D), F32),
        compiler_params=_params("arbitrary"),
    )(me, own, gathered, *deps)


def _mesh_pos():
    return lax.axis_index("x"), lax.axis_index("y"), lax.axis_index("c")


def _other_chips(x, y):
    return [(1 - x, y), (x, 1 - y), (1 - x, 1 - y)]


def _sibling():
    x, y, c = _mesh_pos()
    return [(x, y, 1 - c)]


def _same_core_of_other_chips():
    x, y, c = _mesh_pos()
    return [(cx, cy, c) for (cx, cy) in _other_chips(x, y)]


def _on_sequencer(body, name, operands, out_shapes, sems, peers, collective_id, deps=()):
    ordered = _after(body, len(operands), deps)

    def seq_body(*refs):
        barrier = pltpu.get_barrier_semaphore()
        with_whom = peers()
        for peer in with_whom:
            pl.semaphore_signal(barrier, inc=1, device_id=peer, device_id_type=MESH)
        pl.semaphore_wait(barrier, len(with_whom))
        ordered(*refs)

    return pl.kernel(
        seq_body, name=name, out_type=out_shapes,
        mesh=plsc.ScalarSubcoreMesh(axis_name="seq", num_cores=1),
        scratch_types=sems, compiler_params=pltpu.CompilerParams(collective_id=collective_id, has_side_effects=True),
    )(*operands, *deps)


def all_gather_weights(shards, name, deps=()):
    n = len(shards)

    def body(*refs):
        ins, outs = refs[:n], refs[n:2 * n]
        send, recv, fsend, frecv = refs[2 * n:]
        x, y, c = _mesh_pos()
        k = 2 * x + y
        chips = _other_chips(x, y)

        def half(ref, i, rows_half):
            h = shards[i].shape[0] // 2
            return ref.at[pl.ds(pl.multiple_of(rows_half * h, 8), h), :]

        first = []
        for i in range(n):
            for j, (cx, cy) in enumerate(chips):
                first.append(pltpu.make_async_remote_copy(
                    src_ref=half(ins[i], i, c), dst_ref=half(outs[i].at[k], i, c),
                    send_sem=send.at[i, j], recv_sem=recv.at[i, j], device_id=(cx, cy, c), device_id_type=MESH))
        for cp in first:
            cp.start()
        passed = []
        for i in range(n):
            for j, (cx, cy) in enumerate(chips):
                blk = half(outs[i].at[2 * cx + cy], i, c)
                pltpu.make_async_remote_copy(src_ref=blk, dst_ref=blk, send_sem=send.at[i, j], recv_sem=recv.at[i, j],
                                             device_id=(cx, cy, c), device_id_type=MESH).wait_recv()
                fw = pltpu.make_async_remote_copy(src_ref=blk, dst_ref=blk, send_sem=fsend.at[i, j], recv_sem=frecv.at[i, j],
                                                  device_id=(x, y, 1 - c), device_id_type=MESH)
                fw.start()
                passed.append(fw)
        for i in range(n):
            for j, (cx, cy) in enumerate(chips):
                blk = half(outs[i].at[2 * cx + cy], i, 1 - c)
                pltpu.make_async_remote_copy(src_ref=blk, dst_ref=blk, send_sem=fsend.at[i, j], recv_sem=frecv.at[i, j],
                                             device_id=(x, y, 1 - c), device_id_type=MESH).wait_recv()
        for cp in first + passed:
            cp.wait_send()

    def peers():
        x, y, c = _mesh_pos()
        return [(cx, cy, c) for (cx, cy) in _other_chips(x, y)] + [(x, y, 1 - c)]

    return _on_sequencer(
        body, name, shards, [jax.ShapeDtypeStruct((N_CHIPS,) + s.shape, s.dtype) for s in shards],
        [pltpu.SemaphoreType.DMA((n, 3))] * 4, peers, GATHER_COLLECTIVE_ID, deps)


def place_own(gathered, shards, chip, name):
    n = len(shards)

    def body(chip_ref, *refs):
        for i in range(n):
            refs[2 * n + i][0] = refs[i][...]

    in_specs = [pl.BlockSpec(s.shape, lambda i, chip_ref: (0, 0)) for s in shards] + [ANY] * n
    out_specs = [pl.BlockSpec((1,) + s.shape, lambda i, chip_ref: (chip_ref[0], 0, 0)) for s in shards]
    return pl.pallas_call(
        body, name=name,
        grid_spec=pltpu.PrefetchScalarGridSpec(num_scalar_prefetch=1, grid=(1,), in_specs=in_specs, out_specs=out_specs),
        out_shape=[jax.ShapeDtypeStruct(g.shape, g.dtype) for g in gathered],
        input_output_aliases={1 + n + i: i for i in range(n)},
        compiler_params=_params("arbitrary"),
    )(chip, *shards, *gathered)


def all_gather_rows(shard, name):
    def body(in_ref, out_ref, send, recv, lsem):
        x, y, c = _mesh_pos()
        k = 2 * x + y
        chips = _other_chips(x, y)
        local = pltpu.make_async_copy(in_ref, out_ref.at[k], lsem)
        local.start()
        sent = [pltpu.make_async_remote_copy(src_ref=in_ref, dst_ref=out_ref.at[k], send_sem=send.at[j], recv_sem=recv.at[j],
                                             device_id=(cx, cy, c), device_id_type=MESH) for j, (cx, cy) in enumerate(chips)]
        for cp in sent:
            cp.start()
        for j, (cx, cy) in enumerate(chips):
            blk = out_ref.at[2 * cx + cy]
            pltpu.make_async_remote_copy(src_ref=blk, dst_ref=blk, send_sem=send.at[j], recv_sem=recv.at[j],
                                         device_id=(cx, cy, c), device_id_type=MESH).wait_recv()
        for cp in sent:
            cp.wait_send()
        local.wait()

    return pl.pallas_call(
        body, name=name, in_specs=[ANY], out_specs=ANY,
        out_shape=jax.ShapeDtypeStruct((N_CHIPS,) + shard.shape, shard.dtype),
        scratch_shapes=[pltpu.SemaphoreType.DMA((3,)), pltpu.SemaphoreType.DMA((3,)), pltpu.SemaphoreType.DMA],
    )(shard)


def swap_halves(gs, name):
    n = len(gs)

    def body(*refs):
        ins, outs = refs[:n], refs[n:2 * n]
        send, recv = refs[2 * n:]
        x, y, c = _mesh_pos()
        sent = []
        for i in range(n):
            h = gs[i].shape[1] // 2
            src = ins[i].at[:, pl.ds(pl.multiple_of((1 - c) * h, 8), h), :]
            cp = pltpu.make_async_remote_copy(src_ref=src, dst_ref=outs[i], send_sem=send.at[i], recv_sem=recv.at[i],
                                              device_id=(x, y, 1 - c), device_id_type=MESH)
            cp.start()
            sent.append(cp)
        for cp in sent:
            cp.wait()

    return _on_sequencer(
        body, name, gs, [jax.ShapeDtypeStruct((N_CHIPS, g.shape[1] // 2, g.shape[2]), g.dtype) for g in gs],
        [pltpu.SemaphoreType.DMA((n,)), pltpu.SemaphoreType.DMA((n,))], _sibling, SIBLING_COLLECTIVE_ID)


def scatter_chips(hs, name):
    n = len(hs)

    def body(*refs):
        ins, outs = refs[:n], refs[n:2 * n]
        send, recv = refs[2 * n:]
        x, y, c = _mesh_pos()
        chips = _other_chips(x, y)
        sent = []
        for i in range(n):
            for j, (cx, cy) in enumerate(chips):
                cp = pltpu.make_async_remote_copy(src_ref=ins[i].at[2 * cx + cy], dst_ref=outs[i].at[j],
                                                  send_sem=send.at[i, j], recv_sem=recv.at[i, j],
                                                  device_id=(cx, cy, c), device_id_type=MESH)
                cp.start()
                sent.append(cp)
        for cp in sent:
            cp.wait()

    return _on_sequencer(
        body, name, hs, [jax.ShapeDtypeStruct((N_CHIPS - 1,) + h.shape[1:], h.dtype) for h in hs],
        [pltpu.SemaphoreType.DMA((n, 3)), pltpu.SemaphoreType.DMA((n, 3))], _same_core_of_other_chips, CHIPS_COLLECTIVE_ID)


def swap_reduced(rs, name):
    n = len(rs)

    def body(*refs):
        ins, outs = refs[:n], refs[n:2 * n]
        send, recv = refs[2 * n:]
        x, y, c = _mesh_pos()
        sent = []
        for i in range(n):
            cp = pltpu.make_async_remote_copy(src_ref=ins[i], dst_ref=outs[i], send_sem=send.at[i], recv_sem=recv.at[i],
                                              device_id=(x, y, 1 - c), device_id_type=MESH)
            cp.start()
            sent.append(cp)
        for cp in sent:
            cp.wait()

    return _on_sequencer(
        body, name, rs, [jax.ShapeDtypeStruct(r.shape, r.dtype) for r in rs],
        [pltpu.SemaphoreType.DMA((n,)), pltpu.SemaphoreType.DMA((n,))], _sibling, SIBLING_COLLECTIVE_ID)


def all_gather_devices(parts, name):
    n = len(parts)

    def everyone_else():
        x, y, c = _mesh_pos()
        return [(1 - x if (rel >> 2) & 1 else x, 1 - y if (rel >> 1) & 1 else y, 1 - c if rel & 1 else c) for rel in range(1, 8)]

    def body(*refs):
        ins, outs = refs[:n], refs[n:2 * n]
        send, recv = refs[2 * n:]
        sent = []
        for slot, peer in enumerate(everyone_else()):
            for i in range(n):
                cp = pltpu.make_async_remote_copy(src_ref=ins[i], dst_ref=outs[i].at[slot], send_sem=send.at[i, slot],
                                                  recv_sem=recv.at[i, slot], device_id=peer, device_id_type=MESH)
                cp.start()
                sent.append(cp)
        for cp in sent:
            cp.wait()

    return _on_sequencer(
        body, name, parts, [jax.ShapeDtypeStruct((7,) + p.shape, p.dtype) for p in parts],
        [pltpu.SemaphoreType.DMA((n, 7)), pltpu.SemaphoreType.DMA((n, 7))], everyone_else, ALL_COLLECTIVE_ID)


class ReduceScatter:
    def __init__(self, grads, names, layer, core, chip, name):
        self.grads, self.names, self.layer, self.core, self.chip, self.name = grads, names, layer, core, chip, name
        self.from_sibling = swap_halves(grads, name + "_swap")

    def sum_cores(self, deps=()):
        self.core_sums = add_halves(self.grads, self.from_sibling, self.core, self.name + "_add2", deps)
        self.from_chips = scatter_chips(self.core_sums, self.name + "_scatter")
        return self.core_sums[0]

    def sum_chips(self, deps=()):
        self.half_sums = add_final(self.core_sums, self.from_chips, self.chip, self.name + "_add4", deps)
        self.other_half = swap_reduced(self.half_sums, self.name + "_join")
        return self.half_sums[0]


def _blocked(w):
    return w.reshape(w.shape[0] * w.shape[1], w.shape[2])


def _grp_from_blocks(w):
    return w.reshape(N_CHIPS, 4, 64, C_GROUP_DIM).transpose(1, 0, 2, 3).reshape(4, C_GROUP_DIM, C_GROUP_DIM)


def _grp_to_blocks(w):
    return w.reshape(4, N_CHIPS, 64, C_GROUP_DIM).transpose(1, 0, 2, 3).reshape(N_CHIPS, C_GROUP_DIM, C_GROUP_DIM)


def _dw_cols(h, dact, hs, name, deps):
    return mm_tn(h[None], dact, (N_CHIPS, D, hs), D, hs, N_CHIPS, lambda j: (0, 0), lambda j: (0, j), lambda j: (j, 0, 0), name, deps)


def _dw_rows(y, dxm, name, deps):
    tm = 512
    out = mm_tn(y[None], dxm[None], (1, D, D), tm, D, D // tm, lambda j: (0, j), lambda j: (0, 0), lambda j: (0, j, 0), name, deps)
    return out.reshape(N_CHIPS, D // N_CHIPS, D)


def _dw_hidden(act, other, name, deps):
    tm, tn = FFN_HIDDEN // 2, (D if other.dtype == BF16 else 256)
    n_n = D // tn
    out = mm_tn(act[None], other[None], (1, FFN_HIDDEN, D), tm, tn, 2 * n_n,
                lambda j: (0, j // n_n), lambda j: (0, j % n_n), lambda j: (0, j // n_n, j % n_n), name, deps)
    return out.reshape(N_CHIPS, FFN_SHARD, D)


def kernel(x, norm_mix_g, norm_ffn_g, final_norm_g, a_w_in, a_v_norm_g, a_w_s, a_b_s, a_w_out, b_w_in, b_conv_w, b_w_out, c_w_in, c_w_grp, c_scale, c_w_out, f_w_gate, f_w_up, f_w_down, loss_target, m_norm_mix_g, m_norm_ffn_g, m_final_norm_g, m_a_w_in, m_a_v_norm_g, m_a_w_s, m_a_b_s, m_a_w_out, m_b_w_in, m_b_conv_w, m_b_w_out, m_c_w_in, m_c_w_grp, m_c_scale, m_c_w_out, m_f_w_gate, m_f_w_up, m_f_w_down, v_norm_mix_g, v_norm_ffn_g, v_final_norm_g, v_a_w_in, v_a_v_norm_g, v_a_w_s, v_a_b_s, v_a_w_out, v_b_w_in, v_b_conv_w, v_b_w_out, v_c_w_in, v_c_w_grp, v_c_scale, v_c_w_out, v_f_w_gate, v_f_w_up, v_f_w_down):
    n_ex, seq, _ = x.shape
    t = n_ex * seq
    xi, yi, ci = lax.axis_index("x"), lax.axis_index("y"), lax.axis_index("c")
    chip = (2 * xi + yi).astype(jnp.int32)
    core_arr = ci.astype(jnp.int32).reshape(1)
    chip_arr = chip.reshape(1)
    me_arr = (4 * xi + 2 * yi + ci).astype(jnp.int32).reshape(1)
    bf = lambda w: w.astype(BF16)

    pad8 = lambda v: jnp.pad(v, ((0, 8 - v.shape[0]), (0, 0)))
    small_rows = jnp.concatenate([pad8(a_v_norm_g), pad8(b_conv_w[0]), pad8(c_scale)], axis=0)
    small_gathered = all_gather_rows(small_rows, "ag_small")
    small_full = small_gathered.transpose(1, 0, 2).reshape(24, D)
    gv_full = [small_full[0:1], small_full[1:2]]
    cw_full = small_full[8:11]
    scale_full = small_full[16:17]

    mixer_shards = [
        [bf(a_w_in[0]), bf(a_w_out[0])],
        [bf(b_w_in[0]), bf(b_w_out[0])],
        [bf(c_w_in[0]), bf(c_w_grp[0]).reshape(C_GROUP_DIM, C_GROUP_DIM), bf(c_w_out[0])],
        [bf(a_w_in[1]), bf(a_w_out[1])],
    ]
    hidden_major = lambda w: jnp.swapaxes(w, 1, 2)
    gate_t, up_t = hidden_major(f_w_gate), hidden_major(f_w_up)
    gathered = []
    for i in range(4):
        ffn_shards = [bf(gate_t[i]), bf(up_t[i]), bf(f_w_down[i])]
        if i == 0:
            parts = [(mixer_shards[0], "ag_l0_mixer", [small_gathered]), (ffn_shards, "ag_l0_ffn", [])]
        else:
            parts = [(mixer_shards[i] + ffn_shards, f"ag_l{i}", [])]
        layer = []
        for shards, name, deps in parts:
            layer += place_own(all_gather_weights(shards, name, deps), shards, chip_arr, name.replace("ag", "own"))
        gathered.append(layer)

    mask = (jnp.arange(GMLP_BLOCK)[None, :] // 64) <= (jnp.arange(GMLP_BLOCK)[:, None] // 64)
    gmix = [norm_mix_g[i:i + 1] for i in range(4)]
    gffn = [norm_ffn_g[i:i + 1] for i in range(4)]
    b_chunks = [(j // 3, (j % 3) * 256, 256, j // 4, (j % 4) * 256) for j in range(12)]
    c_chunks = [(0, 0, D, 0, 0)]

    xs = [x.reshape(t, D)]
    saved = []
    for i in range(4):
        ws = gathered[i]
        wg, wu, wd = (w.reshape(FFN_HIDDEN, D) for w in ws[-3:])
        xin = xs[-1]
        if i in (0, 3):
            j = 0 if i == 0 else 1
            win, wout = ws[0], _blocked(ws[1])
            wm32 = jnp.where(mask[None], a_w_s[j], 0.0)
            wm, wmt = bf(wm32), bf(wm32.transpose(0, 2, 1))
            bs = jnp.broadcast_to(a_b_s[j][:, :, None], (A_GROUPS, GMLP_BLOCK, 128))
            h, z, vn, y, xmid = a_fwd(xin, gmix[i], win, gv_full[j], wm, bs, wout, f"a_fwd_l{i}")
            saved.append(dict(h=h, z=z, y=y, vn=vn, win=win, wout=wout, wm=wm, wmt=wmt, bs=bs, gv=gv_full[j]))
        elif i == 1:
            win, wout = ws[0], _blocked(ws[1])
            h, p3 = norm_mm(xin, gmix[i], win, b_chunks, 3, D, "b_in")
            y = b_conv_fwd(p3, cw_full, seq, "b_conv")
            xmid = mm_res(y, wout, xin, "b_out")
            saved.append(dict(h=h, p3=p3, y=y, win=win, wout=wout))
        else:
            win, wgrp, wout = _blocked(ws[0])[None], _grp_from_blocks(ws[1]), _blocked(ws[2])
            h, p = norm_mm(xin, gmix[i], win, c_chunks, 1, D, "c_in")
            dpool = c_pool_fwd(p[0], seq, "c_pool")
            y, xmid = c_out_fwd(dpool, wgrp, scale_full, wout, xin, "c_out")
            saved.append(dict(h=h, d=dpool, y=y, win=win, wgrp=wgrp, wout=wout))
        h2, fa, fb, fs, xout = ffn_fwd(xmid, gffn[i], wg, wu, wd, f"ffn_l{i}")
        saved[-1].update(h2=h2, fa=fa, fb=fb, fs=fs, xmid=xmid, wg=wg, wu=wu, wd=wd)
        xs.append(xout)

    dx, dg_final, loss_part = loss_head(xs[4], loss_target.reshape(t, D), final_norm_g[None], "loss_head")
    loss = lax.psum(loss_part[0, 0], ("x", "y", "c"))

    weights = dict(norm_mix_g=norm_mix_g, norm_ffn_g=norm_ffn_g, final_norm_g=final_norm_g, a_w_in=a_w_in, a_v_norm_g=a_v_norm_g,
                   a_w_s=a_w_s, a_b_s=a_b_s, a_w_out=a_w_out, b_w_in=b_w_in, b_conv_w=b_conv_w, b_w_out=b_w_out, c_w_in=c_w_in,
                   c_w_grp=c_w_grp, c_scale=c_scale, c_w_out=c_w_out, f_w_gate=f_w_gate, f_w_up=f_w_up, f_w_down=f_w_down)
    m_in = dict(norm_mix_g=m_norm_mix_g, norm_ffn_g=m_norm_ffn_g, final_norm_g=m_final_norm_g, a_w_in=m_a_w_in, a_v_norm_g=m_a_v_norm_g,
                a_w_s=m_a_w_s, a_b_s=m_a_b_s, a_w_out=m_a_w_out, b_w_in=m_b_w_in, b_conv_w=m_b_conv_w, b_w_out=m_b_w_out, c_w_in=m_c_w_in,
                c_w_grp=m_c_w_grp, c_scale=m_c_scale, c_w_out=m_c_w_out, f_w_gate=m_f_w_gate, f_w_up=m_f_w_up, f_w_down=m_f_w_down)
    v_in = dict(norm_mix_g=v_norm_mix_g, norm_ffn_g=v_norm_ffn_g, final_norm_g=v_final_norm_g, a_w_in=v_a_w_in, a_v_norm_g=v_a_v_norm_g,
                a_w_s=v_a_w_s, a_b_s=v_a_b_s, a_w_out=v_a_w_out, b_w_in=v_b_w_in, b_conv_w=v_b_conv_w, b_w_out=v_b_w_out, c_w_in=v_c_w_in,
                c_w_grp=v_c_w_grp, c_scale=v_c_scale, c_w_out=v_c_w_out, f_w_gate=v_f_w_gate, f_w_up=v_f_w_up, f_w_down=v_f_w_down)
    grp_rows = lambda a: a.reshape(1, C_GROUP_DIM, C_GROUP_DIM)
    same = lambda a: a
    to_stacked = {nme: same for nme in ("a_w_in", "a_w_out", "b_w_in", "b_w_out", "c_w_in", "c_w_out", "f_w_down")}
    to_stacked.update(f_w_gate=hidden_major, f_w_up=hidden_major, c_w_grp=grp_rows)
    from_stacked = dict(to_stacked, c_w_grp=lambda a: a.reshape(c_w_grp.shape))
    layer_tensors = {0: ["a_w_out", "a_w_in"], 1: ["b_w_out", "b_w_in"], 2: ["c_w_out", "c_w_in", "c_w_grp"], 3: ["a_w_out", "a_w_in"]}
    carried = {}

    def bs_rows(v):
        return jnp.pad(v[:, :, 0].reshape(1, D), ((0, 7), (0, 0)))

    def update(unit):
        done = []
        for pos, nme in enumerate(unit.names):
            stacked_layer = unit.layer if nme.startswith("f_") else (unit.layer // 3 if nme.startswith("a_") else 0)
            view = to_stacked[nme]
            carried[nme] = adamw_layer(view(weights[nme]), view(m_in[nme]), view(v_in[nme]), unit.half_sums[pos], unit.other_half[pos],
                                       core_arr, stacked_layer, carried.get(nme), f"adamw_{nme}_l{unit.layer}")
            done.append(carried[nme][0])
        return done

    ffn_names = ["f_w_gate", "f_w_up", "f_w_down"]
    dg_mix, dg_ffn = [None] * 4, [None] * 4
    small = {}
    newer = older = None
    for i in (3, 2, 1, 0):
        sv = saved[i]
        xin = xs[i]
        deps = ([newer.grads[0]] if newer else []) + ([older.half_sums[0]] if older else [])
        dxm, da, db, dg_ffn[i] = ffn_bwd(dx, sv["fa"], sv["fb"], sv["xmid"], gffn[i], sv["wg"], sv["wu"], sv["wd"], f"ffn_bwd_l{i}", deps)
        last = [newer.sum_cores([dxm])] if newer else []
        g_gate = _dw_hidden(da, sv["h2"], f"dw_gate_l{i}", last)
        g_up = _dw_hidden(db, sv["h2"], f"dw_up_l{i}", [g_gate])
        g_down = _dw_hidden(sv["fs"], dx, f"dw_down_l{i}", [g_up])
        last_ffn = [g_down]
        if i == 0:
            ffn_unit = ReduceScatter([g_down, g_gate, g_up], ["f_w_down", "f_w_gate", "f_w_up"], 0, core_arr, chip_arr, "rs_l0_ffn")
        if i in (0, 3):
            j = 0 if i == 0 else 1
            dx, dz, dwm, dbs, dgv, dg_mix[i] = a_bwd(dxm, xin, gmix[i], sv["z"], sv["vn"], sv["gv"], sv["wm"], sv["wmt"], sv["bs"],
                                                     sv["wout"], sv["win"], f"a_bwd_l{i}")
            if i == 0:
                early = [sum_devices(p, gth, me_arr, f"sum_small_grads_l123_{n}")
                         for n, (p, gth) in enumerate(zip(early_parts, early_gathered))]
                newer.sum_chips([dz] + update(older) + early)
                last_ffn.append(ffn_unit.sum_cores([newer.half_sums[0]]))
            dz = dz[None]
            g_in = _dw_cols(sv["h"], dz, 512, f"dw_a_in_l{i}", last_ffn)
            g_out = _dw_rows(sv["y"], dxm, f"dw_a_out_l{i}", [g_in])
            small[f"wm{j}"], small[f"bs{j}"], small[f"gv{j}"] = dwm, dbs, dgv
            mixer_grads = [g_out, g_in]
        elif i == 1:
            dy = mm_nt(dxm, sv["wout"], "b_dy")
            dp3, small["cw"] = b_conv_bwd(dy, sv["p3"], cw_full, seq, "b_conv_bwd")
            dx, dg_mix[i] = bwd_in(dp3, sv["win"], b_chunks, xin, gmix[i], dxm, "b_bwd_in")
            g_in = mm_tn(sv["h"][None], dp3, (N_CHIPS, D, 768), D, 256, 12,
                         lambda j: (0, 0), lambda j: (j // 4, j % 4), lambda j: (j // 3, 0, j % 3), "dw_b_in", [g_down])
            g_out = _dw_rows(sv["y"], dxm, "dw_b_out", [g_in])
            mixer_grads = [g_out, g_in]
        else:
            outs = c_out_bwd(dxm, sv["d"], sv["wgrp"], scale_full, sv["wout"], "c_out_bwd")
            dyp, dd, small["scale"] = outs[0], list(outs[1:5]), outs[5]
            dpool = c_pool_bwd(dd, seq, "c_pool_bwd")
            dp = jnp.concatenate(dpool, axis=1)[None]
            dx, dg_mix[i] = bwd_in(dp, sv["win"], c_chunks, xin, gmix[i], dxm, "c_bwd_in")
            g_in = _dw_rows(sv["h"], dp[0], "dw_c_in", [g_down])
            dcat = jnp.concatenate(sv["d"], axis=1)
            g_grp = mm_tn(dcat[None], dyp[None], (4, C_GROUP_DIM, C_GROUP_DIM), C_GROUP_DIM, C_GROUP_DIM, 4,
                          lambda j: (0, j), lambda j: (0, j), lambda j: (j, 0, 0), "dw_c_grp", [g_in])
            g_out = _dw_rows(sv["y"], dxm, "dw_c_out", [g_grp])
            mixer_grads = [g_out, g_in, _grp_to_blocks(g_grp)]
        if i > 0:
            unit = ReduceScatter(mixer_grads + [g_gate, g_up, g_down], layer_tensors[i] + ffn_names, i, core_arr, chip_arr, f"rs_l{i}")
        else:
            unit = ReduceScatter(mixer_grads, layer_tensors[0], 0, core_arr, chip_arr, "rs_l0_mixer")
        if newer and i > 0:
            newer.sum_chips([mixer_grads[0]] + (update(older) if older else []))
        if i == 1:
            early_parts = [jnp.concatenate(dg_mix[1:] + dg_ffn[1:] + [dg_final, small["gv1"], small["cw"], small["scale"],
                                                                     bs_rows(small["bs1"])], axis=0),
                           bf(small["wm1"].reshape(128, D))]
            early_gathered = all_gather_devices(early_parts, "ag_small_grads_l123")
        older, newer = newer, unit
    grad_x = dx.reshape(n_ex, seq, D)
    mixer_unit = newer
    ffn_unit.sum_chips(update(older) + [mixer_unit.grads[0]])
    mixer_unit.sum_cores([ffn_unit.half_sums[0]])
    mixer_unit.sum_chips(update(ffn_unit))
    updated = update(mixer_unit)

    late_parts = [jnp.concatenate([dg_mix[0], dg_ffn[0], small["gv0"], bs_rows(small["bs0"])], axis=0), bf(small["wm0"].reshape(128, D))]
    late_gathered = all_gather_devices(late_parts, "ag_small_grads_l0")
    late = [sum_devices(p, gth, me_arr, f"sum_small_grads_l0_{n}", updated) for n, (p, gth) in enumerate(zip(late_parts, late_gathered))]
    (early_rows, early_ws), (late_rows, late_ws) = early, late
    first_rows = lambda a, b, n: jnp.concatenate([a, b], axis=0).reshape(n, 8, D)[:, 0]
    g_norm_mix = first_rows(late_rows[0:8], early_rows[0:24], 4)
    g_norm_ffn = first_rows(late_rows[8:16], early_rows[24:48], 4)
    g_final = early_rows[48]
    g_gv = first_rows(late_rows[16:24], early_rows[56:64], 2)
    g_cw = early_rows[64:67]
    g_scale = early_rows[72:73]
    g_ws = jnp.where(mask[None, None], jnp.concatenate([late_ws, early_ws], axis=0).reshape(2, A_GROUPS, 128, 128), 0.0)
    g_bs = first_rows(late_rows[24:32], early_rows[80:88], 2).reshape(2, A_GROUPS, 128)
    col0 = chip * (D // N_CHIPS)
    cols = lambda v: lax.dynamic_slice_in_dim(v, col0, D // N_CHIPS, axis=1)

    small_grads = {
        "norm_mix_g": g_norm_mix, "norm_ffn_g": g_norm_ffn, "final_norm_g": g_final, "a_v_norm_g": cols(g_gv), "a_w_s": g_ws,
        "a_b_s": g_bs, "b_conv_w": cols(g_cw)[None], "c_scale": cols(g_scale),
    }
    results = {}
    for nme, g in small_grads.items():
        w = weights[nme]
        flat = lambda a: a.reshape(-1, w.shape[-1])
        dl, mn, vn = adamw(flat(w), flat(g), flat(m_in[nme]), flat(v_in[nme]), f"adamw_{nme}")
        results[nme] = tuple(o.reshape(w.shape) for o in (g, dl, mn, vn))
    for nme, outs in carried.items():
        results[nme] = tuple(from_stacked[nme](o) for o in outs)

    names = list(weights)
    return (loss, grad_x, *[results[n][0] for n in names], *[results[n][1] for n in names],
            *[results[n][2] for n in names], *[results[n][3] for n in names])
```

```python
import jax
import jax.numpy as jnp
from jax import lax
from jax.experimental import pallas as pl
from jax.experimental.pallas import tpu as pltpu
from jax.experimental.pallas import tpu_sc as plsc

F32 = jnp.float32
BF16 = jnp.bfloat16
D = 1024
FFN_SHARD = 704
GMLP_BLOCK = 128
A_GROUPS = 8
POOL_WINDOWS = (2, 4, 8, 16)
C_GROUP_DIM = 256
N_CHIPS = 4
EPS = 1e-6
ADAM_LR, ADAM_B1, ADAM_B2, ADAM_EPS, ADAM_WD, ADAM_STEP = 0.001, 0.9, 0.999, 1e-08, 0.01, 10
VMEM_LIMIT_BYTES = 56 * 1024 * 1024
FFN_HIDDEN = N_CHIPS * FFN_SHARD
FFN_FWD_CHUNKS = ((0, 768), (768, 768), (1536, 768), (2304, 512))
FFN_BWD_CHUNKS = ((0, 1024), (1024, 1024), (2048, 768))
FFN_FWD_ROWS = 512
FFN_BWD_ROWS = 256
A_ROWS = 512
MESH = pl.DeviceIdType.MESH
GATHER_COLLECTIVE_ID = 1
SIBLING_COLLECTIVE_ID = 2
CHIPS_COLLECTIVE_ID = 3
ALL_COLLECTIVE_ID = 4
ANY = pl.BlockSpec(memory_space=pl.ANY)
NT_DIMS = (((1,), (1,)), ((), ()))
TN_DIMS = (((0,), (0,)), ((), ()))
INV_SQRT2 = 0.7071067811865476
INV_SQRT_2PI = 0.3989422804014327


def _params(*semantics):
    return pltpu.CompilerParams(dimension_semantics=semantics, vmem_limit_bytes=VMEM_LIMIT_BYTES)


def _dot(a, b):
    return jnp.dot(a, b, preferred_element_type=F32)


def _dot_nt(a, b):
    return lax.dot_general(a, b, NT_DIMS, preferred_element_type=F32)


def _rms(x):
    r = lax.rsqrt(jnp.mean(x * x, axis=-1, keepdims=True) + EPS)
    return x * r, r


def _rms_bwd(x, g, dh):
    xh, r = _rms(x)
    dxh = dh * g
    dx = r * (dxh - xh * jnp.mean(dxh * xh, axis=-1, keepdims=True))
    return dx, jnp.sum(dh * xh, axis=0, keepdims=True)


def _gelu(x):
    return 0.5 * x * (1.0 + lax.erf(x * INV_SQRT2))


def _gelu_grad(x):
    return 0.5 * (1.0 + lax.erf(x * INV_SQRT2)) + x * jnp.exp(-0.5 * x * x) * INV_SQRT_2PI


def _shift_down(v, s, row):
    return jnp.where(row >= s, pltpu.roll(v, s, 0), 0.0)


def _shift_up(v, s, row):
    n = v.shape[0]
    return jnp.where(row < n - s, pltpu.roll(v, n - s, 0), 0.0)


def _row_tile(t, want):
    return want if t % want == 0 else t


def _after(body, first, deps):
    if not deps:
        return body

    def ordered(*refs):
        return body(*refs[:first], *refs[first + len(deps):])

    return ordered


def norm_mm(x, g, w, chunks, n_parts, part_width, name):
    t = x.shape[0]
    tm = _row_tile(t, 512)
    n_shards, _, hs = w.shape

    def body(x_ref, g_ref, w_ref, h_ref, p_ref):
        xh, _ = _rms(x_ref[...])
        h = (xh * g_ref[...]).astype(BF16)
        h_ref[...] = h
        for s in range(n_shards):
            res = _dot(h, w_ref[s]).astype(BF16)
            for (cs, wc, width, part, pc) in chunks:
                if cs == s:
                    p_ref[part, :, pc:pc + width] = res[:, wc:wc + width]

    return pl.pallas_call(
        body, name=name, grid=(t // tm,),
        in_specs=[pl.BlockSpec((tm, D), lambda i: (i, 0)), pl.BlockSpec((1, D), lambda i: (0, 0)),
                  pl.BlockSpec((n_shards, D, hs), lambda i: (0, 0, 0))],
        out_specs=[pl.BlockSpec((tm, D), lambda i: (i, 0)), pl.BlockSpec((n_parts, tm, part_width), lambda i: (0, i, 0))],
        out_shape=[jax.ShapeDtypeStruct((t, D), BF16), jax.ShapeDtypeStruct((n_parts, t, part_width), BF16)],
        compiler_params=_params("arbitrary"),
    )(x, g, w)


def mm_res(a, w, res, name):
    t, k = a.shape
    n = w.shape[1]
    tm = _row_tile(t, 512)

    def body(a_ref, w_ref, r_ref, o_ref):
        o_ref[...] = r_ref[...] + _dot(a_ref[...], w_ref[...])

    return pl.pallas_call(
        body, name=name, grid=(t // tm,),
        in_specs=[pl.BlockSpec((tm, k), lambda i: (i, 0)), pl.BlockSpec((k, n), lambda i: (0, 0)),
                  pl.BlockSpec((tm, n), lambda i: (i, 0))],
        out_specs=pl.BlockSpec((tm, n), lambda i: (i, 0)),
        out_shape=jax.ShapeDtypeStruct((t, n), F32),
        compiler_params=_params("arbitrary"),
    )(a, w, res)


def mm_nt(a, w, name):
    t, n = a.shape
    k = w.shape[0]
    tm = _row_tile(t, 512)

    def body(a_ref, w_ref, o_ref):
        o_ref[...] = _dot_nt(a_ref[...].astype(BF16), w_ref[...]).astype(BF16)

    return pl.pallas_call(
        body, name=name, grid=(t // tm,),
        in_specs=[pl.BlockSpec((tm, n), lambda i: (i, 0)), pl.BlockSpec((k, n), lambda i: (0, 0))],
        out_specs=pl.BlockSpec((tm, k), lambda i: (i, 0)),
        out_shape=jax.ShapeDtypeStruct((t, k), BF16),
        compiler_params=_params("arbitrary"),
    )(a, w)


def bwd_in(dp, w, chunks, x, g, dres, name):
    n_parts, t, part_width = dp.shape
    n_shards, _, hs = w.shape
    tm = _row_tile(t, 512)

    def body(dp_ref, w_ref, x_ref, g_ref, dres_ref, dx_ref, dg_ref):
        acc = jnp.zeros((tm, D), F32)
        for (cs, wc, width, part, pc) in chunks:
            acc = acc + _dot_nt(dp_ref[part, :, pc:pc + width], w_ref[cs, :, wc:wc + width])
        dx, dg = _rms_bwd(x_ref[...], g_ref[...], acc)
        dx_ref[...] = dres_ref[...] + dx

        @pl.when(pl.program_id(0) == 0)
        def _():
            dg_ref[...] = jnp.zeros_like(dg_ref)

        dg_ref[0:1, :] += dg

    return pl.pallas_call(
        body, name=name, grid=(t // tm,),
        in_specs=[pl.BlockSpec((n_parts, tm, part_width), lambda i: (0, i, 0)),
                  pl.BlockSpec((n_shards, D, hs), lambda i: (0, 0, 0)),
                  pl.BlockSpec((tm, D), lambda i: (i, 0)), pl.BlockSpec((1, D), lambda i: (0, 0)),
                  pl.BlockSpec((tm, D), lambda i: (i, 0))],
        out_specs=[pl.BlockSpec((tm, D), lambda i: (i, 0)), pl.BlockSpec((8, D), lambda i: (0, 0))],
        out_shape=[jax.ShapeDtypeStruct((t, D), F32), jax.ShapeDtypeStruct((8, D), F32)],
        compiler_params=_params("arbitrary"),
    )(dp, w, x, g, dres)


def mm_tn(a, b, out_shape, tm, tn, n_tiles, a_idx, b_idx, o_idx, name, deps=()):
    t = a.shape[1]

    def body(a_ref, b_ref, o_ref):
        o_ref[0] = lax.dot_general(a_ref[0].astype(BF16), b_ref[0].astype(BF16), TN_DIMS, preferred_element_type=F32).astype(BF16)

    return pl.pallas_call(
        _after(body, 2, deps), name=name, grid=(n_tiles,),
        in_specs=[pl.BlockSpec((1, t, tm), lambda j: (a_idx(j)[0], 0, a_idx(j)[1])),
                  pl.BlockSpec((1, t, tn), lambda j: (b_idx(j)[0], 0, b_idx(j)[1]))] + [ANY] * len(deps),
        out_specs=pl.BlockSpec((1, tm, tn), lambda j: o_idx(j)),
        out_shape=jax.ShapeDtypeStruct(out_shape, BF16),
        compiler_params=_params("arbitrary"),
    )(a, b, *deps)


def ffn_fwd(x, g, wg, wu, wd, name):
    t = x.shape[0]
    tm = _row_tile(t, FFN_FWD_ROWS)
    hidden = wg.shape[0]

    def body(x_ref, g_ref, wg_ref, wu_ref, wd_ref, h_ref, a_ref, b_ref, s_ref, o_ref):
        xv = x_ref[...]
        xh, _ = _rms(xv)
        h = (xh * g_ref[...]).astype(BF16)
        h_ref[...] = h
        acc = xv
        for c0, cw in FFN_FWD_CHUNKS:
            cols = slice(c0, c0 + cw)
            a = _dot_nt(h, wg_ref[cols, :])
            b = _dot_nt(h, wu_ref[cols, :])
            sig = jax.nn.sigmoid(a)
            silu = a * sig
            s = (silu * b).astype(BF16)
            a_ref[:, cols] = (b * (sig * (1.0 + a * (1.0 - sig)))).astype(BF16)
            b_ref[:, cols] = silu.astype(BF16)
            s_ref[:, cols] = s
            acc = acc + _dot(s, wd_ref[cols, :])
        o_ref[...] = acc

    act = pl.BlockSpec((tm, hidden), lambda i: (i, 0))
    act_shape = jax.ShapeDtypeStruct((t, hidden), BF16)
    wspec = pl.BlockSpec((hidden, D), lambda i: (0, 0), pipeline_mode=pl.Buffered(1))
    row = pl.BlockSpec((tm, D), lambda i: (i, 0))
    return pl.pallas_call(
        body, name=name, grid=(t // tm,),
        in_specs=[row, pl.BlockSpec((1, D), lambda i: (0, 0)), wspec, wspec, wspec],
        out_specs=[row, act, act, act, row],
        out_shape=[jax.ShapeDtypeStruct((t, D), BF16), act_shape, act_shape, act_shape, jax.ShapeDtypeStruct((t, D), F32)],
        compiler_params=_params("arbitrary"),
    )(x, g, wg, wu, wd)


def ffn_bwd(dxo, a, b, x, g, wg, wu, wd, name, deps=()):
    t = x.shape[0]
    tm = _row_tile(t, FFN_BWD_ROWS)
    hidden = wg.shape[0]

    def body(dxo_ref, a_ref, b_ref, x_ref, g_ref, wg_ref, wu_ref, wd_ref, dx_ref, da_ref, db_ref, dg_ref):
        @pl.when(pl.program_id(0) == 0)
        def _():
            dg_ref[...] = jnp.zeros_like(dg_ref)

        dxo = dxo_ref[...]
        dxb = dxo.astype(BF16)
        acc = None
        for c0, cw in FFN_BWD_CHUNKS:
            cols = slice(c0, c0 + cw)
            ds = _dot_nt(dxb, wd_ref[cols, :])
            da = (ds * a_ref[:, cols].astype(F32)).astype(BF16)
            db = (ds * b_ref[:, cols].astype(F32)).astype(BF16)
            da_ref[:, cols] = da
            db_ref[:, cols] = db
            part = _dot(da, wg_ref[cols, :]) + _dot(db, wu_ref[cols, :])
            acc = part if acc is None else acc + part
        dx, dg = _rms_bwd(x_ref[...], g_ref[...], acc)
        dx_ref[...] = dxo + dx
        dg_ref[0:1, :] += dg

    act = pl.BlockSpec((tm, hidden), lambda i: (i, 0))
    act_shape = jax.ShapeDtypeStruct((t, hidden), BF16)
    row = pl.BlockSpec((tm, D), lambda i: (i, 0))
    wspec = pl.BlockSpec((hidden, D), lambda i: (0, 0), pipeline_mode=pl.Buffered(1))
    return pl.pallas_call(
        _after(body, 8, deps), name=name, grid=(t // tm,),
        in_specs=[row, act, act, row, pl.BlockSpec((1, D), lambda i: (0, 0)), wspec, wspec, wspec] + [ANY] * len(deps),
        out_specs=[row, act, act, pl.BlockSpec((8, D), lambda i: (0, 0))],
        out_shape=[jax.ShapeDtypeStruct((t, D), F32), act_shape, act_shape, jax.ShapeDtypeStruct((8, D), F32)],
        compiler_params=_params("arbitrary"),
    )(dxo, a, b, x, g, wg, wu, wd, *deps)


def _layer_norm_stats(v):
    mu = jnp.mean(v, axis=-1, keepdims=True)
    vc = v - mu
    rstd = lax.rsqrt(jnp.mean(vc * vc, axis=-1, keepdims=True) + EPS)
    return vc * rstd, rstd


def a_fwd(x, g, win, gv, wm, bs, wout, name):
    t = x.shape[0]
    tm = _row_tile(t, A_ROWS)
    n_shards, _, hs = win.shape

    def body(x_ref, g_ref, win_ref, gv_ref, wm_ref, bs_ref, wout_ref, h_ref, z_ref, vn_ref, y_ref, o_ref):
        xv = x_ref[...]
        xh, _ = _rms(xv)
        h = (xh * g_ref[...]).astype(BF16)
        h_ref[...] = h
        zs = []
        for s in range(n_shards):
            zb = _dot(h, win_ref[s]).astype(BF16)
            z_ref[:, s * hs:(s + 1) * hs] = zb
            zs.append(zb.astype(F32))
        half = n_shards // 2
        u = _gelu(jnp.concatenate(zs[:half], axis=1))
        vhat, _ = _layer_norm_stats(_gelu(jnp.concatenate(zs[half:], axis=1)))
        vnb = (vhat * gv_ref[...]).astype(BF16)
        vn_ref[...] = vnb
        for n in range(tm // GMLP_BLOCK):
            rows = slice(n * GMLP_BLOCK, (n + 1) * GMLP_BLOCK)
            for grp in range(A_GROUPS):
                cols = slice(grp * 128, (grp + 1) * 128)
                sv = _dot(wm_ref[grp], vnb[rows, cols]) + bs_ref[grp]
                y_ref[rows, cols] = (u[rows, cols] * sv).astype(BF16)
        o_ref[...] = xv + _dot(y_ref[...], wout_ref[...])

    small = pl.BlockSpec((A_GROUPS, 128, 128), lambda i: (0, 0, 0))
    row = pl.BlockSpec((tm, D), lambda i: (i, 0))
    wide = pl.BlockSpec((tm, 2 * D), lambda i: (i, 0))
    gain = pl.BlockSpec((1, D), lambda i: (0, 0))
    return pl.pallas_call(
        body, name=name, grid=(t // tm,),
        in_specs=[row, gain, pl.BlockSpec((n_shards, D, hs), lambda i: (0, 0, 0)), gain, small, small,
                  pl.BlockSpec((D, D), lambda i: (0, 0))],
        out_specs=[row, wide, row, row, row],
        out_shape=[jax.ShapeDtypeStruct((t, D), BF16), jax.ShapeDtypeStruct((t, 2 * D), BF16), jax.ShapeDtypeStruct((t, D), BF16),
                   jax.ShapeDtypeStruct((t, D), BF16), jax.ShapeDtypeStruct((t, D), F32)],
        compiler_params=_params("arbitrary"),
    )(x, g, win, gv, wm, bs, wout)


def a_bwd(dxm, x, g, z, vn, gv, wm, wmt, bs, wout, win, name):
    t = x.shape[0]
    tm = _row_tile(t, A_ROWS)
    n_shards, _, hs = win.shape

    def body(dxm_ref, x_ref, g_ref, z_ref, vn_ref, gv_ref, wm_ref, wmt_ref, bs_ref, wout_ref, win_ref,
             dx_ref, dz_ref, dwm_ref, dbs_ref, dgv_ref, dg_ref, du_ref, dvn_ref):
        @pl.when(pl.program_id(0) == 0)
        def _():
            dwm_ref[...] = jnp.zeros_like(dwm_ref)
            dbs_ref[...] = jnp.zeros_like(dbs_ref)
            dgv_ref[...] = jnp.zeros_like(dgv_ref)
            dg_ref[...] = jnp.zeros_like(dg_ref)

        dxm = dxm_ref[...]
        dyv = _dot_nt(dxm.astype(BF16), wout_ref[...])
        zz = z_ref[...].astype(F32)
        zu, zv = zz[:, :D], zz[:, D:]
        u = _gelu(zu)
        vhat, rstd = _layer_norm_stats(_gelu(zv))
        vnb = vn_ref[...]
        ones = jnp.ones((128, 128), BF16)
        for n in range(tm // GMLP_BLOCK):
            rows = slice(n * GMLP_BLOCK, (n + 1) * GMLP_BLOCK)
            for grp in range(A_GROUPS):
                cols = slice(grp * 128, (grp + 1) * 128)
                blk = vnb[rows, cols]
                sv = _dot(wm_ref[grp], blk) + bs_ref[grp]
                dyb = dyv[rows, cols]
                du_ref[rows, cols] = dyb * sv
                dsv = (dyb * u[rows, cols]).astype(BF16)
                dvn_ref[rows, cols] = _dot(wmt_ref[grp], dsv)
                dwm_ref[grp] += _dot_nt(dsv, blk)
                dbs_ref[grp] += _dot(dsv, ones)
        dvn = dvn_ref[...]
        dgv_ref[0:1, :] += jnp.sum(dvn * vhat, axis=0, keepdims=True)
        dvh = dvn * gv_ref[...]
        dv = rstd * (dvh - jnp.mean(dvh, axis=-1, keepdims=True) - vhat * jnp.mean(dvh * vhat, axis=-1, keepdims=True))
        dz_ref[:, :D] = (du_ref[...] * _gelu_grad(zu)).astype(BF16)
        dz_ref[:, D:] = (dv * _gelu_grad(zv)).astype(BF16)
        dh = None
        for s in range(n_shards):
            part = _dot_nt(dz_ref[:, s * hs:(s + 1) * hs], win_ref[s])
            dh = part if dh is None else dh + part
        dx, dg = _rms_bwd(x_ref[...], g_ref[...], dh)
        dx_ref[...] = dxm + dx
        dg_ref[0:1, :] += dg

    small = pl.BlockSpec((A_GROUPS, 128, 128), lambda i: (0, 0, 0))
    row = pl.BlockSpec((tm, D), lambda i: (i, 0))
    wide = pl.BlockSpec((tm, 2 * D), lambda i: (i, 0))
    gain = pl.BlockSpec((1, D), lambda i: (0, 0))
    sums = pl.BlockSpec((8, D), lambda i: (0, 0))
    small_shape = jax.ShapeDtypeStruct((A_GROUPS, 128, 128), F32)
    sums_shape = jax.ShapeDtypeStruct((8, D), F32)
    return pl.pallas_call(
        body, name=name, grid=(t // tm,),
        in_specs=[row, row, gain, wide, row, gain, small, small, small, pl.BlockSpec((D, D), lambda i: (0, 0)),
                  pl.BlockSpec((n_shards, D, hs), lambda i: (0, 0, 0))],
        out_specs=[row, wide, small, small, sums, sums],
        out_shape=[jax.ShapeDtypeStruct((t, D), F32), jax.ShapeDtypeStruct((t, 2 * D), BF16), small_shape, small_shape,
                   sums_shape, sums_shape],
        scratch_shapes=[pltpu.VMEM((tm, D), F32), pltpu.VMEM((tm, D), F32)],
        compiler_params=_params("arbitrary"),
    )(dxm, x, g, z, vn, gv, wm, wmt, bs, wout, win)


def _conv_terms(p_ref, row):
    gb = p_ref[0].astype(F32)
    gc = p_ref[1].astype(F32)
    xt = p_ref[2].astype(F32)
    q = gc * xt
    return gb, gc, xt, q, _shift_down(q, 1, row), _shift_down(q, 2, row)


def b_conv_fwd(p3, cw, seq, name):
    t = p3.shape[1]
    cb = 256

    def body(p_ref, cw_ref, y_ref):
        row = lax.broadcasted_iota(jnp.int32, (seq, cb), 0)
        gb, _, _, q, q1, q2 = _conv_terms(p_ref, row)
        y_ref[...] = (gb * (cw_ref[2:3, :] * q + cw_ref[1:2, :] * q1 + cw_ref[0:1, :] * q2)).astype(BF16)

    return pl.pallas_call(
        body, name=name, grid=(t // seq, D // cb),
        in_specs=[pl.BlockSpec((3, seq, cb), lambda e, c: (0, e, c)), pl.BlockSpec((3, cb), lambda e, c: (0, c))],
        out_specs=pl.BlockSpec((seq, cb), lambda e, c: (e, c)),
        out_shape=jax.ShapeDtypeStruct((t, D), BF16),
        compiler_params=_params("arbitrary", "arbitrary"),
    )(p3, cw)


def b_conv_bwd(dy, p3, cw, seq, name):
    t = p3.shape[1]
    cb = 256

    def body(dy_ref, p_ref, cw_ref, dp_ref, dcw_ref):
        @pl.when(pl.program_id(1) == 0)
        def _():
            dcw_ref[...] = jnp.zeros_like(dcw_ref)

        row = lax.broadcasted_iota(jnp.int32, (seq, cb), 0)
        gb, gc, xt, q, q1, q2 = _conv_terms(p_ref, row)
        dyv = dy_ref[...].astype(F32)
        conv = cw_ref[2:3, :] * q + cw_ref[1:2, :] * q1 + cw_ref[0:1, :] * q2
        dyc = dyv * gb
        dq = cw_ref[2:3, :] * dyc + cw_ref[1:2, :] * _shift_up(dyc, 1, row) + cw_ref[0:1, :] * _shift_up(dyc, 2, row)
        dp_ref[0] = (dyv * conv).astype(BF16)
        dp_ref[1] = (dq * xt).astype(BF16)
        dp_ref[2] = (dq * gc).astype(BF16)
        dcw_ref[0:1, :] += jnp.sum(dyc * q2, axis=0, keepdims=True)
        dcw_ref[1:2, :] += jnp.sum(dyc * q1, axis=0, keepdims=True)
        dcw_ref[2:3, :] += jnp.sum(dyc * q, axis=0, keepdims=True)

    return pl.pallas_call(
        body, name=name, grid=(D // cb, t // seq),
        in_specs=[pl.BlockSpec((seq, cb), lambda c, e: (e, c)), pl.BlockSpec((3, seq, cb), lambda c, e: (0, e, c)),
                  pl.BlockSpec((3, cb), lambda c, e: (0, c))],
        out_specs=[pl.BlockSpec((3, seq, cb), lambda c, e: (0, e, c)), pl.BlockSpec((8, cb), lambda c, e: (0, c))],
        out_shape=[jax.ShapeDtypeStruct((3, t, D), BF16), jax.ShapeDtypeStruct((8, D), F32)],
        compiler_params=_params("arbitrary", "arbitrary"),
    )(dy, p3, cw)


def c_pool_fwd(p, seq, name):
    t = p.shape[0]

    def make(grp):
        w = POOL_WINDOWS[grp]

        def body_g(p_ref, d_ref):
            row = lax.broadcasted_iota(jnp.int32, (seq, C_GROUP_DIM), 0)
            pv = p_ref[...].astype(F32)
            acc = pv
            sh = 1
            while sh < w:
                acc = acc + _shift_down(acc, sh, row)
                sh *= 2
            d_ref[...] = (acc / jnp.minimum(row + 1, w).astype(F32) - pv).astype(BF16)

        return body_g

    outs = []
    for grp in range(len(POOL_WINDOWS)):
        outs.append(pl.pallas_call(
            make(grp), name=f"{name}_g{grp}", grid=(t // seq,),
            in_specs=[pl.BlockSpec((seq, C_GROUP_DIM), lambda e, grp=grp: (e, grp))],
            out_specs=pl.BlockSpec((seq, C_GROUP_DIM), lambda e: (e, 0)),
            out_shape=jax.ShapeDtypeStruct((t, C_GROUP_DIM), BF16),
            compiler_params=_params("arbitrary"),
        )(p))
    return outs


def c_pool_bwd(dd, seq, name):
    t = dd[0].shape[0]

    def make(w):
        def body_g(dd_ref, dp_ref):
            row = lax.broadcasted_iota(jnp.int32, (seq, C_GROUP_DIM), 0)
            ddv = dd_ref[...]
            acc = ddv / jnp.minimum(row + 1, w).astype(F32)
            sh = 1
            while sh < w:
                acc = acc + _shift_up(acc, sh, row)
                sh *= 2
            dp_ref[...] = (acc - ddv).astype(BF16)

        return body_g

    outs = []
    for grp, w in enumerate(POOL_WINDOWS):
        outs.append(pl.pallas_call(
            make(w), name=f"{name}_g{grp}", grid=(t // seq,),
            in_specs=[pl.BlockSpec((seq, C_GROUP_DIM), lambda e: (e, 0))],
            out_specs=pl.BlockSpec((seq, C_GROUP_DIM), lambda e: (e, 0)),
            out_shape=jax.ShapeDtypeStruct((t, C_GROUP_DIM), BF16),
            compiler_params=_params("arbitrary"),
        )(dd[grp]))
    return outs


def c_out_fwd(d, wgrp, scale, wo, x, name):
    t = x.shape[0]
    tm = _row_tile(t, 512)
    n_g = len(POOL_WINDOWS)

    def body(d0, d1, d2, d3, wg_ref, sc_ref, wo_ref, x_ref, y_ref, o_ref):
        parts = [_dot(dr[...], wg_ref[grp]) for grp, dr in enumerate((d0, d1, d2, d3))]
        y = (jnp.concatenate(parts, axis=1) * sc_ref[...]).astype(BF16)
        y_ref[...] = y
        o_ref[...] = x_ref[...] + _dot(y, wo_ref[...])

    dspec = pl.BlockSpec((tm, C_GROUP_DIM), lambda i: (i, 0))
    row = pl.BlockSpec((tm, D), lambda i: (i, 0))
    return pl.pallas_call(
        body, name=name, grid=(t // tm,),
        in_specs=[dspec] * n_g + [pl.BlockSpec((n_g, C_GROUP_DIM, C_GROUP_DIM), lambda i: (0, 0, 0)),
                                  pl.BlockSpec((1, D), lambda i: (0, 0)), pl.BlockSpec((D, D), lambda i: (0, 0)), row],
        out_specs=[row, row],
        out_shape=[jax.ShapeDtypeStruct((t, D), BF16), jax.ShapeDtypeStruct((t, D), F32)],
        compiler_params=_params("arbitrary"),
    )(*d, wgrp, scale, wo, x)


def c_out_bwd(dxm, d, wgrp, scale, wo, name):
    t = dxm.shape[0]
    tm = _row_tile(t, 512)
    n_g = len(POOL_WINDOWS)

    def body(dxm_ref, d0, d1, d2, d3, wg_ref, sc_ref, wo_ref, dyp_ref, dd0, dd1, dd2, dd3, dsc_ref):
        @pl.when(pl.program_id(0) == 0)
        def _():
            dsc_ref[...] = jnp.zeros_like(dsc_ref)

        dyo = _dot_nt(dxm_ref[...].astype(BF16), wo_ref[...])
        ypre = jnp.concatenate([_dot(dr[...], wg_ref[grp]) for grp, dr in enumerate((d0, d1, d2, d3))], axis=1)
        dsc_ref[0:1, :] += jnp.sum(dyo * ypre, axis=0, keepdims=True)
        dyp = (dyo * sc_ref[...]).astype(BF16)
        dyp_ref[...] = dyp
        for grp, ddr in enumerate((dd0, dd1, dd2, dd3)):
            ddr[...] = _dot_nt(dyp[:, grp * C_GROUP_DIM:(grp + 1) * C_GROUP_DIM], wg_ref[grp])

    dspec = pl.BlockSpec((tm, C_GROUP_DIM), lambda i: (i, 0))
    row = pl.BlockSpec((tm, D), lambda i: (i, 0))
    dshape = jax.ShapeDtypeStruct((t, C_GROUP_DIM), F32)
    return pl.pallas_call(
        body, name=name, grid=(t // tm,),
        in_specs=[row] + [dspec] * n_g + [pl.BlockSpec((n_g, C_GROUP_DIM, C_GROUP_DIM), lambda i: (0, 0, 0)),
                                          pl.BlockSpec((1, D), lambda i: (0, 0)), pl.BlockSpec((D, D), lambda i: (0, 0))],
        out_specs=[row] + [dspec] * n_g + [pl.BlockSpec((8, D), lambda i: (0, 0))],
        out_shape=[jax.ShapeDtypeStruct((t, D), BF16)] + [dshape] * n_g + [jax.ShapeDtypeStruct((8, D), F32)],
        compiler_params=_params("arbitrary"),
    )(dxm, *d, wgrp, scale, wo)


def loss_head(x, tgt, g, name):
    t = x.shape[0]
    tm = _row_tile(t, 512)

    def body(x_ref, t_ref, g_ref, dx_ref, dg_ref, loss_ref):
        @pl.when(pl.program_id(0) == 0)
        def _():
            dg_ref[...] = jnp.zeros_like(dg_ref)
            loss_ref[...] = jnp.zeros_like(loss_ref)

        xv, gvv = x_ref[...], g_ref[...]
        xh, _ = _rms(xv)
        diff = xh * gvv - t_ref[...]
        loss_ref[...] += 0.5 * jnp.sum(jnp.mean(diff * diff, axis=-1, keepdims=True))
        dx, dg = _rms_bwd(xv, gvv, diff * (1.0 / D))
        dx_ref[...] = dx
        dg_ref[0:1, :] += dg

    row = pl.BlockSpec((tm, D), lambda i: (i, 0))
    return pl.pallas_call(
        body, name=name, grid=(t // tm,),
        in_specs=[row, row, pl.BlockSpec((1, D), lambda i: (0, 0))],
        out_specs=[row, pl.BlockSpec((8, D), lambda i: (0, 0)), pl.BlockSpec((8, 128), lambda i: (0, 0))],
        out_shape=[jax.ShapeDtypeStruct((t, D), F32), jax.ShapeDtypeStruct((8, D), F32), jax.ShapeDtypeStruct((8, 128), F32)],
        compiler_params=_params("arbitrary"),
    )(x, tgt, g)


def adamw(w, g, m, v, name):
    rows, cols = w.shape
    tr = rows
    for cand in (512, 256, 128, 64, 32, 16, 8):
        if rows % cand == 0 and rows > cand:
            tr = cand
            break

    def body(w_ref, g_ref, m_ref, v_ref, d_ref, mo_ref, vo_ref):
        gv = g_ref[...]
        mn = ADAM_B1 * m_ref[...] + (1.0 - ADAM_B1) * gv
        vn = ADAM_B2 * v_ref[...] + (1.0 - ADAM_B2) * (gv * gv)
        m_hat = mn / (1.0 - ADAM_B1 ** ADAM_STEP)
        v_hat = vn / (1.0 - ADAM_B2 ** ADAM_STEP)
        d_ref[...] = -ADAM_LR * (m_hat / (jnp.sqrt(v_hat) + ADAM_EPS) + ADAM_WD * w_ref[...])
        mo_ref[...] = mn
        vo_ref[...] = vn

    spec = pl.BlockSpec((tr, cols), lambda i: (i, 0))
    shape = jax.ShapeDtypeStruct((rows, cols), F32)
    return pl.pallas_call(
        body, name=name, grid=(rows // tr,),
        in_specs=[spec] * 4, out_specs=[spec] * 3, out_shape=[shape] * 3,
        compiler_params=_params("arbitrary"),
    )(w, g, m, v)


def adamw_layer(w, m, v, own, recv, core, layer, carried, name):
    n_layers, rows, cols = w.shape
    h = rows // 2

    def body(core_ref, w_ref, m_ref, v_ref, own_ref, recv_ref, *rest):
        g_ref, d_ref, mo_ref, vo_ref = rest[-4:]
        gv = jnp.where(pl.program_id(0) == core_ref[0], own_ref[...], recv_ref[...])
        mn = ADAM_B1 * m_ref[0] + (1.0 - ADAM_B1) * gv
        vn = ADAM_B2 * v_ref[0] + (1.0 - ADAM_B2) * (gv * gv)
        m_hat = mn / (1.0 - ADAM_B1 ** ADAM_STEP)
        v_hat = vn / (1.0 - ADAM_B2 ** ADAM_STEP)
        g_ref[0] = gv
        d_ref[0] = -ADAM_LR * (m_hat / (jnp.sqrt(v_hat) + ADAM_EPS) + ADAM_WD * w_ref[0])
        mo_ref[0] = mn
        vo_ref[0] = vn

    steps = 1
    tr = h // steps
    stacked = pl.BlockSpec((1, tr, cols), lambda half, j, core_ref: (layer, half * steps + j, 0))
    halfspec = pl.BlockSpec((tr, cols), lambda half, j, core_ref: (j, 0))
    n_carried = 0 if carried is None else 4
    shape = jax.ShapeDtypeStruct(w.shape, F32)
    return pl.pallas_call(
        body, name=name,
        grid_spec=pltpu.PrefetchScalarGridSpec(
            num_scalar_prefetch=1, grid=(2, steps),
            in_specs=[stacked] * 3 + [halfspec] * 2 + [ANY] * n_carried, out_specs=[stacked] * 4),
        out_shape=[shape] * 4,
        input_output_aliases={6 + i: i for i in range(n_carried)},
        compiler_params=_params("arbitrary", "arbitrary"),
    )(core, w, m, v, own, recv, *(carried or ()))


def add_halves(gs, ps, core, name, deps=()):
    n = len(gs)

    def body(core_ref, *refs):
        for i in range(n):
            refs[2 * n + i][...] = (refs[i][...].astype(F32) + refs[n + i][...].astype(F32)).astype(BF16)

    in_specs, out_specs, out_shape = [], [], []
    for gt in gs:
        h, c = gt.shape[1] // 2, gt.shape[2]
        in_specs.append(pl.BlockSpec((1, h, c), lambda b, core_ref: (b, core_ref[0], 0)))
    for gt in gs:
        h, c = gt.shape[1] // 2, gt.shape[2]
        in_specs.append(pl.BlockSpec((1, h, c), lambda b, core_ref: (b, 0, 0)))
        out_specs.append(pl.BlockSpec((1, h, c), lambda b, core_ref: (b, 0, 0)))
        out_shape.append(jax.ShapeDtypeStruct((N_CHIPS, h, c), BF16))
    in_specs += [ANY] * len(deps)
    return pl.pallas_call(
        _after(body, 1 + 2 * n, deps), name=name,
        grid_spec=pltpu.PrefetchScalarGridSpec(num_scalar_prefetch=1, grid=(N_CHIPS,), in_specs=in_specs, out_specs=out_specs),
        out_shape=out_shape, compiler_params=_params("arbitrary"),
    )(core, *gs, *ps, *deps)


def add_final(hs, qs, chip, name, deps=()):
    n = len(hs)

    def body(chip_ref, *refs):
        for i in range(n):
            q = refs[n + i]
            refs[2 * n + i][...] = ((refs[i][0].astype(F32) + q[0].astype(F32)) + q[1].astype(F32)) + q[2].astype(F32)

    steps = 2
    in_specs, out_specs, out_shape = [], [], []
    for ht in hs:
        h, c = ht.shape[1], ht.shape[2]
        in_specs.append(pl.BlockSpec((1, h // steps, c), lambda i, chip_ref: (chip_ref[0], i, 0)))
    for ht in hs:
        h, c = ht.shape[1], ht.shape[2]
        in_specs.append(pl.BlockSpec((N_CHIPS - 1, h // steps, c), lambda i, chip_ref: (0, i, 0)))
        out_specs.append(pl.BlockSpec((h // steps, c), lambda i, chip_ref: (i, 0)))
        out_shape.append(jax.ShapeDtypeStruct((h, c), F32))
    in_specs += [ANY] * len(deps)
    return pl.pallas_call(
        _after(body, 1 + 2 * n, deps), name=name,
        grid_spec=pltpu.PrefetchScalarGridSpec(num_scalar_prefetch=1, grid=(steps,), in_specs=in_specs, out_specs=out_specs),
        out_shape=out_shape, compiler_params=_params("arbitrary"),
    )(chip, *hs, *qs, *deps)


def sum_devices(own, gathered, me, name, deps=()):
    rows = own.shape[0]

    def body(me_ref, own_ref, g_ref, o_ref):
        me_dev = me_ref[0]
        acc = None
        for dev in range(8):
            slot = jnp.maximum((me_dev ^ dev) - 1, 0)
            term = jnp.where(me_dev == dev, own_ref[...], g_ref[slot]).astype(F32)
            acc = term if acc is None else acc + term
        o_ref[...] = acc

    tr = ([c for c in range(80, 0, -16) if rows % c == 0] or [rows])[0]
    return pl.pallas_call(
        _after(body, 3, deps), name=name,
        grid_spec=pltpu.PrefetchScalarGridSpec(
            num_scalar_prefetch=1, grid=(rows // tr,),
            in_specs=[pl.BlockSpec((tr, D), lambda i, me_ref: (i, 0)), pl.BlockSpec((7, tr, D), lambda i, me_ref: (0, i, 0))]
            + [ANY] * len(deps),
            out_specs=pl.BlockSpec((tr, D), lambda i, me_ref: (i, 0))),
        out_shape=jax.ShapeDtypeStruct((rows, D), F32),
        compiler_params=_params("arbitrary"),
    )(me, own, gathered, *deps)


def _mesh_pos():
    return lax.axis_index("x"), lax.axis_index("y"), lax.axis_index("c")


def _other_chips(x, y):
    return [(1 - x, y), (x, 1 - y), (1 - x, 1 - y)]


def _sibling():
    x, y, c = _mesh_pos()
    return [(x, y, 1 - c)]


def _same_core_of_other_chips():
    x, y, c = _mesh_pos()
    return [(cx, cy, c) for (cx, cy) in _other_chips(x, y)]


def _on_sequencer(body, name, operands, out_shapes, sems, peers, collective_id, deps=()):
    ordered = _after(body, len(operands), deps)

    def seq_body(*refs):
        barrier = pltpu.get_barrier_semaphore()
        with_whom = peers()
        for peer in with_whom:
            pl.semaphore_signal(barrier, inc=1, device_id=peer, device_id_type=MESH)
        pl.semaphore_wait(barrier, len(with_whom))
        ordered(*refs)

    return pl.kernel(
        seq_body, name=name, out_type=out_shapes,
        mesh=plsc.ScalarSubcoreMesh(axis_name="seq", num_cores=1),
        scratch_types=sems, compiler_params=pltpu.CompilerParams(collective_id=collective_id, has_side_effects=True),
    )(*operands, *deps)


def all_gather_weights(shards, name, deps=()):
    n = len(shards)

    def body(*refs):
        ins, outs = refs[:n], refs[n:2 * n]
        send, recv, fsend, frecv = refs[2 * n:]
        x, y, c = _mesh_pos()
        k = 2 * x + y
        chips = _other_chips(x, y)

        def half(ref, i, rows_half):
            h = shards[i].shape[0] // 2
            return ref.at[pl.ds(pl.multiple_of(rows_half * h, 8), h), :]

        first = []
        for i in range(n):
            for j, (cx, cy) in enumerate(chips):
                first.append(pltpu.make_async_remote_copy(
                    src_ref=half(ins[i], i, c), dst_ref=half(outs[i].at[k], i, c),
                    send_sem=send.at[i, j], recv_sem=recv.at[i, j], device_id=(cx, cy, c), device_id_type=MESH))
        for cp in first:
            cp.start()
        passed = []
        for i in range(n):
            for j, (cx, cy) in enumerate(chips):
                blk = half(outs[i].at[2 * cx + cy], i, c)
                pltpu.make_async_remote_copy(src_ref=blk, dst_ref=blk, send_sem=send.at[i, j], recv_sem=recv.at[i, j],
                                             device_id=(cx, cy, c), device_id_type=MESH).wait_recv()
                fw = pltpu.make_async_remote_copy(src_ref=blk, dst_ref=blk, send_sem=fsend.at[i, j], recv_sem=frecv.at[i, j],
                                                  device_id=(x, y, 1 - c), device_id_type=MESH)
                fw.start()
                passed.append(fw)
        for i in range(n):
            for j, (cx, cy) in enumerate(chips):
                blk = half(outs[i].at[2 * cx + cy], i, 1 - c)
                pltpu.make_async_remote_copy(src_ref=blk, dst_ref=blk, send_sem=fsend.at[i, j], recv_sem=frecv.at[i, j],
                                             device_id=(x, y, 1 - c), device_id_type=MESH).wait_recv()
        for cp in first + passed:
            cp.wait_send()

    def peers():
        x, y, c = _mesh_pos()
        return [(cx, cy, c) for (cx, cy) in _other_chips(x, y)] + [(x, y, 1 - c)]

    return _on_sequencer(
        body, name, shards, [jax.ShapeDtypeStruct((N_CHIPS,) + s.shape, s.dtype) for s in shards],
        [pltpu.SemaphoreType.DMA((n, 3))] * 4, peers, GATHER_COLLECTIVE_ID, deps)


def place_own(gathered, shards, chip, name):
    n = len(shards)

    def body(chip_ref, *refs):
        for i in range(n):
            refs[2 * n + i][0] = refs[i][...]

    in_specs = [pl.BlockSpec(s.shape, lambda i, chip_ref: (0, 0)) for s in shards] + [ANY] * n
    out_specs = [pl.BlockSpec((1,) + s.shape, lambda i, chip_ref: (chip_ref[0], 0, 0)) for s in shards]
    return pl.pallas_call(
        body, name=name,
        grid_spec=pltpu.PrefetchScalarGridSpec(num_scalar_prefetch=1, grid=(1,), in_specs=in_specs, out_specs=out_specs),
        out_shape=[jax.ShapeDtypeStruct(g.shape, g.dtype) for g in gathered],
        input_output_aliases={1 + n + i: i for i in range(n)},
        compiler_params=_params("arbitrary"),
    )(chip, *shards, *gathered)


def all_gather_rows(shard, name):
    def body(in_ref, out_ref, send, recv, lsem):
        x, y, c = _mesh_pos()
        k = 2 * x + y
        chips = _other_chips(x, y)
        local = pltpu.make_async_copy(in_ref, out_ref.at[k], lsem)
        local.start()
        sent = [pltpu.make_async_remote_copy(src_ref=in_ref, dst_ref=out_ref.at[k], send_sem=send.at[j], recv_sem=recv.at[j],
                                             device_id=(cx, cy, c), device_id_type=MESH) for j, (cx, cy) in enumerate(chips)]
        for cp in sent:
            cp.start()
        for j, (cx, cy) in enumerate(chips):
            blk = out_ref.at[2 * cx + cy]
            pltpu.make_async_remote_copy(src_ref=blk, dst_ref=blk, send_sem=send.at[j], recv_sem=recv.at[j],
                                         device_id=(cx, cy, c), device_id_type=MESH).wait_recv()
        for cp in sent:
            cp.wait_send()
        local.wait()

    return pl.pallas_call(
        body, name=name, in_specs=[ANY], out_specs=ANY,
        out_shape=jax.ShapeDtypeStruct((N_CHIPS,) + shard.shape, shard.dtype),
        scratch_shapes=[pltpu.SemaphoreType.DMA((3,)), pltpu.SemaphoreType.DMA((3,)), pltpu.SemaphoreType.DMA],
    )(shard)


def swap_halves(gs, name):
    n = len(gs)

    def body(*refs):
        ins, outs = refs[:n], refs[n:2 * n]
        send, recv = refs[2 * n:]
        x, y, c = _mesh_pos()
        sent = []
        for i in range(n):
            h = gs[i].shape[1] // 2
            src = ins[i].at[:, pl.ds(pl.multiple_of((1 - c) * h, 8), h), :]
            cp = pltpu.make_async_remote_copy(src_ref=src, dst_ref=outs[i], send_sem=send.at[i], recv_sem=recv.at[i],
                                              device_id=(x, y, 1 - c), device_id_type=MESH)
            cp.start()
            sent.append(cp)
        for cp in sent:
            cp.wait()

    return _on_sequencer(
        body, name, gs, [jax.ShapeDtypeStruct((N_CHIPS, g.shape[1] // 2, g.shape[2]), g.dtype) for g in gs],
        [pltpu.SemaphoreType.DMA((n,)), pltpu.SemaphoreType.DMA((n,))], _sibling, SIBLING_COLLECTIVE_ID)


def scatter_chips(hs, name):
    n = len(hs)

    def body(*refs):
        ins, outs = refs[:n], refs[n:2 * n]
        send, recv = refs[2 * n:]
        x, y, c = _mesh_pos()
        chips = _other_chips(x, y)
        sent = []
        for i in range(n):
            for j, (cx, cy) in enumerate(chips):
                cp = pltpu.make_async_remote_copy(src_ref=ins[i].at[2 * cx + cy], dst_ref=outs[i].at[j],
                                                  send_sem=send.at[i, j], recv_sem=recv.at[i, j],
                                                  device_id=(cx, cy, c), device_id_type=MESH)
                cp.start()
                sent.append(cp)
        for cp in sent:
            cp.wait()

    return _on_sequencer(
        body, name, hs, [jax.ShapeDtypeStruct((N_CHIPS - 1,) + h.shape[1:], h.dtype) for h in hs],
        [pltpu.SemaphoreType.DMA((n, 3)), pltpu.SemaphoreType.DMA((n, 3))], _same_core_of_other_chips, CHIPS_COLLECTIVE_ID)


def swap_reduced(rs, name):
    n = len(rs)

    def body(*refs):
        ins, outs = refs[:n], refs[n:2 * n]
        send, recv = refs[2 * n:]
        x, y, c = _mesh_pos()
        sent = []
        for i in range(n):
            cp = pltpu.make_async_remote_copy(src_ref=ins[i], dst_ref=outs[i], send_sem=send.at[i], recv_sem=recv.at[i],
                                              device_id=(x, y, 1 - c), device_id_type=MESH)
            cp.start()
            sent.append(cp)
        for cp in sent:
            cp.wait()

    return _on_sequencer(
        body, name, rs, [jax.ShapeDtypeStruct(r.shape, r.dtype) for r in rs],
        [pltpu.SemaphoreType.DMA((n,)), pltpu.SemaphoreType.DMA((n,))], _sibling, SIBLING_COLLECTIVE_ID)


def all_gather_devices(parts, name):
    n = len(parts)

    def everyone_else():
        x, y, c = _mesh_pos()
        return [(1 - x if (rel >> 2) & 1 else x, 1 - y if (rel >> 1) & 1 else y, 1 - c if rel & 1 else c) for rel in range(1, 8)]

    def body(*refs):
        ins, outs = refs[:n], refs[n:2 * n]
        send, recv = refs[2 * n:]
        sent = []
        for slot, peer in enumerate(everyone_else()):
            for i in range(n):
                cp = pltpu.make_async_remote_copy(src_ref=ins[i], dst_ref=outs[i].at[slot], send_sem=send.at[i, slot],
                                                  recv_sem=recv.at[i, slot], device_id=peer, device_id_type=MESH)
                cp.start()
                sent.append(cp)
        for cp in sent:
            cp.wait()

    return _on_sequencer(
        body, name, parts, [jax.ShapeDtypeStruct((7,) + p.shape, p.dtype) for p in parts],
        [pltpu.SemaphoreType.DMA((n, 7)), pltpu.SemaphoreType.DMA((n, 7))], everyone_else, ALL_COLLECTIVE_ID)


class ReduceScatter:
    def __init__(self, grads, names, layer, core, chip, name):
        self.grads, self.names, self.layer, self.core, self.chip, self.name = grads, names, layer, core, chip, name
        self.from_sibling = swap_halves(grads, name + "_swap")

    def sum_cores(self, deps=()):
        self.core_sums = add_halves(self.grads, self.from_sibling, self.core, self.name + "_add2", deps)
        self.from_chips = scatter_chips(self.core_sums, self.name + "_scatter")
        return self.core_sums[0]

    def sum_chips(self, deps=()):
        self.half_sums = add_final(self.core_sums, self.from_chips, self.chip, self.name + "_add4", deps)
        self.other_half = swap_reduced(self.half_sums, self.name + "_join")
        return self.half_sums[0]


def _blocked(w):
    return w.reshape(w.shape[0] * w.shape[1], w.shape[2])


def _grp_from_blocks(w):
    return w.reshape(N_CHIPS, 4, 64, C_GROUP_DIM).transpose(1, 0, 2, 3).reshape(4, C_GROUP_DIM, C_GROUP_DIM)


def _grp_to_blocks(w):
    return w.reshape(4, N_CHIPS, 64, C_GROUP_DIM).transpose(1, 0, 2, 3).reshape(N_CHIPS, C_GROUP_DIM, C_GROUP_DIM)


def _dw_cols(h, dact, hs, name, deps):
    return mm_tn(h[None], dact, (N_CHIPS, D, hs), D, hs, N_CHIPS, lambda j: (0, 0), lambda j: (0, j), lambda j: (j, 0, 0), name, deps)


def _dw_rows(y, dxm, name, deps):
    tm = 512
    out = mm_tn(y[None], dxm[None], (1, D, D), tm, D, D // tm, lambda j: (0, j), lambda j: (0, 0), lambda j: (0, j, 0), name, deps)
    return out.reshape(N_CHIPS, D // N_CHIPS, D)


def _dw_hidden(act, other, name, deps):
    tm, tn = FFN_HIDDEN // 2, (D if other.dtype == BF16 else 512)
    n_n = D // tn
    out = mm_tn(act[None], other[None], (1, FFN_HIDDEN, D), tm, tn, 2 * n_n,
                lambda j: (0, j // n_n), lambda j: (0, j % n_n), lambda j: (0, j // n_n, j % n_n), name, deps)
    return out.reshape(N_CHIPS, FFN_SHARD, D)


def kernel(x, norm_mix_g, norm_ffn_g, final_norm_g, a_w_in, a_v_norm_g, a_w_s, a_b_s, a_w_out, b_w_in, b_conv_w, b_w_out, c_w_in, c_w_grp, c_scale, c_w_out, f_w_gate, f_w_up, f_w_down, loss_target, m_norm_mix_g, m_norm_ffn_g, m_final_norm_g, m_a_w_in, m_a_v_norm_g, m_a_w_s, m_a_b_s, m_a_w_out, m_b_w_in, m_b_conv_w, m_b_w_out, m_c_w_in, m_c_w_grp, m_c_scale, m_c_w_out, m_f_w_gate, m_f_w_up, m_f_w_down, v_norm_mix_g, v_norm_ffn_g, v_final_norm_g, v_a_w_in, v_a_v_norm_g, v_a_w_s, v_a_b_s, v_a_w_out, v_b_w_in, v_b_conv_w, v_b_w_out, v_c_w_in, v_c_w_grp, v_c_scale, v_c_w_out, v_f_w_gate, v_f_w_up, v_f_w_down):
    n_ex, seq, _ = x.shape
    t = n_ex * seq
    xi, yi, ci = lax.axis_index("x"), lax.axis_index("y"), lax.axis_index("c")
    chip = (2 * xi + yi).astype(jnp.int32)
    core_arr = ci.astype(jnp.int32).reshape(1)
    chip_arr = chip.reshape(1)
    me_arr = (4 * xi + 2 * yi + ci).astype(jnp.int32).reshape(1)
    bf = lambda w: w.astype(BF16)

    pad8 = lambda v: jnp.pad(v, ((0, 8 - v.shape[0]), (0, 0)))
    small_rows = jnp.concatenate([pad8(a_v_norm_g), pad8(b_conv_w[0]), pad8(c_scale)], axis=0)
    small_gathered = all_gather_rows(small_rows, "ag_small")
    small_full = small_gathered.transpose(1, 0, 2).reshape(24, D)
    gv_full = [small_full[0:1], small_full[1:2]]
    cw_full = small_full[8:11]
    scale_full = small_full[16:17]

    mixer_shards = [
        [bf(a_w_in[0]), bf(a_w_out[0])],
        [bf(b_w_in[0]), bf(b_w_out[0])],
        [bf(c_w_in[0]), bf(c_w_grp[0]).reshape(C_GROUP_DIM, C_GROUP_DIM), bf(c_w_out[0])],
        [bf(a_w_in[1]), bf(a_w_out[1])],
    ]
    hidden_major = lambda w: jnp.swapaxes(w, 1, 2)
    gate_t, up_t = hidden_major(f_w_gate), hidden_major(f_w_up)
    gathered = []
    for i in range(4):
        ffn_shards = [bf(gate_t[i]), bf(up_t[i]), bf(f_w_down[i])]
        if i == 0:
            parts = [(mixer_shards[0], "ag_l0_mixer", [small_gathered]), (ffn_shards, "ag_l0_ffn", [])]
        else:
            parts = [(mixer_shards[i] + ffn_shards, f"ag_l{i}", [])]
        layer = []
        for shards, name, deps in parts:
            layer += place_own(all_gather_weights(shards, name, deps), shards, chip_arr, name.replace("ag", "own"))
        gathered.append(layer)

    mask = (jnp.arange(GMLP_BLOCK)[None, :] // 64) <= (jnp.arange(GMLP_BLOCK)[:, None] // 64)
    gmix = [norm_mix_g[i:i + 1] for i in range(4)]
    gffn = [norm_ffn_g[i:i + 1] for i in range(4)]
    b_chunks = [(j // 3, (j % 3) * 256, 256, j // 4, (j % 4) * 256) for j in range(12)]
    c_chunks = [(0, 0, D, 0, 0)]

    xs = [x.reshape(t, D)]
    saved = []
    for i in range(4):
        ws = gathered[i]
        wg, wu, wd = (w.reshape(FFN_HIDDEN, D) for w in ws[-3:])
        xin = xs[-1]
        if i in (0, 3):
            j = 0 if i == 0 else 1
            win, wout = ws[0], _blocked(ws[1])
            wm32 = jnp.where(mask[None], a_w_s[j], 0.0)
            wm, wmt = bf(wm32), bf(wm32.transpose(0, 2, 1))
            bs = jnp.broadcast_to(a_b_s[j][:, :, None], (A_GROUPS, GMLP_BLOCK, 128))
            h, z, vn, y, xmid = a_fwd(xin, gmix[i], win, gv_full[j], wm, bs, wout, f"a_fwd_l{i}")
            saved.append(dict(h=h, z=z, y=y, vn=vn, win=win, wout=wout, wm=wm, wmt=wmt, bs=bs, gv=gv_full[j]))
        elif i == 1:
            win, wout = ws[0], _blocked(ws[1])
            h, p3 = norm_mm(xin, gmix[i], win, b_chunks, 3, D, "b_in")
            y = b_conv_fwd(p3, cw_full, seq, "b_conv")
            xmid = mm_res(y, wout, xin, "b_out")
            saved.append(dict(h=h, p3=p3, y=y, win=win, wout=wout))
        else:
            win, wgrp, wout = _blocked(ws[0])[None], _grp_from_blocks(ws[1]), _blocked(ws[2])
            h, p = norm_mm(xin, gmix[i], win, c_chunks, 1, D, "c_in")
            dpool = c_pool_fwd(p[0], seq, "c_pool")
            y, xmid = c_out_fwd(dpool, wgrp, scale_full, wout, xin, "c_out")
            saved.append(dict(h=h, d=dpool, y=y, win=win, wgrp=wgrp, wout=wout))
        h2, fa, fb, fs, xout = ffn_fwd(xmid, gffn[i], wg, wu, wd, f"ffn_l{i}")
        saved[-1].update(h2=h2, fa=fa, fb=fb, fs=fs, xmid=xmid, wg=wg, wu=wu, wd=wd)
        xs.append(xout)

    dx, dg_final, loss_part = loss_head(xs[4], loss_target.reshape(t, D), final_norm_g[None], "loss_head")
    loss = lax.psum(loss_part[0, 0], ("x", "y", "c"))

    weights = dict(norm_mix_g=norm_mix_g, norm_ffn_g=norm_ffn_g, final_norm_g=final_norm_g, a_w_in=a_w_in, a_v_norm_g=a_v_norm_g,
                   a_w_s=a_w_s, a_b_s=a_b_s, a_w_out=a_w_out, b_w_in=b_w_in, b_conv_w=b_conv_w, b_w_out=b_w_out, c_w_in=c_w_in,
                   c_w_grp=c_w_grp, c_scale=c_scale, c_w_out=c_w_out, f_w_gate=f_w_gate, f_w_up=f_w_up, f_w_down=f_w_down)
    m_in = dict(norm_mix_g=m_norm_mix_g, norm_ffn_g=m_norm_ffn_g, final_norm_g=m_final_norm_g, a_w_in=m_a_w_in, a_v_norm_g=m_a_v_norm_g,
                a_w_s=m_a_w_s, a_b_s=m_a_b_s, a_w_out=m_a_w_out, b_w_in=m_b_w_in, b_conv_w=m_b_conv_w, b_w_out=m_b_w_out, c_w_in=m_c_w_in,
                c_w_grp=m_c_w_grp, c_scale=m_c_scale, c_w_out=m_c_w_out, f_w_gate=m_f_w_gate, f_w_up=m_f_w_up, f_w_down=m_f_w_down)
    v_in = dict(norm_mix_g=v_norm_mix_g, norm_ffn_g=v_norm_ffn_g, final_norm_g=v_final_norm_g, a_w_in=v_a_w_in, a_v_norm_g=v_a_v_norm_g,
                a_w_s=v_a_w_s, a_b_s=v_a_b_s, a_w_out=v_a_w_out, b_w_in=v_b_w_in, b_conv_w=v_b_conv_w, b_w_out=v_b_w_out, c_w_in=v_c_w_in,
                c_w_grp=v_c_w_grp, c_scale=v_c_scale, c_w_out=v_c_w_out, f_w_gate=v_f_w_gate, f_w_up=v_f_w_up, f_w_down=v_f_w_down)
    grp_rows = lambda a: a.reshape(1, C_GROUP_DIM, C_GROUP_DIM)
    same = lambda a: a
    to_stacked = {nme: same for nme in ("a_w_in", "a_w_out", "b_w_in", "b_w_out", "c_w_in", "c_w_out", "f_w_down")}
    to_stacked.update(f_w_gate=hidden_major, f_w_up=hidden_major, c_w_grp=grp_rows)
    from_stacked = dict(to_stacked, c_w_grp=lambda a: a.reshape(c_w_grp.shape))
    layer_tensors = {0: ["a_w_out", "a_w_in"], 1: ["b_w_out", "b_w_in"], 2: ["c_w_out", "c_w_in", "c_w_grp"], 3: ["a_w_out", "a_w_in"]}
    carried = {}

    def bs_rows(v):
        return jnp.pad(v[:, :, 0].reshape(1, D), ((0, 7), (0, 0)))

    def update(unit):
        done = []
        for pos, nme in enumerate(unit.names):
            stacked_layer = unit.layer if nme.startswith("f_") else (unit.layer // 3 if nme.startswith("a_") else 0)
            view = to_stacked[nme]
            carried[nme] = adamw_layer(view(weights[nme]), view(m_in[nme]), view(v_in[nme]), unit.half_sums[pos], unit.other_half[pos],
                                       core_arr, stacked_layer, carried.get(nme), f"adamw_{nme}_l{unit.layer}")
            done.append(carried[nme][0])
        return done

    ffn_names = ["f_w_gate", "f_w_up", "f_w_down"]
    dg_mix, dg_ffn = [None] * 4, [None] * 4
    small = {}
    newer = older = None
    for i in (3, 2, 1, 0):
        sv = saved[i]
        xin = xs[i]
        deps = ([newer.grads[0]] if newer else []) + ([older.half_sums[0]] if older else [])
        dxm, da, db, dg_ffn[i] = ffn_bwd(dx, sv["fa"], sv["fb"], sv["xmid"], gffn[i], sv["wg"], sv["wu"], sv["wd"], f"ffn_bwd_l{i}", deps)
        last = [newer.sum_cores([dxm])] if newer else []
        g_gate = _dw_hidden(da, sv["h2"], f"dw_gate_l{i}", last)
        g_up = _dw_hidden(db, sv["h2"], f"dw_up_l{i}", [g_gate])
        g_down = _dw_hidden(sv["fs"], dx, f"dw_down_l{i}", [g_up])
        last_ffn = [g_down]
        if i == 0:
            ffn_unit = ReduceScatter([g_down, g_gate, g_up], ["f_w_down", "f_w_gate", "f_w_up"], 0, core_arr, chip_arr, "rs_l0_ffn")
        if i in (0, 3):
            j = 0 if i == 0 else 1
            dx, dz, dwm, dbs, dgv, dg_mix[i] = a_bwd(dxm, xin, gmix[i], sv["z"], sv["vn"], sv["gv"], sv["wm"], sv["wmt"], sv["bs"],
                                                     sv["wout"], sv["win"], f"a_bwd_l{i}")
            if i == 0:
                early = [sum_devices(p, gth, me_arr, f"sum_small_grads_l123_{n}")
                         for n, (p, gth) in enumerate(zip(early_parts, early_gathered))]
                newer.sum_chips([dz] + update(older) + early)
                last_ffn.append(ffn_unit.sum_cores([newer.half_sums[0]]))
            dz = dz[None]
            g_in = _dw_cols(sv["h"], dz, 512, f"dw_a_in_l{i}", last_ffn)
            g_out = _dw_rows(sv["y"], dxm, f"dw_a_out_l{i}", [g_in])
            small[f"wm{j}"], small[f"bs{j}"], small[f"gv{j}"] = dwm, dbs, dgv
            mixer_grads = [g_out, g_in]
        elif i == 1:
            dy = mm_nt(dxm, sv["wout"], "b_dy")
            dp3, small["cw"] = b_conv_bwd(dy, sv["p3"], cw_full, seq, "b_conv_bwd")
            dx, dg_mix[i] = bwd_in(dp3, sv["win"], b_chunks, xin, gmix[i], dxm, "b_bwd_in")
            g_in = mm_tn(sv["h"][None], dp3, (N_CHIPS, D, 768), D, 256, 12,
                         lambda j: (0, 0), lambda j: (j // 4, j % 4), lambda j: (j // 3, 0, j % 3), "dw_b_in", [g_down])
            g_out = _dw_rows(sv["y"], dxm, "dw_b_out", [g_in])
            mixer_grads = [g_out, g_in]
        else:
            outs = c_out_bwd(dxm, sv["d"], sv["wgrp"], scale_full, sv["wout"], "c_out_bwd")
            dyp, dd, small["scale"] = outs[0], list(outs[1:5]), outs[5]
            dpool = c_pool_bwd(dd, seq, "c_pool_bwd")
            dp = jnp.concatenate(dpool, axis=1)[None]
            dx, dg_mix[i] = bwd_in(dp, sv["win"], c_chunks, xin, gmix[i], dxm, "c_bwd_in")
            g_in = _dw_rows(sv["h"], dp[0], "dw_c_in", [g_down])
            dcat = jnp.concatenate(sv["d"], axis=1)
            g_grp = mm_tn(dcat[None], dyp[None], (4, C_GROUP_DIM, C_GROUP_DIM), C_GROUP_DIM, C_GROUP_DIM, 4,
                          lambda j: (0, j), lambda j: (0, j), lambda j: (j, 0, 0), "dw_c_grp", [g_in])
            g_out = _dw_rows(sv["y"], dxm, "dw_c_out", [g_grp])
            mixer_grads = [g_out, g_in, _grp_to_blocks(g_grp)]
        if i > 0:
            unit = ReduceScatter(mixer_grads + [g_gate, g_up, g_down], layer_tensors[i] + ffn_names, i, core_arr, chip_arr, f"rs_l{i}")
        else:
            unit = ReduceScatter(mixer_grads, layer_tensors[0], 0, core_arr, chip_arr, "rs_l0_mixer")
        if newer and i > 0:
            newer.sum_chips([mixer_grads[0]] + (update(older) if older else []))
        if i == 1:
            early_parts = [jnp.concatenate(dg_mix[1:] + dg_ffn[1:] + [dg_final, small["gv1"], small["cw"], small["scale"],
                                                                     bs_rows(small["bs1"])], axis=0),
                           bf(small["wm1"].reshape(128, D))]
            early_gathered = all_gather_devices(early_parts, "ag_small_grads_l123")
        older, newer = newer, unit
    grad_x = dx.reshape(n_ex, seq, D)
    mixer_unit = newer
    ffn_unit.sum_chips(update(older) + [mixer_unit.grads[0]])
    mixer_unit.sum_cores([ffn_unit.half_sums[0]])
    mixer_unit.sum_chips(update(ffn_unit))
    updated = update(mixer_unit)

    late_parts = [jnp.concatenate([dg_mix[0], dg_ffn[0], small["gv0"], bs_rows(small["bs0"])], axis=0), bf(small["wm0"].reshape(128, D))]
    late_gathered = all_gather_devices(late_parts, "ag_small_grads_l0")
    late = [sum_devices(p, gth, me_arr, f"sum_small_grads_l0_{n}", updated) for n, (p, gth) in enumerate(zip(late_parts, late_gathered))]
    (early_rows, early_ws), (late_rows, late_ws) = early, late
    first_rows = lambda a, b, n: jnp.concatenate([a, b], axis=0).reshape(n, 8, D)[:, 0]
    g_norm_mix = first_rows(late_rows[0:8], early_rows[0:24], 4)
    g_norm_ffn = first_rows(late_rows[8:16], early_rows[24:48], 4)
    g_final = early_rows[48]
    g_gv = first_rows(late_rows[16:24], early_rows[56:64], 2)
    g_cw = early_rows[64:67]
    g_scale = early_rows[72:73]
    g_ws = jnp.where(mask[None, None], jnp.concatenate([late_ws, early_ws], axis=0).reshape(2, A_GROUPS, 128, 128), 0.0)
    g_bs = first_rows(late_rows[24:32], early_rows[80:88], 2).reshape(2, A_GROUPS, 128)
    col0 = chip * (D // N_CHIPS)
    cols = lambda v: lax.dynamic_slice_in_dim(v, col0, D // N_CHIPS, axis=1)

    small_grads = {
        "norm_mix_g": g_norm_mix, "norm_ffn_g": g_norm_ffn, "final_norm_g": g_final, "a_v_norm_g": cols(g_gv), "a_w_s": g_ws,
        "a_b_s": g_bs, "b_conv_w": cols(g_cw)[None], "c_scale": cols(g_scale),
    }
    results = {}
    for nme, g in small_grads.items():
        w = weights[nme]
        flat = lambda a: a.reshape(-1, w.shape[-1])
        dl, mn, vn = adamw(flat(w), flat(g), flat(m_in[nme]), flat(v_in[nme]), f"adamw_{nme}")
        results[nme] = tuple(o.reshape(w.shape) for o in (g, dl, mn, vn))
    for nme, outs in carried.items():
        results[nme] = tuple(from_stacked[nme](o) for o in outs)

    names = list(weights)
    return (loss, grad_x, *[results[n][0] for n in names], *[results[n][1] for n in names],
            *[results[n][2] for n in names], *[results[n][3] for n in names])
```
